```python
import math
import jax, jax.numpy as jnp
from jax import lax
import numpy as np

D_MODEL = 2048
BATCH = 4
SEQ = 2048
DEPTH = 1

HEAD_DIM = 128
Q_BLOCK = 128
NSA_HEADS = 8
NSA_KV_GROUPS = 2
NSA_HPG = NSA_HEADS // NSA_KV_GROUPS
CMP_LEN = 32
CMP_STRIDE = 16
SEL_BLOCK = 64
N_SEL = 16
WINDOW = 512
SEL_Q_CHUNK = 64
N_NSA_BRANCH = 3
DIFF_HEADS = 4
DIFF_V_DIM = 2 * HEAD_DIM
NUM_BUCKETS = 32
MAX_DISTANCE = 128
N_BIAS_HEADS = NSA_HEADS + DIFF_HEADS
D_FF = 5632
CONV_WIDTH = 3
N_BRANCHES = 2
EPS = 1e-6
NEG = -1e30

NSA_Q_COLS = NSA_HEADS * HEAD_DIM
NSA_KV_COLS = N_NSA_BRANCH * 2 * NSA_KV_GROUPS * HEAD_DIM
NSA_GATE_COLS = NSA_HEADS * N_NSA_BRANCH
DIFF_Q_COLS = DIFF_HEADS * 2 * HEAD_DIM
DIFF_K_COLS = DIFF_HEADS * 2 * HEAD_DIM
DIFF_V_COLS = DIFF_HEADS * DIFF_V_DIM
MERGE_GATE_COLS = N_BRANCHES * D_MODEL
OFF_NSA_KV = NSA_Q_COLS
OFF_NSA_G = OFF_NSA_KV + NSA_KV_COLS
OFF_DQ = OFF_NSA_G + NSA_GATE_COLS
OFF_DK = OFF_DQ + DIFF_Q_COLS
OFF_DV = OFF_DK + DIFF_K_COLS
OFF_MG = OFF_DV + DIFF_V_COLS
IN_COLS = OFF_MG + MERGE_GATE_COLS

kernel_name = "hybrid_nsa_diffattn_convffn_block"


def rms_norm(x, gain):
    xf = x.astype(jnp.float32)
    y = xf * lax.rsqrt(jnp.mean(xf * xf, axis=-1, keepdims=True) + EPS)
    return (y * gain.astype(jnp.float32)).astype(x.dtype)


def t5_bucket(dist):
    n = jnp.maximum(jnp.asarray(dist, jnp.int32), 0)
    max_exact = NUM_BUCKETS // 2
    nf = jnp.maximum(n, max_exact).astype(jnp.float32)
    large = max_exact + (jnp.log(nf / max_exact) / math.log(MAX_DISTANCE / max_exact) * (NUM_BUCKETS - max_exact)).astype(jnp.int32)
    large = jnp.minimum(large, NUM_BUCKETS - 1)
    return jnp.where(n < max_exact, n, large)


def masked_softmax(logits, mask):
    p = jax.nn.softmax(jnp.where(mask, logits, NEG), axis=-1)
    return jnp.where(mask, p, 0.0)


def nsa_mixer(q, kv, gate_logits, cmp_pe, cmp_w1, cmp_w2, q_gain, k_gain, bias_tab):
    B, T = q.shape[0], q.shape[1]
    G, HPG, dk = NSA_KV_GROUPS, NSA_HPG, HEAD_DIM
    scale = dk ** -0.5
    q = rms_norm(q, q_gain)
    qg = q.reshape(B, T, G, HPG, dk).transpose(0, 2, 3, 1, 4)
    bt = bias_tab[:, :NSA_HEADS]
    t_idx = np.arange(T)

    nc = (T - CMP_LEN) // CMP_STRIDE + 1
    starts = np.arange(nc) * CMP_STRIDE
    tok = starts[:, None] + np.arange(CMP_LEN)[None, :]

    def compress(z, i):
        blk = z[:, tok] + cmp_pe[i][:, None, :]
        blk = blk.transpose(0, 1, 3, 2, 4).reshape(B, nc, G, CMP_LEN * dk)
        return jax.nn.gelu(blk @ cmp_w1[i]) @ cmp_w2[i]

    kc = rms_norm(compress(kv[:, :, 0, 0], 0), k_gain[0])
    vc = compress(kv[:, :, 0, 1], 1)
    blk_end = starts + CMP_LEN - 1
    cmask = blk_end[None, :] <= t_idx[:, None]
    cbias = bt[t5_bucket(t_idx[:, None] - blk_end[None, :])].transpose(2, 0, 1).reshape(G, HPG, T, nc)
    s_c = jnp.einsum('bghtd,bcgd->bghtc', qg, kc).astype(jnp.float32) * scale + cbias
    p_cmp = masked_softmax(s_c, cmask)
    o_cmp = jnp.einsum('bghtc,bcgd->bghtd', p_cmp.astype(vc.dtype), vc)

    ns = T // SEL_BLOCK
    n_sel = min(N_SEL, ns)
    sel_start = np.arange(ns) * SEL_BLOCK
    overlap = np.clip(np.minimum(starts[:, None] + CMP_LEN, sel_start[None, :] + SEL_BLOCK)
                      - np.maximum(starts[:, None], sel_start[None, :]), 0, None) / CMP_STRIDE
    imp = jnp.einsum('bghtc,cs->bgts', p_cmp, jnp.asarray(overlap, jnp.float32))
    cur = t_idx // SEL_BLOCK
    j = np.arange(ns)
    causal_blk = j[None, :] <= cur[:, None]
    forced = (j[None, :] == 0) | (j[None, :] == cur[:, None]) | (j[None, :] == cur[:, None] - 1)
    score = jnp.where(forced, 1e4, jnp.where(causal_blk, imp, -1e4))
    _, sel_idx = lax.top_k(score, n_sel)

    k_s = rms_norm(kv[:, :, 1, 0], k_gain[1])
    v_s = kv[:, :, 1, 1]
    kb = k_s.reshape(B, ns, SEL_BLOCK, G, dk).transpose(0, 3, 1, 2, 4)
    vb = v_s.reshape(B, ns, SEL_BLOCK, G, dk).transpose(0, 3, 1, 2, 4)
    nq = T // SEL_Q_CHUNK
    q_ch = qg.reshape(B, G, HPG, nq, SEL_Q_CHUNK, dk).transpose(3, 0, 1, 2, 4, 5)
    idx_ch = sel_idx.reshape(B, G, nq, SEL_Q_CHUNK, n_sel).transpose(2, 0, 1, 3, 4)
    t_ch = jnp.asarray(t_idx.reshape(nq, SEL_Q_CHUNK), jnp.int32)
    bias_grp = bt.reshape(NUM_BUCKETS, G, HPG)
    b_ar = jnp.arange(B)[:, None, None, None]
    g_ar = jnp.arange(G)[None, :, None, None]
    n_keys = n_sel * SEL_BLOCK

    def sel_chunk(args):
        qc, ic, tc = args
        kg = kb[b_ar, g_ar, ic]
        vg = vb[b_ar, g_ar, ic]
        pos = ic[..., None] * SEL_BLOCK + jnp.arange(SEL_BLOCK)
        dist = tc[None, None, :, None, None] - pos
        bias = bias_grp[t5_bucket(dist), g_ar[..., None]].transpose(0, 1, 5, 2, 3, 4)
        s = jnp.einsum('bghqd,bgqnkd->bghqnk', qc, kg).astype(jnp.float32) * scale + bias
        s = s.reshape(B, G, HPG, SEL_Q_CHUNK, n_keys)
        mask = (dist >= 0).reshape(B, G, 1, SEL_Q_CHUNK, n_keys)
        p = masked_softmax(s, mask)
        return jnp.einsum('bghqk,bgqkd->bghqd', p.astype(vg.dtype), vg.reshape(B, G, SEL_Q_CHUNK, n_keys, dk))

    o_slc = lax.map(sel_chunk, (q_ch, idx_ch, t_ch))
    o_slc = o_slc.transpose(1, 2, 3, 0, 4, 5).reshape(B, G, HPG, T, dk)

    k_w = rms_norm(kv[:, :, 2, 0], k_gain[2])
    v_w = kv[:, :, 2, 1]
    nb = T // Q_BLOCK
    nwb = WINDOW // Q_BLOCK
    slab_len = (nwb + 1) * Q_BLOCK

    def to_slab(z):
        zb = z.transpose(0, 2, 1, 3).reshape(B, G, nb, Q_BLOCK, dk)
        zp = jnp.pad(zb, ((0, 0), (0, 0), (nwb, 0), (0, 0), (0, 0)))
        return jnp.concatenate([zp[:, :, s:s + nb] for s in range(nwb + 1)], axis=3)

    k_slab, v_slab = to_slab(k_w), to_slab(v_w)
    rq = np.arange(Q_BLOCK)
    ks = np.arange(slab_len)
    wdist = nwb * Q_BLOCK + rq[:, None] - ks[None, :]
    kpos = (np.arange(nb)[:, None] - nwb) * Q_BLOCK + ks[None, :]
    wmask = ((wdist >= 0) & (wdist < WINDOW))[None] & (kpos >= 0)[:, None, :]
    wbias = bt[t5_bucket(wdist)].transpose(2, 0, 1).reshape(G, HPG, 1, Q_BLOCK, slab_len)
    qw = qg.reshape(B, G, HPG, nb, Q_BLOCK, dk)
    s_w = jnp.einsum('bghnqd,bgnkd->bghnqk', qw, k_slab).astype(jnp.float32) * scale + wbias
    p_w = masked_softmax(s_w, wmask)
    o_win = jnp.einsum('bghnqk,bgnkd->bghnqd', p_w.astype(v_slab.dtype), v_slab).reshape(B, G, HPG, T, dk)

    gt = jax.nn.sigmoid(gate_logits.astype(jnp.float32)).astype(o_win.dtype)
    gt = gt.reshape(B, T, G, HPG, N_NSA_BRANCH).transpose(0, 2, 3, 1, 4)[..., None, :]
    o = gt[..., 0] * o_cmp + gt[..., 1] * o_slc + gt[..., 2] * o_win
    return o.transpose(0, 3, 1, 2, 4).reshape(B, T, NSA_HEADS * dk)


def diff_attention(q, k, v, q_gain, k_gain, lam_q, lam_k, subln_gain, bias_tab, lam_init):
    B, T = q.shape[0], q.shape[1]
    scale = HEAD_DIM ** -0.5
    qh = rms_norm(q, q_gain).transpose(0, 2, 3, 1, 4)
    kh = rms_norm(k, k_gain).transpose(0, 2, 3, 1, 4)
    vh = v.transpose(0, 2, 1, 3)
    lq = lam_q.astype(jnp.float32)
    lk = lam_k.astype(jnp.float32)
    lam = jnp.exp(jnp.sum(lq[0] * lk[0])) - jnp.exp(jnp.sum(lq[1] * lk[1])) + lam_init
    dbt = bias_tab[:, NSA_HEADS:]
    outs = []
    for i in range(T // Q_BLOCK):
        L = (i + 1) * Q_BLOCK
        qi = qh[:, :, :, i * Q_BLOCK:L]
        dist = (i * Q_BLOCK + np.arange(Q_BLOCK))[:, None] - np.arange(L)[None, :]
        bias = dbt[t5_bucket(dist)].transpose(2, 0, 1)
        s = jnp.einsum('bhmqd,bhmkd->bhmqk', qi, kh[:, :, :, :L]).astype(jnp.float32) * scale + bias[None, :, None]
        p = masked_softmax(s, dist >= 0)
        a = p[:, :, 0] - lam * p[:, :, 1]
        outs.append(jnp.einsum('bhqk,bhkd->bhqd', a.astype(vh.dtype), vh[:, :, :L]))
    o = jnp.concatenate(outs, axis=2)
    o = rms_norm(o, subln_gain) * (1.0 - lam_init)
    return o.transpose(0, 2, 1, 3).reshape(B, T, DIFF_HEADS * DIFF_V_DIM)


def setup_inputs(seed: int = 0) -> dict:
    key = jax.random.key(seed)
    ks = jax.random.split(key, 26)
    f32 = jnp.float32

    def nrm(k, shape, s):
        return jax.random.normal(k, shape, f32) * s

    def gain(k, shape):
        return 1.0 + 0.05 * jax.random.normal(k, shape, f32)

    D, F, dk, L = D_MODEL, D_FF, HEAD_DIM, CMP_LEN
    return {
        "x": nrm(ks[0], (BATCH, SEQ, D), 1.0),
        "c": nrm(ks[1], (BATCH, D), 1.0),
        "w_ada": nrm(ks[2], (DEPTH, D, 6 * D), D ** -0.5),
        "b_ada": nrm(ks[3], (DEPTH, 6 * D), 0.02),
        "norm1_gain": gain(ks[4], (DEPTH, D)),
        "norm2_gain": gain(ks[5], (DEPTH, D)),
        "w_in": nrm(ks[6], (DEPTH, D, IN_COLS), D ** -0.5),
        "nsa_q_gain": gain(ks[7], (DEPTH, dk)),
        "nsa_k_gain": gain(ks[8], (DEPTH, N_NSA_BRANCH, dk)),
        "cmp_pe": nrm(ks[9], (DEPTH, 2, L, dk), 0.2),
        "cmp_w1": nrm(ks[10], (DEPTH, 2, L * dk, dk), (L * dk) ** -0.5),
        "cmp_w2": nrm(ks[11], (DEPTH, 2, dk, dk), dk ** -0.5),
        "diff_q_gain": gain(ks[12], (DEPTH, dk)),
        "diff_k_gain": gain(ks[13], (DEPTH, dk)),
        "diff_lambda_q": nrm(ks[14], (DEPTH, 2, dk), 0.1),
        "diff_lambda_k": nrm(ks[15], (DEPTH, 2, dk), 0.1),
        "diff_subln_gain": gain(ks[16], (DEPTH, DIFF_V_DIM)),
        "w_nsa_out": nrm(ks[17], (DEPTH, NSA_Q_COLS, D), NSA_Q_COLS ** -0.5),
        "w_diff_out": nrm(ks[18], (DEPTH, DIFF_V_COLS, D), DIFF_V_COLS ** -0.5),
        "w_o": nrm(ks[19], (DEPTH, D, D), D ** -0.5),
        "w_ffn_up": nrm(ks[20], (DEPTH, D, 2 * F), D ** -0.5),
        "ffn_conv_w": nrm(ks[21], (DEPTH, CONV_WIDTH, 2 * F), CONV_WIDTH ** -0.5),
        "ffn_conv_b": nrm(ks[22], (DEPTH, 2 * F), 0.02),
        "w_ffn_down": nrm(ks[23], (DEPTH, F, D), F ** -0.5),
        "rel_bias": nrm(ks[24], (NUM_BUCKETS, N_BIAS_HEADS), 0.5),
    }


def reference(x, c, w_ada, b_ada, norm1_gain, norm2_gain, w_in, nsa_q_gain, nsa_k_gain, cmp_pe, cmp_w1, cmp_w2,
              diff_q_gain, diff_k_gain, diff_lambda_q, diff_lambda_k, diff_subln_gain, w_nsa_out, w_diff_out, w_o,
              w_ffn_up, ffn_conv_w, ffn_conv_b, w_ffn_down, rel_bias):
    B, T, D = x.shape
    for l in range(DEPTH):
        lam_init = 0.8 - 0.6 * math.exp(-0.3 * l)
        mod = jax.nn.silu(c) @ w_ada[l] + b_ada[l]
        sh1, sc1, g1, sh2, sc2, g2 = [m[:, None, :] for m in jnp.split(mod, 6, axis=-1)]

        h = rms_norm(x, norm1_gain[l]) * (1.0 + sc1) + sh1
        proj = h @ w_in[l]
        nsa_q = proj[..., :OFF_NSA_KV].reshape(B, T, NSA_HEADS, HEAD_DIM)
        nsa_kv = proj[..., OFF_NSA_KV:OFF_NSA_G].reshape(B, T, N_NSA_BRANCH, 2, NSA_KV_GROUPS, HEAD_DIM)
        nsa_g = proj[..., OFF_NSA_G:OFF_DQ].reshape(B, T, NSA_HEADS, N_NSA_BRANCH)
        d_q = proj[..., OFF_DQ:OFF_DK].reshape(B, T, DIFF_HEADS, 2, HEAD_DIM)
        d_k = proj[..., OFF_DK:OFF_DV].reshape(B, T, DIFF_HEADS, 2, HEAD_DIM)
        d_v = proj[..., OFF_DV:OFF_MG].reshape(B, T, DIFF_HEADS, DIFF_V_DIM)
        merge_g = jax.nn.sigmoid(proj[..., OFF_MG:].astype(jnp.float32)).astype(x.dtype).reshape(B, T, N_BRANCHES, D)

        y_nsa = nsa_mixer(nsa_q, nsa_kv, nsa_g, cmp_pe[l], cmp_w1[l], cmp_w2[l], nsa_q_gain[l], nsa_k_gain[l], rel_bias) @ w_nsa_out[l]
        y_diff = diff_attention(d_q, d_k, d_v, diff_q_gain[l], diff_k_gain[l], diff_lambda_q[l], diff_lambda_k[l],
                                diff_subln_gain[l], rel_bias, lam_init) @ w_diff_out[l]
        mixed = (merge_g[:, :, 0] * y_nsa + merge_g[:, :, 1] * y_diff) @ w_o[l]
        x = x + g1 * mixed

        h2 = rms_norm(x, norm2_gain[l]) * (1.0 + sc2) + sh2
        u = h2 @ w_ffn_up[l]
        up = jnp.pad(u, ((0, 0), (CONV_WIDTH - 1, 0), (0, 0)))
        cw = ffn_conv_w[l]
        conv = ffn_conv_b[l] + sum(cw[k] * up[:, k:k + T] for k in range(CONV_WIDTH))
        a, val = jnp.split(conv, 2, axis=-1)
        x = x + g2 * ((jax.nn.silu(a) * val) @ w_ffn_down[l])
    return x
```

```python
import functools
import math

import numpy as np
import jax
import jax.numpy as jnp
from jax import lax
from jax.experimental import pallas as pl
from jax.experimental.pallas import tpu as pltpu

F32 = jnp.float32
BF16 = jnp.bfloat16

HEAD_DIM = 128
NSA_HEADS = 8
NSA_KV_GROUPS = 2
NSA_HPG = NSA_HEADS // NSA_KV_GROUPS
CMP_LEN = 32
CMP_STRIDE = 16
SEL_BLOCK = 64
N_SEL = 16
WINDOW = 512
DIFF_HEADS = 4
NUM_BUCKETS = 32
MAX_DISTANCE = 128
EPS = 1e-6
NEG = -1e30
LAM_INIT = 0.8 - 0.6 * math.exp(-0.3 * 0)

LANE = 128
QB = 128
DQB = 256
VMEM_LIMIT = 56 * 1024 * 1024


def _cparams(n_axes):
    return pltpu.CompilerParams(dimension_semantics=("arbitrary",) * n_axes,
                                vmem_limit_bytes=VMEM_LIMIT)


def _t5_bucket_np(dist):
    n = np.maximum(np.asarray(dist, np.int32), 0)
    max_exact = NUM_BUCKETS // 2
    nf = np.maximum(n, max_exact).astype(np.float32)
    large = max_exact + (np.log(nf / np.float32(max_exact)) / np.float32(math.log(MAX_DISTANCE / max_exact))
                         * np.float32(NUM_BUCKETS - max_exact)).astype(np.int32)
    large = np.minimum(large, NUM_BUCKETS - 1)
    return np.where(n < max_exact, n, large).astype(np.int32)


def _ada_kernel(ct_ref, w_ref, b_ref, o_ref):
    ct = ct_ref[...]
    s = ct * jax.nn.sigmoid(ct)
    w = w_ref[...]
    for b in range(ct.shape[1]):
        o_ref[b:b + 1, :] = jnp.sum(w * s[:, b:b + 1], axis=0, keepdims=True) + b_ref[...]


def _ada(c, w_ada, b_ada, tn=512):
    B, D = c.shape
    N = w_ada.shape[1]
    return pl.pallas_call(
        _ada_kernel,
        grid=(N // tn,),
        in_specs=[pl.BlockSpec((D, B), lambda j: (0, 0)),
                  pl.BlockSpec((D, tn), lambda j: (0, j)),
                  pl.BlockSpec((1, tn), lambda j: (0, j))],
        out_specs=pl.BlockSpec((B, tn), lambda j: (0, j)),
        out_shape=jax.ShapeDtypeStruct((B, N), F32),
        compiler_params=_cparams(1),
        name="ada",
    )(c.T, w_ada, b_ada.reshape(1, N))


def _modnorm(x, gain, sc, sh):
    ms = jnp.mean(x * x, axis=-1, keepdims=True)
    return (x * lax.rsqrt(ms + EPS) * gain) * (1.0 + sc) + sh


def _norm1_kernel(x_ref, g_ref, sc_ref, sh_ref, o_ref):
    o_ref[0] = _modnorm(x_ref[0], g_ref[...], sc_ref[0], sh_ref[0]).astype(o_ref.dtype)


def _norm1(x, gain, mod3, tm=512):
    B, T, D = x.shape
    return pl.pallas_call(
        _norm1_kernel,
        grid=(B, T // tm),
        in_specs=[pl.BlockSpec((1, tm, D), lambda b, i: (b, i, 0)),
                  pl.BlockSpec((1, D), lambda b, i: (0, 0)),
                  pl.BlockSpec((1, 1, D), lambda b, i: (b * 6 + 1, 0, 0)),
                  pl.BlockSpec((1, 1, D), lambda b, i: (b * 6 + 0, 0, 0))],
        out_specs=pl.BlockSpec((1, tm, D), lambda b, i: (b, i, 0)),
        out_shape=jax.ShapeDtypeStruct((B, T, D), BF16),
        compiler_params=_cparams(2),
        name="norm1",
    )(x, gain.reshape(1, D), mod3, mod3)


def _inproj_kernel(a_ref, w_ref, g_ref, o_ref, *, mode):
    acc = jnp.dot(a_ref[...], w_ref[...], preferred_element_type=F32)
    if mode == "norm":
        for k in range(acc.shape[1] // LANE):
            y = acc[:, k * LANE:(k + 1) * LANE]
            ms = jnp.mean(y * y, axis=-1, keepdims=True)
            o_ref[:, k * LANE:(k + 1) * LANE] = (
                y * lax.rsqrt(ms + EPS) * g_ref[:, k * LANE:(k + 1) * LANE]).astype(o_ref.dtype)
    elif mode == "sigmoid":
        o_ref[...] = jax.nn.sigmoid(acc).astype(o_ref.dtype)
    else:
        o_ref[...] = acc.astype(o_ref.dtype)


def _inproj(h2d, w, gains, *, col0, ncols, mode, out_dtype, tm, tn, name):
    M, K = h2d.shape
    j0 = col0 // tn
    return pl.pallas_call(
        functools.partial(_inproj_kernel, mode=mode),
        grid=(ncols // tn, M // tm),
        in_specs=[pl.BlockSpec((tm, K), lambda j, i: (i, 0)),
                  pl.BlockSpec((K, tn), lambda j, i: (0, j0 + j)),
                  pl.BlockSpec((1, tn), (lambda j, i: (0, j)) if mode == "norm" else (lambda j, i: (0, 0)))],
        out_specs=pl.BlockSpec((tm, tn), lambda j, i: (i, j)),
        out_shape=jax.ShapeDtypeStruct((M, ncols), out_dtype),
        compiler_params=_cparams(2),
        name=name,
    )(h2d, w, gains)


def _bias_kernel(idx_ref, tab_ref, o_ref, *, head0, hpg):
    head = head0 + pl.program_id(0) * hpg + pl.program_id(2)
    idx = idx_ref[0]
    acc = jnp.zeros(idx.shape, F32)
    for b in range(NUM_BUCKETS):
        acc = jnp.where(idx == b, tab_ref[b, head], acc)
    o_ref[0, 0, 0] = acc


def _bias_tiles(idx, rel_bias, *, head0, groups, hpg, name):
    N, R, C = idx.shape
    return pl.pallas_call(
        functools.partial(_bias_kernel, head0=head0, hpg=hpg),
        grid=(groups, N, hpg),
        in_specs=[pl.BlockSpec((1, R, C), lambda g, n, h: (n, 0, 0)),
                  pl.BlockSpec(memory_space=pltpu.SMEM)],
        out_specs=pl.BlockSpec((1, 1, 1, R, C), lambda g, n, h: (g, n, h, 0, 0)),
        out_shape=jax.ShapeDtypeStruct((groups, N, hpg, R, C), F32),
        compiler_params=_cparams(3),
        name=name,
    )(jnp.asarray(idx), rel_bias)


def _near_idx(R):
    r = np.arange(R)[:, None]
    c = np.arange(R)[None, :]
    return np.stack([_t5_bucket_np(k * R + r - c) for k in range(3)])


def _cmp_idx(T):
    nb = T // QB
    t = (np.arange(nb)[:, None, None] * QB + np.arange(QB)[None, :, None])
    c = np.arange(LANE)[None, None, :]
    return _t5_bucket_np(t - (c * CMP_STRIDE + CMP_LEN - 1))


def _compress_kernel(zk_ref, zv_ref, pe_ref, w1_ref, w2_ref, kg_ref, kc_ref, vc_ref):
    half = CMP_LEN // 2

    def one(z_ref, i):
        p1 = jnp.zeros((LANE, HEAD_DIM), F32)
        p2 = jnp.zeros((LANE, HEAD_DIM), F32)
        for l in range(half):
            z = z_ref[0, pl.ds(l, LANE, stride=CMP_STRIDE), :]
            w_lo = w1_ref[i, l * HEAD_DIM:(l + 1) * HEAD_DIM, :].astype(BF16)
            w_hi = w1_ref[i, (half + l) * HEAD_DIM:(half + l + 1) * HEAD_DIM, :].astype(BF16)
            p1 = p1 + jnp.dot((z + pe_ref[i, l:l + 1, :]).astype(BF16), w_lo, preferred_element_type=F32)
            p2 = p2 + jnp.dot((z + pe_ref[i, half + l:half + l + 1, :]).astype(BF16), w_hi,
                              preferred_element_type=F32)
        pre = p1 + pltpu.roll(p2, LANE - 1, axis=0)
        hid = jax.nn.gelu(pre)
        return jnp.dot(hid.astype(BF16), w2_ref[i].astype(BF16), preferred_element_type=F32)

    kc = one(zk_ref, 0)
    ms = jnp.mean(kc * kc, axis=-1, keepdims=True)
    kc_ref[0, 0] = (kc * lax.rsqrt(ms + EPS) * kg_ref[...]).astype(kc_ref.dtype)
    vc_ref[0, 0] = one(zv_ref, 1).astype(vc_ref.dtype)


def _compress(cmp_kv, pe, w1, w2, k_gain0):
    B, T, _ = cmp_kv.shape
    G, dk = NSA_KV_GROUPS, HEAD_DIM
    assert (T - CMP_LEN) // CMP_STRIDE + 1 == LANE - 1
    out = jax.ShapeDtypeStruct((B, G, LANE, dk), BF16)
    return pl.pallas_call(
        _compress_kernel,
        grid=(B, G),
        in_specs=[pl.BlockSpec((1, T, dk), lambda b, g: (b, 0, g)),
                  pl.BlockSpec((1, T, dk), lambda b, g: (b, 0, G + g)),
                  pl.BlockSpec((2, CMP_LEN, dk), lambda b, g: (0, 0, 0)),
                  pl.BlockSpec((2, CMP_LEN * dk, dk), lambda b, g: (0, 0, 0)),
                  pl.BlockSpec((2, dk, dk), lambda b, g: (0, 0, 0)),
                  pl.BlockSpec((1, dk), lambda b, g: (0, 0))],
        out_specs=[pl.BlockSpec((1, 1, LANE, dk), lambda b, g: (b, g, 0, 0))] * 2,
        out_shape=[out, out],
        compiler_params=_cparams(2),
        name="compress",
    )(cmp_kv, cmp_kv, pe, w1, w2, k_gain0.reshape(1, dk))


def _qk(q, k):
    return lax.dot_general(q, k, (((1,), (1,)), ((), ())), preferred_element_type=F32)


def _flash_step(s, v, m_ref, l_ref, acc_ref):
    m_old = m_ref[...]
    m_new = jnp.maximum(m_old, jnp.max(s, axis=-1, keepdims=True))
    alpha = jnp.exp(m_old - m_new)
    p = jnp.exp(s - m_new)
    l_ref[...] = alpha * l_ref[...] + jnp.sum(p, axis=-1, keepdims=True)
    acc_ref[...] = alpha * acc_ref[...] + jnp.dot(p.astype(v.dtype), v, preferred_element_type=F32)
    m_ref[...] = m_new


def _flash_reset(m_ref, l_ref, acc_ref):
    m_ref[...] = jnp.full(m_ref.shape, NEG, F32)
    l_ref[...] = jnp.zeros(l_ref.shape, F32)
    acc_ref[...] = jnp.zeros(acc_ref.shape, F32)


def _nsa_kernel(q_ref, ks_ref, vs_ref, kw_ref, vw_ref, kc_ref, vc_ref, gate_ref, cb_ref, nb_ref,
                ovt_ref, exp_ref, o_ref, selexp_ref, m_ref, l_ref, acc_ref):
    i = pl.program_id(2)
    H, R = NSA_HPG, NSA_HPG * QB
    q = q_ref[0]
    qs = jnp.concatenate([q[:, h * HEAD_DIM:(h + 1) * HEAD_DIM] for h in range(H)], axis=0)
    row = lax.broadcasted_iota(jnp.int32, (QB, QB), 0)
    col = lax.broadcasted_iota(jnp.int32, (QB, QB), 1)

    s = _qk(qs, kc_ref[0, 0]).reshape(H, QB, LANE) + cb_ref[0, 0]
    valid = ((col * CMP_STRIDE + (CMP_LEN - 1)) <= (i * QB + row)) & (col < LANE - 1)
    s = jnp.where(valid[None], s, NEG)
    e = jnp.where(valid[None], jnp.exp(s - jnp.max(s, axis=-1, keepdims=True)), 0.0)
    den = jnp.sum(e, axis=-1, keepdims=True)
    p = e / jnp.where(den > 0.0, den, 1.0)
    o_cmp = jnp.dot(p.reshape(R, LANE).astype(BF16), vc_ref[0, 0], preferred_element_type=F32)

    psum = p[0] + p[1] + p[2] + p[3]
    imp_t = lax.dot_general(ovt_ref[...], psum, (((1,), (1,)), ((), ())),
                            precision=lax.Precision.HIGHEST, preferred_element_type=F32)
    ns = imp_t.shape[0]
    blk = lax.broadcasted_iota(jnp.int32, (ns, QB), 0)
    tl = lax.broadcasted_iota(jnp.int32, (ns, QB), 1)
    cur = (i * QB + tl) // SEL_BLOCK
    forced = (blk == 0) | (blk == cur) | (blk == cur - 1)
    score = jnp.where(forced, 1e4, jnp.where(blk <= cur, imp_t, -1e4))
    rank = jnp.zeros((ns, QB), F32)
    for b in range(ns):
        other = score[b:b + 1, :]
        rank = rank + jnp.where(blk > b, jnp.where(other >= score, 1.0, 0.0), jnp.where(other > score, 1.0, 0.0))
    sel_t = jnp.where(rank < float(min(N_SEL, ns)), 1.0, 0.0)
    sel = jnp.concatenate([sel_t, jnp.zeros((LANE - ns, QB), F32)], axis=0).T
    selexp_ref[...] = jnp.dot(sel.astype(BF16), exp_ref[...], preferred_element_type=F32)

    def attend(j, k_ref, v_ref, bias, mask):
        off = pl.multiple_of(j * QB, QB)
        s = _qk(qs, k_ref[0, pl.ds(off, QB), :]).reshape(H, QB, QB) + bias
        if mask is not None:
            s = jnp.where(mask[None], s, NEG)
        _flash_step(s.reshape(R, QB), v_ref[0, pl.ds(off, QB), :], m_ref, l_ref, acc_ref)

    def selmask(j):
        return selexp_ref[:, pl.ds(pl.multiple_of(j * QB, QB), QB)] > 0.5

    _flash_reset(m_ref, l_ref, acc_ref)

    def slc_far(j, carry):
        attend(j, ks_ref, vs_ref, nb_ref[0, 2], selmask(j))
        return carry

    lax.fori_loop(0, jnp.maximum(i - 1, 0), slc_far, 0)

    @pl.when(i >= 1)
    def _():
        attend(i - 1, ks_ref, vs_ref, nb_ref[0, 1], selmask(i - 1))

    attend(i, ks_ref, vs_ref, nb_ref[0, 0], selmask(i) & (row >= col))
    o_slc = acc_ref[...] / l_ref[...]

    _flash_reset(m_ref, l_ref, acc_ref)
    nwb = WINDOW // QB

    @pl.when(i >= nwb)
    def _():
        attend(i - nwb, kw_ref, vw_ref, nb_ref[0, 2], row < col)

    def win_far(j, carry):
        attend(j, kw_ref, vw_ref, nb_ref[0, 2], None)
        return carry

    lax.fori_loop(jnp.maximum(i - nwb + 1, 0), jnp.maximum(i - 1, 0), win_far, 0)

    @pl.when(i >= 1)
    def _():
        attend(i - 1, kw_ref, vw_ref, nb_ref[0, 1], None)

    attend(i, kw_ref, vw_ref, nb_ref[0, 0], row >= col)
    o_win = acc_ref[...] / l_ref[...]

    gate = gate_ref[0]
    outs = []
    for h in range(H):
        sl = slice(h * QB, (h + 1) * QB)
        outs.append(gate[:, 3 * h:3 * h + 1] * o_cmp[sl] + gate[:, 3 * h + 1:3 * h + 2] * o_slc[sl]
                    + gate[:, 3 * h + 2:3 * h + 3] * o_win[sl])
    o_ref[0] = jnp.concatenate(outs, axis=1).astype(o_ref.dtype)


def _nsa(normed, rawv, kc, vc, gates, cbias, nbias, T):
    B = normed.shape[0]
    G, H, dk = NSA_KV_GROUPS, NSA_HPG, HEAD_DIM
    nb = T // QB
    ns = T // SEL_BLOCK
    cstart = np.arange(LANE) * CMP_STRIDE
    sstart = np.arange(ns) * SEL_BLOCK
    overlap = np.clip(np.minimum(cstart[:, None] + CMP_LEN, sstart[None, :] + SEL_BLOCK)
                      - np.maximum(cstart[:, None], sstart[None, :]), 0, None) / CMP_STRIDE
    overlap[LANE - 1:] = 0.0
    ovt = jnp.asarray(overlap.T, F32)
    expand = np.zeros((LANE, T), np.float32)
    expand[np.arange(T) // SEL_BLOCK, np.arange(T)] = 1.0
    kq = NSA_HEADS
    return pl.pallas_call(
        _nsa_kernel,
        grid=(B, G, nb),
        in_specs=[pl.BlockSpec((1, QB, H * dk), lambda b, g, i: (b, i, g)),
                  pl.BlockSpec((1, T, dk), lambda b, g, i: (b, 0, kq + g)),
                  pl.BlockSpec((1, T, dk), lambda b, g, i: (b, 0, g)),
                  pl.BlockSpec((1, T, dk), lambda b, g, i: (b, 0, kq + G + g)),
                  pl.BlockSpec((1, T, dk), lambda b, g, i: (b, 0, G + g)),
                  pl.BlockSpec((1, 1, LANE, dk), lambda b, g, i: (b, g, 0, 0)),
                  pl.BlockSpec((1, 1, LANE, dk), lambda b, g, i: (b, g, 0, 0)),
                  pl.BlockSpec((1, QB, LANE), lambda b, g, i: (b, i, g)),
                  pl.BlockSpec((1, 1, H, QB, LANE), lambda b, g, i: (g, i, 0, 0, 0)),
                  pl.BlockSpec((1, 3, H, QB, QB), lambda b, g, i: (g, 0, 0, 0, 0)),
                  pl.BlockSpec((ns, LANE), lambda b, g, i: (0, 0)),
                  pl.BlockSpec((LANE, T), lambda b, g, i: (0, 0))],
        out_specs=pl.BlockSpec((1, QB, H * dk), lambda b, g, i: (b, i, g)),
        out_shape=jax.ShapeDtypeStruct((B, T, NSA_HEADS * dk), BF16),
        scratch_shapes=[pltpu.VMEM((QB, T), F32),
                        pltpu.VMEM((H * QB, 1), F32),
                        pltpu.VMEM((H * QB, 1), F32),
                        pltpu.VMEM((H * QB, dk), F32)],
        compiler_params=_cparams(3),
        name="nsa",
    )(normed, normed, rawv, normed, rawv, kc, vc, gates, cbias, nbias, ovt, jnp.asarray(expand, BF16))


def _diff_kernel(q_ref, k_ref, v_ref, lq_ref, lk_ref, sg_ref, db_ref, o_ref, m_ref, l_ref, acc_ref):
    i = pl.program_id(2)
    dk = HEAD_DIM
    q = q_ref[0]
    row = lax.broadcasted_iota(jnp.int32, (DQB, DQB), 0)
    col = lax.broadcasted_iota(jnp.int32, (DQB, DQB), 1)
    lqk = lq_ref[...] * lk_ref[...]
    lam = (jnp.exp(jnp.sum(lqk[0:1], axis=-1, keepdims=True))
           - jnp.exp(jnp.sum(lqk[1:2], axis=-1, keepdims=True)) + LAM_INIT)

    def attend(j, bias, mask):
        off = pl.multiple_of(j * DQB, DQB)
        kk = k_ref[0, pl.ds(off, DQB), :]
        v = v_ref[0, pl.ds(off, DQB), :]
        for mm in range(2):
            s = _qk(q[:, mm * dk:(mm + 1) * dk], kk[:, mm * dk:(mm + 1) * dk]) + bias
            if mask is not None:
                s = jnp.where(mask, s, NEG)
            _flash_step(s, v, m_ref.at[mm], l_ref.at[mm], acc_ref.at[mm])

    _flash_reset(m_ref, l_ref, acc_ref)

    def far(j, carry):
        attend(j, db_ref[0, 2, 0], None)
        return carry

    lax.fori_loop(0, jnp.maximum(i - 1, 0), far, 0)

    @pl.when(i >= 1)
    def _():
        attend(i - 1, db_ref[0, 1, 0], None)

    attend(i, db_ref[0, 0, 0], row >= col)
    o = acc_ref[0] / l_ref[0] - lam * (acc_ref[1] / l_ref[1])
    ms = jnp.mean(o * o, axis=-1, keepdims=True)
    o_ref[0] = ((o * lax.rsqrt(ms + EPS) * sg_ref[...]) * (1.0 - LAM_INIT)).astype(o_ref.dtype)


def _diff(normed, rawv, lam_q, lam_k, subln_gain, dbias, T):
    B = normed.shape[0]
    Hd, dk = DIFF_HEADS, HEAD_DIM
    w = 2 * dk
    q0 = (NSA_HEADS + 2 * NSA_KV_GROUPS) * dk // w
    k0 = q0 + Hd
    v0 = 2 * NSA_KV_GROUPS * dk // w
    return pl.pallas_call(
        _diff_kernel,
        grid=(B, Hd, T // DQB),
        in_specs=[pl.BlockSpec((1, DQB, w), lambda b, h, i: (b, i, q0 + h)),
                  pl.BlockSpec((1, T, w), lambda b, h, i: (b, 0, k0 + h)),
                  pl.BlockSpec((1, T, w), lambda b, h, i: (b, 0, v0 + h)),
                  pl.BlockSpec((2, dk), lambda b, h, i: (0, 0)),
                  pl.BlockSpec((2, dk), lambda b, h, i: (0, 0)),
                  pl.BlockSpec((1, w), lambda b, h, i: (0, 0)),
                  pl.BlockSpec((1, 3, 1, DQB, DQB), lambda b, h, i: (h, 0, 0, 0, 0))],
        out_specs=pl.BlockSpec((1, DQB, w), lambda b, h, i: (b, i, h)),
        out_shape=jax.ShapeDtypeStruct((B, T, Hd * w), BF16),
        scratch_shapes=[pltpu.VMEM((2, DQB, 1), F32),
                        pltpu.VMEM((2, DQB, 1), F32),
                        pltpu.VMEM((2, DQB, w), F32)],
        compiler_params=_cparams(3),
        name="diff",
    )(normed, normed, rawv, lam_q, lam_k, subln_gain.reshape(1, w), dbias)


def _merge_kernel(an_ref, ad_ref, wn_ref, wd_ref, gn_ref, gd_ref, o_ref):
    yn = jnp.dot(an_ref[...], wn_ref[...], preferred_element_type=F32)
    yd = jnp.dot(ad_ref[...], wd_ref[...], preferred_element_type=F32)
    o_ref[...] = (gn_ref[...].astype(F32) * yn + gd_ref[...].astype(F32) * yd).astype(o_ref.dtype)


def _merge(o_nsa, o_diff, w_n, w_d, mg, tm=512, tn=512):
    M, K = o_nsa.shape
    N = w_n.shape[1]
    nj = N // tn
    return pl.pallas_call(
        _merge_kernel,
        grid=(nj, M // tm),
        in_specs=[pl.BlockSpec((tm, K), lambda j, i: (i, 0)),
                  pl.BlockSpec((tm, K), lambda j, i: (i, 0)),
                  pl.BlockSpec((K, tn), lambda j, i: (0, j)),
                  pl.BlockSpec((K, tn), lambda j, i: (0, j)),
                  pl.BlockSpec((tm, tn), lambda j, i: (i, j)),
                  pl.BlockSpec((tm, tn), lambda j, i: (i, nj + j))],
        out_specs=pl.BlockSpec((tm, tn), lambda j, i: (i, j)),
        out_shape=jax.ShapeDtypeStruct((M, N), BF16),
        compiler_params=_cparams(2),
        name="merge",
    )(o_nsa, o_diff, w_n, w_d, mg, mg)


def _oproj_kernel(a_ref, w_ref, x_ref, g1_ref, gain_ref, sc_ref, sh_ref, x1_ref, h2_ref):
    y = jnp.dot(a_ref[...], w_ref[...], preferred_element_type=F32)
    x1 = x_ref[...] + g1_ref[0] * y
    x1_ref[...] = x1
    h2_ref[...] = _modnorm(x1, gain_ref[...], sc_ref[0], sh_ref[0]).astype(h2_ref.dtype)


def _oproj(merged, w_o, x2d, mod3, gain2, T, tm=256):
    M, D = x2d.shape
    per = T // tm
    return pl.pallas_call(
        _oproj_kernel,
        grid=(M // tm,),
        in_specs=[pl.BlockSpec((tm, D), lambda i: (i, 0)),
                  pl.BlockSpec((D, D), lambda i: (0, 0)),
                  pl.BlockSpec((tm, D), lambda i: (i, 0)),
                  pl.BlockSpec((1, 1, D), lambda i: ((i // per) * 6 + 2, 0, 0)),
                  pl.BlockSpec((1, D), lambda i: (0, 0)),
                  pl.BlockSpec((1, 1, D), lambda i: ((i // per) * 6 + 4, 0, 0)),
                  pl.BlockSpec((1, 1, D), lambda i: ((i // per) * 6 + 3, 0, 0))],
        out_specs=[pl.BlockSpec((tm, D), lambda i: (i, 0)),
                   pl.BlockSpec((tm, D), lambda i: (i, 0))],
        out_shape=[jax.ShapeDtypeStruct((M, D), F32), jax.ShapeDtypeStruct((M, D), BF16)],
        compiler_params=_cparams(1),
        name="oproj",
    )(merged, w_o, x2d, mod3, gain2.reshape(1, D), mod3, mod3)


def _ffn_up_kernel(h_ref, wa_ref, wv_ref, cwa_ref, cwv_ref, cba_ref, cbv_ref, o_ref, ca_ref, cv_ref, *, per):
    i = pl.program_id(1)
    tm = h_ref.shape[0]
    row = lax.broadcasted_iota(jnp.int32, (tm, 1), 0)

    @pl.when(i % per == 0)
    def _():
        ca_ref[...] = jnp.zeros(ca_ref.shape, F32)
        cv_ref[...] = jnp.zeros(cv_ref.shape, F32)

    def conv(w_ref, cw_ref, cb_ref, carry_ref):
        u = jnp.dot(h_ref[...], w_ref[...], preferred_element_type=F32)
        prev = carry_ref[...]
        u1 = jnp.where(row < 1, pltpu.roll(prev, 1, axis=0)[0:1, :], pltpu.roll(u, 1, axis=0))
        p2 = pltpu.roll(prev, 2, axis=0)
        u2 = pltpu.roll(u, 2, axis=0)
        u2 = jnp.where(row < 1, p2[0:1, :], jnp.where(row < 2, p2[1:2, :], u2))
        carry_ref[...] = u[tm - 8:, :]
        return cb_ref[...] + cw_ref[0:1, :] * u2 + cw_ref[1:2, :] * u1 + cw_ref[2:3, :] * u

    a = conv(wa_ref, cwa_ref, cba_ref, ca_ref)
    val = conv(wv_ref, cwv_ref, cbv_ref, cv_ref)
    o_ref[...] = (a * jax.nn.sigmoid(a) * val).astype(o_ref.dtype)


def _ffn_up(h2, w_up, conv_w, conv_b, T, tm=512, tn=512):
    M, D = h2.shape
    F = w_up.shape[1] // 2
    nj = F // tn
    cb = conv_b.reshape(1, 2 * F)
    return pl.pallas_call(
        functools.partial(_ffn_up_kernel, per=T // tm),
        grid=(nj, M // tm),
        in_specs=[pl.BlockSpec((tm, D), lambda j, i: (i, 0)),
                  pl.BlockSpec((D, tn), lambda j, i: (0, j)),
                  pl.BlockSpec((D, tn), lambda j, i: (0, nj + j)),
                  pl.BlockSpec((3, tn), lambda j, i: (0, j)),
                  pl.BlockSpec((3, tn), lambda j, i: (0, nj + j)),
                  pl.BlockSpec((1, tn), lambda j, i: (0, j)),
                  pl.BlockSpec((1, tn), lambda j, i: (0, nj + j))],
        out_specs=pl.BlockSpec((tm, tn), lambda j, i: (i, j)),
        out_shape=jax.ShapeDtypeStruct((M, F), BF16),
        scratch_shapes=[pltpu.VMEM((8, tn), F32), pltpu.VMEM((8, tn), F32)],
        compiler_params=_cparams(2),
        name="ffn_up",
    )(h2, w_up, w_up, conv_w, conv_w, cb, cb)


def _ffn_down_kernel(a_ref, w_ref, x_ref, g2_ref, o_ref):
    y = jnp.dot(a_ref[...], w_ref[...], preferred_element_type=F32)
    o_ref[...] = x_ref[...] + g2_ref[0] * y


def _ffn_down(act, w_down, x1, mod3, T, tm=512, tn=512):
    M, F = act.shape
    D = w_down.shape[1]
    per = T // tm
    return pl.pallas_call(
        _ffn_down_kernel,
        grid=(D // tn, M // tm),
        in_specs=[pl.BlockSpec((tm, F), lambda j, i: (i, 0)),
                  pl.BlockSpec((F, tn), lambda j, i: (0, j)),
                  pl.BlockSpec((tm, tn), lambda j, i: (i, j)),
                  pl.BlockSpec((1, 1, tn), lambda j, i: ((i // per) * 6 + 5, 0, j))],
        out_specs=pl.BlockSpec((tm, tn), lambda j, i: (i, j)),
        out_shape=jax.ShapeDtypeStruct((M, D), F32),
        compiler_params=_cparams(2),
        name="ffn_down",
    )(act, w_down, x1, mod3)


def _layer(x, c, w_ada, b_ada, norm1_gain, norm2_gain, w_in, nsa_q_gain, nsa_k_gain, cmp_pe, cmp_w1, cmp_w2,
           diff_q_gain, diff_k_gain, diff_lambda_q, diff_lambda_k, diff_subln_gain, w_nsa_out, w_diff_out, w_o,
           w_ffn_up, ffn_conv_w, ffn_conv_b, w_ffn_down, rel_bias):
    B, T, D = x.shape
    dk, G = HEAD_DIM, NSA_KV_GROUPS
    M = B * T
    scale = dk ** -0.5

    n_q = NSA_HEADS * dk
    o_kv = n_q
    o_g = o_kv + 3 * 2 * G * dk
    o_dq = o_g + NSA_HEADS * 3
    o_dk = o_dq + DIFF_HEADS * 2 * dk
    o_dv = o_dk + DIFF_HEADS * 2 * dk
    o_mg = o_dv + DIFF_HEADS * 2 * dk
    kvc = lambda br, kv: o_kv + (br * 2 + kv) * G * dk

    gate_cols = []
    for g in range(G):
        gate_cols += [w_in[:, o_g + g * NSA_HPG * 3:o_g + (g + 1) * NSA_HPG * 3],
                      jnp.zeros((D, LANE - NSA_HPG * 3), w_in.dtype)]
    w_rep = jnp.concatenate(
        [w_in[:, :n_q], w_in[:, kvc(1, 0):kvc(1, 1)], w_in[:, kvc(2, 0):kvc(2, 1)], w_in[:, o_dq:o_dv],
         w_in[:, kvc(1, 1):kvc(2, 0)], w_in[:, kvc(2, 1):o_g], w_in[:, o_dv:o_mg],
         w_in[:, o_mg:],
         w_in[:, kvc(0, 0):kvc(1, 0)]]
        + gate_cols, axis=1).astype(BF16)
    n_norm = n_q + 2 * G * dk + 2 * DIFF_HEADS * 2 * dk
    n_raw = 2 * G * dk + DIFF_HEADS * 2 * dk
    n_sig = 2 * D
    n_cmp = 2 * G * dk
    n_gate = G * LANE
    gains = jnp.concatenate([jnp.tile(nsa_q_gain * scale, NSA_HEADS), jnp.tile(nsa_k_gain[1], G),
                             jnp.tile(nsa_k_gain[2], G), jnp.tile(diff_q_gain * scale, 2 * DIFF_HEADS),
                             jnp.tile(diff_k_gain, 2 * DIFF_HEADS)]).reshape(1, n_norm)

    mod3 = _ada(c, w_ada, b_ada).reshape(B * 6, 1, D)
    h = _norm1(x, norm1_gain, mod3).reshape(M, D)

    proj = functools.partial(_inproj, h, w_rep, tm=512)
    normed = proj(gains, col0=0, ncols=n_norm, mode="norm", out_dtype=BF16, tn=512, name="inproj_norm")
    rawv = proj(gains, col0=n_norm, ncols=n_raw, mode="raw", out_dtype=BF16, tn=512, name="inproj_raw")
    mgate = proj(gains, col0=n_norm + n_raw, ncols=n_sig, mode="sigmoid", out_dtype=BF16, tn=512, name="inproj_mg")
    cmpkv = proj(gains, col0=n_norm + n_raw + n_sig, ncols=n_cmp, mode="raw", out_dtype=F32, tn=512,
                 name="inproj_cmp")
    gates = proj(gains, col0=n_norm + n_raw + n_sig + n_cmp, ncols=n_gate, mode="sigmoid", out_dtype=F32, tn=256,
                 name="inproj_gate")
    normed = normed.reshape(B, T, n_norm)
    rawv = rawv.reshape(B, T, n_raw)

    nbias = _bias_tiles(_near_idx(QB), rel_bias, head0=0, groups=G, hpg=NSA_HPG, name="bias_nsa")
    cbias = _bias_tiles(_cmp_idx(T), rel_bias, head0=0, groups=G, hpg=NSA_HPG, name="bias_cmp")
    dbias = _bias_tiles(_near_idx(DQB), rel_bias, head0=NSA_HEADS, groups=DIFF_HEADS, hpg=1, name="bias_diff")

    kc, vc = _compress(cmpkv.reshape(B, T, n_cmp), cmp_pe, cmp_w1, cmp_w2, nsa_k_gain[0])
    o_nsa = _nsa(normed, rawv, kc, vc, gates.reshape(B, T, n_gate), cbias, nbias, T)
    o_diff = _diff(normed, rawv, diff_lambda_q, diff_lambda_k, diff_subln_gain, dbias, T)

    merged = _merge(o_nsa.reshape(M, -1), o_diff.reshape(M, -1), w_nsa_out.astype(BF16), w_diff_out.astype(BF16),
                    mgate)
    x1, h2 = _oproj(merged, w_o.astype(BF16), x.reshape(M, D), mod3, norm2_gain, T)
    act = _ffn_up(h2, w_ffn_up.astype(BF16), ffn_conv_w, ffn_conv_b, T)
    out = _ffn_down(act, w_ffn_down.astype(BF16), x1, mod3, T)
    return out.reshape(B, T, D)


def kernel(x, c, w_ada, b_ada, norm1_gain, norm2_gain, w_in, nsa_q_gain, nsa_k_gain, cmp_pe, cmp_w1, cmp_w2,
           diff_q_gain, diff_k_gain, diff_lambda_q, diff_lambda_k, diff_subln_gain, w_nsa_out, w_diff_out, w_o,
           w_ffn_up, ffn_conv_w, ffn_conv_b, w_ffn_down, rel_bias):
    return _layer(x, c, w_ada[0], b_ada[0], norm1_gain[0], norm2_gain[0], w_in[0], nsa_q_gain[0], nsa_k_gain[0],
                  cmp_pe[0], cmp_w1[0], cmp_w2[0], diff_q_gain[0], diff_k_gain[0], diff_lambda_q[0],
                  diff_lambda_k[0], diff_subln_gain[0], w_nsa_out[0], w_diff_out[0], w_o[0], w_ffn_up[0],
                  ffn_conv_w[0], ffn_conv_b[0], w_ffn_down[0], rel_bias)
```

```python
import functools
import math

import numpy as np
import jax
import jax.numpy as jnp
from jax import lax
from jax.experimental import pallas as pl
from jax.experimental.pallas import tpu as pltpu

F32 = jnp.float32
BF16 = jnp.bfloat16

HEAD_DIM = 128
NSA_HEADS = 8
NSA_KV_GROUPS = 2
NSA_HPG = NSA_HEADS // NSA_KV_GROUPS
CMP_LEN = 32
CMP_STRIDE = 16
SEL_BLOCK = 64
N_SEL = 16
WINDOW = 512
DIFF_HEADS = 4
NUM_BUCKETS = 32
MAX_DISTANCE = 128
EPS = 1e-6
NEG = -1e30
LAM_INIT = 0.8 - 0.6 * math.exp(-0.3 * 0)
LOG2E = math.log2(math.e)
MASK_BIG = -(2.0 ** 100)

LANE = 128
QB = 128
DQB = 256
VMEM_LIMIT = 56 * 1024 * 1024


def _cparams(n_axes):
    return pltpu.CompilerParams(dimension_semantics=("arbitrary",) * n_axes,
                                vmem_limit_bytes=VMEM_LIMIT)


def _t5_bucket_np(dist):
    n = np.maximum(np.asarray(dist, np.int32), 0)
    max_exact = NUM_BUCKETS // 2
    nf = np.maximum(n, max_exact).astype(np.float32)
    large = max_exact + (np.log(nf / np.float32(max_exact)) / np.float32(math.log(MAX_DISTANCE / max_exact))
                         * np.float32(NUM_BUCKETS - max_exact)).astype(np.int32)
    large = np.minimum(large, NUM_BUCKETS - 1)
    return np.where(n < max_exact, n, large).astype(np.int32)


def _ada_kernel(ct_ref, w_ref, b_ref, o_ref):
    ct = ct_ref[...]
    s = ct * jax.nn.sigmoid(ct)
    w = w_ref[...]
    for b in range(ct.shape[1]):
        o_ref[b:b + 1, :] = jnp.sum(w * s[:, b:b + 1], axis=0, keepdims=True) + b_ref[...]


def _ada(c, w_ada, b_ada, tn=512):
    B, D = c.shape
    N = w_ada.shape[1]
    return pl.pallas_call(
        _ada_kernel,
        grid=(N // tn,),
        in_specs=[pl.BlockSpec((D, B), lambda j: (0, 0)),
                  pl.BlockSpec((D, tn), lambda j: (0, j)),
                  pl.BlockSpec((1, tn), lambda j: (0, j))],
        out_specs=pl.BlockSpec((B, tn), lambda j: (0, j)),
        out_shape=jax.ShapeDtypeStruct((B, N), F32),
        compiler_params=_cparams(1),
        name="ada",
    )(c.T, w_ada, b_ada.reshape(1, N))


def _modnorm(x, gain, sc, sh):
    ms = jnp.mean(x * x, axis=-1, keepdims=True)
    return (x * lax.rsqrt(ms + EPS) * gain) * (1.0 + sc) + sh


def _norm1_kernel(x_ref, g_ref, sc_ref, sh_ref, o_ref):
    o_ref[0] = _modnorm(x_ref[0], g_ref[...], sc_ref[0], sh_ref[0]).astype(o_ref.dtype)


def _norm1(x, gain, mod3, tm=512):
    B, T, D = x.shape
    return pl.pallas_call(
        _norm1_kernel,
        grid=(B, T // tm),
        in_specs=[pl.BlockSpec((1, tm, D), lambda b, i: (b, i, 0)),
                  pl.BlockSpec((1, D), lambda b, i: (0, 0)),
                  pl.BlockSpec((1, 1, D), lambda b, i: (b * 6 + 1, 0, 0)),
                  pl.BlockSpec((1, 1, D), lambda b, i: (b * 6 + 0, 0, 0))],
        out_specs=pl.BlockSpec((1, tm, D), lambda b, i: (b, i, 0)),
        out_shape=jax.ShapeDtypeStruct((B, T, D), BF16),
        compiler_params=_cparams(2),
        name="norm1",
    )(x, gain.reshape(1, D), mod3, mod3)


def _inproj_kernel(a_ref, w_ref, g_ref, o_ref, *, mode):
    acc = jnp.dot(a_ref[...], w_ref[...], preferred_element_type=F32)
    if mode == "norm":
        for k in range(acc.shape[1] // LANE):
            y = acc[:, k * LANE:(k + 1) * LANE]
            ms = jnp.mean(y * y, axis=-1, keepdims=True)
            o_ref[:, k * LANE:(k + 1) * LANE] = (
                y * lax.rsqrt(ms + EPS) * g_ref[:, k * LANE:(k + 1) * LANE]).astype(o_ref.dtype)
    elif mode == "sigmoid":
        o_ref[...] = jax.nn.sigmoid(acc).astype(o_ref.dtype)
    else:
        o_ref[...] = acc.astype(o_ref.dtype)


def _inproj(h2d, w, gains, *, col0, ncols, mode, out_dtype, tm, tn, name):
    M, K = h2d.shape
    j0 = col0 // tn
    return pl.pallas_call(
        functools.partial(_inproj_kernel, mode=mode),
        grid=(ncols // tn, M // tm),
        in_specs=[pl.BlockSpec((tm, K), lambda j, i: (i, 0)),
                  pl.BlockSpec((K, tn), lambda j, i: (0, j0 + j)),
                  pl.BlockSpec((1, tn), (lambda j, i: (0, j)) if mode == "norm" else (lambda j, i: (0, 0)))],
        out_specs=pl.BlockSpec((tm, tn), lambda j, i: (i, j)),
        out_shape=jax.ShapeDtypeStruct((M, ncols), out_dtype),
        compiler_params=_cparams(2),
        name=name,
    )(h2d, w, gains)


def _bias_kernel(idx_ref, tab_ref, o_ref, *, head0, hpg, rel, mult):
    head = head0 + pl.program_id(0) * hpg + pl.program_id(2)
    idx = idx_ref[0]
    acc = jnp.zeros(idx.shape, F32)
    for b in range(NUM_BUCKETS):
        acc = jnp.where(idx == b, tab_ref[b, head], acc)
    if rel:
        acc = acc - tab_ref[NUM_BUCKETS - 1, head]
    o_ref[0, 0, 0] = jnp.where(idx < 0, NEG, acc * mult)


def _bias_tiles(idx, rel_bias, *, head0, groups, hpg, name, rel=False, mult=1.0):
    N, R, C = idx.shape
    return pl.pallas_call(
        functools.partial(_bias_kernel, head0=head0, hpg=hpg, rel=rel, mult=mult),
        grid=(groups, N, hpg),
        in_specs=[pl.BlockSpec((1, R, C), lambda g, n, h: (n, 0, 0)),
                  pl.BlockSpec(memory_space=pltpu.SMEM)],
        out_specs=pl.BlockSpec((1, 1, 1, R, C), lambda g, n, h: (g, n, h, 0, 0)),
        out_shape=jax.ShapeDtypeStruct((groups, N, hpg, R, C), F32),
        compiler_params=_cparams(3),
        name=name,
    )(jnp.asarray(idx), rel_bias)


def _causal_idx(R):
    r = np.arange(R)[:, None]
    c = np.arange(R)[None, :]
    return np.stack([np.where(r >= c, _t5_bucket_np(r - c), -1), _t5_bucket_np(R + r - c)]).astype(np.int32)


def _window_idx(R):
    r = np.arange(R)[:, None]
    c = np.arange(R)[None, :]
    edge = np.where(r < c, NUM_BUCKETS - 1, -1)
    return np.concatenate([_causal_idx(R), edge[None]]).astype(np.int32)


def _cmp_idx(T):
    nb = T // QB
    t = (np.arange(nb)[:, None, None] * QB + np.arange(QB)[None, :, None])
    c = np.arange(LANE)[None, None, :]
    end = c * CMP_STRIDE + CMP_LEN - 1
    return np.where((end <= t) & (c < LANE - 1), _t5_bucket_np(t - end), -1).astype(np.int32)


def _compress_kernel(zk_ref, zv_ref, pe_ref, w1_ref, w2_ref, kg_ref, kc_ref, vc_ref):
    half = CMP_LEN // 2

    def one(z_ref, i):
        p1 = jnp.zeros((LANE, HEAD_DIM), F32)
        p2 = jnp.zeros((LANE, HEAD_DIM), F32)
        for l in range(half):
            z = z_ref[0, pl.ds(l, LANE, stride=CMP_STRIDE), :]
            w_lo = w1_ref[i, l * HEAD_DIM:(l + 1) * HEAD_DIM, :].astype(BF16)
            w_hi = w1_ref[i, (half + l) * HEAD_DIM:(half + l + 1) * HEAD_DIM, :].astype(BF16)
            p1 = p1 + jnp.dot((z + pe_ref[i, l:l + 1, :]).astype(BF16), w_lo, preferred_element_type=F32)
            p2 = p2 + jnp.dot((z + pe_ref[i, half + l:half + l + 1, :]).astype(BF16), w_hi,
                              preferred_element_type=F32)
        pre = p1 + pltpu.roll(p2, LANE - 1, axis=0)
        hid = jax.nn.gelu(pre)
        return jnp.dot(hid.astype(BF16), w2_ref[i].astype(BF16), preferred_element_type=F32)

    kc = one(zk_ref, 0)
    ms = jnp.mean(kc * kc, axis=-1, keepdims=True)
    kc_ref[0, 0] = (kc * lax.rsqrt(ms + EPS) * kg_ref[...]).astype(kc_ref.dtype)
    vc_ref[0, 0] = one(zv_ref, 1).astype(vc_ref.dtype)


def _compress(cmp_kv, pe, w1, w2, k_gain0):
    B, T, _ = cmp_kv.shape
    G, dk = NSA_KV_GROUPS, HEAD_DIM
    assert (T - CMP_LEN) // CMP_STRIDE + 1 == LANE - 1
    out = jax.ShapeDtypeStruct((B, G, LANE, dk), BF16)
    return pl.pallas_call(
        _compress_kernel,
        grid=(B, G),
        in_specs=[pl.BlockSpec((1, T, dk), lambda b, g: (b, 0, g)),
                  pl.BlockSpec((1, T, dk), lambda b, g: (b, 0, G + g)),
                  pl.BlockSpec((2, CMP_LEN, dk), lambda b, g: (0, 0, 0)),
                  pl.BlockSpec((2, CMP_LEN * dk, dk), lambda b, g: (0, 0, 0)),
                  pl.BlockSpec((2, dk, dk), lambda b, g: (0, 0, 0)),
                  pl.BlockSpec((1, dk), lambda b, g: (0, 0))],
        out_specs=[pl.BlockSpec((1, 1, LANE, dk), lambda b, g: (b, g, 0, 0))] * 2,
        out_shape=[out, out],
        compiler_params=_cparams(2),
        name="compress",
    )(cmp_kv, cmp_kv, pe, w1, w2, k_gain0.reshape(1, dk))


def _qk(q, k):
    return lax.dot_general(q, k, (((1,), (1,)), ((), ())), preferred_element_type=F32)


def _lane_fold(x, op):
    acc = x[..., :LANE]
    for t in range(1, x.shape[-1] // LANE):
        acc = op(acc, x[..., t * LANE:(t + 1) * LANE])
    return acc


def _softmax_parts(parts):
    m = jnp.max(functools.reduce(jnp.maximum, [_lane_fold(s, jnp.maximum) for s in parts]), axis=-1, keepdims=True)
    ps = [jnp.exp2(s - m) for s in parts]
    den = jnp.sum(functools.reduce(jnp.add, [_lane_fold(p, jnp.add) for p in ps]), axis=-1, keepdims=True)
    return ps, den


def _nsa_kernel(q_ref, ks_ref, vs_ref, kw_ref, vw_ref, kc_ref, vc_ref, gate_ref, cb_ref, nb_ref,
                ovt_ref, exp_ref, o_ref):
    H, R = NSA_HPG, NSA_HPG * QB
    T = ks_ref.shape[1]
    ns = ovt_ref.shape[0]
    nwb = WINDOW // QB
    row = lax.broadcasted_iota(jnp.int32, (QB, LANE), 0)
    col = lax.broadcasted_iota(jnp.int32, (QB, LANE), 1)
    blk = lax.broadcasted_iota(jnp.int32, (ns, QB), 0)
    tl = lax.broadcasted_iota(jnp.int32, (ns, QB), 1)
    kc = kc_ref[0, 0]
    vc = vc_ref[0, 0]

    def attend(qs, k_ref, v_ref, spans):
        parts = []
        for a, b, add in spans:
            s = _qk(qs, k_ref[0, a:b, :]).reshape(H, QB, b - a)
            parts.append(s if add is None else s + add)
        ps, den = _softmax_parts(parts)
        o = None
        for (a, b, _), p in zip(spans, ps):
            pv = jnp.dot(p.reshape(R, b - a).astype(BF16), v_ref[0, a:b, :], preferred_element_type=F32)
            o = pv if o is None else o + pv
        return o / den.reshape(R, 1)

    def block(i):
        lo, hi = i * QB, (i + 1) * QB
        q = q_ref[0]
        qs = jnp.concatenate([q[:, h * HEAD_DIM:(h + 1) * HEAD_DIM] for h in range(H)], axis=0)

        s = _qk(qs, kc).reshape(H, QB, LANE) + cb_ref[0, 0]
        e = jnp.exp2(s - jnp.max(s, axis=-1, keepdims=True))
        if i == 0:
            valid = (col * CMP_STRIDE + (CMP_LEN - 1)) <= row
            e = jnp.where(valid[None], e, 0.0)
            den = jnp.sum(e, axis=-1, keepdims=True)
            p = e / jnp.where(den > 0.0, den, 1.0)
        else:
            p = e / jnp.sum(e, axis=-1, keepdims=True)
        o_cmp = jnp.dot(p.reshape(R, LANE).astype(BF16), vc, preferred_element_type=F32)

        slc_spans = [(lo, hi, nb_ref[0, 0])]
        if i >= 1:
            psum = p[0] + p[1] + p[2] + p[3]
            imp_t = lax.dot_general(ovt_ref[...], psum, (((1,), (1,)), ((), ())),
                                    precision=lax.Precision.HIGHEST, preferred_element_type=F32)
            cur = (lo + tl) // SEL_BLOCK
            forced = (blk == 0) | (blk == cur) | (blk == cur - 1)
            score = jnp.where(forced, 1e4, jnp.where(blk <= cur, imp_t, -1e4))
            rank = jnp.zeros((ns, QB), F32)
            for b in range(ns):
                other = score[b:b + 1, :]
                rank = rank + jnp.where(blk > b, jnp.where(other >= score, 1.0, 0.0),
                                        jnp.where(other > score, 1.0, 0.0))
            unsel_t = jnp.where(rank < float(min(N_SEL, ns)), 0.0, 1.0)
            unsel = jnp.concatenate([unsel_t, jnp.zeros((LANE - ns, QB), F32)], axis=0).T
            drop = jnp.dot(unsel.astype(BF16), exp_ref[:, 0:lo], preferred_element_type=F32)
            slc_spans.insert(0, (lo - QB, lo, nb_ref[0, 1] + drop[None, :, lo - QB:lo]))
            if i >= 2:
                slc_spans.insert(0, (0, lo - QB, drop[None, :, 0:lo - QB]))

        o_slc = attend(qs, ks_ref, vs_ref, slc_spans)

        win_spans = []
        if i >= nwb:
            win_spans.append(((i - nwb) * QB, (i - nwb + 1) * QB, nb_ref[0, 2]))
        mid_a, mid_b = max(i - nwb + 1, 0) * QB, (i - 1) * QB
        if mid_b > mid_a:
            win_spans.append((mid_a, mid_b, None))
        if i >= 1:
            win_spans.append((lo - QB, lo, nb_ref[0, 1]))
        win_spans.append((lo, hi, nb_ref[0, 0]))
        o_win = attend(qs, kw_ref, vw_ref, win_spans)

        gate = gate_ref[0]
        outs = []
        for h in range(H):
            sl = slice(h * QB, (h + 1) * QB)
            outs.append(gate[:, 3 * h:3 * h + 1] * o_cmp[sl] + gate[:, 3 * h + 1:3 * h + 2] * o_slc[sl]
                        + gate[:, 3 * h + 2:3 * h + 3] * o_win[sl])
        o_ref[0] = jnp.concatenate(outs, axis=1).astype(o_ref.dtype)

    for i in range(T // QB):
        pl.when(pl.program_id(2) == i)(functools.partial(block, i))


def _nsa(normed, rawv, kc, vc, gates, cbias, nbias, T):
    B = normed.shape[0]
    G, H, dk = NSA_KV_GROUPS, NSA_HPG, HEAD_DIM
    nb = T // QB
    ns = T // SEL_BLOCK
    cstart = np.arange(LANE) * CMP_STRIDE
    sstart = np.arange(ns) * SEL_BLOCK
    overlap = np.clip(np.minimum(cstart[:, None] + CMP_LEN, sstart[None, :] + SEL_BLOCK)
                      - np.maximum(cstart[:, None], sstart[None, :]), 0, None) / CMP_STRIDE
    overlap[LANE - 1:] = 0.0
    ovt = jnp.asarray(overlap.T, F32)
    expand = np.zeros((LANE, T), np.float32)
    expand[np.arange(T) // SEL_BLOCK, np.arange(T)] = MASK_BIG
    kq = NSA_HEADS
    return pl.pallas_call(
        _nsa_kernel,
        grid=(B, G, nb),
        in_specs=[pl.BlockSpec((1, QB, H * dk), lambda b, g, i: (b, i, g)),
                  pl.BlockSpec((1, T, dk), lambda b, g, i: (b, 0, kq + g)),
                  pl.BlockSpec((1, T, dk), lambda b, g, i: (b, 0, g)),
                  pl.BlockSpec((1, T, dk), lambda b, g, i: (b, 0, kq + G + g)),
                  pl.BlockSpec((1, T, dk), lambda b, g, i: (b, 0, G + g)),
                  pl.BlockSpec((1, 1, LANE, dk), lambda b, g, i: (b, g, 0, 0)),
                  pl.BlockSpec((1, 1, LANE, dk), lambda b, g, i: (b, g, 0, 0)),
                  pl.BlockSpec((1, QB, LANE), lambda b, g, i: (b, i, g)),
                  pl.BlockSpec((1, 1, H, QB, LANE), lambda b, g, i: (g, i, 0, 0, 0)),
                  pl.BlockSpec((1, 3, H, QB, QB), lambda b, g, i: (g, 0, 0, 0, 0)),
                  pl.BlockSpec((ns, LANE), lambda b, g, i: (0, 0)),
                  pl.BlockSpec((LANE, T), lambda b, g, i: (0, 0))],
        out_specs=pl.BlockSpec((1, QB, H * dk), lambda b, g, i: (b, i, g)),
        out_shape=jax.ShapeDtypeStruct((B, T, NSA_HEADS * dk), BF16),
        compiler_params=_cparams(3),
        name="nsa",
    )(normed, normed, rawv, normed, rawv, kc, vc, gates, cbias, nbias, ovt, jnp.asarray(expand, BF16))


def _diff_kernel(q_ref, k_ref, v_ref, lq_ref, lk_ref, sg_ref, db_ref, o_ref):
    dk = HEAD_DIM
    T = k_ref.shape[1]
    lqk = lq_ref[...] * lk_ref[...]
    lam = (jnp.exp(jnp.sum(lqk[0:1], axis=-1, keepdims=True))
           - jnp.exp(jnp.sum(lqk[1:2], axis=-1, keepdims=True)) + LAM_INIT)
    for i in range(T // DQB):
        lo, hi = i * DQB, (i + 1) * DQB
        halves = []
        for mm in range(2):
            cols = slice(mm * dk, (mm + 1) * dk)
            q = q_ref[0, lo:hi, cols]
            bounds, parts = [], []
            if i >= 2:
                bounds.append((0, lo - DQB))
                parts.append(_qk(q, k_ref[0, 0:lo - DQB, cols]))
            if i >= 1:
                bounds.append((lo - DQB, lo))
                parts.append(_qk(q, k_ref[0, lo - DQB:lo, cols]) + db_ref[0, 1, 0])
            bounds.append((lo, hi))
            parts.append(_qk(q, k_ref[0, lo:hi, cols]) + db_ref[0, 0, 0])
            ps, den = _softmax_parts(parts)
            o = None
            for (a, b), p in zip(bounds, ps):
                pv = jnp.dot(p.astype(BF16), v_ref[0, a:b, :], preferred_element_type=F32)
                o = pv if o is None else o + pv
            halves.append(o / den)
        o = halves[0] - lam * halves[1]
        ms = jnp.mean(o * o, axis=-1, keepdims=True)
        o_ref[0, lo:hi, :] = ((o * lax.rsqrt(ms + EPS) * sg_ref[...]) * (1.0 - LAM_INIT)).astype(o_ref.dtype)


def _diff(normed, rawv, lam_q, lam_k, subln_gain, dbias, T):
    B = normed.shape[0]
    Hd, dk = DIFF_HEADS, HEAD_DIM
    w = 2 * dk
    q0 = (NSA_HEADS + 2 * NSA_KV_GROUPS) * dk // w
    k0 = q0 + Hd
    v0 = 2 * NSA_KV_GROUPS * dk // w
    return pl.pallas_call(
        _diff_kernel,
        grid=(B, Hd),
        in_specs=[pl.BlockSpec((1, T, w), lambda b, h: (b, 0, q0 + h)),
                  pl.BlockSpec((1, T, w), lambda b, h: (b, 0, k0 + h)),
                  pl.BlockSpec((1, T, w), lambda b, h: (b, 0, v0 + h)),
                  pl.BlockSpec((2, dk), lambda b, h: (0, 0)),
                  pl.BlockSpec((2, dk), lambda b, h: (0, 0)),
                  pl.BlockSpec((1, w), lambda b, h: (0, 0)),
                  pl.BlockSpec((1, 2, 1, DQB, DQB), lambda b, h: (h, 0, 0, 0, 0))],
        out_specs=pl.BlockSpec((1, T, w), lambda b, h: (b, 0, h)),
        out_shape=jax.ShapeDtypeStruct((B, T, Hd * w), BF16),
        compiler_params=_cparams(2),
        name="diff",
    )(normed, normed, rawv, lam_q, lam_k, subln_gain.reshape(1, w), dbias)


def _merge_kernel(an_ref, ad_ref, wn_ref, wd_ref, gn_ref, gd_ref, o_ref):
    yn = jnp.dot(an_ref[...], wn_ref[...], preferred_element_type=F32)
    yd = jnp.dot(ad_ref[...], wd_ref[...], preferred_element_type=F32)
    o_ref[...] = (gn_ref[...].astype(F32) * yn + gd_ref[...].astype(F32) * yd).astype(o_ref.dtype)


def _merge(o_nsa, o_diff, w_n, w_d, mg, tm=512, tn=512):
    M, K = o_nsa.shape
    N = w_n.shape[1]
    nj = N // tn
    return pl.pallas_call(
        _merge_kernel,
        grid=(nj, M // tm),
        in_specs=[pl.BlockSpec((tm, K), lambda j, i: (i, 0)),
                  pl.BlockSpec((tm, K), lambda j, i: (i, 0)),
                  pl.BlockSpec((K, tn), lambda j, i: (0, j)),
                  pl.BlockSpec((K, tn), lambda j, i: (0, j)),
                  pl.BlockSpec((tm, tn), lambda j, i: (i, j)),
                  pl.BlockSpec((tm, tn), lambda j, i: (i, nj + j))],
        out_specs=pl.BlockSpec((tm, tn), lambda j, i: (i, j)),
        out_shape=jax.ShapeDtypeStruct((M, N), BF16),
        compiler_params=_cparams(2),
        name="merge",
    )(o_nsa, o_diff, w_n, w_d, mg, mg)


def _oproj_kernel(a_ref, w_ref, x_ref, g1_ref, gain_ref, sc_ref, sh_ref, x1_ref, h2_ref):
    y = jnp.dot(a_ref[...], w_ref[...], preferred_element_type=F32)
    x1 = x_ref[...] + g1_ref[0] * y
    x1_ref[...] = x1
    h2_ref[...] = _modnorm(x1, gain_ref[...], sc_ref[0], sh_ref[0]).astype(h2_ref.dtype)


def _oproj(merged, w_o, x2d, mod3, gain2, T, tm=256):
    M, D = x2d.shape
    per = T // tm
    return pl.pallas_call(
        _oproj_kernel,
        grid=(M // tm,),
        in_specs=[pl.BlockSpec((tm, D), lambda i: (i, 0)),
                  pl.BlockSpec((D, D), lambda i: (0, 0)),
                  pl.BlockSpec((tm, D), lambda i: (i, 0)),
                  pl.BlockSpec((1, 1, D), lambda i: ((i // per) * 6 + 2, 0, 0)),
                  pl.BlockSpec((1, D), lambda i: (0, 0)),
                  pl.BlockSpec((1, 1, D), lambda i: ((i // per) * 6 + 4, 0, 0)),
                  pl.BlockSpec((1, 1, D), lambda i: ((i // per) * 6 + 3, 0, 0))],
        out_specs=[pl.BlockSpec((tm, D), lambda i: (i, 0)),
                   pl.BlockSpec((tm, D), lambda i: (i, 0))],
        out_shape=[jax.ShapeDtypeStruct((M, D), F32), jax.ShapeDtypeStruct((M, D), BF16)],
        compiler_params=_cparams(1),
        name="oproj",
    )(merged, w_o, x2d, mod3, gain2.reshape(1, D), mod3, mod3)


def _ffn_up_kernel(h_ref, wa_ref, wv_ref, cwa_ref, cwv_ref, cba_ref, cbv_ref, o_ref, ca_ref, cv_ref, *, per):
    i = pl.program_id(1)
    tm = h_ref.shape[0]
    row = lax.broadcasted_iota(jnp.int32, (tm, 1), 0)

    @pl.when(i % per == 0)
    def _():
        ca_ref[...] = jnp.zeros(ca_ref.shape, F32)
        cv_ref[...] = jnp.zeros(cv_ref.shape, F32)

    def conv(w_ref, cw_ref, cb_ref, carry_ref):
        u = jnp.dot(h_ref[...], w_ref[...], preferred_element_type=F32)
        prev = carry_ref[...]
        u1 = jnp.where(row < 1, pltpu.roll(prev, 1, axis=0)[0:1, :], pltpu.roll(u, 1, axis=0))
        p2 = pltpu.roll(prev, 2, axis=0)
        u2 = pltpu.roll(u, 2, axis=0)
        u2 = jnp.where(row < 1, p2[0:1, :], jnp.where(row < 2, p2[1:2, :], u2))
        carry_ref[...] = u[tm - 8:, :]
        return cb_ref[...] + cw_ref[0:1, :] * u2 + cw_ref[1:2, :] * u1 + cw_ref[2:3, :] * u

    a = conv(wa_ref, cwa_ref, cba_ref, ca_ref)
    val = conv(wv_ref, cwv_ref, cbv_ref, cv_ref)
    o_ref[...] = (a * jax.nn.sigmoid(a) * val).astype(o_ref.dtype)


def _ffn_up(h2, w_up, conv_w, conv_b, T, tm=512, tn=512):
    M, D = h2.shape
    F = w_up.shape[1] // 2
    nj = F // tn
    cb = conv_b.reshape(1, 2 * F)
    return pl.pallas_call(
        functools.partial(_ffn_up_kernel, per=T // tm),
        grid=(nj, M // tm),
        in_specs=[pl.BlockSpec((tm, D), lambda j, i: (i, 0)),
                  pl.BlockSpec((D, tn), lambda j, i: (0, j)),
                  pl.BlockSpec((D, tn), lambda j, i: (0, nj + j)),
                  pl.BlockSpec((3, tn), lambda j, i: (0, j)),
                  pl.BlockSpec((3, tn), lambda j, i: (0, nj + j)),
                  pl.BlockSpec((1, tn), lambda j, i: (0, j)),
                  pl.BlockSpec((1, tn), lambda j, i: (0, nj + j))],
        out_specs=pl.BlockSpec((tm, tn), lambda j, i: (i, j)),
        out_shape=jax.ShapeDtypeStruct((M, F), BF16),
        scratch_shapes=[pltpu.VMEM((8, tn), F32), pltpu.VMEM((8, tn), F32)],
        compiler_params=_cparams(2),
        name="ffn_up",
    )(h2, w_up, w_up, conv_w, conv_w, cb, cb)


def _ffn_down_kernel(a_ref, w_ref, x_ref, g2_ref, o_ref):
    y = jnp.dot(a_ref[...], w_ref[...], preferred_element_type=F32)
    o_ref[...] = x_ref[...] + g2_ref[0] * y


def _ffn_down(act, w_down, x1, mod3, T, tm=512, tn=512):
    M, F = act.shape
    D = w_down.shape[1]
    per = T // tm
    return pl.pallas_call(
        _ffn_down_kernel,
        grid=(D // tn, M // tm),
        in_specs=[pl.BlockSpec((tm, F), lambda j, i: (i, 0)),
                  pl.BlockSpec((F, tn), lambda j, i: (0, j)),
                  pl.BlockSpec((tm, tn), lambda j, i: (i, j)),
                  pl.BlockSpec((1, 1, tn), lambda j, i: ((i // per) * 6 + 5, 0, j))],
        out_specs=pl.BlockSpec((tm, tn), lambda j, i: (i, j)),
        out_shape=jax.ShapeDtypeStruct((M, D), F32),
        compiler_params=_cparams(2),
        name="ffn_down",
    )(act, w_down, x1, mod3)


def _layer(x, c, w_ada, b_ada, norm1_gain, norm2_gain, w_in, nsa_q_gain, nsa_k_gain, cmp_pe, cmp_w1, cmp_w2,
           diff_q_gain, diff_k_gain, diff_lambda_q, diff_lambda_k, diff_subln_gain, w_nsa_out, w_diff_out, w_o,
           w_ffn_up, ffn_conv_w, ffn_conv_b, w_ffn_down, rel_bias):
    B, T, D = x.shape
    dk, G = HEAD_DIM, NSA_KV_GROUPS
    M = B * T
    scale = dk ** -0.5

    n_q = NSA_HEADS * dk
    o_kv = n_q
    o_g = o_kv + 3 * 2 * G * dk
    o_dq = o_g + NSA_HEADS * 3
    o_dk = o_dq + DIFF_HEADS * 2 * dk
    o_dv = o_dk + DIFF_HEADS * 2 * dk
    o_mg = o_dv + DIFF_HEADS * 2 * dk
    kvc = lambda br, kv: o_kv + (br * 2 + kv) * G * dk

    gate_cols = []
    for g in range(G):
        gate_cols += [w_in[:, o_g + g * NSA_HPG * 3:o_g + (g + 1) * NSA_HPG * 3],
                      jnp.zeros((D, LANE - NSA_HPG * 3), w_in.dtype)]
    w_rep = jnp.concatenate(
        [w_in[:, :n_q], w_in[:, kvc(1, 0):kvc(1, 1)], w_in[:, kvc(2, 0):kvc(2, 1)], w_in[:, o_dq:o_dv],
         w_in[:, kvc(1, 1):kvc(2, 0)], w_in[:, kvc(2, 1):o_g], w_in[:, o_dv:o_mg],
         w_in[:, o_mg:],
         w_in[:, kvc(0, 0):kvc(1, 0)]]
        + gate_cols, axis=1).astype(BF16)
    n_norm = n_q + 2 * G * dk + 2 * DIFF_HEADS * 2 * dk
    n_raw = 2 * G * dk + DIFF_HEADS * 2 * dk
    n_sig = 2 * D
    n_cmp = 2 * G * dk
    n_gate = G * LANE
    gains = jnp.concatenate([jnp.tile(nsa_q_gain * (scale * LOG2E), NSA_HEADS), jnp.tile(nsa_k_gain[1], G),
                             jnp.tile(nsa_k_gain[2], G), jnp.tile(diff_q_gain * (scale * LOG2E), 2 * DIFF_HEADS),
                             jnp.tile(diff_k_gain, 2 * DIFF_HEADS)]).reshape(1, n_norm)

    mod3 = _ada(c, w_ada, b_ada).reshape(B * 6, 1, D)
    h = _norm1(x, norm1_gain, mod3).reshape(M, D)

    proj = functools.partial(_inproj, h, w_rep, tm=512)
    normed = proj(gains, col0=0, ncols=n_norm, mode="norm", out_dtype=BF16, tn=512, name="inproj_norm")
    rawv = proj(gains, col0=n_norm, ncols=n_raw, mode="raw", out_dtype=BF16, tn=512, name="inproj_raw")
    mgate = proj(gains, col0=n_norm + n_raw, ncols=n_sig, mode="sigmoid", out_dtype=BF16, tn=512, name="inproj_mg")
    cmpkv = proj(gains, col0=n_norm + n_raw + n_sig, ncols=n_cmp, mode="raw", out_dtype=F32, tn=512,
                 name="inproj_cmp")
    gates = proj(gains, col0=n_norm + n_raw + n_sig + n_cmp, ncols=n_gate, mode="sigmoid", out_dtype=F32, tn=256,
                 name="inproj_gate")
    normed = normed.reshape(B, T, n_norm)
    rawv = rawv.reshape(B, T, n_raw)

    nbias = _bias_tiles(_window_idx(QB), rel_bias, head0=0, groups=G, hpg=NSA_HPG, name="bias_nsa",
                        rel=True, mult=LOG2E)
    cbias = _bias_tiles(_cmp_idx(T), rel_bias, head0=0, groups=G, hpg=NSA_HPG, name="bias_cmp", mult=LOG2E)
    dbias = _bias_tiles(_causal_idx(DQB), rel_bias, head0=NSA_HEADS, groups=DIFF_HEADS, hpg=1, name="bias_diff",
                        rel=True, mult=LOG2E)

    kc, vc = _compress(cmpkv.reshape(B, T, n_cmp), cmp_pe, cmp_w1, cmp_w2, nsa_k_gain[0])
    o_nsa = _nsa(normed, rawv, kc, vc, gates.reshape(B, T, n_gate), cbias, nbias, T)
    o_diff = _diff(normed, rawv, diff_lambda_q, diff_lambda_k, diff_subln_gain, dbias, T)

    merged = _merge(o_nsa.reshape(M, -1), o_diff.reshape(M, -1), w_nsa_out.astype(BF16), w_diff_out.astype(BF16),
                    mgate)
    x1, h2 = _oproj(merged, w_o.astype(BF16), x.reshape(M, D), mod3, norm2_gain, T)
    act = _ffn_up(h2, w_ffn_up.astype(BF16), ffn_conv_w, ffn_conv_b, T)
    out = _ffn_down(act, w_ffn_down.astype(BF16), x1, mod3, T)
    return out.reshape(B, T, D)


def kernel(x, c, w_ada, b_ada, norm1_gain, norm2_gain, w_in, nsa_q_gain, nsa_k_gain, cmp_pe, cmp_w1, cmp_w2,
           diff_q_gain, diff_k_gain, diff_lambda_q, diff_lambda_k, diff_subln_gain, w_nsa_out, w_diff_out, w_o,
           w_ffn_up, ffn_conv_w, ffn_conv_b, w_ffn_down, rel_bias):
    return _layer(x, c, w_ada[0], b_ada[0], norm1_gain[0], norm2_gain[0], w_in[0], nsa_q_gain[0], nsa_k_gain[0],
                  cmp_pe[0], cmp_w1[0], cmp_w2[0], diff_q_gain[0], diff_k_gain[0], diff_lambda_q[0],
                  diff_lambda_k[0], diff_subln_gain[0], w_nsa_out[0], w_diff_out[0], w_o[0], w_ffn_up[0],
                  ffn_conv_w[0], ffn_conv_b[0], w_ffn_down[0], rel_bias)
```

```python
import functools
import math

import numpy as np
import jax
import jax.numpy as jnp
from jax import lax
from jax.experimental import pallas as pl
from jax.experimental.pallas import tpu as pltpu

F32 = jnp.float32
BF16 = jnp.bfloat16

HEAD_DIM = 128
NSA_HEADS = 8
NSA_KV_GROUPS = 2
NSA_HPG = NSA_HEADS // NSA_KV_GROUPS
CMP_LEN = 32
CMP_STRIDE = 16
SEL_BLOCK = 64
N_SEL = 16
WINDOW = 512
DIFF_HEADS = 4
NUM_BUCKETS = 32
MAX_DISTANCE = 128
EPS = 1e-6
NEG = -1e30
LAM_INIT = 0.8 - 0.6 * math.exp(-0.3 * 0)
LOG2E = math.log2(math.e)
MASK_BIG = -(2.0 ** 100)

LANE = 128
QB = 128
DQB = 256
SUB_ROWS = 256
VMEM_LIMIT = 56 * 1024 * 1024


def _cparams(n_axes):
    return pltpu.CompilerParams(dimension_semantics=("arbitrary",) * n_axes,
                                vmem_limit_bytes=VMEM_LIMIT)


def _t5_bucket_np(dist):
    n = np.maximum(np.asarray(dist, np.int32), 0)
    max_exact = NUM_BUCKETS // 2
    nf = np.maximum(n, max_exact).astype(np.float32)
    large = max_exact + (np.log(nf / np.float32(max_exact)) / np.float32(math.log(MAX_DISTANCE / max_exact))
                         * np.float32(NUM_BUCKETS - max_exact)).astype(np.int32)
    large = np.minimum(large, NUM_BUCKETS - 1)
    return np.where(n < max_exact, n, large).astype(np.int32)


def _ada_kernel(ct_ref, w_ref, b_ref, o_ref):
    ct = ct_ref[...]
    s = ct * jax.nn.sigmoid(ct)
    w = w_ref[...]
    for b in range(ct.shape[1]):
        o_ref[b:b + 1, :] = jnp.sum(w * s[:, b:b + 1], axis=0, keepdims=True) + b_ref[...]


def _ada(c, w_ada, b_ada, tn=512):
    B, D = c.shape
    N = w_ada.shape[1]
    return pl.pallas_call(
        _ada_kernel,
        grid=(N // tn,),
        in_specs=[pl.BlockSpec((D, B), lambda j: (0, 0)),
                  pl.BlockSpec((D, tn), lambda j: (0, j)),
                  pl.BlockSpec((1, tn), lambda j: (0, j))],
        out_specs=pl.BlockSpec((B, tn), lambda j: (0, j)),
        out_shape=jax.ShapeDtypeStruct((B, N), F32),
        compiler_params=_cparams(1),
        name="ada",
    )(c.T, w_ada, b_ada.reshape(1, N))


def _modnorm(x, gain, sc, sh):
    ms = jnp.mean(x * x, axis=-1, keepdims=True)
    return (x * lax.rsqrt(ms + EPS) * gain) * (1.0 + sc) + sh


def _norm1_kernel(x_ref, g_ref, sc_ref, sh_ref, o_ref):
    o_ref[0] = _modnorm(x_ref[0], g_ref[...], sc_ref[0], sh_ref[0]).astype(o_ref.dtype)


def _norm1(x, gain, mod3, tm=512):
    B, T, D = x.shape
    return pl.pallas_call(
        _norm1_kernel,
        grid=(B, T // tm),
        in_specs=[pl.BlockSpec((1, tm, D), lambda b, i: (b, i, 0)),
                  pl.BlockSpec((1, D), lambda b, i: (0, 0)),
                  pl.BlockSpec((1, 1, D), lambda b, i: (b * 6 + 1, 0, 0)),
                  pl.BlockSpec((1, 1, D), lambda b, i: (b * 6 + 0, 0, 0))],
        out_specs=pl.BlockSpec((1, tm, D), lambda b, i: (b, i, 0)),
        out_shape=jax.ShapeDtypeStruct((B, T, D), BF16),
        compiler_params=_cparams(2),
        name="norm1",
    )(x, gain.reshape(1, D), mod3, mod3)


def _row_chunks(tm):
    return [slice(r, r + SUB_ROWS) for r in range(0, tm, SUB_ROWS)]


def _inproj_kernel(a_ref, w_ref, g_ref, o_ref, *, mode):
    for rows in _row_chunks(a_ref.shape[0]):
        acc = jnp.dot(a_ref[rows, :], w_ref[...], preferred_element_type=F32)
        if mode == "norm":
            for k in range(acc.shape[1] // LANE):
                y = acc[:, k * LANE:(k + 1) * LANE]
                ms = jnp.mean(y * y, axis=-1, keepdims=True)
                o_ref[rows, k * LANE:(k + 1) * LANE] = (
                    y * lax.rsqrt(ms + EPS) * g_ref[:, k * LANE:(k + 1) * LANE]).astype(o_ref.dtype)
        elif mode == "sigmoid":
            o_ref[rows, :] = jax.nn.sigmoid(acc).astype(o_ref.dtype)
        else:
            o_ref[rows, :] = acc.astype(o_ref.dtype)


def _inproj(h2d, w, gains, *, col0, ncols, mode, out_dtype, tm, tn, name):
    M, K = h2d.shape
    j0 = col0 // tn
    return pl.pallas_call(
        functools.partial(_inproj_kernel, mode=mode),
        grid=(ncols // tn, M // tm),
        in_specs=[pl.BlockSpec((tm, K), lambda j, i: (i, 0)),
                  pl.BlockSpec((K, tn), lambda j, i: (0, j0 + j)),
                  pl.BlockSpec((1, tn), (lambda j, i: (0, j)) if mode == "norm" else (lambda j, i: (0, 0)))],
        out_specs=pl.BlockSpec((tm, tn), lambda j, i: (i, j)),
        out_shape=jax.ShapeDtypeStruct((M, ncols), out_dtype),
        compiler_params=_cparams(2),
        name=name,
    )(h2d, w, gains)


def _bias_kernel(idx_ref, tab_ref, o_ref, *, head0, hpg, rel, mult):
    head = head0 + pl.program_id(0) * hpg + pl.program_id(2)
    idx = idx_ref[0]
    acc = jnp.zeros(idx.shape, F32)
    for b in range(NUM_BUCKETS):
        acc = jnp.where(idx == b, tab_ref[b, head], acc)
    if rel:
        acc = acc - tab_ref[NUM_BUCKETS - 1, head]
    o_ref[0, 0, 0] = jnp.where(idx < 0, NEG, acc * mult)


def _bias_tiles(idx, rel_bias, *, head0, groups, hpg, name, rel=False, mult=1.0):
    N, R, C = idx.shape
    return pl.pallas_call(
        functools.partial(_bias_kernel, head0=head0, hpg=hpg, rel=rel, mult=mult),
        grid=(groups, N, hpg),
        in_specs=[pl.BlockSpec((1, R, C), lambda g, n, h: (n, 0, 0)),
                  pl.BlockSpec(memory_space=pltpu.SMEM)],
        out_specs=pl.BlockSpec((1, 1, 1, R, C), lambda g, n, h: (g, n, h, 0, 0)),
        out_shape=jax.ShapeDtypeStruct((groups, N, hpg, R, C), F32),
        compiler_params=_cparams(3),
        name=name,
    )(jnp.asarray(idx), rel_bias)


def _causal_idx(R):
    r = np.arange(R)[:, None]
    c = np.arange(R)[None, :]
    return np.stack([np.where(r >= c, _t5_bucket_np(r - c), -1), _t5_bucket_np(R + r - c)]).astype(np.int32)


def _window_idx(R):
    r = np.arange(R)[:, None]
    c = np.arange(R)[None, :]
    edge = np.where(r < c, NUM_BUCKETS - 1, -1)
    return np.concatenate([_causal_idx(R), edge[None]]).astype(np.int32)


def _cmp_idx(T):
    nb = T // QB
    t = (np.arange(nb)[:, None, None] * QB + np.arange(QB)[None, :, None])
    c = np.arange(LANE)[None, None, :]
    end = c * CMP_STRIDE + CMP_LEN - 1
    return np.where((end <= t) & (c < LANE - 1), _t5_bucket_np(t - end), -1).astype(np.int32)


def _compress_kernel(zk_ref, zv_ref, pe_ref, w1_ref, w2_ref, kg_ref, kc_ref, vc_ref):
    half = CMP_LEN // 2

    def one(z_ref, i):
        p1 = jnp.zeros((LANE, HEAD_DIM), F32)
        p2 = jnp.zeros((LANE, HEAD_DIM), F32)
        for l in range(half):
            z = z_ref[0, pl.ds(l, LANE, stride=CMP_STRIDE), :]
            w_lo = w1_ref[i, l * HEAD_DIM:(l + 1) * HEAD_DIM, :].astype(BF16)
            w_hi = w1_ref[i, (half + l) * HEAD_DIM:(half + l + 1) * HEAD_DIM, :].astype(BF16)
            p1 = p1 + jnp.dot((z + pe_ref[i, l:l + 1, :]).astype(BF16), w_lo, preferred_element_type=F32)
            p2 = p2 + jnp.dot((z + pe_ref[i, half + l:half + l + 1, :]).astype(BF16), w_hi,
                              preferred_element_type=F32)
        pre = p1 + pltpu.roll(p2, LANE - 1, axis=0)
        hid = jax.nn.gelu(pre)
        return jnp.dot(hid.astype(BF16), w2_ref[i].astype(BF16), preferred_element_type=F32)

    kc = one(zk_ref, 0)
    ms = jnp.mean(kc * kc, axis=-1, keepdims=True)
    kc_ref[0, 0] = (kc * lax.rsqrt(ms + EPS) * kg_ref[...]).astype(kc_ref.dtype)
    vc_ref[0, 0] = one(zv_ref, 1).astype(vc_ref.dtype)


def _compress(cmp_kv, pe, w1, w2, k_gain0):
    B, T, _ = cmp_kv.shape
    G, dk = NSA_KV_GROUPS, HEAD_DIM
    assert (T - CMP_LEN) // CMP_STRIDE + 1 == LANE - 1
    out = jax.ShapeDtypeStruct((B, G, LANE, dk), BF16)
    return pl.pallas_call(
        _compress_kernel,
        grid=(B, G),
        in_specs=[pl.BlockSpec((1, T, dk), lambda b, g: (b, 0, g)),
                  pl.BlockSpec((1, T, dk), lambda b, g: (b, 0, G + g)),
                  pl.BlockSpec((2, CMP_LEN, dk), lambda b, g: (0, 0, 0)),
                  pl.BlockSpec((2, CMP_LEN * dk, dk), lambda b, g: (0, 0, 0)),
                  pl.BlockSpec((2, dk, dk), lambda b, g: (0, 0, 0)),
                  pl.BlockSpec((1, dk), lambda b, g: (0, 0))],
        out_specs=[pl.BlockSpec((1, 1, LANE, dk), lambda b, g: (b, g, 0, 0))] * 2,
        out_shape=[out, out],
        compiler_params=_cparams(2),
        name="compress",
    )(cmp_kv, cmp_kv, pe, w1, w2, k_gain0.reshape(1, dk))


def _qk(q, k):
    return lax.dot_general(q, k, (((1,), (1,)), ((), ())), preferred_element_type=F32)


def _lane_fold(x, op):
    acc = x[..., :LANE]
    for t in range(1, x.shape[-1] // LANE):
        acc = op(acc, x[..., t * LANE:(t + 1) * LANE])
    return acc


def _softmax_parts(parts):
    m = jnp.max(functools.reduce(jnp.maximum, [_lane_fold(s, jnp.maximum) for s in parts]), axis=-1, keepdims=True)
    ps = [jnp.exp2(s - m) for s in parts]
    den = jnp.sum(functools.reduce(jnp.add, [_lane_fold(p, jnp.add) for p in ps]), axis=-1, keepdims=True)
    return ps, den


def _nsa_kernel(q_ref, ks_ref, vs_ref, kw_ref, vw_ref, kc_ref, vc_ref, gate_ref, cb_ref, nb_ref,
                ovt_ref, exp_ref, o_ref):
    H, R = NSA_HPG, NSA_HPG * QB
    T = ks_ref.shape[1]
    ns = ovt_ref.shape[0]
    nwb = WINDOW // QB
    row = lax.broadcasted_iota(jnp.int32, (QB, LANE), 0)
    col = lax.broadcasted_iota(jnp.int32, (QB, LANE), 1)
    blk = lax.broadcasted_iota(jnp.int32, (ns, QB), 0)
    tl = lax.broadcasted_iota(jnp.int32, (ns, QB), 1)
    kc = kc_ref[0, 0]
    vc = vc_ref[0, 0]

    def attend(qs, k_ref, v_ref, spans):
        parts = []
        for a, b, add in spans:
            s = _qk(qs, k_ref[0, a:b, :]).reshape(H, QB, b - a)
            parts.append(s if add is None else s + add)
        ps, den = _softmax_parts(parts)
        o = None
        for (a, b, _), p in zip(spans, ps):
            pv = jnp.dot(p.reshape(R, b - a).astype(BF16), v_ref[0, a:b, :], preferred_element_type=F32)
            o = pv if o is None else o + pv
        return o / den.reshape(R, 1)

    def block(i):
        lo, hi = i * QB, (i + 1) * QB
        q = q_ref[0]
        qs = jnp.concatenate([q[:, h * HEAD_DIM:(h + 1) * HEAD_DIM] for h in range(H)], axis=0)

        s = _qk(qs, kc).reshape(H, QB, LANE) + cb_ref[0, 0]
        e = jnp.exp2(s - jnp.max(s, axis=-1, keepdims=True))
        if i == 0:
            valid = (col * CMP_STRIDE + (CMP_LEN - 1)) <= row
            e = jnp.where(valid[None], e, 0.0)
            den = jnp.sum(e, axis=-1, keepdims=True)
            p = e / jnp.where(den > 0.0, den, 1.0)
        else:
            p = e / jnp.sum(e, axis=-1, keepdims=True)
        o_cmp = jnp.dot(p.reshape(R, LANE).astype(BF16), vc, preferred_element_type=F32)

        slc_spans = [(lo, hi, nb_ref[0, 0])]
        if i >= 1:
            psum = p[0] + p[1] + p[2] + p[3]
            imp_t = lax.dot_general(ovt_ref[...], psum, (((1,), (1,)), ((), ())),
                                    precision=lax.Precision.HIGHEST, preferred_element_type=F32)
            cur = (lo + tl) // SEL_BLOCK
            forced = (blk == 0) | (blk == cur) | (blk == cur - 1)
            score = jnp.where(forced, 1e4, jnp.where(blk <= cur, imp_t, -1e4))
            rank = jnp.zeros((ns, QB), F32)
            for b in range(ns):
                other = score[b:b + 1, :]
                rank = rank + jnp.where(blk > b, jnp.where(other >= score, 1.0, 0.0),
                                        jnp.where(other > score, 1.0, 0.0))
            unsel_t = jnp.where(rank < float(min(N_SEL, ns)), 0.0, 1.0)
            unsel = jnp.concatenate([unsel_t, jnp.zeros((LANE - ns, QB), F32)], axis=0).T
            drop = jnp.dot(unsel.astype(BF16), exp_ref[:, 0:lo], preferred_element_type=F32)
            slc_spans.insert(0, (lo - QB, lo, nb_ref[0, 1] + drop[None, :, lo - QB:lo]))
            if i >= 2:
                slc_spans.insert(0, (0, lo - QB, drop[None, :, 0:lo - QB]))

        o_slc = attend(qs, ks_ref, vs_ref, slc_spans)

        win_spans = []
        if i >= nwb:
            win_spans.append(((i - nwb) * QB, (i - nwb + 1) * QB, nb_ref[0, 2]))
        mid_a, mid_b = max(i - nwb + 1, 0) * QB, (i - 1) * QB
        if mid_b > mid_a:
            win_spans.append((mid_a, mid_b, None))
        if i >= 1:
            win_spans.append((lo - QB, lo, nb_ref[0, 1]))
        win_spans.append((lo, hi, nb_ref[0, 0]))
        o_win = attend(qs, kw_ref, vw_ref, win_spans)

        gate = gate_ref[0]
        outs = []
        for h in range(H):
            sl = slice(h * QB, (h + 1) * QB)
            outs.append(gate[:, 3 * h:3 * h + 1] * o_cmp[sl] + gate[:, 3 * h + 1:3 * h + 2] * o_slc[sl]
                        + gate[:, 3 * h + 2:3 * h + 3] * o_win[sl])
        o_ref[0] = jnp.concatenate(outs, axis=1).astype(o_ref.dtype)

    for i in range(T // QB):
        pl.when(pl.program_id(2) == i)(functools.partial(block, i))


def _nsa(normed, rawv, kc, vc, gates, cbias, nbias, T):
    B = normed.shape[0]
    G, H, dk = NSA_KV_GROUPS, NSA_HPG, HEAD_DIM
    nb = T // QB
    ns = T // SEL_BLOCK
    cstart = np.arange(LANE) * CMP_STRIDE
    sstart = np.arange(ns) * SEL_BLOCK
    overlap = np.clip(np.minimum(cstart[:, None] + CMP_LEN, sstart[None, :] + SEL_BLOCK)
                      - np.maximum(cstart[:, None], sstart[None, :]), 0, None) / CMP_STRIDE
    overlap[LANE - 1:] = 0.0
    ovt = jnp.asarray(overlap.T, F32)
    expand = np.zeros((LANE, T), np.float32)
    expand[np.arange(T) // SEL_BLOCK, np.arange(T)] = MASK_BIG
    kq = NSA_HEADS
    return pl.pallas_call(
        _nsa_kernel,
        grid=(B, G, nb),
        in_specs=[pl.BlockSpec((1, QB, H * dk), lambda b, g, i: (b, i, g)),
                  pl.BlockSpec((1, T, dk), lambda b, g, i: (b, 0, kq + g)),
                  pl.BlockSpec((1, T, dk), lambda b, g, i: (b, 0, g)),
                  pl.BlockSpec((1, T, dk), lambda b, g, i: (b, 0, kq + G + g)),
                  pl.BlockSpec((1, T, dk), lambda b, g, i: (b, 0, G + g)),
                  pl.BlockSpec((1, 1, LANE, dk), lambda b, g, i: (b, g, 0, 0)),
                  pl.BlockSpec((1, 1, LANE, dk), lambda b, g, i: (b, g, 0, 0)),
                  pl.BlockSpec((1, QB, LANE), lambda b, g, i: (b, i, g)),
                  pl.BlockSpec((1, 1, H, QB, LANE), lambda b, g, i: (g, i, 0, 0, 0)),
                  pl.BlockSpec((1, 3, H, QB, QB), lambda b, g, i: (g, 0, 0, 0, 0)),
                  pl.BlockSpec((ns, LANE), lambda b, g, i: (0, 0)),
                  pl.BlockSpec((LANE, T), lambda b, g, i: (0, 0))],
        out_specs=pl.BlockSpec((1, QB, H * dk), lambda b, g, i: (b, i, g)),
        out_shape=jax.ShapeDtypeStruct((B, T, NSA_HEADS * dk), BF16),
        compiler_params=_cparams(3),
        name="nsa",
    )(normed, normed, rawv, normed, rawv, kc, vc, gates, cbias, nbias, ovt, jnp.asarray(expand, BF16))


def _diff_kernel(q_ref, k_ref, v_ref, lq_ref, lk_ref, sg_ref, db_ref, o_ref):
    dk = HEAD_DIM
    T = k_ref.shape[1]
    lqk = lq_ref[...] * lk_ref[...]
    lam = (jnp.exp(jnp.sum(lqk[0:1], axis=-1, keepdims=True))
           - jnp.exp(jnp.sum(lqk[1:2], axis=-1, keepdims=True)) + LAM_INIT)
    for i in range(T // DQB):
        lo, hi = i * DQB, (i + 1) * DQB
        halves = []
        for mm in range(2):
            cols = slice(mm * dk, (mm + 1) * dk)
            q = q_ref[0, lo:hi, cols]
            bounds, parts = [], []
            if i >= 2:
                bounds.append((0, lo - DQB))
                parts.append(_qk(q, k_ref[0, 0:lo - DQB, cols]))
            if i >= 1:
                bounds.append((lo - DQB, lo))
                parts.append(_qk(q, k_ref[0, lo - DQB:lo, cols]) + db_ref[0, 1, 0])
            bounds.append((lo, hi))
            parts.append(_qk(q, k_ref[0, lo:hi, cols]) + db_ref[0, 0, 0])
            ps, den = _softmax_parts(parts)
            o = None
            for (a, b), p in zip(bounds, ps):
                pv = jnp.dot(p.astype(BF16), v_ref[0, a:b, :], preferred_element_type=F32)
                o = pv if o is None else o + pv
            halves.append(o / den)
        o = halves[0] - lam * halves[1]
        ms = jnp.mean(o * o, axis=-1, keepdims=True)
        o_ref[0, lo:hi, :] = ((o * lax.rsqrt(ms + EPS) * sg_ref[...]) * (1.0 - LAM_INIT)).astype(o_ref.dtype)


def _diff(normed, rawv, lam_q, lam_k, subln_gain, dbias, T):
    B = normed.shape[0]
    Hd, dk = DIFF_HEADS, HEAD_DIM
    w = 2 * dk
    q0 = (NSA_HEADS + 2 * NSA_KV_GROUPS) * dk // w
    k0 = q0 + Hd
    v0 = 2 * NSA_KV_GROUPS * dk // w
    return pl.pallas_call(
        _diff_kernel,
        grid=(B, Hd),
        in_specs=[pl.BlockSpec((1, T, w), lambda b, h: (b, 0, q0 + h)),
                  pl.BlockSpec((1, T, w), lambda b, h: (b, 0, k0 + h)),
                  pl.BlockSpec((1, T, w), lambda b, h: (b, 0, v0 + h)),
                  pl.BlockSpec((2, dk), lambda b, h: (0, 0)),
                  pl.BlockSpec((2, dk), lambda b, h: (0, 0)),
                  pl.BlockSpec((1, w), lambda b, h: (0, 0)),
                  pl.BlockSpec((1, 2, 1, DQB, DQB), lambda b, h: (h, 0, 0, 0, 0))],
        out_specs=pl.BlockSpec((1, T, w), lambda b, h: (b, 0, h)),
        out_shape=jax.ShapeDtypeStruct((B, T, Hd * w), BF16),
        compiler_params=_cparams(2),
        name="diff",
    )(normed, normed, rawv, lam_q, lam_k, subln_gain.reshape(1, w), dbias)


def _merge_kernel(an_ref, ad_ref, wn_ref, wd_ref, gn_ref, gd_ref, o_ref):
    for rows in _row_chunks(an_ref.shape[0]):
        yn = jnp.dot(an_ref[rows, :], wn_ref[...], preferred_element_type=F32)
        yd = jnp.dot(ad_ref[rows, :], wd_ref[...], preferred_element_type=F32)
        o_ref[rows, :] = (gn_ref[rows, :].astype(F32) * yn + gd_ref[rows, :].astype(F32) * yd).astype(o_ref.dtype)


def _merge(o_nsa, o_diff, w_n, w_d, mg, tm=1024, tn=512):
    M, K = o_nsa.shape
    N = w_n.shape[1]
    nj = N // tn
    return pl.pallas_call(
        _merge_kernel,
        grid=(nj, M // tm),
        in_specs=[pl.BlockSpec((tm, K), lambda j, i: (i, 0)),
                  pl.BlockSpec((tm, K), lambda j, i: (i, 0)),
                  pl.BlockSpec((K, tn), lambda j, i: (0, j)),
                  pl.BlockSpec((K, tn), lambda j, i: (0, j)),
                  pl.BlockSpec((tm, tn), lambda j, i: (i, j)),
                  pl.BlockSpec((tm, tn), lambda j, i: (i, nj + j))],
        out_specs=pl.BlockSpec((tm, tn), lambda j, i: (i, j)),
        out_shape=jax.ShapeDtypeStruct((M, N), BF16),
        compiler_params=_cparams(2),
        name="merge",
    )(o_nsa, o_diff, w_n, w_d, mg, mg)


def _oproj_kernel(a_ref, w_ref, x_ref, g1_ref, gain_ref, sc_ref, sh_ref, x1_ref, h2_ref):
    for rows in _row_chunks(a_ref.shape[0]):
        y = jnp.dot(a_ref[rows, :], w_ref[...], preferred_element_type=F32)
        x1 = x_ref[rows, :] + g1_ref[0] * y
        x1_ref[rows, :] = x1
        h2_ref[rows, :] = _modnorm(x1, gain_ref[...], sc_ref[0], sh_ref[0]).astype(h2_ref.dtype)


def _oproj(merged, w_o, x2d, mod3, gain2, T, tm=512):
    M, D = x2d.shape
    per = T // tm
    return pl.pallas_call(
        _oproj_kernel,
        grid=(M // tm,),
        in_specs=[pl.BlockSpec((tm, D), lambda i: (i, 0)),
                  pl.BlockSpec((D, D), lambda i: (0, 0)),
                  pl.BlockSpec((tm, D), lambda i: (i, 0)),
                  pl.BlockSpec((1, 1, D), lambda i: ((i // per) * 6 + 2, 0, 0)),
                  pl.BlockSpec((1, D), lambda i: (0, 0)),
                  pl.BlockSpec((1, 1, D), lambda i: ((i // per) * 6 + 4, 0, 0)),
                  pl.BlockSpec((1, 1, D), lambda i: ((i // per) * 6 + 3, 0, 0))],
        out_specs=[pl.BlockSpec((tm, D), lambda i: (i, 0)),
                   pl.BlockSpec((tm, D), lambda i: (i, 0))],
        out_shape=[jax.ShapeDtypeStruct((M, D), F32), jax.ShapeDtypeStruct((M, D), BF16)],
        compiler_params=_cparams(1),
        name="oproj",
    )(merged, w_o, x2d, mod3, gain2.reshape(1, D), mod3, mod3)


def _ffn_up_kernel(h_ref, wa_ref, wv_ref, cwa_ref, cwv_ref, cba_ref, cbv_ref, o_ref, wab_ref, wvb_ref,
                   ca_ref, cv_ref, *, per):
    i = pl.program_id(1)
    row = lax.broadcasted_iota(jnp.int32, (SUB_ROWS, 1), 0)

    @pl.when(i == 0)
    def _():
        wab_ref[...] = wa_ref[...].astype(BF16)
        wvb_ref[...] = wv_ref[...].astype(BF16)

    @pl.when(i % per == 0)
    def _():
        ca_ref[...] = jnp.zeros(ca_ref.shape, F32)
        cv_ref[...] = jnp.zeros(cv_ref.shape, F32)

    def conv(u, prev, cw_ref, cb_ref):
        u1 = jnp.where(row < 1, pltpu.roll(prev, 1, axis=0)[0:1, :], pltpu.roll(u, 1, axis=0))
        p2 = pltpu.roll(prev, 2, axis=0)
        u2 = pltpu.roll(u, 2, axis=0)
        u2 = jnp.where(row < 1, p2[0:1, :], jnp.where(row < 2, p2[1:2, :], u2))
        return cb_ref[...] + cw_ref[0:1, :] * u2 + cw_ref[1:2, :] * u1 + cw_ref[2:3, :] * u

    prev_a, prev_v = ca_ref[...], cv_ref[...]
    for rows in _row_chunks(h_ref.shape[0]):
        hs = h_ref[rows, :]
        ua = jnp.dot(hs, wab_ref[...], preferred_element_type=F32)
        uv = jnp.dot(hs, wvb_ref[...], preferred_element_type=F32)
        a = conv(ua, prev_a, cwa_ref, cba_ref)
        val = conv(uv, prev_v, cwv_ref, cbv_ref)
        o_ref[rows, :] = (a * jax.nn.sigmoid(a) * val).astype(o_ref.dtype)
        prev_a, prev_v = ua[SUB_ROWS - 8:, :], uv[SUB_ROWS - 8:, :]
    ca_ref[...] = prev_a
    cv_ref[...] = prev_v


def _ffn_up(h2, w_up, conv_w, conv_b, T, tm=1024, tn=512):
    M, D = h2.shape
    F = w_up.shape[1] // 2
    nj = F // tn
    cb = conv_b.reshape(1, 2 * F)
    return pl.pallas_call(
        functools.partial(_ffn_up_kernel, per=T // tm),
        grid=(nj, M // tm),
        in_specs=[pl.BlockSpec((tm, D), lambda j, i: (i, 0)),
                  pl.BlockSpec((D, tn), lambda j, i: (0, j)),
                  pl.BlockSpec((D, tn), lambda j, i: (0, nj + j)),
                  pl.BlockSpec((3, tn), lambda j, i: (0, j)),
                  pl.BlockSpec((3, tn), lambda j, i: (0, nj + j)),
                  pl.BlockSpec((1, tn), lambda j, i: (0, j)),
                  pl.BlockSpec((1, tn), lambda j, i: (0, nj + j))],
        out_specs=pl.BlockSpec((tm, tn), lambda j, i: (i, j)),
        out_shape=jax.ShapeDtypeStruct((M, F), BF16),
        scratch_shapes=[pltpu.VMEM((D, tn), BF16), pltpu.VMEM((D, tn), BF16),
                        pltpu.VMEM((8, tn), F32), pltpu.VMEM((8, tn), F32)],
        compiler_params=_cparams(2),
        name="ffn_up",
    )(h2, w_up, w_up, conv_w, conv_w, cb, cb)


def _ffn_down_kernel(a_ref, w_ref, x_ref, g2_ref, o_ref):
    for rows in _row_chunks(a_ref.shape[0]):
        y = jnp.dot(a_ref[rows, :], w_ref[...], preferred_element_type=F32)
        o_ref[rows, :] = x_ref[rows, :] + g2_ref[0] * y


def _ffn_down(act, w_down, x1, mod3, T, tm=512, tn=512):
    M, F = act.shape
    D = w_down.shape[1]
    per = T // tm
    return pl.pallas_call(
        _ffn_down_kernel,
        grid=(D // tn, M // tm),
        in_specs=[pl.BlockSpec((tm, F), lambda j, i: (i, 0)),
                  pl.BlockSpec((F, tn), lambda j, i: (0, j)),
                  pl.BlockSpec((tm, tn), lambda j, i: (i, j)),
                  pl.BlockSpec((1, 1, tn), lambda j, i: ((i // per) * 6 + 5, 0, j))],
        out_specs=pl.BlockSpec((tm, tn), lambda j, i: (i, j)),
        out_shape=jax.ShapeDtypeStruct((M, D), F32),
        compiler_params=_cparams(2),
        name="ffn_down",
    )(act, w_down, x1, mod3)


def _layer(x, c, w_ada, b_ada, norm1_gain, norm2_gain, w_in, nsa_q_gain, nsa_k_gain, cmp_pe, cmp_w1, cmp_w2,
           diff_q_gain, diff_k_gain, diff_lambda_q, diff_lambda_k, diff_subln_gain, w_nsa_out, w_diff_out, w_o,
           w_ffn_up, ffn_conv_w, ffn_conv_b, w_ffn_down, rel_bias):
    B, T, D = x.shape
    dk, G = HEAD_DIM, NSA_KV_GROUPS
    M = B * T
    scale = dk ** -0.5

    n_q = NSA_HEADS * dk
    o_kv = n_q
    o_g = o_kv + 3 * 2 * G * dk
    o_dq = o_g + NSA_HEADS * 3
    o_dk = o_dq + DIFF_HEADS * 2 * dk
    o_dv = o_dk + DIFF_HEADS * 2 * dk
    o_mg = o_dv + DIFF_HEADS * 2 * dk
    kvc = lambda br, kv: o_kv + (br * 2 + kv) * G * dk

    gate_cols = []
    for g in range(G):
        gate_cols += [w_in[:, o_g + g * NSA_HPG * 3:o_g + (g + 1) * NSA_HPG * 3],
                      jnp.zeros((D, LANE - NSA_HPG * 3), w_in.dtype)]
    w_rep = jnp.concatenate(
        [w_in[:, :n_q], w_in[:, kvc(1, 0):kvc(1, 1)], w_in[:, kvc(2, 0):kvc(2, 1)], w_in[:, o_dq:o_dv],
         w_in[:, kvc(1, 1):kvc(2, 0)], w_in[:, kvc(2, 1):o_g], w_in[:, o_dv:o_mg],
         w_in[:, o_mg:],
         w_in[:, kvc(0, 0):kvc(1, 0)]]
        + gate_cols, axis=1).astype(BF16)
    n_norm = n_q + 2 * G * dk + 2 * DIFF_HEADS * 2 * dk
    n_raw = 2 * G * dk + DIFF_HEADS * 2 * dk
    n_sig = 2 * D
    n_cmp = 2 * G * dk
    n_gate = G * LANE
    gains = jnp.concatenate([jnp.tile(nsa_q_gain * (scale * LOG2E), NSA_HEADS), jnp.tile(nsa_k_gain[1], G),
                             jnp.tile(nsa_k_gain[2], G), jnp.tile(diff_q_gain * (scale * LOG2E), 2 * DIFF_HEADS),
                             jnp.tile(diff_k_gain, 2 * DIFF_HEADS)]).reshape(1, n_norm)

    mod3 = _ada(c, w_ada, b_ada).reshape(B * 6, 1, D)
    h = _norm1(x, norm1_gain, mod3).reshape(M, D)

    proj = functools.partial(_inproj, h, w_rep, tm=1024)
    normed = proj(gains, col0=0, ncols=n_norm, mode="norm", out_dtype=BF16, tn=512, name="inproj_norm")
    rawv = proj(gains, col0=n_norm, ncols=n_raw, mode="raw", out_dtype=BF16, tn=512, name="inproj_raw")
    mgate = proj(gains, col0=n_norm + n_raw, ncols=n_sig, mode="sigmoid", out_dtype=BF16, tn=512, name="inproj_mg")
    cmpkv = proj(gains, col0=n_norm + n_raw + n_sig, ncols=n_cmp, mode="raw", out_dtype=F32, tn=512,
                 name="inproj_cmp")
    gates = proj(gains, col0=n_norm + n_raw + n_sig + n_cmp, ncols=n_gate, mode="sigmoid", out_dtype=F32, tn=256,
                 name="inproj_gate")
    normed = normed.reshape(B, T, n_norm)
    rawv = rawv.reshape(B, T, n_raw)

    nbias = _bias_tiles(_window_idx(QB), rel_bias, head0=0, groups=G, hpg=NSA_HPG, name="bias_nsa",
                        rel=True, mult=LOG2E)
    cbias = _bias_tiles(_cmp_idx(T), rel_bias, head0=0, groups=G, hpg=NSA_HPG, name="bias_cmp", mult=LOG2E)
    dbias = _bias_tiles(_causal_idx(DQB), rel_bias, head0=NSA_HEADS, groups=DIFF_HEADS, hpg=1, name="bias_diff",
                        rel=True, mult=LOG2E)

    kc, vc = _compress(cmpkv.reshape(B, T, n_cmp), cmp_pe, cmp_w1, cmp_w2, nsa_k_gain[0])
    o_nsa = _nsa(normed, rawv, kc, vc, gates.reshape(B, T, n_gate), cbias, nbias, T)
    o_diff = _diff(normed, rawv, diff_lambda_q, diff_lambda_k, diff_subln_gain, dbias, T)

    merged = _merge(o_nsa.reshape(M, -1), o_diff.reshape(M, -1), w_nsa_out.astype(BF16), w_diff_out.astype(BF16),
                    mgate)
    x1, h2 = _oproj(merged, w_o.astype(BF16), x.reshape(M, D), mod3, norm2_gain, T)
    act = _ffn_up(h2, w_ffn_up, ffn_conv_w, ffn_conv_b, T)
    out = _ffn_down(act, w_ffn_down.astype(BF16), x1, mod3, T)
    return out.reshape(B, T, D)


def kernel(x, c, w_ada, b_ada, norm1_gain, norm2_gain, w_in, nsa_q_gain, nsa_k_gain, cmp_pe, cmp_w1, cmp_w2,
           diff_q_gain, diff_k_gain, diff_lambda_q, diff_lambda_k, diff_subln_gain, w_nsa_out, w_diff_out, w_o,
           w_ffn_up, ffn_conv_w, ffn_conv_b, w_ffn_down, rel_bias):
    return _layer(x, c, w_ada[0], b_ada[0], norm1_gain[0], norm2_gain[0], w_in[0], nsa_q_gain[0], nsa_k_gain[0],
                  cmp_pe[0], cmp_w1[0], cmp_w2[0], diff_q_gain[0], diff_k_gain[0], diff_lambda_q[0],
                  diff_lambda_k[0], diff_subln_gain[0], w_nsa_out[0], w_diff_out[0], w_o[0], w_ffn_up[0],
                  ffn_conv_w[0], ffn_conv_b[0], w_ffn_down[0], rel_bias)
```

```python
import functools
import math

import numpy as np
import jax
import jax.numpy as jnp
from jax import lax
from jax.experimental import pallas as pl
from jax.experimental.pallas import tpu as pltpu

F32 = jnp.float32
BF16 = jnp.bfloat16

HEAD_DIM = 128
NSA_HEADS = 8
NSA_KV_GROUPS = 2
NSA_HPG = NSA_HEADS // NSA_KV_GROUPS
CMP_LEN = 32
CMP_STRIDE = 16
SEL_BLOCK = 64
N_SEL = 16
WINDOW = 512
DIFF_HEADS = 4
NUM_BUCKETS = 32
MAX_DISTANCE = 128
EPS = 1e-6
NEG = -1e30
LAM_INIT = 0.8 - 0.6 * math.exp(-0.3 * 0)
LOG2E = math.log2(math.e)
MASK_BIG = -(2.0 ** 100)

LANE = 128
QB = 128
DQB = 256
SUB_ROWS = 256
VMEM_LIMIT = 56 * 1024 * 1024


def _cparams(n_axes):
    return pltpu.CompilerParams(dimension_semantics=("arbitrary",) * n_axes,
                                vmem_limit_bytes=VMEM_LIMIT)


def _t5_bucket_np(dist):
    n = np.maximum(np.asarray(dist, np.int32), 0)
    max_exact = NUM_BUCKETS // 2
    nf = np.maximum(n, max_exact).astype(np.float32)
    large = max_exact + (np.log(nf / np.float32(max_exact)) / np.float32(math.log(MAX_DISTANCE / max_exact))
                         * np.float32(NUM_BUCKETS - max_exact)).astype(np.int32)
    large = np.minimum(large, NUM_BUCKETS - 1)
    return np.where(n < max_exact, n, large).astype(np.int32)


def _ada_kernel(ct_ref, w_ref, b_ref, o_ref):
    ct = ct_ref[...]
    s = ct * jax.nn.sigmoid(ct)
    w = w_ref[...]
    for b in range(ct.shape[1]):
        o_ref[b:b + 1, :] = jnp.sum(w * s[:, b:b + 1], axis=0, keepdims=True) + b_ref[...]


def _ada(c, w_ada, b_ada, tn=512):
    B, D = c.shape
    N = w_ada.shape[1]
    return pl.pallas_call(
        _ada_kernel,
        grid=(N // tn,),
        in_specs=[pl.BlockSpec((D, B), lambda j: (0, 0)),
                  pl.BlockSpec((D, tn), lambda j: (0, j)),
                  pl.BlockSpec((1, tn), lambda j: (0, j))],
        out_specs=pl.BlockSpec((B, tn), lambda j: (0, j)),
        out_shape=jax.ShapeDtypeStruct((B, N), F32),
        compiler_params=_cparams(1),
        name="ada",
    )(c.T, w_ada, b_ada.reshape(1, N))


def _modnorm(x, gain, sc, sh):
    ms = jnp.mean(x * x, axis=-1, keepdims=True)
    return (x * lax.rsqrt(ms + EPS) * gain) * (1.0 + sc) + sh


def _norm1_kernel(x_ref, g_ref, sc_ref, sh_ref, o_ref):
    o_ref[0] = _modnorm(x_ref[0], g_ref[...], sc_ref[0], sh_ref[0]).astype(o_ref.dtype)


def _norm1(x, gain, mod3, tm=512):
    B, T, D = x.shape
    return pl.pallas_call(
        _norm1_kernel,
        grid=(B, T // tm),
        in_specs=[pl.BlockSpec((1, tm, D), lambda b, i: (b, i, 0)),
                  pl.BlockSpec((1, D), lambda b, i: (0, 0)),
                  pl.BlockSpec((1, 1, D), lambda b, i: (b * 6 + 1, 0, 0)),
                  pl.BlockSpec((1, 1, D), lambda b, i: (b * 6 + 0, 0, 0))],
        out_specs=pl.BlockSpec((1, tm, D), lambda b, i: (b, i, 0)),
        out_shape=jax.ShapeDtypeStruct((B, T, D), BF16),
        compiler_params=_cparams(2),
        name="norm1",
    )(x, gain.reshape(1, D), mod3, mod3)


def _row_chunks(tm):
    return [slice(r, r + SUB_ROWS) for r in range(0, tm, SUB_ROWS)]


def _inproj_kernel(a_ref, w_ref, *rest, shift, n_norm, mode):
    if shift:
        wx_ref, g_ref, o_ref, wb_ref = rest
    else:
        g_ref, o_ref, wb_ref = rest

    @pl.when(pl.program_id(1) == 0)
    def _():
        w = w_ref[...]
        if shift:
            w = jnp.concatenate([w[:, shift:], wx_ref[:, :shift]], axis=1)
        wb_ref[...] = w.astype(BF16)

    for rows in _row_chunks(a_ref.shape[0]):
        acc = jnp.dot(a_ref[rows, :], wb_ref[...], preferred_element_type=F32)
        for k in range(n_norm // LANE):
            y = acc[:, k * LANE:(k + 1) * LANE]
            ms = jnp.mean(y * y, axis=-1, keepdims=True)
            o_ref[rows, k * LANE:(k + 1) * LANE] = (
                y * lax.rsqrt(ms + EPS) * g_ref[:, k * LANE:(k + 1) * LANE]).astype(o_ref.dtype)
        if n_norm < acc.shape[1]:
            rest_acc = acc[:, n_norm:]
            if mode == "sigmoid":
                rest_acc = jax.nn.sigmoid(rest_acc)
            o_ref[rows, n_norm:] = rest_acc.astype(o_ref.dtype)


def _inproj(h2d, w_in, gains, *, src0, ncols, n_norm, mode, out_dtype, tm, tn, name):
    M, K = h2d.shape
    shift = src0 % LANE
    j0, rem = divmod(src0 - shift, tn)
    assert rem == 0 and ncols % tn == 0
    in_specs = [pl.BlockSpec((tm, K), lambda j, i: (i, 0)),
                pl.BlockSpec((K, tn), lambda j, i: (0, j0 + j))]
    args = [h2d, w_in]
    if shift:
        in_specs.append(pl.BlockSpec((K, LANE), lambda j, i: (0, (j0 + j + 1) * (tn // LANE))))
        args.append(w_in)
    in_specs.append(pl.BlockSpec((1, tn), lambda j, i: (0, j)))
    args.append(gains)
    return pl.pallas_call(
        functools.partial(_inproj_kernel, shift=shift, n_norm=n_norm, mode=mode),
        grid=(ncols // tn, M // tm),
        in_specs=in_specs,
        out_specs=pl.BlockSpec((tm, tn), lambda j, i: (i, j)),
        out_shape=jax.ShapeDtypeStruct((M, ncols), out_dtype),
        scratch_shapes=[pltpu.VMEM((K, tn), BF16)],
        compiler_params=_cparams(2),
        name=name,
    )(*args)


def _bias_kernel(idx_ref, tab_ref, o_ref, *, head0, hpg, rel, mult):
    head = head0 + pl.program_id(0) * hpg + pl.program_id(2)
    idx = idx_ref[0]
    acc = jnp.zeros(idx.shape, F32)
    for b in range(NUM_BUCKETS):
        acc = jnp.where(idx == b, tab_ref[b, head], acc)
    if rel:
        acc = acc - tab_ref[NUM_BUCKETS - 1, head]
    o_ref[0, 0, 0] = jnp.where(idx < 0, NEG, acc * mult)


def _bias_tiles(idx, rel_bias, *, head0, groups, hpg, name, rel=False, mult=1.0):
    N, R, C = idx.shape
    return pl.pallas_call(
        functools.partial(_bias_kernel, head0=head0, hpg=hpg, rel=rel, mult=mult),
        grid=(groups, N, hpg),
        in_specs=[pl.BlockSpec((1, R, C), lambda g, n, h: (n, 0, 0)),
                  pl.BlockSpec(memory_space=pltpu.SMEM)],
        out_specs=pl.BlockSpec((1, 1, 1, R, C), lambda g, n, h: (g, n, h, 0, 0)),
        out_shape=jax.ShapeDtypeStruct((groups, N, hpg, R, C), F32),
        compiler_params=_cparams(3),
        name=name,
    )(jnp.asarray(idx), rel_bias)


def _causal_idx(R):
    r = np.arange(R)[:, None]
    c = np.arange(R)[None, :]
    return np.stack([np.where(r >= c, _t5_bucket_np(r - c), -1), _t5_bucket_np(R + r - c)]).astype(np.int32)


def _window_idx(R):
    r = np.arange(R)[:, None]
    c = np.arange(R)[None, :]
    edge = np.where(r < c, NUM_BUCKETS - 1, -1)
    return np.concatenate([_causal_idx(R), edge[None]]).astype(np.int32)


def _cmp_idx(T):
    nb = T // QB
    t = (np.arange(nb)[:, None, None] * QB + np.arange(QB)[None, :, None])
    c = np.arange(LANE)[None, None, :]
    end = c * CMP_STRIDE + CMP_LEN - 1
    return np.where((end <= t) & (c < LANE - 1), _t5_bucket_np(t - end), -1).astype(np.int32)


def _compress_kernel(zk_ref, zv_ref, pe_ref, w1_ref, w2_ref, kg_ref, kc_ref, vc_ref):
    half = CMP_LEN // 2

    def one(z_ref, i):
        p1 = jnp.zeros((LANE, HEAD_DIM), F32)
        p2 = jnp.zeros((LANE, HEAD_DIM), F32)
        for l in range(half):
            z = z_ref[0, pl.ds(l, LANE, stride=CMP_STRIDE), :]
            w_lo = w1_ref[i, l * HEAD_DIM:(l + 1) * HEAD_DIM, :].astype(BF16)
            w_hi = w1_ref[i, (half + l) * HEAD_DIM:(half + l + 1) * HEAD_DIM, :].astype(BF16)
            p1 = p1 + jnp.dot((z + pe_ref[i, l:l + 1, :]).astype(BF16), w_lo, preferred_element_type=F32)
            p2 = p2 + jnp.dot((z + pe_ref[i, half + l:half + l + 1, :]).astype(BF16), w_hi,
                              preferred_element_type=F32)
        pre = p1 + pltpu.roll(p2, LANE - 1, axis=0)
        hid = jax.nn.gelu(pre)
        return jnp.dot(hid.astype(BF16), w2_ref[i].astype(BF16), preferred_element_type=F32)

    kc = one(zk_ref, 0)
    ms = jnp.mean(kc * kc, axis=-1, keepdims=True)
    kc_ref[0, 0] = (kc * lax.rsqrt(ms + EPS) * kg_ref[...]).astype(kc_ref.dtype)
    vc_ref[0, 0] = one(zv_ref, 1).astype(vc_ref.dtype)


def _compress(cmp_kv, pe, w1, w2, k_gain0):
    B, T, _ = cmp_kv.shape
    G, dk = NSA_KV_GROUPS, HEAD_DIM
    assert (T - CMP_LEN) // CMP_STRIDE + 1 == LANE - 1
    out = jax.ShapeDtypeStruct((B, G, LANE, dk), BF16)
    return pl.pallas_call(
        _compress_kernel,
        grid=(B, G),
        in_specs=[pl.BlockSpec((1, T, dk), lambda b, g: (b, 0, g)),
                  pl.BlockSpec((1, T, dk), lambda b, g: (b, 0, G + g)),
                  pl.BlockSpec((2, CMP_LEN, dk), lambda b, g: (0, 0, 0)),
                  pl.BlockSpec((2, CMP_LEN * dk, dk), lambda b, g: (0, 0, 0)),
                  pl.BlockSpec((2, dk, dk), lambda b, g: (0, 0, 0)),
                  pl.BlockSpec((1, dk), lambda b, g: (0, 0))],
        out_specs=[pl.BlockSpec((1, 1, LANE, dk), lambda b, g: (b, g, 0, 0))] * 2,
        out_shape=[out, out],
        compiler_params=_cparams(2),
        name="compress",
    )(cmp_kv, cmp_kv, pe, w1, w2, k_gain0.reshape(1, dk))


def _qk(q, k):
    return lax.dot_general(q, k, (((1,), (1,)), ((), ())), preferred_element_type=F32)


def _lane_fold(x, op):
    acc = x[..., :LANE]
    for t in range(1, x.shape[-1] // LANE):
        acc = op(acc, x[..., t * LANE:(t + 1) * LANE])
    return acc


def _softmax_parts(parts):
    m = jnp.max(functools.reduce(jnp.maximum, [_lane_fold(s, jnp.maximum) for s in parts]), axis=-1, keepdims=True)
    ps = [jnp.exp2(s - m) for s in parts]
    den = jnp.sum(functools.reduce(jnp.add, [_lane_fold(p, jnp.add) for p in ps]), axis=-1, keepdims=True)
    return ps, den


def _nsa_kernel(q_ref, ks_ref, vs_ref, kw_ref, vw_ref, kc_ref, vc_ref, gate_ref, cb_ref, nb_ref,
                ovt_ref, exp_ref, o_ref):
    H, R = NSA_HPG, NSA_HPG * QB
    T = ks_ref.shape[1]
    ns = ovt_ref.shape[0]
    nwb = WINDOW // QB
    row = lax.broadcasted_iota(jnp.int32, (QB, LANE), 0)
    col = lax.broadcasted_iota(jnp.int32, (QB, LANE), 1)
    blk = lax.broadcasted_iota(jnp.int32, (ns, QB), 0)
    tl = lax.broadcasted_iota(jnp.int32, (ns, QB), 1)
    kc = kc_ref[0, 0]
    vc = vc_ref[0, 0]

    def attend(qs, k_ref, v_ref, spans):
        parts = []
        for a, b, add in spans:
            s = _qk(qs, k_ref[0, a:b, :]).reshape(H, QB, b - a)
            parts.append(s if add is None else s + add)
        ps, den = _softmax_parts(parts)
        o = None
        for (a, b, _), p in zip(spans, ps):
            pv = jnp.dot(p.reshape(R, b - a).astype(BF16), v_ref[0, a:b, :], preferred_element_type=F32)
            o = pv if o is None else o + pv
        return o / den.reshape(R, 1)

    def block(i):
        lo, hi = i * QB, (i + 1) * QB
        q = q_ref[0]
        qs = jnp.concatenate([q[:, h * HEAD_DIM:(h + 1) * HEAD_DIM] for h in range(H)], axis=0)

        s = _qk(qs, kc).reshape(H, QB, LANE) + cb_ref[0, 0]
        e = jnp.exp2(s - jnp.max(s, axis=-1, keepdims=True))
        if i == 0:
            valid = (col * CMP_STRIDE + (CMP_LEN - 1)) <= row
            e = jnp.where(valid[None], e, 0.0)
            den = jnp.sum(e, axis=-1, keepdims=True)
            p = e / jnp.where(den > 0.0, den, 1.0)
        else:
            p = e / jnp.sum(e, axis=-1, keepdims=True)
        o_cmp = jnp.dot(p.reshape(R, LANE).astype(BF16), vc, preferred_element_type=F32)

        slc_spans = [(lo, hi, nb_ref[0, 0])]
        if i >= 1:
            psum = p[0] + p[1] + p[2] + p[3]
            imp_t = lax.dot_general(ovt_ref[...], psum, (((1,), (1,)), ((), ())),
                                    precision=lax.Precision.HIGHEST, preferred_element_type=F32)
            cur = (lo + tl) // SEL_BLOCK
            forced = (blk == 0) | (blk == cur) | (blk == cur - 1)
            score = jnp.where(forced, 1e4, jnp.where(blk <= cur, imp_t, -1e4))
            rank = jnp.zeros((ns, QB), F32)
            for b in range(ns):
                other = score[b:b + 1, :]
                rank = rank + jnp.where(blk > b, jnp.where(other >= score, 1.0, 0.0),
                                        jnp.where(other > score, 1.0, 0.0))
            unsel_t = jnp.where(rank < float(min(N_SEL, ns)), 0.0, 1.0)
            unsel = jnp.concatenate([unsel_t, jnp.zeros((LANE - ns, QB), F32)], axis=0).T
            drop = jnp.dot(unsel.astype(BF16), exp_ref[:, 0:lo], preferred_element_type=F32)
            slc_spans.insert(0, (lo - QB, lo, nb_ref[0, 1] + drop[None, :, lo - QB:lo]))
            if i >= 2:
                slc_spans.insert(0, (0, lo - QB, drop[None, :, 0:lo - QB]))

        o_slc = attend(qs, ks_ref, vs_ref, slc_spans)

        win_spans = []
        if i >= nwb:
            win_spans.append(((i - nwb) * QB, (i - nwb + 1) * QB, nb_ref[0, 2]))
        mid_a, mid_b = max(i - nwb + 1, 0) * QB, (i - 1) * QB
        if mid_b > mid_a:
            win_spans.append((mid_a, mid_b, None))
        if i >= 1:
            win_spans.append((lo - QB, lo, nb_ref[0, 1]))
        win_spans.append((lo, hi, nb_ref[0, 0]))
        o_win = attend(qs, kw_ref, vw_ref, win_spans)

        gate = gate_ref[0]
        first = pl.program_id(1) == 0

        def gcol(h, br):
            c = 3 * h + br
            return jnp.where(first, gate[:, c:c + 1], gate[:, 3 * H + c:3 * H + c + 1])

        outs = []
        for h in range(H):
            sl = slice(h * QB, (h + 1) * QB)
            outs.append(gcol(h, 0) * o_cmp[sl] + gcol(h, 1) * o_slc[sl] + gcol(h, 2) * o_win[sl])
        o_ref[0] = jnp.concatenate(outs, axis=1).astype(o_ref.dtype)

    for i in range(T // QB):
        pl.when(pl.program_id(2) == i)(functools.partial(block, i))


def _nsa(qn, kv, kc, vc, gates, cbias, nbias):
    B, T, _ = qn.shape
    G, H, dk = NSA_KV_GROUPS, NSA_HPG, HEAD_DIM
    nb = T // QB
    ns = T // SEL_BLOCK
    cstart = np.arange(LANE) * CMP_STRIDE
    sstart = np.arange(ns) * SEL_BLOCK
    overlap = np.clip(np.minimum(cstart[:, None] + CMP_LEN, sstart[None, :] + SEL_BLOCK)
                      - np.maximum(cstart[:, None], sstart[None, :]), 0, None) / CMP_STRIDE
    overlap[LANE - 1:] = 0.0
    ovt = jnp.asarray(overlap.T, F32)
    expand = np.zeros((LANE, T), np.float32)
    expand[np.arange(T) // SEL_BLOCK, np.arange(T)] = MASK_BIG
    return pl.pallas_call(
        _nsa_kernel,
        grid=(B, G, nb),
        in_specs=[pl.BlockSpec((1, QB, H * dk), lambda b, g, i: (b, i, g)),
                  pl.BlockSpec((1, T, dk), lambda b, g, i: (b, 0, g)),
                  pl.BlockSpec((1, T, dk), lambda b, g, i: (b, 0, G + g)),
                  pl.BlockSpec((1, T, dk), lambda b, g, i: (b, 0, 2 * G + g)),
                  pl.BlockSpec((1, T, dk), lambda b, g, i: (b, 0, 3 * G + g)),
                  pl.BlockSpec((1, 1, LANE, dk), lambda b, g, i: (b, g, 0, 0)),
                  pl.BlockSpec((1, 1, LANE, dk), lambda b, g, i: (b, g, 0, 0)),
                  pl.BlockSpec((1, QB, LANE), lambda b, g, i: (b, i, 0)),
                  pl.BlockSpec((1, 1, H, QB, LANE), lambda b, g, i: (g, i, 0, 0, 0)),
                  pl.BlockSpec((1, 3, H, QB, QB), lambda b, g, i: (g, 0, 0, 0, 0)),
                  pl.BlockSpec((ns, LANE), lambda b, g, i: (0, 0)),
                  pl.BlockSpec((LANE, T), lambda b, g, i: (0, 0))],
        out_specs=pl.BlockSpec((1, QB, H * dk), lambda b, g, i: (b, i, g)),
        out_shape=jax.ShapeDtypeStruct((B, T, NSA_HEADS * dk), BF16),
        compiler_params=_cparams(3),
        name="nsa",
    )(qn, kv, kv, kv, kv, kc, vc, gates, cbias, nbias, ovt, jnp.asarray(expand, BF16))


def _diff_kernel(q_ref, k_ref, v_ref, lq_ref, lk_ref, sg_ref, db_ref, o_ref):
    dk = HEAD_DIM
    T = k_ref.shape[1]
    lqk = lq_ref[...] * lk_ref[...]
    lam = (jnp.exp(jnp.sum(lqk[0:1], axis=-1, keepdims=True))
           - jnp.exp(jnp.sum(lqk[1:2], axis=-1, keepdims=True)) + LAM_INIT)
    for i in range(T // DQB):
        lo, hi = i * DQB, (i + 1) * DQB
        halves = []
        for mm in range(2):
            cols = slice(mm * dk, (mm + 1) * dk)
            q = q_ref[0, lo:hi, cols]
            bounds, parts = [], []
            if i >= 2:
                bounds.append((0, lo - DQB))
                parts.append(_qk(q, k_ref[0, 0:lo - DQB, cols]))
            if i >= 1:
                bounds.append((lo - DQB, lo))
                parts.append(_qk(q, k_ref[0, lo - DQB:lo, cols]) + db_ref[0, 1, 0])
            bounds.append((lo, hi))
            parts.append(_qk(q, k_ref[0, lo:hi, cols]) + db_ref[0, 0, 0])
            ps, den = _softmax_parts(parts)
            o = None
            for (a, b), p in zip(bounds, ps):
                pv = jnp.dot(p.astype(BF16), v_ref[0, a:b, :], preferred_element_type=F32)
                o = pv if o is None else o + pv
            halves.append(o / den)
        o = halves[0] - lam * halves[1]
        ms = jnp.mean(o * o, axis=-1, keepdims=True)
        o_ref[0, lo:hi, :] = ((o * lax.rsqrt(ms + EPS) * sg_ref[...]) * (1.0 - LAM_INIT)).astype(o_ref.dtype)


def _diff(dqk, dv, lam_q, lam_k, subln_gain, dbias):
    B, T, _ = dqk.shape
    Hd, dk = DIFF_HEADS, HEAD_DIM
    w = 2 * dk
    return pl.pallas_call(
        _diff_kernel,
        grid=(B, Hd),
        in_specs=[pl.BlockSpec((1, T, w), lambda b, h: (b, 0, h)),
                  pl.BlockSpec((1, T, w), lambda b, h: (b, 0, Hd + h)),
                  pl.BlockSpec((1, T, w), lambda b, h: (b, 0, h)),
                  pl.BlockSpec((2, dk), lambda b, h: (0, 0)),
                  pl.BlockSpec((2, dk), lambda b, h: (0, 0)),
                  pl.BlockSpec((1, w), lambda b, h: (0, 0)),
                  pl.BlockSpec((1, 2, 1, DQB, DQB), lambda b, h: (h, 0, 0, 0, 0))],
        out_specs=pl.BlockSpec((1, T, w), lambda b, h: (b, 0, h)),
        out_shape=jax.ShapeDtypeStruct((B, T, Hd * w), BF16),
        compiler_params=_cparams(2),
        name="diff",
    )(dqk, dqk, dv, lam_q, lam_k, subln_gain.reshape(1, w), dbias)


def _merge_kernel(an_ref, ad_ref, wn_ref, wd_ref, gn_ref, gd_ref, o_ref, wnb_ref, wdb_ref):
    @pl.when(pl.program_id(1) == 0)
    def _():
        wnb_ref[...] = wn_ref[...].astype(BF16)
        wdb_ref[...] = wd_ref[...].astype(BF16)

    for rows in _row_chunks(an_ref.shape[0]):
        yn = jnp.dot(an_ref[rows, :], wnb_ref[...], preferred_element_type=F32)
        yd = jnp.dot(ad_ref[rows, :], wdb_ref[...], preferred_element_type=F32)
        o_ref[rows, :] = (gn_ref[rows, :].astype(F32) * yn + gd_ref[rows, :].astype(F32) * yd).astype(o_ref.dtype)


def _merge(o_nsa, o_diff, w_n, w_d, mg, tm=1024, tn=512):
    M, K = o_nsa.shape
    N = w_n.shape[1]
    nj = N // tn
    return pl.pallas_call(
        _merge_kernel,
        grid=(nj, M // tm),
        in_specs=[pl.BlockSpec((tm, K), lambda j, i: (i, 0)),
                  pl.BlockSpec((tm, K), lambda j, i: (i, 0)),
                  pl.BlockSpec((K, tn), lambda j, i: (0, j)),
                  pl.BlockSpec((K, tn), lambda j, i: (0, j)),
                  pl.BlockSpec((tm, tn), lambda j, i: (i, j)),
                  pl.BlockSpec((tm, tn), lambda j, i: (i, nj + j))],
        out_specs=pl.BlockSpec((tm, tn), lambda j, i: (i, j)),
        out_shape=jax.ShapeDtypeStruct((M, N), BF16),
        scratch_shapes=[pltpu.VMEM((K, tn), BF16), pltpu.VMEM((K, tn), BF16)],
        compiler_params=_cparams(2),
        name="merge",
    )(o_nsa, o_diff, w_n, w_d, mg, mg)


def _oproj_kernel(a_ref, w_ref, x_ref, g1_ref, gain_ref, sc_ref, sh_ref, x1_ref, h2_ref):
    for rows in _row_chunks(a_ref.shape[0]):
        y = jnp.dot(a_ref[rows, :], w_ref[...], preferred_element_type=F32)
        x1 = x_ref[rows, :] + g1_ref[0] * y
        x1_ref[rows, :] = x1
        h2_ref[rows, :] = _modnorm(x1, gain_ref[...], sc_ref[0], sh_ref[0]).astype(h2_ref.dtype)


def _oproj(merged, w_o, x2d, mod3, gain2, T, tm=512):
    M, D = x2d.shape
    per = T // tm
    return pl.pallas_call(
        _oproj_kernel,
        grid=(M // tm,),
        in_specs=[pl.BlockSpec((tm, D), lambda i: (i, 0)),
                  pl.BlockSpec((D, D), lambda i: (0, 0)),
                  pl.BlockSpec((tm, D), lambda i: (i, 0)),
                  pl.BlockSpec((1, 1, D), lambda i: ((i // per) * 6 + 2, 0, 0)),
                  pl.BlockSpec((1, D), lambda i: (0, 0)),
                  pl.BlockSpec((1, 1, D), lambda i: ((i // per) * 6 + 4, 0, 0)),
                  pl.BlockSpec((1, 1, D), lambda i: ((i // per) * 6 + 3, 0, 0))],
        out_specs=[pl.BlockSpec((tm, D), lambda i: (i, 0)),
                   pl.BlockSpec((tm, D), lambda i: (i, 0))],
        out_shape=[jax.ShapeDtypeStruct((M, D), F32), jax.ShapeDtypeStruct((M, D), BF16)],
        compiler_params=_cparams(1),
        name="oproj",
    )(merged, w_o, x2d, mod3, gain2.reshape(1, D), mod3, mod3)


def _ffn_up_kernel(h_ref, wa_ref, wv_ref, cwa_ref, cwv_ref, cba_ref, cbv_ref, o_ref, wab_ref, wvb_ref,
                   ca_ref, cv_ref, *, per):
    i = pl.program_id(1)
    row = lax.broadcasted_iota(jnp.int32, (SUB_ROWS, 1), 0)

    @pl.when(i == 0)
    def _():
        wab_ref[...] = wa_ref[...].astype(BF16)
        wvb_ref[...] = wv_ref[...].astype(BF16)

    @pl.when(i % per == 0)
    def _():
        ca_ref[...] = jnp.zeros(ca_ref.shape, F32)
        cv_ref[...] = jnp.zeros(cv_ref.shape, F32)

    def conv(u, prev, cw_ref, cb_ref):
        u1 = jnp.where(row < 1, pltpu.roll(prev, 1, axis=0)[0:1, :], pltpu.roll(u, 1, axis=0))
        p2 = pltpu.roll(prev, 2, axis=0)
        u2 = pltpu.roll(u, 2, axis=0)
        u2 = jnp.where(row < 1, p2[0:1, :], jnp.where(row < 2, p2[1:2, :], u2))
        return cb_ref[...] + cw_ref[0:1, :] * u2 + cw_ref[1:2, :] * u1 + cw_ref[2:3, :] * u

    prev_a, prev_v = ca_ref[...], cv_ref[...]
    for rows in _row_chunks(h_ref.shape[0]):
        hs = h_ref[rows, :]
        ua = jnp.dot(hs, wab_ref[...], preferred_element_type=F32)
        uv = jnp.dot(hs, wvb_ref[...], preferred_element_type=F32)
        a = conv(ua, prev_a, cwa_ref, cba_ref)
        val = conv(uv, prev_v, cwv_ref, cbv_ref)
        o_ref[rows, :] = (a * jax.nn.sigmoid(a) * val).astype(o_ref.dtype)
        prev_a, prev_v = ua[SUB_ROWS - 8:, :], uv[SUB_ROWS - 8:, :]
    ca_ref[...] = prev_a
    cv_ref[...] = prev_v


def _ffn_up(h2, w_up, conv_w, conv_b, T, tm=1024, tn=512):
    M, D = h2.shape
    F = w_up.shape[1] // 2
    nj = F // tn
    cb = conv_b.reshape(1, 2 * F)
    return pl.pallas_call(
        functools.partial(_ffn_up_kernel, per=T // tm),
        grid=(nj, M // tm),
        in_specs=[pl.BlockSpec((tm, D), lambda j, i: (i, 0)),
                  pl.BlockSpec((D, tn), lambda j, i: (0, j)),
                  pl.BlockSpec((D, tn), lambda j, i: (0, nj + j)),
                  pl.BlockSpec((3, tn), lambda j, i: (0, j)),
                  pl.BlockSpec((3, tn), lambda j, i: (0, nj + j)),
                  pl.BlockSpec((1, tn), lambda j, i: (0, j)),
                  pl.BlockSpec((1, tn), lambda j, i: (0, nj + j))],
        out_specs=pl.BlockSpec((tm, tn), lambda j, i: (i, j)),
        out_shape=jax.ShapeDtypeStruct((M, F), BF16),
        scratch_shapes=[pltpu.VMEM((D, tn), BF16), pltpu.VMEM((D, tn), BF16),
                        pltpu.VMEM((8, tn), F32), pltpu.VMEM((8, tn), F32)],
        compiler_params=_cparams(2),
        name="ffn_up",
    )(h2, w_up, w_up, conv_w, conv_w, cb, cb)


def _ffn_down_kernel(a_ref, w_ref, x_ref, g2_ref, o_ref, wb_ref):
    @pl.when(pl.program_id(1) == 0)
    def _():
        wb_ref[...] = w_ref[...].astype(BF16)

    for rows in _row_chunks(a_ref.shape[0]):
        y = jnp.dot(a_ref[rows, :], wb_ref[...], preferred_element_type=F32)
        o_ref[rows, :] = x_ref[rows, :] + g2_ref[0] * y


def _ffn_down(act, w_down, x1, mod3, T, tm=512, tn=512):
    M, F = act.shape
    D = w_down.shape[1]
    per = T // tm
    return pl.pallas_call(
        _ffn_down_kernel,
        grid=(D // tn, M // tm),
        in_specs=[pl.BlockSpec((tm, F), lambda j, i: (i, 0)),
                  pl.BlockSpec((F, tn), lambda j, i: (0, j)),
                  pl.BlockSpec((tm, tn), lambda j, i: (i, j)),
                  pl.BlockSpec((1, 1, tn), lambda j, i: ((i // per) * 6 + 5, 0, j))],
        out_specs=pl.BlockSpec((tm, tn), lambda j, i: (i, j)),
        out_shape=jax.ShapeDtypeStruct((M, D), F32),
        scratch_shapes=[pltpu.VMEM((F, tn), BF16)],
        compiler_params=_cparams(2),
        name="ffn_down",
    )(act, w_down, x1, mod3)


def _layer(x, c, w_ada, b_ada, norm1_gain, norm2_gain, w_in, nsa_q_gain, nsa_k_gain, cmp_pe, cmp_w1, cmp_w2,
           diff_q_gain, diff_k_gain, diff_lambda_q, diff_lambda_k, diff_subln_gain, w_nsa_out, w_diff_out, w_o,
           w_ffn_up, ffn_conv_w, ffn_conv_b, w_ffn_down, rel_bias):
    B, T, D = x.shape
    dk, G = HEAD_DIM, NSA_KV_GROUPS
    M = B * T
    scale = dk ** -0.5

    n_q = NSA_HEADS * dk
    o_kv = n_q
    o_g = o_kv + 3 * 2 * G * dk
    o_dq = o_g + NSA_HEADS * 3
    o_dk = o_dq + DIFF_HEADS * 2 * dk
    o_dv = o_dk + DIFF_HEADS * 2 * dk
    o_mg = o_dv + DIFF_HEADS * 2 * dk
    n_kv = 2 * G * dk
    n_dqk = 2 * DIFF_HEADS * 2 * dk

    mod3 = _ada(c, w_ada, b_ada).reshape(B * 6, 1, D)
    h = _norm1(x, norm1_gain, mod3).reshape(M, D)

    ones = jnp.ones((n_kv // 2,), F32)
    g_q = jnp.tile(nsa_q_gain * (scale * LOG2E), NSA_HEADS).reshape(1, n_q)
    g_kv = jnp.concatenate([jnp.tile(nsa_k_gain[1], G), ones, jnp.tile(nsa_k_gain[2], G), ones]).reshape(1, 2 * n_kv)
    g_dqk = jnp.concatenate([jnp.tile(diff_q_gain * (scale * LOG2E), 2 * DIFF_HEADS),
                             jnp.tile(diff_k_gain, 2 * DIFF_HEADS)]).reshape(1, n_dqk)
    g_one = jnp.ones((1, 2 * D), F32)
    proj = functools.partial(_inproj, h, w_in, tm=1024)
    qn = proj(g_q, src0=0, ncols=n_q, n_norm=512, mode="raw", out_dtype=BF16, tn=512, name="inproj_q")
    cmpkv = proj(g_one, src0=o_kv, ncols=n_kv, n_norm=0, mode="raw", out_dtype=F32, tn=512, name="inproj_cmp")
    kv = proj(g_kv, src0=o_kv + n_kv, ncols=2 * n_kv, n_norm=n_kv // 2, mode="raw", out_dtype=BF16, tn=512,
              name="inproj_kv")
    gates = proj(g_one, src0=o_g, ncols=LANE, n_norm=0, mode="sigmoid", out_dtype=F32, tn=LANE, name="inproj_gate")
    dqk = proj(g_dqk, src0=o_dq, ncols=n_dqk, n_norm=512, mode="raw", out_dtype=BF16, tn=512, name="inproj_dqk")
    dv = proj(g_one, src0=o_dv, ncols=o_mg - o_dv, n_norm=0, mode="raw", out_dtype=BF16, tn=512, name="inproj_dv")
    mgate = proj(g_one, src0=o_mg, ncols=2 * D, n_norm=0, mode="sigmoid", out_dtype=BF16, tn=512, name="inproj_mg")

    nbias = _bias_tiles(_window_idx(QB), rel_bias, head0=0, groups=G, hpg=NSA_HPG, name="bias_nsa",
                        rel=True, mult=LOG2E)
    cbias = _bias_tiles(_cmp_idx(T), rel_bias, head0=0, groups=G, hpg=NSA_HPG, name="bias_cmp", mult=LOG2E)
    dbias = _bias_tiles(_causal_idx(DQB), rel_bias, head0=NSA_HEADS, groups=DIFF_HEADS, hpg=1, name="bias_diff",
                        rel=True, mult=LOG2E)

    kc, vc = _compress(cmpkv.reshape(B, T, n_kv), cmp_pe, cmp_w1, cmp_w2, nsa_k_gain[0])
    o_nsa = _nsa(qn.reshape(B, T, n_q), kv.reshape(B, T, 2 * n_kv), kc, vc, gates.reshape(B, T, LANE), cbias, nbias)
    o_diff = _diff(dqk.reshape(B, T, n_dqk), dv.reshape(B, T, -1), diff_lambda_q, diff_lambda_k, diff_subln_gain,
                   dbias)

    merged = _merge(o_nsa.reshape(M, -1), o_diff.reshape(M, -1), w_nsa_out, w_diff_out, mgate)
    x1, h2 = _oproj(merged, w_o.astype(BF16), x.reshape(M, D), mod3, norm2_gain, T)
    act = _ffn_up(h2, w_ffn_up, ffn_conv_w, ffn_conv_b, T)
    out = _ffn_down(act, w_ffn_down, x1, mod3, T)
    return out.reshape(B, T, D)


def kernel(x, c, w_ada, b_ada, norm1_gain, norm2_gain, w_in, nsa_q_gain, nsa_k_gain, cmp_pe, cmp_w1, cmp_w2,
           diff_q_gain, diff_k_gain, diff_lambda_q, diff_lambda_k, diff_subln_gain, w_nsa_out, w_diff_out, w_o,
           w_ffn_up, ffn_conv_w, ffn_conv_b, w_ffn_down, rel_bias):
    return _layer(x, c, w_ada[0], b_ada[0], norm1_gain[0], norm2_gain[0], w_in[0], nsa_q_gain[0], nsa_k_gain[0],
                  cmp_pe[0], cmp_w1[0], cmp_w2[0], diff_q_gain[0], diff_k_gain[0], diff_lambda_q[0],
                  diff_lambda_k[0], diff_subln_gain[0], w_nsa_out[0], w_diff_out[0], w_o[0], w_ffn_up[0],
                  ffn_conv_w[0], ffn_conv_b[0], w_ffn_down[0], rel_bias)
```

```python
import functools
import math

import numpy as np
import jax
import jax.numpy as jnp
from jax import lax
from jax.experimental import pallas as pl
from jax.experimental.pallas import tpu as pltpu

F32 = jnp.float32
BF16 = jnp.bfloat16

HEAD_DIM = 128
NSA_HEADS = 8
NSA_KV_GROUPS = 2
NSA_HPG = NSA_HEADS // NSA_KV_GROUPS
CMP_LEN = 32
CMP_STRIDE = 16
SEL_BLOCK = 64
N_SEL = 16
WINDOW = 512
DIFF_HEADS = 4
NUM_BUCKETS = 32
MAX_DISTANCE = 128
EPS = 1e-6
NEG = -1e30
LAM_INIT = 0.8 - 0.6 * math.exp(-0.3 * 0)
LOG2E = math.log2(math.e)
MASK_BIG = -(2.0 ** 100)

LANE = 128
QB = 128
DQB = 256
SUB_ROWS = 256
VMEM_LIMIT = 56 * 1024 * 1024


def _cparams(n_axes):
    return pltpu.CompilerParams(dimension_semantics=("arbitrary",) * n_axes,
                                vmem_limit_bytes=VMEM_LIMIT)


def _t5_bucket_np(dist):
    n = np.maximum(np.asarray(dist, np.int32), 0)
    max_exact = NUM_BUCKETS // 2
    nf = np.maximum(n, max_exact).astype(np.float32)
    large = max_exact + (np.log(nf / np.float32(max_exact)) / np.float32(math.log(MAX_DISTANCE / max_exact))
                         * np.float32(NUM_BUCKETS - max_exact)).astype(np.int32)
    large = np.minimum(large, NUM_BUCKETS - 1)
    return np.where(n < max_exact, n, large).astype(np.int32)


def _ada_kernel(ct_ref, w_ref, b_ref, o_ref):
    ct = ct_ref[...]
    s = ct * jax.nn.sigmoid(ct)
    w = w_ref[...]
    for b in range(ct.shape[1]):
        o_ref[b:b + 1, :] = jnp.sum(w * s[:, b:b + 1], axis=0, keepdims=True) + b_ref[...]


def _ada(c, w_ada, b_ada, tn=512):
    B, D = c.shape
    N = w_ada.shape[1]
    return pl.pallas_call(
        _ada_kernel,
        grid=(N // tn,),
        in_specs=[pl.BlockSpec((D, B), lambda j: (0, 0)),
                  pl.BlockSpec((D, tn), lambda j: (0, j)),
                  pl.BlockSpec((1, tn), lambda j: (0, j))],
        out_specs=pl.BlockSpec((B, tn), lambda j: (0, j)),
        out_shape=jax.ShapeDtypeStruct((B, N), F32),
        compiler_params=_cparams(1),
        name="ada",
    )(c.T, w_ada, b_ada.reshape(1, N))


def _modnorm(x, gain, sc, sh):
    ms = jnp.mean(x * x, axis=-1, keepdims=True)
    return (x * lax.rsqrt(ms + EPS) * gain) * (1.0 + sc) + sh


def _norm1_kernel(x_ref, g_ref, sc_ref, sh_ref, o_ref):
    o_ref[0] = _modnorm(x_ref[0], g_ref[...], sc_ref[0], sh_ref[0]).astype(o_ref.dtype)


def _norm1(x, gain, mod3, tm=512):
    B, T, D = x.shape
    return pl.pallas_call(
        _norm1_kernel,
        grid=(B, T // tm),
        in_specs=[pl.BlockSpec((1, tm, D), lambda b, i: (b, i, 0)),
                  pl.BlockSpec((1, D), lambda b, i: (0, 0)),
                  pl.BlockSpec((1, 1, D), lambda b, i: (b * 6 + 1, 0, 0)),
                  pl.BlockSpec((1, 1, D), lambda b, i: (b * 6 + 0, 0, 0))],
        out_specs=pl.BlockSpec((1, tm, D), lambda b, i: (b, i, 0)),
        out_shape=jax.ShapeDtypeStruct((B, T, D), BF16),
        compiler_params=_cparams(2),
        name="norm1",
    )(x, gain.reshape(1, D), mod3, mod3)


def _row_chunks(tm):
    return [slice(r, r + SUB_ROWS) for r in range(0, tm, SUB_ROWS)]


def _inproj_kernel(a_ref, w_ref, *rest, shift, n_norm, mode):
    if shift:
        wx_ref, g_ref, o_ref, wb_ref = rest
    else:
        g_ref, o_ref, wb_ref = rest

    @pl.when(pl.program_id(1) == 0)
    def _():
        w = w_ref[...]
        if shift:
            w = jnp.concatenate([w[:, shift:], wx_ref[:, :shift]], axis=1)
        wb_ref[...] = w.astype(BF16)

    for rows in _row_chunks(a_ref.shape[0]):
        acc = jnp.dot(a_ref[rows, :], wb_ref[...], preferred_element_type=F32)
        for k in range(n_norm // LANE):
            y = acc[:, k * LANE:(k + 1) * LANE]
            ms = jnp.mean(y * y, axis=-1, keepdims=True)
            o_ref[rows, k * LANE:(k + 1) * LANE] = (
                y * lax.rsqrt(ms + EPS) * g_ref[:, k * LANE:(k + 1) * LANE]).astype(o_ref.dtype)
        if n_norm < acc.shape[1]:
            rest_acc = acc[:, n_norm:]
            if mode == "sigmoid":
                rest_acc = jax.nn.sigmoid(rest_acc)
            o_ref[rows, n_norm:] = rest_acc.astype(o_ref.dtype)


def _inproj(h2d, w_in, gains, *, src0, ncols, n_norm, mode, out_dtype, tm, tn, name):
    M, K = h2d.shape
    shift = src0 % LANE
    j0, rem = divmod(src0 - shift, tn)
    assert rem == 0 and ncols % tn == 0
    in_specs = [pl.BlockSpec((tm, K), lambda j, i: (i, 0)),
                pl.BlockSpec((None, K, tn), lambda j, i: (0, 0, j0 + j))]
    args = [h2d, w_in]
    if shift:
        in_specs.append(pl.BlockSpec((None, K, LANE), lambda j, i: (0, 0, (j0 + j + 1) * (tn // LANE))))
        args.append(w_in)
    in_specs.append(pl.BlockSpec((1, tn), lambda j, i: (0, j)))
    args.append(gains)
    return pl.pallas_call(
        functools.partial(_inproj_kernel, shift=shift, n_norm=n_norm, mode=mode),
        grid=(ncols // tn, M // tm),
        in_specs=in_specs,
        out_specs=pl.BlockSpec((tm, tn), lambda j, i: (i, j)),
        out_shape=jax.ShapeDtypeStruct((M, ncols), out_dtype),
        scratch_shapes=[pltpu.VMEM((K, tn), BF16)],
        compiler_params=_cparams(2),
        name=name,
    )(*args)


def _bias_kernel(idx_ref, tab_ref, o_ref, *, head0, hpg, rel, mult):
    head = head0 + pl.program_id(0) * hpg + pl.program_id(2)
    idx = idx_ref[0]
    acc = jnp.zeros(idx.shape, F32)
    for b in range(NUM_BUCKETS):
        acc = jnp.where(idx == b, tab_ref[b, head], acc)
    if rel:
        acc = acc - tab_ref[NUM_BUCKETS - 1, head]
    o_ref[0, 0, 0] = jnp.where(idx < 0, NEG, acc * mult)


def _bias_tiles(idx, rel_bias, *, head0, groups, hpg, name, rel=False, mult=1.0):
    N, R, C = idx.shape
    return pl.pallas_call(
        functools.partial(_bias_kernel, head0=head0, hpg=hpg, rel=rel, mult=mult),
        grid=(groups, N, hpg),
        in_specs=[pl.BlockSpec((1, R, C), lambda g, n, h: (n, 0, 0)),
                  pl.BlockSpec(memory_space=pltpu.SMEM)],
        out_specs=pl.BlockSpec((1, 1, 1, R, C), lambda g, n, h: (g, n, h, 0, 0)),
        out_shape=jax.ShapeDtypeStruct((groups, N, hpg, R, C), F32),
        compiler_params=_cparams(3),
        name=name,
    )(jnp.asarray(idx), rel_bias)


def _causal_idx(R):
    r = np.arange(R)[:, None]
    c = np.arange(R)[None, :]
    return np.stack([np.where(r >= c, _t5_bucket_np(r - c), -1), _t5_bucket_np(R + r - c)]).astype(np.int32)


def _window_idx(R):
    r = np.arange(R)[:, None]
    c = np.arange(R)[None, :]
    edge = np.where(r < c, NUM_BUCKETS - 1, -1)
    return np.concatenate([_causal_idx(R), edge[None]]).astype(np.int32)


def _cmp_idx(T):
    per = QB // CMP_STRIDE
    u0 = (T // QB - 1) * per
    assert u0 + LANE <= 2 * LANE
    r = np.arange(QB)[:, None]
    end = (np.arange(2 * LANE)[None, :] - u0) * CMP_STRIDE + CMP_LEN - 1
    return np.where(end <= r, _t5_bucket_np(r - end), -1).astype(np.int32)[None], u0, per


def _compress_kernel(zk_ref, zv_ref, pe_ref, w1_ref, w2_ref, kg_ref, kc_ref, vc_ref):
    half = CMP_LEN // 2

    def one(z_ref, i):
        p1 = jnp.zeros((LANE, HEAD_DIM), F32)
        p2 = jnp.zeros((LANE, HEAD_DIM), F32)
        for l in range(half):
            z = z_ref[0, pl.ds(l, LANE, stride=CMP_STRIDE), :]
            w_lo = w1_ref[i, l * HEAD_DIM:(l + 1) * HEAD_DIM, :].astype(BF16)
            w_hi = w1_ref[i, (half + l) * HEAD_DIM:(half + l + 1) * HEAD_DIM, :].astype(BF16)
            p1 = p1 + jnp.dot((z + pe_ref[i, l:l + 1, :]).astype(BF16), w_lo, preferred_element_type=F32)
            p2 = p2 + jnp.dot((z + pe_ref[i, half + l:half + l + 1, :]).astype(BF16), w_hi,
                              preferred_element_type=F32)
        pre = p1 + pltpu.roll(p2, LANE - 1, axis=0)
        hid = jax.nn.gelu(pre)
        return jnp.dot(hid.astype(BF16), w2_ref[i].astype(BF16), preferred_element_type=F32)

    kc = one(zk_ref, 0)
    ms = jnp.mean(kc * kc, axis=-1, keepdims=True)
    kc_ref[0, 0] = (kc * lax.rsqrt(ms + EPS) * kg_ref[...]).astype(kc_ref.dtype)
    vc_ref[0, 0] = one(zv_ref, 1).astype(vc_ref.dtype)


def _compress(cmp_kv, pe, w1, w2, k_gain0):
    B, T, _ = cmp_kv.shape
    G, dk = NSA_KV_GROUPS, HEAD_DIM
    assert (T - CMP_LEN) // CMP_STRIDE + 1 == LANE - 1
    out = jax.ShapeDtypeStruct((B, G, LANE, dk), BF16)
    return pl.pallas_call(
        _compress_kernel,
        grid=(B, G),
        in_specs=[pl.BlockSpec((1, T, dk), lambda b, g: (b, 0, g)),
                  pl.BlockSpec((1, T, dk), lambda b, g: (b, 0, G + g)),
                  pl.BlockSpec((2, CMP_LEN, dk), lambda b, g: (0, 0, 0)),
                  pl.BlockSpec((2, CMP_LEN * dk, dk), lambda b, g: (0, 0, 0)),
                  pl.BlockSpec((2, dk, dk), lambda b, g: (0, 0, 0)),
                  pl.BlockSpec((1, dk), lambda b, g: (0, 0))],
        out_specs=[pl.BlockSpec((1, 1, LANE, dk), lambda b, g: (b, g, 0, 0))] * 2,
        out_shape=[out, out],
        compiler_params=_cparams(2),
        name="compress",
    )(cmp_kv, cmp_kv, pe, w1, w2, k_gain0.reshape(1, dk))


def _qk(q, k):
    return lax.dot_general(q, k, (((1,), (1,)), ((), ())), preferred_element_type=F32)


def _lane_fold(x, op):
    acc = x[..., :LANE]
    for t in range(1, x.shape[-1] // LANE):
        acc = op(acc, x[..., t * LANE:(t + 1) * LANE])
    return acc


def _softmax_parts(parts):
    m = jnp.max(functools.reduce(jnp.maximum, [_lane_fold(s, jnp.maximum) for s in parts]), axis=-1, keepdims=True)
    ps = [jnp.exp2(s - m) for s in parts]
    den = jnp.sum(functools.reduce(jnp.add, [_lane_fold(p, jnp.add) for p in ps]), axis=-1, keepdims=True)
    return ps, den


def _nsa_kernel(q_ref, ks_ref, vs_ref, kw_ref, vw_ref, kc_ref, vc_ref, gate_ref, cb_ref, nb_ref,
                ovt_ref, exp_ref, o_ref, *, cmp_u0, cmp_per):
    H, R = NSA_HPG, NSA_HPG * QB
    T = ks_ref.shape[1]
    ns = ovt_ref.shape[0]
    nwb = WINDOW // QB
    row = lax.broadcasted_iota(jnp.int32, (QB, LANE), 0)
    col = lax.broadcasted_iota(jnp.int32, (QB, LANE), 1)
    blk = lax.broadcasted_iota(jnp.int32, (ns, QB), 0)
    tl = lax.broadcasted_iota(jnp.int32, (ns, QB), 1)
    kc = kc_ref[0, 0]
    vc = vc_ref[0, 0]

    def attend(qs, k_ref, v_ref, spans):
        parts = []
        for a, b, add in spans:
            s = _qk(qs, k_ref[0, a:b, :]).reshape(H, QB, b - a)
            parts.append(s if add is None else s + add)
        ps, den = _softmax_parts(parts)
        o = None
        for (a, b, _), p in zip(spans, ps):
            pv = jnp.dot(p.reshape(R, b - a).astype(BF16), v_ref[0, a:b, :], preferred_element_type=F32)
            o = pv if o is None else o + pv
        return o / den.reshape(R, 1)

    def block(i):
        lo, hi = i * QB, (i + 1) * QB
        q = q_ref[0]
        qs = jnp.concatenate([q[:, h * HEAD_DIM:(h + 1) * HEAD_DIM] for h in range(H)], axis=0)

        u = cmp_u0 - cmp_per * i
        s = _qk(qs, kc).reshape(H, QB, LANE) + cb_ref[0, 0][:, :, u:u + LANE]
        e = jnp.exp2(s - jnp.max(s, axis=-1, keepdims=True))
        if i == 0:
            valid = (col * CMP_STRIDE + (CMP_LEN - 1)) <= row
            e = jnp.where(valid[None], e, 0.0)
            den = jnp.sum(e, axis=-1, keepdims=True)
            p = e / jnp.where(den > 0.0, den, 1.0)
        else:
            p = e / jnp.sum(e, axis=-1, keepdims=True)
        o_cmp = jnp.dot(p.reshape(R, LANE).astype(BF16), vc, preferred_element_type=F32)

        slc_spans = [(lo, hi, nb_ref[0, 0])]
        if i >= 1:
            psum = p[0] + p[1] + p[2] + p[3]
            imp_t = lax.dot_general(ovt_ref[...], psum, (((1,), (1,)), ((), ())),
                                    precision=lax.Precision.HIGHEST, preferred_element_type=F32)
            cur = (lo + tl) // SEL_BLOCK
            forced = (blk == 0) | (blk == cur) | (blk == cur - 1)
            score = jnp.where(forced, 1e4, jnp.where(blk <= cur, imp_t, -1e4))
            rank = jnp.zeros((ns, QB), F32)
            for b in range(ns):
                other = score[b:b + 1, :]
                rank = rank + jnp.where(blk > b, jnp.where(other >= score, 1.0, 0.0),
                                        jnp.where(other > score, 1.0, 0.0))
            unsel_t = jnp.where(rank < float(min(N_SEL, ns)), 0.0, 1.0)
            unsel = jnp.concatenate([unsel_t, jnp.zeros((LANE - ns, QB), F32)], axis=0).T
            drop = jnp.dot(unsel.astype(BF16), exp_ref[:, 0:lo], preferred_element_type=F32)
            slc_spans.insert(0, (lo - QB, lo, nb_ref[0, 1] + drop[None, :, lo - QB:lo]))
            if i >= 2:
                slc_spans.insert(0, (0, lo - QB, drop[None, :, 0:lo - QB]))

        o_slc = attend(qs, ks_ref, vs_ref, slc_spans)

        win_spans = []
        if i >= nwb:
            win_spans.append(((i - nwb) * QB, (i - nwb + 1) * QB, nb_ref[0, 2]))
        mid_a, mid_b = max(i - nwb + 1, 0) * QB, (i - 1) * QB
        if mid_b > mid_a:
            win_spans.append((mid_a, mid_b, None))
        if i >= 1:
            win_spans.append((lo - QB, lo, nb_ref[0, 1]))
        win_spans.append((lo, hi, nb_ref[0, 0]))
        o_win = attend(qs, kw_ref, vw_ref, win_spans)

        gate = gate_ref[0]
        first = pl.program_id(1) == 0

        def gcol(h, br):
            c = 3 * h + br
            return jnp.where(first, gate[:, c:c + 1], gate[:, 3 * H + c:3 * H + c + 1])

        outs = []
        for h in range(H):
            sl = slice(h * QB, (h + 1) * QB)
            outs.append(gcol(h, 0) * o_cmp[sl] + gcol(h, 1) * o_slc[sl] + gcol(h, 2) * o_win[sl])
        o_ref[0] = jnp.concatenate(outs, axis=1).astype(o_ref.dtype)

    for i in range(T // QB):
        pl.when(pl.program_id(2) == i)(functools.partial(block, i))


def _nsa(qn, kv, kc, vc, gates, cbias, cmp_u0, cmp_per, nbias):
    B, T, _ = qn.shape
    G, H, dk = NSA_KV_GROUPS, NSA_HPG, HEAD_DIM
    nb = T // QB
    ns = T // SEL_BLOCK
    cstart = np.arange(LANE) * CMP_STRIDE
    sstart = np.arange(ns) * SEL_BLOCK
    overlap = np.clip(np.minimum(cstart[:, None] + CMP_LEN, sstart[None, :] + SEL_BLOCK)
                      - np.maximum(cstart[:, None], sstart[None, :]), 0, None) / CMP_STRIDE
    overlap[LANE - 1:] = 0.0
    ovt = jnp.asarray(overlap.T, F32)
    expand = np.zeros((LANE, T), np.float32)
    expand[np.arange(T) // SEL_BLOCK, np.arange(T)] = MASK_BIG
    return pl.pallas_call(
        functools.partial(_nsa_kernel, cmp_u0=cmp_u0, cmp_per=cmp_per),
        grid=(B, G, nb),
        in_specs=[pl.BlockSpec((1, QB, H * dk), lambda b, g, i: (b, i, g)),
                  pl.BlockSpec((1, T, dk), lambda b, g, i: (b, 0, g)),
                  pl.BlockSpec((1, T, dk), lambda b, g, i: (b, 0, G + g)),
                  pl.BlockSpec((1, T, dk), lambda b, g, i: (b, 0, 2 * G + g)),
                  pl.BlockSpec((1, T, dk), lambda b, g, i: (b, 0, 3 * G + g)),
                  pl.BlockSpec((1, 1, LANE, dk), lambda b, g, i: (b, g, 0, 0)),
                  pl.BlockSpec((1, 1, LANE, dk), lambda b, g, i: (b, g, 0, 0)),
                  pl.BlockSpec((1, QB, LANE), lambda b, g, i: (b, i, 0)),
                  pl.BlockSpec((1, 1, H, QB, 2 * LANE), lambda b, g, i: (g, 0, 0, 0, 0)),
                  pl.BlockSpec((1, 3, H, QB, QB), lambda b, g, i: (g, 0, 0, 0, 0)),
                  pl.BlockSpec((ns, LANE), lambda b, g, i: (0, 0)),
                  pl.BlockSpec((LANE, T), lambda b, g, i: (0, 0))],
        out_specs=pl.BlockSpec((1, QB, H * dk), lambda b, g, i: (b, i, g)),
        out_shape=jax.ShapeDtypeStruct((B, T, NSA_HEADS * dk), BF16),
        compiler_params=_cparams(3),
        name="nsa",
    )(qn, kv, kv, kv, kv, kc, vc, gates, cbias, nbias, ovt, jnp.asarray(expand, BF16))


def _diff_kernel(q_ref, k_ref, v_ref, lq_ref, lk_ref, sg_ref, db_ref, o_ref):
    dk = HEAD_DIM
    T = k_ref.shape[1]
    lqk = lq_ref[...] * lk_ref[...]
    lam = (jnp.exp(jnp.sum(lqk[0:1], axis=-1, keepdims=True))
           - jnp.exp(jnp.sum(lqk[1:2], axis=-1, keepdims=True)) + LAM_INIT)
    for i in range(T // DQB):
        lo, hi = i * DQB, (i + 1) * DQB
        halves = []
        for mm in range(2):
            cols = slice(mm * dk, (mm + 1) * dk)
            q = q_ref[0, lo:hi, cols]
            bounds, parts = [], []
            if i >= 2:
                bounds.append((0, lo - DQB))
                parts.append(_qk(q, k_ref[0, 0:lo - DQB, cols]))
            if i >= 1:
                bounds.append((lo - DQB, lo))
                parts.append(_qk(q, k_ref[0, lo - DQB:lo, cols]) + db_ref[0, 1, 0])
            bounds.append((lo, hi))
            parts.append(_qk(q, k_ref[0, lo:hi, cols]) + db_ref[0, 0, 0])
            ps, den = _softmax_parts(parts)
            o = None
            for (a, b), p in zip(bounds, ps):
                pv = jnp.dot(p.astype(BF16), v_ref[0, a:b, :], preferred_element_type=F32)
                o = pv if o is None else o + pv
            halves.append(o / den)
        o = halves[0] - lam * halves[1]
        ms = jnp.mean(o * o, axis=-1, keepdims=True)
        o_ref[0, lo:hi, :] = ((o * lax.rsqrt(ms + EPS) * sg_ref[...]) * (1.0 - LAM_INIT)).astype(o_ref.dtype)


def _diff(dqk, dv, lam_q, lam_k, subln_gain, dbias):
    B, T, _ = dqk.shape
    Hd, dk = DIFF_HEADS, HEAD_DIM
    w = 2 * dk
    return pl.pallas_call(
        _diff_kernel,
        grid=(B, Hd),
        in_specs=[pl.BlockSpec((1, T, w), lambda b, h: (b, 0, h)),
                  pl.BlockSpec((1, T, w), lambda b, h: (b, 0, Hd + h)),
                  pl.BlockSpec((1, T, w), lambda b, h: (b, 0, h)),
                  pl.BlockSpec((2, dk), lambda b, h: (0, 0)),
                  pl.BlockSpec((2, dk), lambda b, h: (0, 0)),
                  pl.BlockSpec((1, w), lambda b, h: (0, 0)),
                  pl.BlockSpec((1, 2, 1, DQB, DQB), lambda b, h: (h, 0, 0, 0, 0))],
        out_specs=pl.BlockSpec((1, T, w), lambda b, h: (b, 0, h)),
        out_shape=jax.ShapeDtypeStruct((B, T, Hd * w), BF16),
        compiler_params=_cparams(2),
        name="diff",
    )(dqk, dqk, dv, lam_q, lam_k, subln_gain.reshape(1, w), dbias)


def _merge_kernel(an_ref, ad_ref, wn_ref, wd_ref, gn_ref, gd_ref, o_ref, wnb_ref, wdb_ref):
    @pl.when(pl.program_id(1) == 0)
    def _():
        wnb_ref[...] = wn_ref[...].astype(BF16)
        wdb_ref[...] = wd_ref[...].astype(BF16)

    for rows in _row_chunks(an_ref.shape[0]):
        yn = jnp.dot(an_ref[rows, :], wnb_ref[...], preferred_element_type=F32)
        yd = jnp.dot(ad_ref[rows, :], wdb_ref[...], preferred_element_type=F32)
        o_ref[rows, :] = (gn_ref[rows, :].astype(F32) * yn + gd_ref[rows, :].astype(F32) * yd).astype(o_ref.dtype)


def _merge(o_nsa, o_diff, w_n, w_d, mg, tm=1024, tn=512):
    M, K = o_nsa.shape
    N = w_n.shape[1]
    nj = N // tn
    return pl.pallas_call(
        _merge_kernel,
        grid=(nj, M // tm),
        in_specs=[pl.BlockSpec((tm, K), lambda j, i: (i, 0)),
                  pl.BlockSpec((tm, K), lambda j, i: (i, 0)),
                  pl.BlockSpec((K, tn), lambda j, i: (0, j)),
                  pl.BlockSpec((K, tn), lambda j, i: (0, j)),
                  pl.BlockSpec((tm, tn), lambda j, i: (i, j)),
                  pl.BlockSpec((tm, tn), lambda j, i: (i, nj + j))],
        out_specs=pl.BlockSpec((tm, tn), lambda j, i: (i, j)),
        out_shape=jax.ShapeDtypeStruct((M, N), BF16),
        scratch_shapes=[pltpu.VMEM((K, tn), BF16), pltpu.VMEM((K, tn), BF16)],
        compiler_params=_cparams(2),
        name="merge",
    )(o_nsa, o_diff, w_n, w_d, mg, mg)


def _oproj_kernel(a_ref, w_ref, x_ref, g1_ref, gain_ref, sc_ref, sh_ref, x1_ref, h2_ref):
    for rows in _row_chunks(a_ref.shape[0]):
        y = jnp.dot(a_ref[rows, :], w_ref[...], preferred_element_type=F32)
        x1 = x_ref[rows, :] + g1_ref[0] * y
        x1_ref[rows, :] = x1
        h2_ref[rows, :] = _modnorm(x1, gain_ref[...], sc_ref[0], sh_ref[0]).astype(h2_ref.dtype)


def _oproj(merged, w_o, x2d, mod3, gain2, T, tm=512):
    M, D = x2d.shape
    per = T // tm
    return pl.pallas_call(
        _oproj_kernel,
        grid=(M // tm,),
        in_specs=[pl.BlockSpec((tm, D), lambda i: (i, 0)),
                  pl.BlockSpec((D, D), lambda i: (0, 0)),
                  pl.BlockSpec((tm, D), lambda i: (i, 0)),
                  pl.BlockSpec((1, 1, D), lambda i: ((i // per) * 6 + 2, 0, 0)),
                  pl.BlockSpec((1, D), lambda i: (0, 0)),
                  pl.BlockSpec((1, 1, D), lambda i: ((i // per) * 6 + 4, 0, 0)),
                  pl.BlockSpec((1, 1, D), lambda i: ((i // per) * 6 + 3, 0, 0))],
        out_specs=[pl.BlockSpec((tm, D), lambda i: (i, 0)),
                   pl.BlockSpec((tm, D), lambda i: (i, 0))],
        out_shape=[jax.ShapeDtypeStruct((M, D), F32), jax.ShapeDtypeStruct((M, D), BF16)],
        compiler_params=_cparams(1),
        name="oproj",
    )(merged, w_o, x2d, mod3, gain2.reshape(1, D), mod3, mod3)


def _ffn_up_kernel(h_ref, wa_ref, wv_ref, cwa_ref, cwv_ref, cba_ref, cbv_ref, o_ref, wab_ref, wvb_ref,
                   ca_ref, cv_ref, *, per):
    i = pl.program_id(1)
    row = lax.broadcasted_iota(jnp.int32, (SUB_ROWS, 1), 0)

    @pl.when(i == 0)
    def _():
        wab_ref[...] = wa_ref[...].astype(BF16)
        wvb_ref[...] = wv_ref[...].astype(BF16)

    @pl.when(i % per == 0)
    def _():
        ca_ref[...] = jnp.zeros(ca_ref.shape, F32)
        cv_ref[...] = jnp.zeros(cv_ref.shape, F32)

    def conv(u, prev, cw_ref, cb_ref):
        u1 = jnp.where(row < 1, pltpu.roll(prev, 1, axis=0)[0:1, :], pltpu.roll(u, 1, axis=0))
        p2 = pltpu.roll(prev, 2, axis=0)
        u2 = pltpu.roll(u, 2, axis=0)
        u2 = jnp.where(row < 1, p2[0:1, :], jnp.where(row < 2, p2[1:2, :], u2))
        return cb_ref[...] + cw_ref[0:1, :] * u2 + cw_ref[1:2, :] * u1 + cw_ref[2:3, :] * u

    prev_a, prev_v = ca_ref[...], cv_ref[...]
    for rows in _row_chunks(h_ref.shape[0]):
        hs = h_ref[rows, :]
        ua = jnp.dot(hs, wab_ref[...], preferred_element_type=F32)
        uv = jnp.dot(hs, wvb_ref[...], preferred_element_type=F32)
        a = conv(ua, prev_a, cwa_ref, cba_ref)
        val = conv(uv, prev_v, cwv_ref, cbv_ref)
        o_ref[rows, :] = (a * jax.nn.sigmoid(a) * val).astype(o_ref.dtype)
        prev_a, prev_v = ua[SUB_ROWS - 8:, :], uv[SUB_ROWS - 8:, :]
    ca_ref[...] = prev_a
    cv_ref[...] = prev_v


def _ffn_up(h2, w_up, conv_w, conv_b, T, tm=1024, tn=512):
    M, D = h2.shape
    F = w_up.shape[1] // 2
    nj = F // tn
    cb = conv_b.reshape(1, 2 * F)
    return pl.pallas_call(
        functools.partial(_ffn_up_kernel, per=T // tm),
        grid=(nj, M // tm),
        in_specs=[pl.BlockSpec((tm, D), lambda j, i: (i, 0)),
                  pl.BlockSpec((D, tn), lambda j, i: (0, j)),
                  pl.BlockSpec((D, tn), lambda j, i: (0, nj + j)),
                  pl.BlockSpec((3, tn), lambda j, i: (0, j)),
                  pl.BlockSpec((3, tn), lambda j, i: (0, nj + j)),
                  pl.BlockSpec((1, tn), lambda j, i: (0, j)),
                  pl.BlockSpec((1, tn), lambda j, i: (0, nj + j))],
        out_specs=pl.BlockSpec((tm, tn), lambda j, i: (i, j)),
        out_shape=jax.ShapeDtypeStruct((M, F), BF16),
        scratch_shapes=[pltpu.VMEM((D, tn), BF16), pltpu.VMEM((D, tn), BF16),
                        pltpu.VMEM((8, tn), F32), pltpu.VMEM((8, tn), F32)],
        compiler_params=_cparams(2),
        name="ffn_up",
    )(h2, w_up, w_up, conv_w, conv_w, cb, cb)


def _ffn_down_kernel(a_ref, w_ref, x_ref, g2_ref, o_ref, wb_ref):
    @pl.when(pl.program_id(1) == 0)
    def _():
        wb_ref[...] = w_ref[...].astype(BF16)

    for rows in _row_chunks(a_ref.shape[0]):
        y = jnp.dot(a_ref[rows, :], wb_ref[...], preferred_element_type=F32)
        o_ref[rows, :] = x_ref[rows, :] + g2_ref[0] * y


def _ffn_down(act, w_down, x1, mod3, T, tm=512, tn=512):
    M, F = act.shape
    D = w_down.shape[1]
    per = T // tm
    return pl.pallas_call(
        _ffn_down_kernel,
        grid=(D // tn, M // tm),
        in_specs=[pl.BlockSpec((tm, F), lambda j, i: (i, 0)),
                  pl.BlockSpec((F, tn), lambda j, i: (0, j)),
                  pl.BlockSpec((tm, tn), lambda j, i: (i, j)),
                  pl.BlockSpec((1, 1, tn), lambda j, i: ((i // per) * 6 + 5, 0, j))],
        out_specs=pl.BlockSpec((tm, tn), lambda j, i: (i, j)),
        out_shape=jax.ShapeDtypeStruct((M, D), F32),
        scratch_shapes=[pltpu.VMEM((F, tn), BF16)],
        compiler_params=_cparams(2),
        name="ffn_down",
    )(act, w_down, x1, mod3)


def _layer(x, c, w_ada, b_ada, norm1_gain, norm2_gain, w_in, nsa_q_gain, nsa_k_gain, cmp_pe, cmp_w1, cmp_w2,
           diff_q_gain, diff_k_gain, diff_lambda_q, diff_lambda_k, diff_subln_gain, w_nsa_out, w_diff_out, w_o,
           w_ffn_up, ffn_conv_w, ffn_conv_b, w_ffn_down, rel_bias):
    B, T, D = x.shape
    dk, G = HEAD_DIM, NSA_KV_GROUPS
    M = B * T
    scale = dk ** -0.5

    n_q = NSA_HEADS * dk
    o_kv = n_q
    o_g = o_kv + 3 * 2 * G * dk
    o_dq = o_g + NSA_HEADS * 3
    o_dk = o_dq + DIFF_HEADS * 2 * dk
    o_dv = o_dk + DIFF_HEADS * 2 * dk
    o_mg = o_dv + DIFF_HEADS * 2 * dk
    n_kv = 2 * G * dk
    n_dqk = 2 * DIFF_HEADS * 2 * dk

    mod3 = _ada(c, w_ada, b_ada).reshape(B * 6, 1, D)
    h = _norm1(x, norm1_gain, mod3).reshape(M, D)

    ones = jnp.ones((n_kv // 2,), F32)
    g_q = jnp.tile(nsa_q_gain * (scale * LOG2E), NSA_HEADS).reshape(1, n_q)
    g_kv = jnp.concatenate([jnp.tile(nsa_k_gain[1], G), ones, jnp.tile(nsa_k_gain[2], G), ones]).reshape(1, 2 * n_kv)
    g_dqk = jnp.concatenate([jnp.tile(diff_q_gain * (scale * LOG2E), 2 * DIFF_HEADS),
                             jnp.tile(diff_k_gain, 2 * DIFF_HEADS)]).reshape(1, n_dqk)
    g_one = jnp.ones((1, 2 * D), F32)
    proj = functools.partial(_inproj, h, w_in, tm=1024)
    qn = proj(g_q, src0=0, ncols=n_q, n_norm=512, mode="raw", out_dtype=BF16, tn=512, name="inproj_q")
    cmpkv = proj(g_one, src0=o_kv, ncols=n_kv, n_norm=0, mode="raw", out_dtype=F32, tn=512, name="inproj_cmp")
    kv = proj(g_kv, src0=o_kv + n_kv, ncols=2 * n_kv, n_norm=n_kv // 2, mode="raw", out_dtype=BF16, tn=512,
              name="inproj_kv")
    gates = proj(g_one, src0=o_g, ncols=LANE, n_norm=0, mode="sigmoid", out_dtype=F32, tn=LANE, name="inproj_gate")
    dqk = proj(g_dqk, src0=o_dq, ncols=n_dqk, n_norm=512, mode="raw", out_dtype=BF16, tn=512, name="inproj_dqk")
    dv = proj(g_one, src0=o_dv, ncols=o_mg - o_dv, n_norm=0, mode="raw", out_dtype=BF16, tn=512, name="inproj_dv")
    mgate = proj(g_one, src0=o_mg, ncols=2 * D, n_norm=0, mode="sigmoid", out_dtype=BF16, tn=512, name="inproj_mg")

    nbias = _bias_tiles(_window_idx(QB), rel_bias, head0=0, groups=G, hpg=NSA_HPG, name="bias_nsa",
                        rel=True, mult=LOG2E)
    cmp_idx, cmp_u0, cmp_per = _cmp_idx(T)
    cbias = _bias_tiles(cmp_idx, rel_bias, head0=0, groups=G, hpg=NSA_HPG, name="bias_cmp", mult=LOG2E)
    dbias = _bias_tiles(_causal_idx(DQB), rel_bias, head0=NSA_HEADS, groups=DIFF_HEADS, hpg=1, name="bias_diff",
                        rel=True, mult=LOG2E)

    kc, vc = _compress(cmpkv.reshape(B, T, n_kv), cmp_pe, cmp_w1, cmp_w2, nsa_k_gain[0])
    o_nsa = _nsa(qn.reshape(B, T, n_q), kv.reshape(B, T, 2 * n_kv), kc, vc, gates.reshape(B, T, LANE), cbias,
                 cmp_u0, cmp_per, nbias)
    o_diff = _diff(dqk.reshape(B, T, n_dqk), dv.reshape(B, T, -1), diff_lambda_q, diff_lambda_k, diff_subln_gain,
                   dbias)

    merged = _merge(o_nsa.reshape(M, -1), o_diff.reshape(M, -1), w_nsa_out, w_diff_out, mgate)
    x1, h2 = _oproj(merged, w_o.astype(BF16), x.reshape(M, D), mod3, norm2_gain, T)
    act = _ffn_up(h2, w_ffn_up, ffn_conv_w, ffn_conv_b, T)
    out = _ffn_down(act, w_ffn_down, x1, mod3, T)
    return out.reshape(B, T, D)


def kernel(x, c, w_ada, b_ada, norm1_gain, norm2_gain, w_in, nsa_q_gain, nsa_k_gain, cmp_pe, cmp_w1, cmp_w2,
           diff_q_gain, diff_k_gain, diff_lambda_q, diff_lambda_k, diff_subln_gain, w_nsa_out, w_diff_out, w_o,
           w_ffn_up, ffn_conv_w, ffn_conv_b, w_ffn_down, rel_bias):
    return _layer(x, c, w_ada[0], b_ada[0], norm1_gain[0], norm2_gain[0], w_in, nsa_q_gain[0], nsa_k_gain[0],
                  cmp_pe[0], cmp_w1[0], cmp_w2[0], diff_q_gain[0], diff_k_gain[0], diff_lambda_q[0],
                  diff_lambda_k[0], diff_subln_gain[0], w_nsa_out[0], w_diff_out[0], w_o[0], w_ffn_up[0],
                  ffn_conv_w[0], ffn_conv_b[0], w_ffn_down[0], rel_bias)
```

```python
import functools
import math

import numpy as np
import jax
import jax.numpy as jnp
from jax import lax
from jax.experimental import pallas as pl
from jax.experimental.pallas import tpu as pltpu

F32 = jnp.float32
BF16 = jnp.bfloat16

HEAD_DIM = 128
NSA_HEADS = 8
NSA_KV_GROUPS = 2
NSA_HPG = NSA_HEADS // NSA_KV_GROUPS
CMP_LEN = 32
CMP_STRIDE = 16
SEL_BLOCK = 64
N_SEL = 16
WINDOW = 512
DIFF_HEADS = 4
NUM_BUCKETS = 32
MAX_DISTANCE = 128
EPS = 1e-6
NEG = -1e30
LAM_INIT = 0.8 - 0.6 * math.exp(-0.3 * 0)
LOG2E = math.log2(math.e)
MASK_BIG = -(2.0 ** 100)

LANE = 128
QB = 128
DQB = 256
SUB_ROWS = 256
VMEM_LIMIT = 56 * 1024 * 1024


def _cparams(n_axes):
    return pltpu.CompilerParams(dimension_semantics=("arbitrary",) * n_axes,
                                vmem_limit_bytes=VMEM_LIMIT)


def _t5_bucket_np(dist):
    n = np.maximum(np.asarray(dist, np.int32), 0)
    max_exact = NUM_BUCKETS // 2
    nf = np.maximum(n, max_exact).astype(np.float32)
    large = max_exact + (np.log(nf / np.float32(max_exact)) / np.float32(math.log(MAX_DISTANCE / max_exact))
                         * np.float32(NUM_BUCKETS - max_exact)).astype(np.int32)
    large = np.minimum(large, NUM_BUCKETS - 1)
    return np.where(n < max_exact, n, large).astype(np.int32)


def _ada_kernel(ct_ref, w_ref, b_ref, o_ref):
    ct = ct_ref[...]
    s = ct * jax.nn.sigmoid(ct)
    w = w_ref[...]
    for b in range(ct.shape[1]):
        o_ref[b:b + 1, :] = jnp.sum(w * s[:, b:b + 1], axis=0, keepdims=True) + b_ref[...]


def _ada(c, w_ada, b_ada, tn=512):
    B, D = c.shape
    N = w_ada.shape[1]
    return pl.pallas_call(
        _ada_kernel,
        grid=(N // tn,),
        in_specs=[pl.BlockSpec((D, B), lambda j: (0, 0)),
                  pl.BlockSpec((D, tn), lambda j: (0, j)),
                  pl.BlockSpec((1, tn), lambda j: (0, j))],
        out_specs=pl.BlockSpec((B, tn), lambda j: (0, j)),
        out_shape=jax.ShapeDtypeStruct((B, N), F32),
        compiler_params=_cparams(1),
        name="ada",
    )(c.T, w_ada, b_ada.reshape(1, N))


def _modnorm(x, gain, sc, sh):
    ms = jnp.mean(x * x, axis=-1, keepdims=True)
    return (x * lax.rsqrt(ms + EPS) * gain) * (1.0 + sc) + sh


def _norm1_kernel(x_ref, g_ref, sc_ref, sh_ref, o_ref):
    o_ref[0] = _modnorm(x_ref[0], g_ref[...], sc_ref[0], sh_ref[0]).astype(o_ref.dtype)


def _norm1(x, gain, mod3, tm=512):
    B, T, D = x.shape
    return pl.pallas_call(
        _norm1_kernel,
        grid=(B, T // tm),
        in_specs=[pl.BlockSpec((1, tm, D), lambda b, i: (b, i, 0)),
                  pl.BlockSpec((1, D), lambda b, i: (0, 0)),
                  pl.BlockSpec((1, 1, D), lambda b, i: (b * 6 + 1, 0, 0)),
                  pl.BlockSpec((1, 1, D), lambda b, i: (b * 6 + 0, 0, 0))],
        out_specs=pl.BlockSpec((1, tm, D), lambda b, i: (b, i, 0)),
        out_shape=jax.ShapeDtypeStruct((B, T, D), BF16),
        compiler_params=_cparams(2),
        name="norm1",
    )(x, gain.reshape(1, D), mod3, mod3)


def _row_chunks(tm):
    return [slice(r, r + SUB_ROWS) for r in range(0, tm, SUB_ROWS)]


def _inproj_kernel(a_ref, wt_ref, g_ref, o_ref, wb_ref, *, n_norm, mode):
    @pl.when(pl.program_id(1) == 0)
    def _():
        wb_ref[...] = wt_ref[0].astype(BF16)

    for rows in _row_chunks(a_ref.shape[0]):
        acc = _qk(a_ref[rows, :], wb_ref[...])
        for k in range(n_norm // LANE):
            y = acc[:, k * LANE:(k + 1) * LANE]
            ms = jnp.mean(y * y, axis=-1, keepdims=True)
            o_ref[rows, k * LANE:(k + 1) * LANE] = (
                y * lax.rsqrt(ms + EPS) * g_ref[:, k * LANE:(k + 1) * LANE]).astype(o_ref.dtype)
        if n_norm < acc.shape[1]:
            rest_acc = acc[:, n_norm:]
            if mode == "sigmoid":
                rest_acc = jax.nn.sigmoid(rest_acc)
            o_ref[rows, n_norm:] = rest_acc.astype(o_ref.dtype)


def _inproj(h2d, w_in_t, gains, *, src0, ncols, n_norm, mode, out_dtype, tm, tn, name):
    M, K = h2d.shape
    assert src0 % 8 == 0 and ncols % tn == 0
    return pl.pallas_call(
        functools.partial(_inproj_kernel, n_norm=n_norm, mode=mode),
        grid=(ncols // tn, M // tm),
        in_specs=[pl.BlockSpec((tm, K), lambda j, i: (i, 0)),
                  pl.BlockSpec((pl.Element(1), pl.Element(tn), pl.Element(K)),
                               lambda j, i: (0, pl.multiple_of(src0 + j * tn, 8), 0)),
                  pl.BlockSpec((1, tn), lambda j, i: (0, j))],
        out_specs=pl.BlockSpec((tm, tn), lambda j, i: (i, j)),
        out_shape=jax.ShapeDtypeStruct((M, ncols), out_dtype),
        scratch_shapes=[pltpu.VMEM((tn, K), BF16)],
        compiler_params=_cparams(2),
        name=name,
    )(h2d, w_in_t, gains)


def _bias_kernel(idx_ref, tab_ref, o_ref, *, head0, hpg, rel, mult):
    head = head0 + pl.program_id(0) * hpg + pl.program_id(2)
    idx = idx_ref[0]
    acc = jnp.zeros(idx.shape, F32)
    for b in range(NUM_BUCKETS):
        acc = jnp.where(idx == b, tab_ref[b, head], acc)
    if rel:
        acc = acc - tab_ref[NUM_BUCKETS - 1, head]
    o_ref[0, 0, 0] = jnp.where(idx < 0, NEG, acc * mult)


def _bias_tiles(idx, rel_bias, *, head0, groups, hpg, name, rel=False, mult=1.0):
    N, R, C = idx.shape
    return pl.pallas_call(
        functools.partial(_bias_kernel, head0=head0, hpg=hpg, rel=rel, mult=mult),
        grid=(groups, N, hpg),
        in_specs=[pl.BlockSpec((1, R, C), lambda g, n, h: (n, 0, 0)),
                  pl.BlockSpec(memory_space=pltpu.SMEM)],
        out_specs=pl.BlockSpec((1, 1, 1, R, C), lambda g, n, h: (g, n, h, 0, 0)),
        out_shape=jax.ShapeDtypeStruct((groups, N, hpg, R, C), F32),
        compiler_params=_cparams(3),
        name=name,
    )(jnp.asarray(idx), rel_bias)


def _causal_idx(R):
    r = np.arange(R)[:, None]
    c = np.arange(R)[None, :]
    return np.stack([np.where(r >= c, _t5_bucket_np(r - c), -1), _t5_bucket_np(R + r - c)]).astype(np.int32)


def _window_idx(R):
    r = np.arange(R)[:, None]
    c = np.arange(R)[None, :]
    edge = np.where(r < c, NUM_BUCKETS - 1, -1)
    return np.concatenate([_causal_idx(R), edge[None]]).astype(np.int32)


def _cmp_idx(T):
    per = QB // CMP_STRIDE
    u0 = (T // QB - 1) * per
    assert u0 + LANE <= 2 * LANE
    r = np.arange(QB)[:, None]
    end = (np.arange(2 * LANE)[None, :] - u0) * CMP_STRIDE + CMP_LEN - 1
    return np.where(end <= r, _t5_bucket_np(r - end), -1).astype(np.int32)[None], u0, per


def _compress_kernel(zk_ref, zv_ref, pe_ref, w1_ref, w2_ref, kg_ref, kc_ref, vc_ref):
    half = CMP_LEN // 2

    def one(z_ref, i):
        p1 = jnp.zeros((LANE, HEAD_DIM), F32)
        p2 = jnp.zeros((LANE, HEAD_DIM), F32)
        for l in range(half):
            z = z_ref[0, pl.ds(l, LANE, stride=CMP_STRIDE), :]
            w_lo = w1_ref[i, l * HEAD_DIM:(l + 1) * HEAD_DIM, :].astype(BF16)
            w_hi = w1_ref[i, (half + l) * HEAD_DIM:(half + l + 1) * HEAD_DIM, :].astype(BF16)
            p1 = p1 + jnp.dot((z + pe_ref[i, l:l + 1, :]).astype(BF16), w_lo, preferred_element_type=F32)
            p2 = p2 + jnp.dot((z + pe_ref[i, half + l:half + l + 1, :]).astype(BF16), w_hi,
                              preferred_element_type=F32)
        pre = p1 + pltpu.roll(p2, LANE - 1, axis=0)
        hid = jax.nn.gelu(pre)
        return jnp.dot(hid.astype(BF16), w2_ref[i].astype(BF16), preferred_element_type=F32)

    kc = one(zk_ref, 0)
    ms = jnp.mean(kc * kc, axis=-1, keepdims=True)
    kc_ref[0, 0] = (kc * lax.rsqrt(ms + EPS) * kg_ref[...]).astype(kc_ref.dtype)
    vc_ref[0, 0] = one(zv_ref, 1).astype(vc_ref.dtype)


def _compress(cmp_kv, pe, w1, w2, k_gain0):
    B, T, _ = cmp_kv.shape
    G, dk = NSA_KV_GROUPS, HEAD_DIM
    assert (T - CMP_LEN) // CMP_STRIDE + 1 == LANE - 1
    out = jax.ShapeDtypeStruct((B, G, LANE, dk), BF16)
    return pl.pallas_call(
        _compress_kernel,
        grid=(B, G),
        in_specs=[pl.BlockSpec((1, T, dk), lambda b, g: (b, 0, g)),
                  pl.BlockSpec((1, T, dk), lambda b, g: (b, 0, G + g)),
                  pl.BlockSpec((2, CMP_LEN, dk), lambda b, g: (0, 0, 0)),
                  pl.BlockSpec((2, CMP_LEN * dk, dk), lambda b, g: (0, 0, 0)),
                  pl.BlockSpec((2, dk, dk), lambda b, g: (0, 0, 0)),
                  pl.BlockSpec((1, dk), lambda b, g: (0, 0))],
        out_specs=[pl.BlockSpec((1, 1, LANE, dk), lambda b, g: (b, g, 0, 0))] * 2,
        out_shape=[out, out],
        compiler_params=_cparams(2),
        name="compress",
    )(cmp_kv, cmp_kv, pe, w1, w2, k_gain0.reshape(1, dk))


def _qk(q, k):
    return lax.dot_general(q, k, (((1,), (1,)), ((), ())), preferred_element_type=F32)


def _lane_fold(x, op):
    acc = x[..., :LANE]
    for t in range(1, x.shape[-1] // LANE):
        acc = op(acc, x[..., t * LANE:(t + 1) * LANE])
    return acc


def _softmax_parts(parts):
    m = jnp.max(functools.reduce(jnp.maximum, [_lane_fold(s, jnp.maximum) for s in parts]), axis=-1, keepdims=True)
    ps = [jnp.exp2(s - m) for s in parts]
    den = jnp.sum(functools.reduce(jnp.add, [_lane_fold(p, jnp.add) for p in ps]), axis=-1, keepdims=True)
    return ps, den


def _nsa_kernel(q_ref, ks_ref, vs_ref, kw_ref, vw_ref, kc_ref, vc_ref, gate_ref, cb_ref, nb_ref,
                ovt_ref, exp_ref, o_ref, *, cmp_u0, cmp_per):
    H, R = NSA_HPG, NSA_HPG * QB
    T = ks_ref.shape[1]
    ns = ovt_ref.shape[0]
    nwb = WINDOW // QB
    row = lax.broadcasted_iota(jnp.int32, (QB, LANE), 0)
    col = lax.broadcasted_iota(jnp.int32, (QB, LANE), 1)
    blk = lax.broadcasted_iota(jnp.int32, (ns, QB), 0)
    tl = lax.broadcasted_iota(jnp.int32, (ns, QB), 1)
    kc = kc_ref[0, 0]
    vc = vc_ref[0, 0]

    def attend(qs, k_ref, v_ref, spans):
        parts = []
        for a, b, add in spans:
            s = _qk(qs, k_ref[0, a:b, :]).reshape(H, QB, b - a)
            parts.append(s if add is None else s + add)
        ps, den = _softmax_parts(parts)
        o = None
        for (a, b, _), p in zip(spans, ps):
            pv = jnp.dot(p.reshape(R, b - a).astype(BF16), v_ref[0, a:b, :], preferred_element_type=F32)
            o = pv if o is None else o + pv
        return o / den.reshape(R, 1)

    def block(i):
        lo, hi = i * QB, (i + 1) * QB
        q = q_ref[0]
        qs = jnp.concatenate([q[:, h * HEAD_DIM:(h + 1) * HEAD_DIM] for h in range(H)], axis=0)

        u = cmp_u0 - cmp_per * i
        s = _qk(qs, kc).reshape(H, QB, LANE) + cb_ref[0, 0][:, :, u:u + LANE]
        e = jnp.exp2(s - jnp.max(s, axis=-1, keepdims=True))
        if i == 0:
            valid = (col * CMP_STRIDE + (CMP_LEN - 1)) <= row
            e = jnp.where(valid[None], e, 0.0)
            den = jnp.sum(e, axis=-1, keepdims=True)
            p = e / jnp.where(den > 0.0, den, 1.0)
        else:
            p = e / jnp.sum(e, axis=-1, keepdims=True)
        o_cmp = jnp.dot(p.reshape(R, LANE).astype(BF16), vc, preferred_element_type=F32)

        slc_spans = [(lo, hi, nb_ref[0, 0])]
        if i >= 1:
            psum = p[0] + p[1] + p[2] + p[3]
            imp_t = lax.dot_general(ovt_ref[...], psum, (((1,), (1,)), ((), ())),
                                    precision=lax.Precision.HIGHEST, preferred_element_type=F32)
            cur = (lo + tl) // SEL_BLOCK
            forced = (blk == 0) | (blk == cur) | (blk == cur - 1)
            score = jnp.where(forced, 1e4, jnp.where(blk <= cur, imp_t, -1e4))
            rank = jnp.zeros((ns, QB), F32)
            for b in range(ns):
                other = score[b:b + 1, :]
                rank = rank + jnp.where(blk > b, jnp.where(other >= score, 1.0, 0.0),
                                        jnp.where(other > score, 1.0, 0.0))
            unsel_t = jnp.where(rank < float(min(N_SEL, ns)), 0.0, 1.0)
            unsel = jnp.concatenate([unsel_t, jnp.zeros((LANE - ns, QB), F32)], axis=0).T
            drop = jnp.dot(unsel.astype(BF16), exp_ref[:, 0:lo], preferred_element_type=F32)
            slc_spans.insert(0, (lo - QB, lo, nb_ref[0, 1] + drop[None, :, lo - QB:lo]))
            if i >= 2:
                slc_spans.insert(0, (0, lo - QB, drop[None, :, 0:lo - QB]))

        o_slc = attend(qs, ks_ref, vs_ref, slc_spans)

        win_spans = []
        if i >= nwb:
            win_spans.append(((i - nwb) * QB, (i - nwb + 1) * QB, nb_ref[0, 2]))
        mid_a, mid_b = max(i - nwb + 1, 0) * QB, (i - 1) * QB
        if mid_b > mid_a:
            win_spans.append((mid_a, mid_b, None))
        if i >= 1:
            win_spans.append((lo - QB, lo, nb_ref[0, 1]))
        win_spans.append((lo, hi, nb_ref[0, 0]))
        o_win = attend(qs, kw_ref, vw_ref, win_spans)

        gate = gate_ref[0]
        first = pl.program_id(1) == 0

        def gcol(h, br):
            c = 3 * h + br
            return jnp.where(first, gate[:, c:c + 1], gate[:, 3 * H + c:3 * H + c + 1])

        outs = []
        for h in range(H):
            sl = slice(h * QB, (h + 1) * QB)
            outs.append(gcol(h, 0) * o_cmp[sl] + gcol(h, 1) * o_slc[sl] + gcol(h, 2) * o_win[sl])
        o_ref[0] = jnp.concatenate(outs, axis=1).astype(o_ref.dtype)

    for i in range(T // QB):
        pl.when(pl.program_id(2) == i)(functools.partial(block, i))


def _nsa(qn, kv, kc, vc, gates, cbias, cmp_u0, cmp_per, nbias):
    B, T, _ = qn.shape
    G, H, dk = NSA_KV_GROUPS, NSA_HPG, HEAD_DIM
    nb = T // QB
    ns = T // SEL_BLOCK
    cstart = np.arange(LANE) * CMP_STRIDE
    sstart = np.arange(ns) * SEL_BLOCK
    overlap = np.clip(np.minimum(cstart[:, None] + CMP_LEN, sstart[None, :] + SEL_BLOCK)
                      - np.maximum(cstart[:, None], sstart[None, :]), 0, None) / CMP_STRIDE
    overlap[LANE - 1:] = 0.0
    ovt = jnp.asarray(overlap.T, F32)
    expand = np.zeros((LANE, T), np.float32)
    expand[np.arange(T) // SEL_BLOCK, np.arange(T)] = MASK_BIG
    return pl.pallas_call(
        functools.partial(_nsa_kernel, cmp_u0=cmp_u0, cmp_per=cmp_per),
        grid=(B, G, nb),
        in_specs=[pl.BlockSpec((1, QB, H * dk), lambda b, g, i: (b, i, g)),
                  pl.BlockSpec((1, T, dk), lambda b, g, i: (b, 0, g)),
                  pl.BlockSpec((1, T, dk), lambda b, g, i: (b, 0, G + g)),
                  pl.BlockSpec((1, T, dk), lambda b, g, i: (b, 0, 2 * G + g)),
                  pl.BlockSpec((1, T, dk), lambda b, g, i: (b, 0, 3 * G + g)),
                  pl.BlockSpec((1, 1, LANE, dk), lambda b, g, i: (b, g, 0, 0)),
                  pl.BlockSpec((1, 1, LANE, dk), lambda b, g, i: (b, g, 0, 0)),
                  pl.BlockSpec((1, QB, LANE), lambda b, g, i: (b, i, 0)),
                  pl.BlockSpec((1, 1, H, QB, 2 * LANE), lambda b, g, i: (g, 0, 0, 0, 0)),
                  pl.BlockSpec((1, 3, H, QB, QB), lambda b, g, i: (g, 0, 0, 0, 0)),
                  pl.BlockSpec((ns, LANE), lambda b, g, i: (0, 0)),
                  pl.BlockSpec((LANE, T), lambda b, g, i: (0, 0))],
        out_specs=pl.BlockSpec((1, QB, H * dk), lambda b, g, i: (b, i, g)),
        out_shape=jax.ShapeDtypeStruct((B, T, NSA_HEADS * dk), BF16),
        compiler_params=_cparams(3),
        name="nsa",
    )(qn, kv, kv, kv, kv, kc, vc, gates, cbias, nbias, ovt, jnp.asarray(expand, BF16))


def _diff_kernel(q_ref, k_ref, v_ref, lq_ref, lk_ref, sg_ref, db_ref, o_ref):
    dk = HEAD_DIM
    T = k_ref.shape[1]
    lqk = lq_ref[...] * lk_ref[...]
    lam = (jnp.exp(jnp.sum(lqk[0:1], axis=-1, keepdims=True))
           - jnp.exp(jnp.sum(lqk[1:2], axis=-1, keepdims=True)) + LAM_INIT)
    for i in range(T // DQB):
        lo, hi = i * DQB, (i + 1) * DQB
        halves = []
        for mm in range(2):
            cols = slice(mm * dk, (mm + 1) * dk)
            q = q_ref[0, lo:hi, cols]
            bounds, parts = [], []
            if i >= 2:
                bounds.append((0, lo - DQB))
                parts.append(_qk(q, k_ref[0, 0:lo - DQB, cols]))
            if i >= 1:
                bounds.append((lo - DQB, lo))
                parts.append(_qk(q, k_ref[0, lo - DQB:lo, cols]) + db_ref[0, 1, 0])
            bounds.append((lo, hi))
            parts.append(_qk(q, k_ref[0, lo:hi, cols]) + db_ref[0, 0, 0])
            ps, den = _softmax_parts(parts)
            o = None
            for (a, b), p in zip(bounds, ps):
                pv = jnp.dot(p.astype(BF16), v_ref[0, a:b, :], preferred_element_type=F32)
                o = pv if o is None else o + pv
            halves.append(o / den)
        o = halves[0] - lam * halves[1]
        ms = jnp.mean(o * o, axis=-1, keepdims=True)
        o_ref[0, lo:hi, :] = ((o * lax.rsqrt(ms + EPS) * sg_ref[...]) * (1.0 - LAM_INIT)).astype(o_ref.dtype)


def _diff(dqk, dv, lam_q, lam_k, subln_gain, dbias):
    B, T, _ = dqk.shape
    Hd, dk = DIFF_HEADS, HEAD_DIM
    w = 2 * dk
    return pl.pallas_call(
        _diff_kernel,
        grid=(B, Hd),
        in_specs=[pl.BlockSpec((1, T, w), lambda b, h: (b, 0, h)),
                  pl.BlockSpec((1, T, w), lambda b, h: (b, 0, Hd + h)),
                  pl.BlockSpec((1, T, w), lambda b, h: (b, 0, h)),
                  pl.BlockSpec((2, dk), lambda b, h: (0, 0)),
                  pl.BlockSpec((2, dk), lambda b, h: (0, 0)),
                  pl.BlockSpec((1, w), lambda b, h: (0, 0)),
                  pl.BlockSpec((1, 2, 1, DQB, DQB), lambda b, h: (h, 0, 0, 0, 0))],
        out_specs=pl.BlockSpec((1, T, w), lambda b, h: (b, 0, h)),
        out_shape=jax.ShapeDtypeStruct((B, T, Hd * w), BF16),
        compiler_params=_cparams(2),
        name="diff",
    )(dqk, dqk, dv, lam_q, lam_k, subln_gain.reshape(1, w), dbias)


def _merge_kernel(an_ref, ad_ref, wn_ref, wd_ref, gn_ref, gd_ref, o_ref, wnb_ref, wdb_ref):
    @pl.when(pl.program_id(1) == 0)
    def _():
        wnb_ref[...] = wn_ref[...].astype(BF16)
        wdb_ref[...] = wd_ref[...].astype(BF16)

    for rows in _row_chunks(an_ref.shape[0]):
        yn = jnp.dot(an_ref[rows, :], wnb_ref[...], preferred_element_type=F32)
        yd = jnp.dot(ad_ref[rows, :], wdb_ref[...], preferred_element_type=F32)
        o_ref[rows, :] = (gn_ref[rows, :].astype(F32) * yn + gd_ref[rows, :].astype(F32) * yd).astype(o_ref.dtype)


def _merge(o_nsa, o_diff, w_n, w_d, mg, tm=1024, tn=512):
    M, K = o_nsa.shape
    N = w_n.shape[1]
    nj = N // tn
    return pl.pallas_call(
        _merge_kernel,
        grid=(nj, M // tm),
        in_specs=[pl.BlockSpec((tm, K), lambda j, i: (i, 0)),
                  pl.BlockSpec((tm, K), lambda j, i: (i, 0)),
                  pl.BlockSpec((K, tn), lambda j, i: (0, j)),
                  pl.BlockSpec((K, tn), lambda j, i: (0, j)),
                  pl.BlockSpec((tm, tn), lambda j, i: (i, j)),
                  pl.BlockSpec((tm, tn), lambda j, i: (i, nj + j))],
        out_specs=pl.BlockSpec((tm, tn), lambda j, i: (i, j)),
        out_shape=jax.ShapeDtypeStruct((M, N), BF16),
        scratch_shapes=[pltpu.VMEM((K, tn), BF16), pltpu.VMEM((K, tn), BF16)],
        compiler_params=_cparams(2),
        name="merge",
    )(o_nsa, o_diff, w_n, w_d, mg, mg)


def _oproj_kernel(a_ref, w_ref, x_ref, g1_ref, gain_ref, sc_ref, sh_ref, x1_ref, h2_ref):
    for rows in _row_chunks(a_ref.shape[0]):
        y = jnp.dot(a_ref[rows, :], w_ref[...], preferred_element_type=F32)
        x1 = x_ref[rows, :] + g1_ref[0] * y
        x1_ref[rows, :] = x1
        h2_ref[rows, :] = _modnorm(x1, gain_ref[...], sc_ref[0], sh_ref[0]).astype(h2_ref.dtype)


def _oproj(merged, w_o, x2d, mod3, gain2, T, tm=512):
    M, D = x2d.shape
    per = T // tm
    return pl.pallas_call(
        _oproj_kernel,
        grid=(M // tm,),
        in_specs=[pl.BlockSpec((tm, D), lambda i: (i, 0)),
                  pl.BlockSpec((D, D), lambda i: (0, 0)),
                  pl.BlockSpec((tm, D), lambda i: (i, 0)),
                  pl.BlockSpec((1, 1, D), lambda i: ((i // per) * 6 + 2, 0, 0)),
                  pl.BlockSpec((1, D), lambda i: (0, 0)),
                  pl.BlockSpec((1, 1, D), lambda i: ((i // per) * 6 + 4, 0, 0)),
                  pl.BlockSpec((1, 1, D), lambda i: ((i // per) * 6 + 3, 0, 0))],
        out_specs=[pl.BlockSpec((tm, D), lambda i: (i, 0)),
                   pl.BlockSpec((tm, D), lambda i: (i, 0))],
        out_shape=[jax.ShapeDtypeStruct((M, D), F32), jax.ShapeDtypeStruct((M, D), BF16)],
        compiler_params=_cparams(1),
        name="oproj",
    )(merged, w_o, x2d, mod3, gain2.reshape(1, D), mod3, mod3)


def _ffn_up_kernel(h_ref, wa_ref, wv_ref, cwa_ref, cwv_ref, cba_ref, cbv_ref, o_ref, wab_ref, wvb_ref,
                   ca_ref, cv_ref, *, per):
    i = pl.program_id(1)
    row = lax.broadcasted_iota(jnp.int32, (SUB_ROWS, 1), 0)

    @pl.when(i == 0)
    def _():
        wab_ref[...] = wa_ref[...].astype(BF16)
        wvb_ref[...] = wv_ref[...].astype(BF16)

    @pl.when(i % per == 0)
    def _():
        ca_ref[...] = jnp.zeros(ca_ref.shape, F32)
        cv_ref[...] = jnp.zeros(cv_ref.shape, F32)

    def conv(u, prev, cw_ref, cb_ref):
        u1 = jnp.where(row < 1, pltpu.roll(prev, 1, axis=0)[0:1, :], pltpu.roll(u, 1, axis=0))
        p2 = pltpu.roll(prev, 2, axis=0)
        u2 = pltpu.roll(u, 2, axis=0)
        u2 = jnp.where(row < 1, p2[0:1, :], jnp.where(row < 2, p2[1:2, :], u2))
        return cb_ref[...] + cw_ref[0:1, :] * u2 + cw_ref[1:2, :] * u1 + cw_ref[2:3, :] * u

    prev_a, prev_v = ca_ref[...], cv_ref[...]
    for rows in _row_chunks(h_ref.shape[0]):
        hs = h_ref[rows, :]
        ua = jnp.dot(hs, wab_ref[...], preferred_element_type=F32)
        uv = jnp.dot(hs, wvb_ref[...], preferred_element_type=F32)
        a = conv(ua, prev_a, cwa_ref, cba_ref)
        val = conv(uv, prev_v, cwv_ref, cbv_ref)
        o_ref[rows, :] = (a * jax.nn.sigmoid(a) * val).astype(o_ref.dtype)
        prev_a, prev_v = ua[SUB_ROWS - 8:, :], uv[SUB_ROWS - 8:, :]
    ca_ref[...] = prev_a
    cv_ref[...] = prev_v


def _ffn_up(h2, w_up, conv_w, conv_b, T, tm=1024, tn=512):
    M, D = h2.shape
    F = w_up.shape[1] // 2
    nj = F // tn
    cb = conv_b.reshape(1, 2 * F)
    return pl.pallas_call(
        functools.partial(_ffn_up_kernel, per=T // tm),
        grid=(nj, M // tm),
        in_specs=[pl.BlockSpec((tm, D), lambda j, i: (i, 0)),
                  pl.BlockSpec((D, tn), lambda j, i: (0, j)),
                  pl.BlockSpec((D, tn), lambda j, i: (0, nj + j)),
                  pl.BlockSpec((3, tn), lambda j, i: (0, j)),
                  pl.BlockSpec((3, tn), lambda j, i: (0, nj + j)),
                  pl.BlockSpec((1, tn), lambda j, i: (0, j)),
                  pl.BlockSpec((1, tn), lambda j, i: (0, nj + j))],
        out_specs=pl.BlockSpec((tm, tn), lambda j, i: (i, j)),
        out_shape=jax.ShapeDtypeStruct((M, F), BF16),
        scratch_shapes=[pltpu.VMEM((D, tn), BF16), pltpu.VMEM((D, tn), BF16),
                        pltpu.VMEM((8, tn), F32), pltpu.VMEM((8, tn), F32)],
        compiler_params=_cparams(2),
        name="ffn_up",
    )(h2, w_up, w_up, conv_w, conv_w, cb, cb)


def _ffn_down_kernel(a_ref, w_ref, x_ref, g2_ref, o_ref, wb_ref):
    @pl.when(pl.program_id(1) == 0)
    def _():
        wb_ref[...] = w_ref[...].astype(BF16)

    for rows in _row_chunks(a_ref.shape[0]):
        y = jnp.dot(a_ref[rows, :], wb_ref[...], preferred_element_type=F32)
        o_ref[rows, :] = x_ref[rows, :] + g2_ref[0] * y


def _ffn_down(act, w_down, x1, mod3, T, tm=512, tn=512):
    M, F = act.shape
    D = w_down.shape[1]
    per = T // tm
    return pl.pallas_call(
        _ffn_down_kernel,
        grid=(D // tn, M // tm),
        in_specs=[pl.BlockSpec((tm, F), lambda j, i: (i, 0)),
                  pl.BlockSpec((F, tn), lambda j, i: (0, j)),
                  pl.BlockSpec((tm, tn), lambda j, i: (i, j)),
                  pl.BlockSpec((1, 1, tn), lambda j, i: ((i // per) * 6 + 5, 0, j))],
        out_specs=pl.BlockSpec((tm, tn), lambda j, i: (i, j)),
        out_shape=jax.ShapeDtypeStruct((M, D), F32),
        scratch_shapes=[pltpu.VMEM((F, tn), BF16)],
        compiler_params=_cparams(2),
        name="ffn_down",
    )(act, w_down, x1, mod3)


def _layer(x, c, w_ada, b_ada, norm1_gain, norm2_gain, w_in, nsa_q_gain, nsa_k_gain, cmp_pe, cmp_w1, cmp_w2,
           diff_q_gain, diff_k_gain, diff_lambda_q, diff_lambda_k, diff_subln_gain, w_nsa_out, w_diff_out, w_o,
           w_ffn_up, ffn_conv_w, ffn_conv_b, w_ffn_down, rel_bias):
    B, T, D = x.shape
    dk, G = HEAD_DIM, NSA_KV_GROUPS
    M = B * T
    scale = dk ** -0.5

    n_q = NSA_HEADS * dk
    o_kv = n_q
    o_g = o_kv + 3 * 2 * G * dk
    o_dq = o_g + NSA_HEADS * 3
    o_dk = o_dq + DIFF_HEADS * 2 * dk
    o_dv = o_dk + DIFF_HEADS * 2 * dk
    o_mg = o_dv + DIFF_HEADS * 2 * dk
    n_kv = 2 * G * dk
    n_dqk = 2 * DIFF_HEADS * 2 * dk

    mod3 = _ada(c, w_ada, b_ada).reshape(B * 6, 1, D)
    h = _norm1(x, norm1_gain, mod3).reshape(M, D)

    ones = jnp.ones((n_kv // 2,), F32)
    g_q = jnp.tile(nsa_q_gain * (scale * LOG2E), NSA_HEADS).reshape(1, n_q)
    g_kv = jnp.concatenate([jnp.tile(nsa_k_gain[1], G), ones, jnp.tile(nsa_k_gain[2], G), ones]).reshape(1, 2 * n_kv)
    g_dqk = jnp.concatenate([jnp.tile(diff_q_gain * (scale * LOG2E), 2 * DIFF_HEADS),
                             jnp.tile(diff_k_gain, 2 * DIFF_HEADS)]).reshape(1, n_dqk)
    g_one = jnp.ones((1, 2 * D), F32)
    proj = functools.partial(_inproj, h, jnp.swapaxes(w_in, 1, 2), tm=1024)
    qn = proj(g_q, src0=0, ncols=n_q, n_norm=512, mode="raw", out_dtype=BF16, tn=512, name="inproj_q")
    cmpkv = proj(g_one, src0=o_kv, ncols=n_kv, n_norm=0, mode="raw", out_dtype=F32, tn=512, name="inproj_cmp")
    kv = proj(g_kv, src0=o_kv + n_kv, ncols=2 * n_kv, n_norm=n_kv // 2, mode="raw", out_dtype=BF16, tn=512,
              name="inproj_kv")
    gates = proj(g_one, src0=o_g, ncols=LANE, n_norm=0, mode="sigmoid", out_dtype=F32, tn=LANE, name="inproj_gate")
    dqk = proj(g_dqk, src0=o_dq, ncols=n_dqk, n_norm=512, mode="raw", out_dtype=BF16, tn=512, name="inproj_dqk")
    dv = proj(g_one, src0=o_dv, ncols=o_mg - o_dv, n_norm=0, mode="raw", out_dtype=BF16, tn=512, name="inproj_dv")
    mgate = proj(g_one, src0=o_mg, ncols=2 * D, n_norm=0, mode="sigmoid", out_dtype=BF16, tn=512, name="inproj_mg")

    nbias = _bias_tiles(_window_idx(QB), rel_bias, head0=0, groups=G, hpg=NSA_HPG, name="bias_nsa",
                        rel=True, mult=LOG2E)
    cmp_idx, cmp_u0, cmp_per = _cmp_idx(T)
    cbias = _bias_tiles(cmp_idx, rel_bias, head0=0, groups=G, hpg=NSA_HPG, name="bias_cmp", mult=LOG2E)
    dbias = _bias_tiles(_causal_idx(DQB), rel_bias, head0=NSA_HEADS, groups=DIFF_HEADS, hpg=1, name="bias_diff",
                        rel=True, mult=LOG2E)

    kc, vc = _compress(cmpkv.reshape(B, T, n_kv), cmp_pe, cmp_w1, cmp_w2, nsa_k_gain[0])
    o_nsa = _nsa(qn.reshape(B, T, n_q), kv.reshape(B, T, 2 * n_kv), kc, vc, gates.reshape(B, T, LANE), cbias,
                 cmp_u0, cmp_per, nbias)
    o_diff = _diff(dqk.reshape(B, T, n_dqk), dv.reshape(B, T, -1), diff_lambda_q, diff_lambda_k, diff_subln_gain,
                   dbias)

    merged = _merge(o_nsa.reshape(M, -1), o_diff.reshape(M, -1), w_nsa_out, w_diff_out, mgate)
    x1, h2 = _oproj(merged, w_o.astype(BF16), x.reshape(M, D), mod3, norm2_gain, T)
    act = _ffn_up(h2, w_ffn_up, ffn_conv_w, ffn_conv_b, T)
    out = _ffn_down(act, w_ffn_down, x1, mod3, T)
    return out.reshape(B, T, D)


def kernel(x, c, w_ada, b_ada, norm1_gain, norm2_gain, w_in, nsa_q_gain, nsa_k_gain, cmp_pe, cmp_w1, cmp_w2,
           diff_q_gain, diff_k_gain, diff_lambda_q, diff_lambda_k, diff_subln_gain, w_nsa_out, w_diff_out, w_o,
           w_ffn_up, ffn_conv_w, ffn_conv_b, w_ffn_down, rel_bias):
    return _layer(x, c, w_ada[0], b_ada[0], norm1_gain[0], norm2_gain[0], w_in, nsa_q_gain[0], nsa_k_gain[0],
                  cmp_pe[0], cmp_w1[0], cmp_w2[0], diff_q_gain[0], diff_k_gain[0], diff_lambda_q[0],
                  diff_lambda_k[0], diff_subln_gain[0], w_nsa_out[0], w_diff_out[0], w_o[0], w_ffn_up[0],
                  ffn_conv_w[0], ffn_conv_b[0], w_ffn_down[0], rel_bias)
```

```python
import functools
import math

import numpy as np
import jax
import jax.numpy as jnp
from jax import lax
from jax.experimental import pallas as pl
from jax.experimental.pallas import tpu as pltpu

F32 = jnp.float32
BF16 = jnp.bfloat16

HEAD_DIM = 128
NSA_HEADS = 8
NSA_KV_GROUPS = 2
NSA_HPG = NSA_HEADS // NSA_KV_GROUPS
CMP_LEN = 32
CMP_STRIDE = 16
SEL_BLOCK = 64
N_SEL = 16
WINDOW = 512
DIFF_HEADS = 4
NUM_BUCKETS = 32
MAX_DISTANCE = 128
EPS = 1e-6
NEG = -1e30
LAM_INIT = 0.8 - 0.6 * math.exp(-0.3 * 0)
LOG2E = math.log2(math.e)
MASK_BIG = -(2.0 ** 100)

LANE = 128
QB = 128
DQB = 256
SUB_ROWS = 256
INPROJ_TM = 2048
INPROJ_TN = 1024
VMEM_LIMIT = 56 * 1024 * 1024


def _cparams(n_axes):
    return pltpu.CompilerParams(dimension_semantics=("arbitrary",) * n_axes,
                                vmem_limit_bytes=VMEM_LIMIT)


def _t5_bucket_np(dist):
    n = np.maximum(np.asarray(dist, np.int32), 0)
    max_exact = NUM_BUCKETS // 2
    nf = np.maximum(n, max_exact).astype(np.float32)
    large = max_exact + (np.log(nf / np.float32(max_exact)) / np.float32(math.log(MAX_DISTANCE / max_exact))
                         * np.float32(NUM_BUCKETS - max_exact)).astype(np.int32)
    large = np.minimum(large, NUM_BUCKETS - 1)
    return np.where(n < max_exact, n, large).astype(np.int32)


def _ada_kernel(ct_ref, w_ref, b_ref, o_ref):
    ct = ct_ref[...]
    s = ct * jax.nn.sigmoid(ct)
    w = w_ref[...]
    for b in range(ct.shape[1]):
        o_ref[b:b + 1, :] = jnp.sum(w * s[:, b:b + 1], axis=0, keepdims=True) + b_ref[...]


def _ada(c, w_ada, b_ada, tn=512):
    B, D = c.shape
    N = w_ada.shape[1]
    return pl.pallas_call(
        _ada_kernel,
        grid=(N // tn,),
        in_specs=[pl.BlockSpec((D, B), lambda j: (0, 0)),
                  pl.BlockSpec((D, tn), lambda j: (0, j)),
                  pl.BlockSpec((1, tn), lambda j: (0, j))],
        out_specs=pl.BlockSpec((B, tn), lambda j: (0, j)),
        out_shape=jax.ShapeDtypeStruct((B, N), F32),
        compiler_params=_cparams(1),
        name="ada",
    )(c.T, w_ada, b_ada.reshape(1, N))


def _modnorm(x, gain, sc, sh):
    ms = jnp.mean(x * x, axis=-1, keepdims=True)
    return (x * lax.rsqrt(ms + EPS) * gain) * (1.0 + sc) + sh


def _norm1_kernel(x_ref, g_ref, sc_ref, sh_ref, o_ref):
    o_ref[0] = _modnorm(x_ref[0], g_ref[...], sc_ref[0], sh_ref[0]).astype(o_ref.dtype)


def _norm1(x, gain, mod3, tm=512):
    B, T, D = x.shape
    return pl.pallas_call(
        _norm1_kernel,
        grid=(B, T // tm),
        in_specs=[pl.BlockSpec((1, tm, D), lambda b, i: (b, i, 0)),
                  pl.BlockSpec((1, D), lambda b, i: (0, 0)),
                  pl.BlockSpec((1, 1, D), lambda b, i: (b * 6 + 1, 0, 0)),
                  pl.BlockSpec((1, 1, D), lambda b, i: (b * 6 + 0, 0, 0))],
        out_specs=pl.BlockSpec((1, tm, D), lambda b, i: (b, i, 0)),
        out_shape=jax.ShapeDtypeStruct((B, T, D), BF16),
        compiler_params=_cparams(2),
        name="norm1",
    )(x, gain.reshape(1, D), mod3, mod3)


def _row_chunks(tm):
    return [slice(r, r + SUB_ROWS) for r in range(0, tm, SUB_ROWS)]


def _inproj_kernel(a_ref, wt_ref, g_ref, o_ref, wb_ref, *, heads, mode):
    @pl.when(pl.program_id(1) == 0)
    def _():
        wb_ref[...] = wt_ref[0].astype(BF16)

    for rows in _row_chunks(a_ref.shape[0]):
        acc = _qk(a_ref[rows, :], wb_ref[...])
        for k, is_head in enumerate(heads):
            lanes = slice(k * LANE, (k + 1) * LANE)
            y = acc[:, lanes]
            if is_head:
                ms = jnp.mean(y * y, axis=-1, keepdims=True)
                y = y * lax.rsqrt(ms + EPS) * g_ref[:, lanes]
            elif mode == "sigmoid":
                y = jax.nn.sigmoid(y)
            o_ref[rows, lanes] = y.astype(o_ref.dtype)


def _inproj(h2d, w_in_t, gains, *, src0, ncols, heads, mode, out_dtype, tm, tn, name):
    M, K = h2d.shape
    assert src0 % 8 == 0 and ncols % tn == 0 and len(heads) == tn // LANE
    return pl.pallas_call(
        functools.partial(_inproj_kernel, heads=heads, mode=mode),
        grid=(ncols // tn, M // tm),
        in_specs=[pl.BlockSpec((tm, K), lambda j, i: (i, 0)),
                  pl.BlockSpec((pl.Element(1), pl.Element(tn), pl.Element(K)),
                               lambda j, i: (0, pl.multiple_of(src0 + j * tn, 8), 0)),
                  pl.BlockSpec((1, tn), lambda j, i: (0, j))],
        out_specs=pl.BlockSpec((tm, tn), lambda j, i: (i, j)),
        out_shape=jax.ShapeDtypeStruct((M, ncols), out_dtype),
        scratch_shapes=[pltpu.VMEM((tn, K), BF16)],
        compiler_params=_cparams(2),
        name=name,
    )(h2d, w_in_t, gains)


def _bias_kernel(idx_ref, tab_ref, o_ref, *, head0, hpg, rel, mult):
    head = head0 + pl.program_id(0) * hpg + pl.program_id(2)
    idx = idx_ref[0]
    acc = jnp.zeros(idx.shape, F32)
    for b in range(NUM_BUCKETS):
        acc = jnp.where(idx == b, tab_ref[b, head], acc)
    if rel:
        acc = acc - tab_ref[NUM_BUCKETS - 1, head]
    o_ref[0, 0, 0] = jnp.where(idx < 0, NEG, acc * mult)


def _bias_tiles(idx, rel_bias, *, head0, groups, hpg, name, rel=False, mult=1.0):
    N, R, C = idx.shape
    return pl.pallas_call(
        functools.partial(_bias_kernel, head0=head0, hpg=hpg, rel=rel, mult=mult),
        grid=(groups, N, hpg),
        in_specs=[pl.BlockSpec((1, R, C), lambda g, n, h: (n, 0, 0)),
                  pl.BlockSpec(memory_space=pltpu.SMEM)],
        out_specs=pl.BlockSpec((1, 1, 1, R, C), lambda g, n, h: (g, n, h, 0, 0)),
        out_shape=jax.ShapeDtypeStruct((groups, N, hpg, R, C), F32),
        compiler_params=_cparams(3),
        name=name,
    )(jnp.asarray(idx), rel_bias)


def _causal_idx(R):
    r = np.arange(R)[:, None]
    c = np.arange(R)[None, :]
    return np.stack([np.where(r >= c, _t5_bucket_np(r - c), -1), _t5_bucket_np(R + r - c)]).astype(np.int32)


def _window_idx(R):
    r = np.arange(R)[:, None]
    c = np.arange(R)[None, :]
    edge = np.where(r < c, NUM_BUCKETS - 1, -1)
    return np.concatenate([_causal_idx(R), edge[None]]).astype(np.int32)


def _cmp_idx(T):
    per = QB // CMP_STRIDE
    u0 = (T // QB - 1) * per
    assert u0 + LANE <= 2 * LANE
    r = np.arange(QB)[:, None]
    end = (np.arange(2 * LANE)[None, :] - u0) * CMP_STRIDE + CMP_LEN - 1
    return np.where(end <= r, _t5_bucket_np(r - end), -1).astype(np.int32)[None], u0, per


def _compress_kernel(zk_ref, zv_ref, pe_ref, w1_ref, w2_ref, kg_ref, kc_ref, vc_ref):
    half = CMP_LEN // 2

    def one(z_ref, i):
        p1 = jnp.zeros((LANE, HEAD_DIM), F32)
        p2 = jnp.zeros((LANE, HEAD_DIM), F32)
        for l in range(half):
            z = z_ref[0, pl.ds(l, LANE, stride=CMP_STRIDE), :]
            w_lo = w1_ref[i, l * HEAD_DIM:(l + 1) * HEAD_DIM, :].astype(BF16)
            w_hi = w1_ref[i, (half + l) * HEAD_DIM:(half + l + 1) * HEAD_DIM, :].astype(BF16)
            p1 = p1 + jnp.dot((z + pe_ref[i, l:l + 1, :]).astype(BF16), w_lo, preferred_element_type=F32)
            p2 = p2 + jnp.dot((z + pe_ref[i, half + l:half + l + 1, :]).astype(BF16), w_hi,
                              preferred_element_type=F32)
        pre = p1 + pltpu.roll(p2, LANE - 1, axis=0)
        hid = jax.nn.gelu(pre)
        return jnp.dot(hid.astype(BF16), w2_ref[i].astype(BF16), preferred_element_type=F32)

    kc = one(zk_ref, 0)
    ms = jnp.mean(kc * kc, axis=-1, keepdims=True)
    kc_ref[0, 0] = (kc * lax.rsqrt(ms + EPS) * kg_ref[...]).astype(kc_ref.dtype)
    vc_ref[0, 0] = one(zv_ref, 1).astype(vc_ref.dtype)


def _compress(cmp_kv, pe, w1, w2, k_gain0):
    B, T, _ = cmp_kv.shape
    G, dk = NSA_KV_GROUPS, HEAD_DIM
    assert (T - CMP_LEN) // CMP_STRIDE + 1 == LANE - 1
    out = jax.ShapeDtypeStruct((B, G, LANE, dk), BF16)
    return pl.pallas_call(
        _compress_kernel,
        grid=(B, G),
        in_specs=[pl.BlockSpec((1, T, dk), lambda b, g: (b, 0, g)),
                  pl.BlockSpec((1, T, dk), lambda b, g: (b, 0, G + g)),
                  pl.BlockSpec((2, CMP_LEN, dk), lambda b, g: (0, 0, 0)),
                  pl.BlockSpec((2, CMP_LEN * dk, dk), lambda b, g: (0, 0, 0)),
                  pl.BlockSpec((2, dk, dk), lambda b, g: (0, 0, 0)),
                  pl.BlockSpec((1, dk), lambda b, g: (0, 0))],
        out_specs=[pl.BlockSpec((1, 1, LANE, dk), lambda b, g: (b, g, 0, 0))] * 2,
        out_shape=[out, out],
        compiler_params=_cparams(2),
        name="compress",
    )(cmp_kv, cmp_kv, pe, w1, w2, k_gain0.reshape(1, dk))


def _qk(q, k):
    return lax.dot_general(q, k, (((1,), (1,)), ((), ())), preferred_element_type=F32)


def _lane_fold(x, op):
    acc = x[..., :LANE]
    for t in range(1, x.shape[-1] // LANE):
        acc = op(acc, x[..., t * LANE:(t + 1) * LANE])
    return acc


def _softmax_parts(parts):
    m = jnp.max(functools.reduce(jnp.maximum, [_lane_fold(s, jnp.maximum) for s in parts]), axis=-1, keepdims=True)
    ps = [jnp.exp2(s - m) for s in parts]
    den = jnp.sum(functools.reduce(jnp.add, [_lane_fold(p, jnp.add) for p in ps]), axis=-1, keepdims=True)
    return ps, den


def _nsa_kernel(q_ref, ks_ref, vs_ref, kw_ref, vw_ref, kc_ref, vc_ref, gate_ref, cb_ref, nb_ref,
                ovt_ref, exp_ref, o_ref, *, cmp_u0, cmp_per):
    H, R = NSA_HPG, NSA_HPG * QB
    T = ks_ref.shape[1]
    ns = ovt_ref.shape[0]
    nwb = WINDOW // QB
    row = lax.broadcasted_iota(jnp.int32, (QB, LANE), 0)
    col = lax.broadcasted_iota(jnp.int32, (QB, LANE), 1)
    blk = lax.broadcasted_iota(jnp.int32, (ns, QB), 0)
    tl = lax.broadcasted_iota(jnp.int32, (ns, QB), 1)
    kc = kc_ref[0, 0]
    vc = vc_ref[0, 0]

    def attend(qs, k_ref, v_ref, spans):
        parts = []
        for a, b, add in spans:
            s = _qk(qs, k_ref[0, a:b, :]).reshape(H, QB, b - a)
            parts.append(s if add is None else s + add)
        ps, den = _softmax_parts(parts)
        o = None
        for (a, b, _), p in zip(spans, ps):
            pv = jnp.dot(p.reshape(R, b - a).astype(BF16), v_ref[0, a:b, :], preferred_element_type=F32)
            o = pv if o is None else o + pv
        return o / den.reshape(R, 1)

    def block(i):
        lo, hi = i * QB, (i + 1) * QB
        q = q_ref[0, lo:hi, :]
        qs = jnp.concatenate([q[:, h * HEAD_DIM:(h + 1) * HEAD_DIM] for h in range(H)], axis=0)

        u = cmp_u0 - cmp_per * i
        s = _qk(qs, kc).reshape(H, QB, LANE) + cb_ref[0, 0][:, :, u:u + LANE]
        e = jnp.exp2(s - jnp.max(s, axis=-1, keepdims=True))
        if i == 0:
            valid = (col * CMP_STRIDE + (CMP_LEN - 1)) <= row
            e = jnp.where(valid[None], e, 0.0)
            den = jnp.sum(e, axis=-1, keepdims=True)
            p = e / jnp.where(den > 0.0, den, 1.0)
        else:
            p = e / jnp.sum(e, axis=-1, keepdims=True)
        o_cmp = jnp.dot(p.reshape(R, LANE).astype(BF16), vc, preferred_element_type=F32)

        slc_spans = [(lo, hi, nb_ref[0, 0])]
        if i >= 1:
            psum = p[0] + p[1] + p[2] + p[3]
            imp_t = lax.dot_general(ovt_ref[...], psum, (((1,), (1,)), ((), ())),
                                    precision=lax.Precision.HIGHEST, preferred_element_type=F32)
            cur = (lo + tl) // SEL_BLOCK
            forced = (blk == 0) | (blk == cur) | (blk == cur - 1)
            score = jnp.where(forced, 1e4, jnp.where(blk <= cur, imp_t, -1e4))
            rank = jnp.zeros((ns, QB), F32)
            for b in range(ns):
                other = score[b:b + 1, :]
                rank = rank + jnp.where(blk > b, jnp.where(other >= score, 1.0, 0.0),
                                        jnp.where(other > score, 1.0, 0.0))
            unsel_t = jnp.where(rank < float(min(N_SEL, ns)), 0.0, 1.0)
            unsel = jnp.concatenate([unsel_t, jnp.zeros((LANE - ns, QB), F32)], axis=0).T
            drop = jnp.dot(unsel.astype(BF16), exp_ref[:, 0:lo], preferred_element_type=F32)
            slc_spans.insert(0, (lo - QB, lo, nb_ref[0, 1] + drop[None, :, lo - QB:lo]))
            if i >= 2:
                slc_spans.insert(0, (0, lo - QB, drop[None, :, 0:lo - QB]))

        o_slc = attend(qs, ks_ref, vs_ref, slc_spans)

        win_spans = []
        if i >= nwb:
            win_spans.append(((i - nwb) * QB, (i - nwb + 1) * QB, nb_ref[0, 2]))
        mid_a, mid_b = max(i - nwb + 1, 0) * QB, (i - 1) * QB
        if mid_b > mid_a:
            win_spans.append((mid_a, mid_b, None))
        if i >= 1:
            win_spans.append((lo - QB, lo, nb_ref[0, 1]))
        win_spans.append((lo, hi, nb_ref[0, 0]))
        o_win = attend(qs, kw_ref, vw_ref, win_spans)

        gate = gate_ref[0, lo:hi, :]
        first = pl.program_id(1) == 0

        def gcol(h, br):
            c = 3 * h + br
            return jnp.where(first, gate[:, c:c + 1], gate[:, 3 * H + c:3 * H + c + 1])

        outs = []
        for h in range(H):
            sl = slice(h * QB, (h + 1) * QB)
            outs.append(gcol(h, 0) * o_cmp[sl] + gcol(h, 1) * o_slc[sl] + gcol(h, 2) * o_win[sl])
        o_ref[0, lo:hi, :] = jnp.concatenate(outs, axis=1).astype(o_ref.dtype)

    nb = T // QB

    def pair(k):
        block(k)
        block(nb - 1 - k)

    for k in range(nb // 2):
        pl.when(pl.program_id(2) == k)(functools.partial(pair, k))


def _nsa(qn, kv, kc, vc, gates, cbias, cmp_u0, cmp_per, nbias):
    B, T, _ = qn.shape
    G, H, dk = NSA_KV_GROUPS, NSA_HPG, HEAD_DIM
    nb = T // QB
    ns = T // SEL_BLOCK
    cstart = np.arange(LANE) * CMP_STRIDE
    sstart = np.arange(ns) * SEL_BLOCK
    overlap = np.clip(np.minimum(cstart[:, None] + CMP_LEN, sstart[None, :] + SEL_BLOCK)
                      - np.maximum(cstart[:, None], sstart[None, :]), 0, None) / CMP_STRIDE
    overlap[LANE - 1:] = 0.0
    ovt = jnp.asarray(overlap.T, F32)
    expand = np.zeros((LANE, T), np.float32)
    expand[np.arange(T) // SEL_BLOCK, np.arange(T)] = MASK_BIG
    return pl.pallas_call(
        functools.partial(_nsa_kernel, cmp_u0=cmp_u0, cmp_per=cmp_per),
        grid=(B, G, nb // 2),
        in_specs=[pl.BlockSpec((1, T, H * dk), lambda b, g, i: (b, 0, g)),
                  pl.BlockSpec((1, T, dk), lambda b, g, i: (b, 0, g)),
                  pl.BlockSpec((1, T, dk), lambda b, g, i: (b, 0, G + g)),
                  pl.BlockSpec((1, T, dk), lambda b, g, i: (b, 0, 2 * G + g)),
                  pl.BlockSpec((1, T, dk), lambda b, g, i: (b, 0, 3 * G + g)),
                  pl.BlockSpec((1, 1, LANE, dk), lambda b, g, i: (b, g, 0, 0)),
                  pl.BlockSpec((1, 1, LANE, dk), lambda b, g, i: (b, g, 0, 0)),
                  pl.BlockSpec((1, T, LANE), lambda b, g, i: (b, 0, 0)),
                  pl.BlockSpec((1, 1, H, QB, 2 * LANE), lambda b, g, i: (g, 0, 0, 0, 0)),
                  pl.BlockSpec((1, 3, H, QB, QB), lambda b, g, i: (g, 0, 0, 0, 0)),
                  pl.BlockSpec((ns, LANE), lambda b, g, i: (0, 0)),
                  pl.BlockSpec((LANE, T), lambda b, g, i: (0, 0))],
        out_specs=pl.BlockSpec((1, T, H * dk), lambda b, g, i: (b, 0, g)),
        out_shape=jax.ShapeDtypeStruct((B, T, NSA_HEADS * dk), BF16),
        compiler_params=_cparams(3),
        name="nsa",
    )(qn, kv, kv, kv, kv, kc, vc, gates, cbias, nbias, ovt, jnp.asarray(expand, BF16))


def _diff_kernel(q_ref, k_ref, v_ref, lq_ref, lk_ref, sg_ref, db_ref, o_ref):
    dk = HEAD_DIM
    T = k_ref.shape[1]
    lqk = lq_ref[...] * lk_ref[...]
    lam = (jnp.exp(jnp.sum(lqk[0:1], axis=-1, keepdims=True))
           - jnp.exp(jnp.sum(lqk[1:2], axis=-1, keepdims=True)) + LAM_INIT)
    for i in range(T // DQB):
        lo, hi = i * DQB, (i + 1) * DQB
        halves = []
        for mm in range(2):
            cols = slice(mm * dk, (mm + 1) * dk)
            q = q_ref[0, lo:hi, cols]
            bounds, parts = [], []
            if i >= 2:
                bounds.append((0, lo - DQB))
                parts.append(_qk(q, k_ref[0, 0:lo - DQB, cols]))
            if i >= 1:
                bounds.append((lo - DQB, lo))
                parts.append(_qk(q, k_ref[0, lo - DQB:lo, cols]) + db_ref[0, 1, 0])
            bounds.append((lo, hi))
            parts.append(_qk(q, k_ref[0, lo:hi, cols]) + db_ref[0, 0, 0])
            ps, den = _softmax_parts(parts)
            o = None
            for (a, b), p in zip(bounds, ps):
                pv = jnp.dot(p.astype(BF16), v_ref[0, a:b, :], preferred_element_type=F32)
                o = pv if o is None else o + pv
            halves.append(o / den)
        o = halves[0] - lam * halves[1]
        ms = jnp.mean(o * o, axis=-1, keepdims=True)
        o_ref[0, lo:hi, :] = ((o * lax.rsqrt(ms + EPS) * sg_ref[...]) * (1.0 - LAM_INIT)).astype(o_ref.dtype)


def _diff(dqk, dv, lam_q, lam_k, subln_gain, dbias):
    B, T, _ = dqk.shape
    Hd, dk = DIFF_HEADS, HEAD_DIM
    w = 2 * dk
    return pl.pallas_call(
        _diff_kernel,
        grid=(B, Hd),
        in_specs=[pl.BlockSpec((1, T, w), lambda b, h: (b, 0, h)),
                  pl.BlockSpec((1, T, w), lambda b, h: (b, 0, Hd + h)),
                  pl.BlockSpec((1, T, w), lambda b, h: (b, 0, h)),
                  pl.BlockSpec((2, dk), lambda b, h: (0, 0)),
                  pl.BlockSpec((2, dk), lambda b, h: (0, 0)),
                  pl.BlockSpec((1, w), lambda b, h: (0, 0)),
                  pl.BlockSpec((1, 2, 1, DQB, DQB), lambda b, h: (h, 0, 0, 0, 0))],
        out_specs=pl.BlockSpec((1, T, w), lambda b, h: (b, 0, h)),
        out_shape=jax.ShapeDtypeStruct((B, T, Hd * w), BF16),
        compiler_params=_cparams(2),
        name="diff",
    )(dqk, dqk, dv, lam_q, lam_k, subln_gain.reshape(1, w), dbias)


def _merge_kernel(an_ref, ad_ref, wn_ref, wd_ref, gn_ref, gd_ref, o_ref, wnb_ref, wdb_ref):
    @pl.when(pl.program_id(1) == 0)
    def _():
        wnb_ref[...] = wn_ref[...].astype(BF16)
        wdb_ref[...] = wd_ref[...].astype(BF16)

    for rows in _row_chunks(an_ref.shape[0]):
        yn = jnp.dot(an_ref[rows, :], wnb_ref[...], preferred_element_type=F32)
        yd = jnp.dot(ad_ref[rows, :], wdb_ref[...], preferred_element_type=F32)
        o_ref[rows, :] = (gn_ref[rows, :].astype(F32) * yn + gd_ref[rows, :].astype(F32) * yd).astype(o_ref.dtype)


def _merge(o_nsa, o_diff, w_n, w_d, mg, tm=1024, tn=1024):
    M, K = o_nsa.shape
    N = w_n.shape[1]
    nj = N // tn
    return pl.pallas_call(
        _merge_kernel,
        grid=(nj, M // tm),
        in_specs=[pl.BlockSpec((tm, K), lambda j, i: (i, 0)),
                  pl.BlockSpec((tm, K), lambda j, i: (i, 0)),
                  pl.BlockSpec((K, tn), lambda j, i: (0, j)),
                  pl.BlockSpec((K, tn), lambda j, i: (0, j)),
                  pl.BlockSpec((tm, tn), lambda j, i: (i, j)),
                  pl.BlockSpec((tm, tn), lambda j, i: (i, nj + j))],
        out_specs=pl.BlockSpec((tm, tn), lambda j, i: (i, j)),
        out_shape=jax.ShapeDtypeStruct((M, N), BF16),
        scratch_shapes=[pltpu.VMEM((K, tn), BF16), pltpu.VMEM((K, tn), BF16)],
        compiler_params=_cparams(2),
        name="merge",
    )(o_nsa, o_diff, w_n, w_d, mg, mg)


def _oproj_kernel(a_ref, w_ref, x_ref, g1_ref, gain_ref, sc_ref, sh_ref, x1_ref, h2_ref):
    for rows in _row_chunks(a_ref.shape[0]):
        y = jnp.dot(a_ref[rows, :], w_ref[...], preferred_element_type=F32)
        x1 = x_ref[rows, :] + g1_ref[0] * y
        x1_ref[rows, :] = x1
        h2_ref[rows, :] = _modnorm(x1, gain_ref[...], sc_ref[0], sh_ref[0]).astype(h2_ref.dtype)


def _oproj(merged, w_o, x2d, mod3, gain2, T, tm=512):
    M, D = x2d.shape
    per = T // tm
    return pl.pallas_call(
        _oproj_kernel,
        grid=(M // tm,),
        in_specs=[pl.BlockSpec((tm, D), lambda i: (i, 0)),
                  pl.BlockSpec((D, D), lambda i: (0, 0)),
                  pl.BlockSpec((tm, D), lambda i: (i, 0)),
                  pl.BlockSpec((1, 1, D), lambda i: ((i // per) * 6 + 2, 0, 0)),
                  pl.BlockSpec((1, D), lambda i: (0, 0)),
                  pl.BlockSpec((1, 1, D), lambda i: ((i // per) * 6 + 4, 0, 0)),
                  pl.BlockSpec((1, 1, D), lambda i: ((i // per) * 6 + 3, 0, 0))],
        out_specs=[pl.BlockSpec((tm, D), lambda i: (i, 0)),
                   pl.BlockSpec((tm, D), lambda i: (i, 0))],
        out_shape=[jax.ShapeDtypeStruct((M, D), F32), jax.ShapeDtypeStruct((M, D), BF16)],
        compiler_params=_cparams(1),
        name="oproj",
    )(merged, w_o, x2d, mod3, gain2.reshape(1, D), mod3, mod3)


def _ffn_up_kernel(h_ref, wa_ref, wv_ref, cwa_ref, cwv_ref, cba_ref, cbv_ref, o_ref, wab_ref, wvb_ref,
                   ca_ref, cv_ref, *, per):
    i = pl.program_id(1)
    row = lax.broadcasted_iota(jnp.int32, (SUB_ROWS, 1), 0)

    @pl.when(i == 0)
    def _():
        wab_ref[...] = wa_ref[...].astype(BF16)
        wvb_ref[...] = wv_ref[...].astype(BF16)

    @pl.when(i % per == 0)
    def _():
        ca_ref[...] = jnp.zeros(ca_ref.shape, F32)
        cv_ref[...] = jnp.zeros(cv_ref.shape, F32)

    def conv(u, prev, cw_ref, cb_ref):
        u1 = jnp.where(row < 1, pltpu.roll(prev, 1, axis=0)[0:1, :], pltpu.roll(u, 1, axis=0))
        p2 = pltpu.roll(prev, 2, axis=0)
        u2 = pltpu.roll(u, 2, axis=0)
        u2 = jnp.where(row < 1, p2[0:1, :], jnp.where(row < 2, p2[1:2, :], u2))
        return cb_ref[...] + cw_ref[0:1, :] * u2 + cw_ref[1:2, :] * u1 + cw_ref[2:3, :] * u

    prev_a, prev_v = ca_ref[...], cv_ref[...]
    for rows in _row_chunks(h_ref.shape[0]):
        hs = h_ref[rows, :]
        ua = jnp.dot(hs, wab_ref[...], preferred_element_type=F32)
        uv = jnp.dot(hs, wvb_ref[...], preferred_element_type=F32)
        a = conv(ua, prev_a, cwa_ref, cba_ref)
        val = conv(uv, prev_v, cwv_ref, cbv_ref)
        o_ref[rows, :] = (a * jax.nn.sigmoid(a) * val).astype(o_ref.dtype)
        prev_a, prev_v = ua[SUB_ROWS - 8:, :], uv[SUB_ROWS - 8:, :]
    ca_ref[...] = prev_a
    cv_ref[...] = prev_v


def _ffn_up(h2, w_up, conv_w, conv_b, T, tm=2048, tn=512):
    M, D = h2.shape
    F = w_up.shape[1] // 2
    nj = F // tn
    cb = conv_b.reshape(1, 2 * F)
    return pl.pallas_call(
        functools.partial(_ffn_up_kernel, per=T // tm),
        grid=(nj, M // tm),
        in_specs=[pl.BlockSpec((tm, D), lambda j, i: (i, 0)),
                  pl.BlockSpec((D, tn), lambda j, i: (0, j)),
                  pl.BlockSpec((D, tn), lambda j, i: (0, nj + j)),
                  pl.BlockSpec((3, tn), lambda j, i: (0, j)),
                  pl.BlockSpec((3, tn), lambda j, i: (0, nj + j)),
                  pl.BlockSpec((1, tn), lambda j, i: (0, j)),
                  pl.BlockSpec((1, tn), lambda j, i: (0, nj + j))],
        out_specs=pl.BlockSpec((tm, tn), lambda j, i: (i, j)),
        out_shape=jax.ShapeDtypeStruct((M, F), BF16),
        scratch_shapes=[pltpu.VMEM((D, tn), BF16), pltpu.VMEM((D, tn), BF16),
                        pltpu.VMEM((8, tn), F32), pltpu.VMEM((8, tn), F32)],
        compiler_params=_cparams(2),
        name="ffn_up",
    )(h2, w_up, w_up, conv_w, conv_w, cb, cb)


def _ffn_down_kernel(a_ref, w_ref, x_ref, g2_ref, o_ref, wb_ref):
    @pl.when(pl.program_id(1) == 0)
    def _():
        wb_ref[...] = w_ref[...].astype(BF16)

    for rows in _row_chunks(a_ref.shape[0]):
        y = jnp.dot(a_ref[rows, :], wb_ref[...], preferred_element_type=F32)
        o_ref[rows, :] = x_ref[rows, :] + g2_ref[0] * y


def _ffn_down(act, w_down, x1, mod3, T, tm=512, tn=512):
    M, F = act.shape
    D = w_down.shape[1]
    per = T // tm
    return pl.pallas_call(
        _ffn_down_kernel,
        grid=(D // tn, M // tm),
        in_specs=[pl.BlockSpec((tm, F), lambda j, i: (i, 0)),
                  pl.BlockSpec((F, tn), lambda j, i: (0, j)),
                  pl.BlockSpec((tm, tn), lambda j, i: (i, j)),
                  pl.BlockSpec((1, 1, tn), lambda j, i: ((i // per) * 6 + 5, 0, j))],
        out_specs=pl.BlockSpec((tm, tn), lambda j, i: (i, j)),
        out_shape=jax.ShapeDtypeStruct((M, D), F32),
        scratch_shapes=[pltpu.VMEM((F, tn), BF16)],
        compiler_params=_cparams(2),
        name="ffn_down",
    )(act, w_down, x1, mod3)


def _layer(x, c, w_ada, b_ada, norm1_gain, norm2_gain, w_in, nsa_q_gain, nsa_k_gain, cmp_pe, cmp_w1, cmp_w2,
           diff_q_gain, diff_k_gain, diff_lambda_q, diff_lambda_k, diff_subln_gain, w_nsa_out, w_diff_out, w_o,
           w_ffn_up, ffn_conv_w, ffn_conv_b, w_ffn_down, rel_bias):
    B, T, D = x.shape
    dk, G = HEAD_DIM, NSA_KV_GROUPS
    M = B * T
    scale = dk ** -0.5

    n_q = NSA_HEADS * dk
    o_kv = n_q
    o_g = o_kv + 3 * 2 * G * dk
    o_dq = o_g + NSA_HEADS * 3
    o_dk = o_dq + DIFF_HEADS * 2 * dk
    o_dv = o_dk + DIFF_HEADS * 2 * dk
    o_mg = o_dv + DIFF_HEADS * 2 * dk
    n_kv = 2 * G * dk
    n_dqk = 2 * DIFF_HEADS * 2 * dk

    mod3 = _ada(c, w_ada, b_ada).reshape(B * 6, 1, D)
    h = _norm1(x, norm1_gain, mod3).reshape(M, D)

    ones = jnp.ones((n_kv // 2,), F32)
    g_q = jnp.tile(nsa_q_gain * (scale * LOG2E), NSA_HEADS).reshape(1, n_q)
    g_kv = jnp.concatenate([jnp.tile(nsa_k_gain[1], G), ones, jnp.tile(nsa_k_gain[2], G), ones]).reshape(1, 2 * n_kv)
    g_dqk = jnp.concatenate([jnp.tile(diff_q_gain * (scale * LOG2E), 2 * DIFF_HEADS),
                             jnp.tile(diff_k_gain, 2 * DIFF_HEADS)]).reshape(1, n_dqk)
    g_one = jnp.ones((1, 2 * D), F32)
    proj = functools.partial(_inproj, h, jnp.swapaxes(w_in, 1, 2), tm=INPROJ_TM)
    wide = INPROJ_TN
    yes, no = (True,) * (wide // LANE), (False,) * (wide // LANE)
    qn = proj(g_q, src0=0, ncols=n_q, heads=yes, mode="raw", out_dtype=BF16, tn=wide, name="inproj_q")
    cmpkv = proj(g_one, src0=o_kv, ncols=n_kv, heads=(False,) * (n_kv // LANE), mode="raw", out_dtype=F32, tn=n_kv,
                 name="inproj_cmp")
    kv_heads = ((True,) * G + (False,) * G) * 2
    kv = proj(g_kv, src0=o_kv + n_kv, ncols=2 * n_kv, heads=kv_heads, mode="raw", out_dtype=BF16, tn=2 * n_kv,
              name="inproj_kv")
    gates = proj(g_one, src0=o_g, ncols=LANE, heads=(False,), mode="sigmoid", out_dtype=F32, tn=LANE,
                 name="inproj_gate")
    dqk = proj(g_dqk, src0=o_dq, ncols=n_dqk, heads=yes, mode="raw", out_dtype=BF16, tn=wide, name="inproj_dqk")
    dv = proj(g_one, src0=o_dv, ncols=o_mg - o_dv, heads=no, mode="raw", out_dtype=BF16, tn=wide, name="inproj_dv")
    mgate = proj(g_one, src0=o_mg, ncols=2 * D, heads=no, mode="sigmoid", out_dtype=BF16, tn=wide, name="inproj_mg")

    nbias = _bias_tiles(_window_idx(QB), rel_bias, head0=0, groups=G, hpg=NSA_HPG, name="bias_nsa",
                        rel=True, mult=LOG2E)
    cmp_idx, cmp_u0, cmp_per = _cmp_idx(T)
    cbias = _bias_tiles(cmp_idx, rel_bias, head0=0, groups=G, hpg=NSA_HPG, name="bias_cmp", mult=LOG2E)
    dbias = _bias_tiles(_causal_idx(DQB), rel_bias, head0=NSA_HEADS, groups=DIFF_HEADS, hpg=1, name="bias_diff",
                        rel=True, mult=LOG2E)

    kc, vc = _compress(cmpkv.reshape(B, T, n_kv), cmp_pe, cmp_w1, cmp_w2, nsa_k_gain[0])
    o_nsa = _nsa(qn.reshape(B, T, n_q), kv.reshape(B, T, 2 * n_kv), kc, vc, gates.reshape(B, T, LANE), cbias,
                 cmp_u0, cmp_per, nbias)
    o_diff = _diff(dqk.reshape(B, T, n_dqk), dv.reshape(B, T, -1), diff_lambda_q, diff_lambda_k, diff_subln_gain,
                   dbias)

    merged = _merge(o_nsa.reshape(M, -1), o_diff.reshape(M, -1), w_nsa_out, w_diff_out, mgate)
    x1, h2 = _oproj(merged, w_o.astype(BF16), x.reshape(M, D), mod3, norm2_gain, T)
    act = _ffn_up(h2, w_ffn_up, ffn_conv_w, ffn_conv_b, T)
    out = _ffn_down(act, w_ffn_down, x1, mod3, T)
    return out.reshape(B, T, D)


def kernel(x, c, w_ada, b_ada, norm1_gain, norm2_gain, w_in, nsa_q_gain, nsa_k_gain, cmp_pe, cmp_w1, cmp_w2,
           diff_q_gain, diff_k_gain, diff_lambda_q, diff_lambda_k, diff_subln_gain, w_nsa_out, w_diff_out, w_o,
           w_ffn_up, ffn_conv_w, ffn_conv_b, w_ffn_down, rel_bias):
    return _layer(x, c, w_ada[0], b_ada[0], norm1_gain[0], norm2_gain[0], w_in, nsa_q_gain[0], nsa_k_gain[0],
                  cmp_pe[0], cmp_w1[0], cmp_w2[0], diff_q_gain[0], diff_k_gain[0], diff_lambda_q[0],
                  diff_lambda_k[0], diff_subln_gain[0], w_nsa_out[0], w_diff_out[0], w_o[0], w_ffn_up[0],
                  ffn_conv_w[0], ffn_conv_b[0], w_ffn_down[0], rel_bias)
```

```python
import functools
import math

import numpy as np
import jax
import jax.numpy as jnp
from jax import lax
from jax.experimental import pallas as pl
from jax.experimental.pallas import tpu as pltpu

F32 = jnp.float32
BF16 = jnp.bfloat16

HEAD_DIM = 128
NSA_HEADS = 8
NSA_KV_GROUPS = 2
NSA_HPG = NSA_HEADS // NSA_KV_GROUPS
CMP_LEN = 32
CMP_STRIDE = 16
SEL_BLOCK = 64
N_SEL = 16
WINDOW = 512
DIFF_HEADS = 4
NUM_BUCKETS = 32
MAX_DISTANCE = 128
EPS = 1e-6
NEG = -1e30
LAM_INIT = 0.8 - 0.6 * math.exp(-0.3 * 0)
LOG2E = math.log2(math.e)
MASK_BIG = -(2.0 ** 100)

LANE = 128
QB = 128
DQB = 256
SUB_ROWS = 256
INPROJ_TM = 2048
INPROJ_TN = 1024
VMEM_LIMIT = 56 * 1024 * 1024


def _cparams(n_axes):
    return pltpu.CompilerParams(dimension_semantics=("arbitrary",) * n_axes,
                                vmem_limit_bytes=VMEM_LIMIT)


def _t5_bucket_np(dist):
    n = np.maximum(np.asarray(dist, np.int32), 0)
    max_exact = NUM_BUCKETS // 2
    nf = np.maximum(n, max_exact).astype(np.float32)
    large = max_exact + (np.log(nf / np.float32(max_exact)) / np.float32(math.log(MAX_DISTANCE / max_exact))
                         * np.float32(NUM_BUCKETS - max_exact)).astype(np.int32)
    large = np.minimum(large, NUM_BUCKETS - 1)
    return np.where(n < max_exact, n, large).astype(np.int32)


def _ada_kernel(ct_ref, w_ref, b_ref, o_ref):
    ct = ct_ref[...]
    s = ct * jax.nn.sigmoid(ct)
    w = w_ref[...]
    for b in range(ct.shape[1]):
        o_ref[b:b + 1, :] = jnp.sum(w * s[:, b:b + 1], axis=0, keepdims=True) + b_ref[...]


def _ada(c, w_ada, b_ada, tn=512):
    B, D = c.shape
    N = w_ada.shape[1]
    return pl.pallas_call(
        _ada_kernel,
        grid=(N // tn,),
        in_specs=[pl.BlockSpec((D, B), lambda j: (0, 0)),
                  pl.BlockSpec((D, tn), lambda j: (0, j)),
                  pl.BlockSpec((1, tn), lambda j: (0, j))],
        out_specs=pl.BlockSpec((B, tn), lambda j: (0, j)),
        out_shape=jax.ShapeDtypeStruct((B, N), F32),
        compiler_params=_cparams(1),
        name="ada",
    )(c.T, w_ada, b_ada.reshape(1, N))


def _modnorm(x, gain, sc, sh):
    ms = jnp.mean(x * x, axis=-1, keepdims=True)
    return (x * lax.rsqrt(ms + EPS) * gain) * (1.0 + sc) + sh


def _norm1_kernel(x_ref, g_ref, sc_ref, sh_ref, o_ref):
    o_ref[0] = _modnorm(x_ref[0], g_ref[...], sc_ref[0], sh_ref[0]).astype(o_ref.dtype)


def _norm1(x, gain, mod3, tm=512):
    B, T, D = x.shape
    return pl.pallas_call(
        _norm1_kernel,
        grid=(B, T // tm),
        in_specs=[pl.BlockSpec((1, tm, D), lambda b, i: (b, i, 0)),
                  pl.BlockSpec((1, D), lambda b, i: (0, 0)),
                  pl.BlockSpec((1, 1, D), lambda b, i: (b * 6 + 1, 0, 0)),
                  pl.BlockSpec((1, 1, D), lambda b, i: (b * 6 + 0, 0, 0))],
        out_specs=pl.BlockSpec((1, tm, D), lambda b, i: (b, i, 0)),
        out_shape=jax.ShapeDtypeStruct((B, T, D), BF16),
        compiler_params=_cparams(2),
        name="norm1",
    )(x, gain.reshape(1, D), mod3, mod3)


def _row_chunks(tm):
    return [slice(r, r + SUB_ROWS) for r in range(0, tm, SUB_ROWS)]


def _inproj_kernel(a_ref, wt_ref, g_ref, o_ref, wb_ref, *, heads, mode):
    @pl.when(pl.program_id(1) == 0)
    def _():
        wb_ref[...] = wt_ref[0].astype(BF16)

    for rows in _row_chunks(a_ref.shape[0]):
        acc = _qk(a_ref[rows, :], wb_ref[...])
        for k, is_head in enumerate(heads):
            lanes = slice(k * LANE, (k + 1) * LANE)
            y = acc[:, lanes]
            if is_head:
                ms = jnp.mean(y * y, axis=-1, keepdims=True)
                y = y * lax.rsqrt(ms + EPS) * g_ref[:, lanes]
            elif mode == "sigmoid":
                y = jax.nn.sigmoid(y)
            o_ref[rows, lanes] = y.astype(o_ref.dtype)


def _inproj(h2d, w_in_t, gains, *, src0, ncols, heads, mode, out_dtype, tm, tn, name):
    M, K = h2d.shape
    assert src0 % 8 == 0 and ncols % tn == 0 and len(heads) == tn // LANE
    return pl.pallas_call(
        functools.partial(_inproj_kernel, heads=heads, mode=mode),
        grid=(ncols // tn, M // tm),
        in_specs=[pl.BlockSpec((tm, K), lambda j, i: (i, 0)),
                  pl.BlockSpec((pl.Element(1), pl.Element(tn), pl.Element(K)),
                               lambda j, i: (0, pl.multiple_of(src0 + j * tn, 8), 0)),
                  pl.BlockSpec((1, tn), lambda j, i: (0, j))],
        out_specs=pl.BlockSpec((tm, tn), lambda j, i: (i, j)),
        out_shape=jax.ShapeDtypeStruct((M, ncols), out_dtype),
        scratch_shapes=[pltpu.VMEM((tn, K), BF16)],
        compiler_params=_cparams(2),
        name=name,
    )(h2d, w_in_t, gains)


def _bias_kernel(idx_ref, tab_ref, o_ref, *, head0, hpg, rel, mult):
    head = head0 + pl.program_id(0) * hpg + pl.program_id(2)
    idx = idx_ref[0]
    acc = jnp.zeros(idx.shape, F32)
    for b in range(NUM_BUCKETS):
        acc = jnp.where(idx == b, tab_ref[b, head], acc)
    if rel:
        acc = acc - tab_ref[NUM_BUCKETS - 1, head]
    o_ref[0, 0, 0] = jnp.where(idx < 0, NEG, acc * mult)


def _bias_tiles(idx, rel_bias, *, head0, groups, hpg, name, rel=False, mult=1.0):
    N, R, C = idx.shape
    return pl.pallas_call(
        functools.partial(_bias_kernel, head0=head0, hpg=hpg, rel=rel, mult=mult),
        grid=(groups, N, hpg),
        in_specs=[pl.BlockSpec((1, R, C), lambda g, n, h: (n, 0, 0)),
                  pl.BlockSpec(memory_space=pltpu.SMEM)],
        out_specs=pl.BlockSpec((1, 1, 1, R, C), lambda g, n, h: (g, n, h, 0, 0)),
        out_shape=jax.ShapeDtypeStruct((groups, N, hpg, R, C), F32),
        compiler_params=_cparams(3),
        name=name,
    )(jnp.asarray(idx), rel_bias)


def _causal_idx(R):
    r = np.arange(R)[:, None]
    c = np.arange(R)[None, :]
    return np.stack([np.where(r >= c, _t5_bucket_np(r - c), -1), _t5_bucket_np(R + r - c)]).astype(np.int32)


def _window_idx(R):
    r = np.arange(R)[:, None]
    c = np.arange(R)[None, :]
    edge = np.where(r < c, NUM_BUCKETS - 1, -1)
    return np.concatenate([_causal_idx(R), edge[None]]).astype(np.int32)


def _cmp_idx(T):
    per = QB // CMP_STRIDE
    u0 = (T // QB - 1) * per
    assert u0 + LANE <= 2 * LANE
    r = np.arange(QB)[:, None]
    end = (np.arange(2 * LANE)[None, :] - u0) * CMP_STRIDE + CMP_LEN - 1
    return np.where(end <= r, _t5_bucket_np(r - end), -1).astype(np.int32)[None], u0, per


def _compress_kernel(zk_ref, zv_ref, pe_ref, w1_ref, w2_ref, kg_ref, kc_ref, vc_ref):
    half = CMP_LEN // 2

    def one(z_ref, i):
        p1 = jnp.zeros((LANE, HEAD_DIM), F32)
        p2 = jnp.zeros((LANE, HEAD_DIM), F32)
        for l in range(half):
            z = z_ref[0, pl.ds(l, LANE, stride=CMP_STRIDE), :]
            w_lo = w1_ref[i, l * HEAD_DIM:(l + 1) * HEAD_DIM, :].astype(BF16)
            w_hi = w1_ref[i, (half + l) * HEAD_DIM:(half + l + 1) * HEAD_DIM, :].astype(BF16)
            p1 = p1 + jnp.dot((z + pe_ref[i, l:l + 1, :]).astype(BF16), w_lo, preferred_element_type=F32)
            p2 = p2 + jnp.dot((z + pe_ref[i, half + l:half + l + 1, :]).astype(BF16), w_hi,
                              preferred_element_type=F32)
        pre = p1 + pltpu.roll(p2, LANE - 1, axis=0)
        hid = jax.nn.gelu(pre)
        return jnp.dot(hid.astype(BF16), w2_ref[i].astype(BF16), preferred_element_type=F32)

    kc = one(zk_ref, 0)
    ms = jnp.mean(kc * kc, axis=-1, keepdims=True)
    kc_ref[0, 0] = (kc * lax.rsqrt(ms + EPS) * kg_ref[...]).astype(kc_ref.dtype)
    vc_ref[0, 0] = one(zv_ref, 1).astype(vc_ref.dtype)


def _compress(cmp_kv, pe, w1, w2, k_gain0):
    B, T, _ = cmp_kv.shape
    G, dk = NSA_KV_GROUPS, HEAD_DIM
    assert (T - CMP_LEN) // CMP_STRIDE + 1 == LANE - 1
    out = jax.ShapeDtypeStruct((B, G, LANE, dk), BF16)
    return pl.pallas_call(
        _compress_kernel,
        grid=(B, G),
        in_specs=[pl.BlockSpec((1, T, dk), lambda b, g: (b, 0, g)),
                  pl.BlockSpec((1, T, dk), lambda b, g: (b, 0, G + g)),
                  pl.BlockSpec((2, CMP_LEN, dk), lambda b, g: (0, 0, 0)),
                  pl.BlockSpec((2, CMP_LEN * dk, dk), lambda b, g: (0, 0, 0)),
                  pl.BlockSpec((2, dk, dk), lambda b, g: (0, 0, 0)),
                  pl.BlockSpec((1, dk), lambda b, g: (0, 0))],
        out_specs=[pl.BlockSpec((1, 1, LANE, dk), lambda b, g: (b, g, 0, 0))] * 2,
        out_shape=[out, out],
        compiler_params=_cparams(2),
        name="compress",
    )(cmp_kv, cmp_kv, pe, w1, w2, k_gain0.reshape(1, dk))


def _qk(q, k):
    return lax.dot_general(q, k, (((1,), (1,)), ((), ())), preferred_element_type=F32)


def _lane_fold(x, op):
    acc = x[..., :LANE]
    for t in range(1, x.shape[-1] // LANE):
        acc = op(acc, x[..., t * LANE:(t + 1) * LANE])
    return acc


def _softmax_parts(parts):
    m = jnp.max(functools.reduce(jnp.maximum, [_lane_fold(s, jnp.maximum) for s in parts]), axis=-1, keepdims=True)
    ps = [jnp.exp2(s - m) for s in parts]
    den = jnp.sum(functools.reduce(jnp.add, [_lane_fold(p, jnp.add) for p in ps]), axis=-1, keepdims=True)
    return ps, den


def _nsa_kernel(q_ref, ks_ref, vs_ref, kw_ref, vw_ref, kc_ref, vc_ref, gate_ref, cb_ref, nb_ref,
                ovt_ref, exp_ref, o_ref, *, cmp_u0, cmp_per):
    H, R = NSA_HPG, NSA_HPG * QB
    T = ks_ref.shape[1]
    ns = ovt_ref.shape[0]
    nwb = WINDOW // QB
    row = lax.broadcasted_iota(jnp.int32, (QB, LANE), 0)
    col = lax.broadcasted_iota(jnp.int32, (QB, LANE), 1)
    blk = lax.broadcasted_iota(jnp.int32, (ns, QB), 0)
    tl = lax.broadcasted_iota(jnp.int32, (ns, QB), 1)
    kc = kc_ref[0, 0]
    vc = vc_ref[0, 0]

    def attend_task(qs, k_ref, v_ref, spans_fn, out, key):
        spans = spans_fn()
        parts = []
        for a, b, add in spans:
            s = _qk(qs, k_ref[0, a:b, :]).reshape(H, QB, b - a)
            parts.append(s if add is None else s + add)
        yield
        ps, den = _softmax_parts(parts)
        yield
        o = None
        for (a, b, _), p in zip(spans, ps):
            pv = jnp.dot(p.reshape(R, b - a).astype(BF16), v_ref[0, a:b, :], preferred_element_type=F32)
            o = pv if o is None else o + pv
        out[key] = o / den.reshape(R, 1)
        yield

    def cmp_task(i, qs, out):
        lo = i * QB
        u = cmp_u0 - cmp_per * i
        s = _qk(qs, kc).reshape(H, QB, LANE) + cb_ref[0, 0][:, :, u:u + LANE]
        yield
        e = jnp.exp2(s - jnp.max(s, axis=-1, keepdims=True))
        if i == 0:
            valid = (col * CMP_STRIDE + (CMP_LEN - 1)) <= row
            e = jnp.where(valid[None], e, 0.0)
            den = jnp.sum(e, axis=-1, keepdims=True)
            p = e / jnp.where(den > 0.0, den, 1.0)
        else:
            p = e / jnp.sum(e, axis=-1, keepdims=True)
        if i >= 1:
            psum = p[0] + p[1] + p[2] + p[3]
            imp_t = lax.dot_general(ovt_ref[...], psum, (((1,), (1,)), ((), ())),
                                    precision=lax.Precision.HIGHEST, preferred_element_type=F32)
        yield
        out["cmp", i] = jnp.dot(p.reshape(R, LANE).astype(BF16), vc, preferred_element_type=F32)
        if i >= 1:
            cur = (lo + tl) // SEL_BLOCK
            forced = (blk == 0) | (blk == cur) | (blk == cur - 1)
            score = jnp.where(forced, 1e4, jnp.where(blk <= cur, imp_t, -1e4))
            rank = jnp.zeros((ns, QB), F32)
            for b in range(ns):
                other = score[b:b + 1, :]
                rank = rank + jnp.where(blk > b, jnp.where(other >= score, 1.0, 0.0),
                                        jnp.where(other > score, 1.0, 0.0))
            unsel_t = jnp.where(rank < float(min(N_SEL, ns)), 0.0, 1.0)
            unsel = jnp.concatenate([unsel_t, jnp.zeros((LANE - ns, QB), F32)], axis=0).T
            out["drop", i] = jnp.dot(unsel.astype(BF16), exp_ref[:, 0:lo], preferred_element_type=F32)
        yield

    def slc_spans(i, out):
        lo, hi = i * QB, (i + 1) * QB
        spans = [(lo, hi, nb_ref[0, 0])]
        if i >= 1:
            drop = out["drop", i]
            spans.insert(0, (lo - QB, lo, nb_ref[0, 1] + drop[None, :, lo - QB:lo]))
            if i >= 2:
                spans.insert(0, (0, lo - QB, drop[None, :, 0:lo - QB]))
        return spans

    def win_spans(i):
        lo, hi = i * QB, (i + 1) * QB
        spans = []
        if i >= nwb:
            spans.append(((i - nwb) * QB, (i - nwb + 1) * QB, nb_ref[0, 2]))
        mid_a, mid_b = max(i - nwb + 1, 0) * QB, (i - 1) * QB
        if mid_b > mid_a:
            spans.append((mid_a, mid_b, None))
        if i >= 1:
            spans.append((lo - QB, lo, nb_ref[0, 1]))
        spans.append((lo, hi, nb_ref[0, 0]))
        return spans

    def combine(i, out):
        lo, hi = i * QB, (i + 1) * QB
        gate = gate_ref[0, lo:hi, :]
        first = pl.program_id(1) == 0

        def gcol(h, br):
            c = 3 * h + br
            return jnp.where(first, gate[:, c:c + 1], gate[:, 3 * H + c:3 * H + c + 1])

        outs = []
        for h in range(H):
            sl = slice(h * QB, (h + 1) * QB)
            outs.append(gcol(h, 0) * out["cmp", i][sl] + gcol(h, 1) * out["slc", i][sl]
                        + gcol(h, 2) * out["win", i][sl])
        o_ref[0, lo:hi, :] = jnp.concatenate(outs, axis=1).astype(o_ref.dtype)

    nb = T // QB

    def pair(k):
        out = {}
        tasks = []
        blocks = (k, nb - 1 - k)
        qss = {}
        for i in blocks:
            q = q_ref[0, i * QB:(i + 1) * QB, :]
            qss[i] = jnp.concatenate([q[:, h * HEAD_DIM:(h + 1) * HEAD_DIM] for h in range(H)], axis=0)
            tasks.append(cmp_task(i, qss[i], out))
            tasks.append(attend_task(qss[i], kw_ref, vw_ref, functools.partial(win_spans, i), out, ("win", i)))
        for i in blocks:
            tasks.append(attend_task(qss[i], ks_ref, vs_ref, functools.partial(slc_spans, i, out), out, ("slc", i)))
        n_stage = 3
        for step in range(len(tasks) + n_stage - 1):
            for t in range(step - n_stage + 1, step + 1):
                if 0 <= t < len(tasks):
                    next(tasks[t])
        for i in blocks:
            combine(i, out)

    for k in range(nb // 2):
        pl.when(pl.program_id(2) == k)(functools.partial(pair, k))


def _nsa(qn, kv, kc, vc, gates, cbias, cmp_u0, cmp_per, nbias):
    B, T, _ = qn.shape
    G, H, dk = NSA_KV_GROUPS, NSA_HPG, HEAD_DIM
    nb = T // QB
    ns = T // SEL_BLOCK
    cstart = np.arange(LANE) * CMP_STRIDE
    sstart = np.arange(ns) * SEL_BLOCK
    overlap = np.clip(np.minimum(cstart[:, None] + CMP_LEN, sstart[None, :] + SEL_BLOCK)
                      - np.maximum(cstart[:, None], sstart[None, :]), 0, None) / CMP_STRIDE
    overlap[LANE - 1:] = 0.0
    ovt = jnp.asarray(overlap.T, F32)
    expand = np.zeros((LANE, T), np.float32)
    expand[np.arange(T) // SEL_BLOCK, np.arange(T)] = MASK_BIG
    return pl.pallas_call(
        functools.partial(_nsa_kernel, cmp_u0=cmp_u0, cmp_per=cmp_per),
        grid=(B, G, nb // 2),
        in_specs=[pl.BlockSpec((1, T, H * dk), lambda b, g, i: (b, 0, g)),
                  pl.BlockSpec((1, T, dk), lambda b, g, i: (b, 0, g)),
                  pl.BlockSpec((1, T, dk), lambda b, g, i: (b, 0, G + g)),
                  pl.BlockSpec((1, T, dk), lambda b, g, i: (b, 0, 2 * G + g)),
                  pl.BlockSpec((1, T, dk), lambda b, g, i: (b, 0, 3 * G + g)),
                  pl.BlockSpec((1, 1, LANE, dk), lambda b, g, i: (b, g, 0, 0)),
                  pl.BlockSpec((1, 1, LANE, dk), lambda b, g, i: (b, g, 0, 0)),
                  pl.BlockSpec((1, T, LANE), lambda b, g, i: (b, 0, 0)),
                  pl.BlockSpec((1, 1, H, QB, 2 * LANE), lambda b, g, i: (g, 0, 0, 0, 0)),
                  pl.BlockSpec((1, 3, H, QB, QB), lambda b, g, i: (g, 0, 0, 0, 0)),
                  pl.BlockSpec((ns, LANE), lambda b, g, i: (0, 0)),
                  pl.BlockSpec((LANE, T), lambda b, g, i: (0, 0))],
        out_specs=pl.BlockSpec((1, T, H * dk), lambda b, g, i: (b, 0, g)),
        out_shape=jax.ShapeDtypeStruct((B, T, NSA_HEADS * dk), BF16),
        compiler_params=_cparams(3),
        name="nsa",
    )(qn, kv, kv, kv, kv, kc, vc, gates, cbias, nbias, ovt, jnp.asarray(expand, BF16))


def _diff_kernel(q_ref, k_ref, v_ref, lq_ref, lk_ref, sg_ref, db_ref, o_ref):
    dk = HEAD_DIM
    T = k_ref.shape[1]
    lqk = lq_ref[...] * lk_ref[...]
    lam = (jnp.exp(jnp.sum(lqk[0:1], axis=-1, keepdims=True))
           - jnp.exp(jnp.sum(lqk[1:2], axis=-1, keepdims=True)) + LAM_INIT)
    def task(i, mm, out):
        lo, hi = i * DQB, (i + 1) * DQB
        cols = slice(mm * dk, (mm + 1) * dk)
        q = q_ref[0, lo:hi, cols]
        bounds, parts = [], []
        if i >= 2:
            bounds.append((0, lo - DQB))
            parts.append(_qk(q, k_ref[0, 0:lo - DQB, cols]))
        if i >= 1:
            bounds.append((lo - DQB, lo))
            parts.append(_qk(q, k_ref[0, lo - DQB:lo, cols]) + db_ref[0, 1, 0])
        bounds.append((lo, hi))
        parts.append(_qk(q, k_ref[0, lo:hi, cols]) + db_ref[0, 0, 0])
        yield
        ps, den = _softmax_parts(parts)
        yield
        o = None
        for (a, b), p in zip(bounds, ps):
            pv = jnp.dot(p.astype(BF16), v_ref[0, a:b, :], preferred_element_type=F32)
            o = pv if o is None else o + pv
        out[i, mm] = o / den
        yield

    out = {}
    tasks = [task(i, mm, out) for i in range(T // DQB) for mm in range(2)]
    n_stage = 3
    for step in range(len(tasks) + n_stage - 1):
        for t in range(step - n_stage + 1, step + 1):
            if 0 <= t < len(tasks):
                next(tasks[t])

    for i in range(T // DQB):
        lo, hi = i * DQB, (i + 1) * DQB
        o = out[i, 0] - lam * out[i, 1]
        ms = jnp.mean(o * o, axis=-1, keepdims=True)
        o_ref[0, lo:hi, :] = ((o * lax.rsqrt(ms + EPS) * sg_ref[...]) * (1.0 - LAM_INIT)).astype(o_ref.dtype)


def _diff(dqk, dv, lam_q, lam_k, subln_gain, dbias):
    B, T, _ = dqk.shape
    Hd, dk = DIFF_HEADS, HEAD_DIM
    w = 2 * dk
    return pl.pallas_call(
        _diff_kernel,
        grid=(B, Hd),
        in_specs=[pl.BlockSpec((1, T, w), lambda b, h: (b, 0, h)),
                  pl.BlockSpec((1, T, w), lambda b, h: (b, 0, Hd + h)),
                  pl.BlockSpec((1, T, w), lambda b, h: (b, 0, h)),
                  pl.BlockSpec((2, dk), lambda b, h: (0, 0)),
                  pl.BlockSpec((2, dk), lambda b, h: (0, 0)),
                  pl.BlockSpec((1, w), lambda b, h: (0, 0)),
                  pl.BlockSpec((1, 2, 1, DQB, DQB), lambda b, h: (h, 0, 0, 0, 0))],
        out_specs=pl.BlockSpec((1, T, w), lambda b, h: (b, 0, h)),
        out_shape=jax.ShapeDtypeStruct((B, T, Hd * w), BF16),
        compiler_params=_cparams(2),
        name="diff",
    )(dqk, dqk, dv, lam_q, lam_k, subln_gain.reshape(1, w), dbias)


def _merge_kernel(an_ref, ad_ref, wn_ref, wd_ref, gn_ref, gd_ref, o_ref, wnb_ref, wdb_ref):
    @pl.when(pl.program_id(1) == 0)
    def _():
        wnb_ref[...] = wn_ref[...].astype(BF16)
        wdb_ref[...] = wd_ref[...].astype(BF16)

    for rows in _row_chunks(an_ref.shape[0]):
        yn = jnp.dot(an_ref[rows, :], wnb_ref[...], preferred_element_type=F32)
        yd = jnp.dot(ad_ref[rows, :], wdb_ref[...], preferred_element_type=F32)
        o_ref[rows, :] = (gn_ref[rows, :].astype(F32) * yn + gd_ref[rows, :].astype(F32) * yd).astype(o_ref.dtype)


def _merge(o_nsa, o_diff, w_n, w_d, mg, tm=1024, tn=1024):
    M, K = o_nsa.shape
    N = w_n.shape[1]
    nj = N // tn
    return pl.pallas_call(
        _merge_kernel,
        grid=(nj, M // tm),
        in_specs=[pl.BlockSpec((tm, K), lambda j, i: (i, 0)),
                  pl.BlockSpec((tm, K), lambda j, i: (i, 0)),
                  pl.BlockSpec((K, tn), lambda j, i: (0, j)),
                  pl.BlockSpec((K, tn), lambda j, i: (0, j)),
                  pl.BlockSpec((tm, tn), lambda j, i: (i, j)),
                  pl.BlockSpec((tm, tn), lambda j, i: (i, nj + j))],
        out_specs=pl.BlockSpec((tm, tn), lambda j, i: (i, j)),
        out_shape=jax.ShapeDtypeStruct((M, N), BF16),
        scratch_shapes=[pltpu.VMEM((K, tn), BF16), pltpu.VMEM((K, tn), BF16)],
        compiler_params=_cparams(2),
        name="merge",
    )(o_nsa, o_diff, w_n, w_d, mg, mg)


def _oproj_kernel(a_ref, w_ref, x_ref, g1_ref, gain_ref, sc_ref, sh_ref, x1_ref, h2_ref):
    for rows in _row_chunks(a_ref.shape[0]):
        y = jnp.dot(a_ref[rows, :], w_ref[...], preferred_element_type=F32)
        x1 = x_ref[rows, :] + g1_ref[0] * y
        x1_ref[rows, :] = x1
        h2_ref[rows, :] = _modnorm(x1, gain_ref[...], sc_ref[0], sh_ref[0]).astype(h2_ref.dtype)


def _oproj(merged, w_o, x2d, mod3, gain2, T, tm=512):
    M, D = x2d.shape
    per = T // tm
    return pl.pallas_call(
        _oproj_kernel,
        grid=(M // tm,),
        in_specs=[pl.BlockSpec((tm, D), lambda i: (i, 0)),
                  pl.BlockSpec((D, D), lambda i: (0, 0)),
                  pl.BlockSpec((tm, D), lambda i: (i, 0)),
                  pl.BlockSpec((1, 1, D), lambda i: ((i // per) * 6 + 2, 0, 0)),
                  pl.BlockSpec((1, D), lambda i: (0, 0)),
                  pl.BlockSpec((1, 1, D), lambda i: ((i // per) * 6 + 4, 0, 0)),
                  pl.BlockSpec((1, 1, D), lambda i: ((i // per) * 6 + 3, 0, 0))],
        out_specs=[pl.BlockSpec((tm, D), lambda i: (i, 0)),
                   pl.BlockSpec((tm, D), lambda i: (i, 0))],
        out_shape=[jax.ShapeDtypeStruct((M, D), F32), jax.ShapeDtypeStruct((M, D), BF16)],
        compiler_params=_cparams(1),
        name="oproj",
    )(merged, w_o, x2d, mod3, gain2.reshape(1, D), mod3, mod3)


def _ffn_up_kernel(h_ref, wa_ref, wv_ref, cwa_ref, cwv_ref, cba_ref, cbv_ref, o_ref, wab_ref, wvb_ref,
                   ca_ref, cv_ref, sa_ref, sv_ref, *, per):
    i = pl.program_id(1)

    @pl.when(i == 0)
    def _():
        wab_ref[...] = wa_ref[...].astype(BF16)
        wvb_ref[...] = wv_ref[...].astype(BF16)

    @pl.when(i % per == 0)
    def _():
        ca_ref[...] = jnp.zeros(ca_ref.shape, F32)
        cv_ref[...] = jnp.zeros(cv_ref.shape, F32)

    def conv(u, prev, cw_ref, cb_ref, s_ref):
        s_ref[0:8, :] = prev
        s_ref[8:8 + SUB_ROWS, :] = u
        u1 = s_ref[7:7 + SUB_ROWS, :]
        u2 = s_ref[6:6 + SUB_ROWS, :]
        return cb_ref[...] + cw_ref[0:1, :] * u2 + cw_ref[1:2, :] * u1 + cw_ref[2:3, :] * u

    prev_a, prev_v = ca_ref[...], cv_ref[...]
    for n, rows in enumerate(_row_chunks(h_ref.shape[0])):
        hs = h_ref[rows, :]
        ua = jnp.dot(hs, wab_ref[...], preferred_element_type=F32)
        uv = jnp.dot(hs, wvb_ref[...], preferred_element_type=F32)
        a = conv(ua, prev_a, cwa_ref, cba_ref, sa_ref.at[n % 2])
        val = conv(uv, prev_v, cwv_ref, cbv_ref, sv_ref.at[n % 2])
        o_ref[rows, :] = (a * jax.nn.sigmoid(a) * val).astype(o_ref.dtype)
        prev_a, prev_v = ua[SUB_ROWS - 8:, :], uv[SUB_ROWS - 8:, :]
    ca_ref[...] = prev_a
    cv_ref[...] = prev_v


def _ffn_up(h2, w_up, conv_w, conv_b, T, tm=2048, tn=512):
    M, D = h2.shape
    F = w_up.shape[1] // 2
    nj = F // tn
    cb = conv_b.reshape(1, 2 * F)
    return pl.pallas_call(
        functools.partial(_ffn_up_kernel, per=T // tm),
        grid=(nj, M // tm),
        in_specs=[pl.BlockSpec((tm, D), lambda j, i: (i, 0)),
                  pl.BlockSpec((D, tn), lambda j, i: (0, j)),
                  pl.BlockSpec((D, tn), lambda j, i: (0, nj + j)),
                  pl.BlockSpec((3, tn), lambda j, i: (0, j)),
                  pl.BlockSpec((3, tn), lambda j, i: (0, nj + j)),
                  pl.BlockSpec((1, tn), lambda j, i: (0, j)),
                  pl.BlockSpec((1, tn), lambda j, i: (0, nj + j))],
        out_specs=pl.BlockSpec((tm, tn), lambda j, i: (i, j)),
        out_shape=jax.ShapeDtypeStruct((M, F), BF16),
        scratch_shapes=[pltpu.VMEM((D, tn), BF16), pltpu.VMEM((D, tn), BF16),
                        pltpu.VMEM((8, tn), F32), pltpu.VMEM((8, tn), F32),
                        pltpu.VMEM((2, SUB_ROWS + 8, tn), F32), pltpu.VMEM((2, SUB_ROWS + 8, tn), F32)],
        compiler_params=_cparams(2),
        name="ffn_up",
    )(h2, w_up, w_up, conv_w, conv_w, cb, cb)


def _ffn_down_kernel(a_ref, w_ref, x_ref, g2_ref, o_ref, wb_ref):
    @pl.when(pl.program_id(1) == 0)
    def _():
        wb_ref[...] = w_ref[...].astype(BF16)

    for rows in _row_chunks(a_ref.shape[0]):
        y = jnp.dot(a_ref[rows, :], wb_ref[...], preferred_element_type=F32)
        o_ref[rows, :] = x_ref[rows, :] + g2_ref[0] * y


def _ffn_down(act, w_down, x1, mod3, T, tm=512, tn=512):
    M, F = act.shape
    D = w_down.shape[1]
    per = T // tm
    return pl.pallas_call(
        _ffn_down_kernel,
        grid=(D // tn, M // tm),
        in_specs=[pl.BlockSpec((tm, F), lambda j, i: (i, 0)),
                  pl.BlockSpec((F, tn), lambda j, i: (0, j)),
                  pl.BlockSpec((tm, tn), lambda j, i: (i, j)),
                  pl.BlockSpec((1, 1, tn), lambda j, i: ((i // per) * 6 + 5, 0, j))],
        out_specs=pl.BlockSpec((tm, tn), lambda j, i: (i, j)),
        out_shape=jax.ShapeDtypeStruct((M, D), F32),
        scratch_shapes=[pltpu.VMEM((F, tn), BF16)],
        compiler_params=_cparams(2),
        name="ffn_down",
    )(act, w_down, x1, mod3)


def _layer(x, c, w_ada, b_ada, norm1_gain, norm2_gain, w_in, nsa_q_gain, nsa_k_gain, cmp_pe, cmp_w1, cmp_w2,
           diff_q_gain, diff_k_gain, diff_lambda_q, diff_lambda_k, diff_subln_gain, w_nsa_out, w_diff_out, w_o,
           w_ffn_up, ffn_conv_w, ffn_conv_b, w_ffn_down, rel_bias):
    B, T, D = x.shape
    dk, G = HEAD_DIM, NSA_KV_GROUPS
    M = B * T
    scale = dk ** -0.5

    n_q = NSA_HEADS * dk
    o_kv = n_q
    o_g = o_kv + 3 * 2 * G * dk
    o_dq = o_g + NSA_HEADS * 3
    o_dk = o_dq + DIFF_HEADS * 2 * dk
    o_dv = o_dk + DIFF_HEADS * 2 * dk
    o_mg = o_dv + DIFF_HEADS * 2 * dk
    n_kv = 2 * G * dk
    n_dqk = 2 * DIFF_HEADS * 2 * dk

    mod3 = _ada(c, w_ada, b_ada).reshape(B * 6, 1, D)
    h = _norm1(x, norm1_gain, mod3).reshape(M, D)

    ones = jnp.ones((n_kv // 2,), F32)
    g_q = jnp.tile(nsa_q_gain * (scale * LOG2E), NSA_HEADS).reshape(1, n_q)
    g_kv = jnp.concatenate([jnp.tile(nsa_k_gain[1], G), ones, jnp.tile(nsa_k_gain[2], G), ones]).reshape(1, 2 * n_kv)
    g_dqk = jnp.concatenate([jnp.tile(diff_q_gain * (scale * LOG2E), 2 * DIFF_HEADS),
                             jnp.tile(diff_k_gain, 2 * DIFF_HEADS)]).reshape(1, n_dqk)
    g_one = jnp.ones((1, 2 * D), F32)
    proj = functools.partial(_inproj, h, jnp.swapaxes(w_in, 1, 2), tm=INPROJ_TM)
    wide = INPROJ_TN
    yes, no = (True,) * (wide // LANE), (False,) * (wide // LANE)
    qn = proj(g_q, src0=0, ncols=n_q, heads=yes, mode="raw", out_dtype=BF16, tn=wide, name="inproj_q")
    cmpkv = proj(g_one, src0=o_kv, ncols=n_kv, heads=(False,) * (n_kv // LANE), mode="raw", out_dtype=F32, tn=n_kv,
                 name="inproj_cmp")
    kv_heads = ((True,) * G + (False,) * G) * 2
    kv = proj(g_kv, src0=o_kv + n_kv, ncols=2 * n_kv, heads=kv_heads, mode="raw", out_dtype=BF16, tn=2 * n_kv,
              name="inproj_kv")
    gates = proj(g_one, src0=o_g, ncols=LANE, heads=(False,), mode="sigmoid", out_dtype=F32, tn=LANE,
                 name="inproj_gate")
    dqk = proj(g_dqk, src0=o_dq, ncols=n_dqk, heads=yes, mode="raw", out_dtype=BF16, tn=wide, name="inproj_dqk")
    dv = proj(g_one, src0=o_dv, ncols=o_mg - o_dv, heads=no, mode="raw", out_dtype=BF16, tn=wide, name="inproj_dv")
    mgate = proj(g_one, src0=o_mg, ncols=2 * D, heads=no, mode="sigmoid", out_dtype=BF16, tn=wide, name="inproj_mg")

    nbias = _bias_tiles(_window_idx(QB), rel_bias, head0=0, groups=G, hpg=NSA_HPG, name="bias_nsa",
                        rel=True, mult=LOG2E)
    cmp_idx, cmp_u0, cmp_per = _cmp_idx(T)
    cbias = _bias_tiles(cmp_idx, rel_bias, head0=0, groups=G, hpg=NSA_HPG, name="bias_cmp", mult=LOG2E)
    dbias = _bias_tiles(_causal_idx(DQB), rel_bias, head0=NSA_HEADS, groups=DIFF_HEADS, hpg=1, name="bias_diff",
                        rel=True, mult=LOG2E)

    kc, vc = _compress(cmpkv.reshape(B, T, n_kv), cmp_pe, cmp_w1, cmp_w2, nsa_k_gain[0])
    o_nsa = _nsa(qn.reshape(B, T, n_q), kv.reshape(B, T, 2 * n_kv), kc, vc, gates.reshape(B, T, LANE), cbias,
                 cmp_u0, cmp_per, nbias)
    o_diff = _diff(dqk.reshape(B, T, n_dqk), dv.reshape(B, T, -1), diff_lambda_q, diff_lambda_k, diff_subln_gain,
                   dbias)

    merged = _merge(o_nsa.reshape(M, -1), o_diff.reshape(M, -1), w_nsa_out, w_diff_out, mgate)
    x1, h2 = _oproj(merged, w_o.astype(BF16), x.reshape(M, D), mod3, norm2_gain, T)
    act = _ffn_up(h2, w_ffn_up, ffn_conv_w, ffn_conv_b, T)
    out = _ffn_down(act, w_ffn_down, x1, mod3, T)
    return out.reshape(B, T, D)


def kernel(x, c, w_ada, b_ada, norm1_gain, norm2_gain, w_in, nsa_q_gain, nsa_k_gain, cmp_pe, cmp_w1, cmp_w2,
           diff_q_gain, diff_k_gain, diff_lambda_q, diff_lambda_k, diff_subln_gain, w_nsa_out, w_diff_out, w_o,
           w_ffn_up, ffn_conv_w, ffn_conv_b, w_ffn_down, rel_bias):
    return _layer(x, c, w_ada[0], b_ada[0], norm1_gain[0], norm2_gain[0], w_in, nsa_q_gain[0], nsa_k_gain[0],
                  cmp_pe[0], cmp_w1[0], cmp_w2[0], diff_q_gain[0], diff_k_gain[0], diff_lambda_q[0],
                  diff_lambda_k[0], diff_subln_gain[0], w_nsa_out[0], w_diff_out[0], w_o[0], w_ffn_up[0],
                  ffn_conv_w[0], ffn_conv_b[0], w_ffn_down[0], rel_bias)
```

```python
import functools
import math

import numpy as np
import jax
import jax.numpy as jnp
from jax import lax
from jax.experimental import pallas as pl
from jax.experimental.pallas import tpu as pltpu

F32 = jnp.float32
BF16 = jnp.bfloat16

HEAD_DIM = 128
NSA_HEADS = 8
NSA_KV_GROUPS = 2
NSA_HPG = NSA_HEADS // NSA_KV_GROUPS
CMP_LEN = 32
CMP_STRIDE = 16
SEL_BLOCK = 64
N_SEL = 16
WINDOW = 512
DIFF_HEADS = 4
NUM_BUCKETS = 32
MAX_DISTANCE = 128
EPS = 1e-6
NEG = -1e30
LAM_INIT = 0.8 - 0.6 * math.exp(-0.3 * 0)
LOG2E = math.log2(math.e)
MASK_BIG = -(2.0 ** 100)

LANE = 128
QB = 128
DQB = 256
SUB_ROWS = 256
INPROJ_TM = 2048
INPROJ_TN = 1024
VMEM_LIMIT = 56 * 1024 * 1024


def _cparams(n_axes):
    return pltpu.CompilerParams(dimension_semantics=("arbitrary",) * n_axes,
                                vmem_limit_bytes=VMEM_LIMIT)


def _t5_bucket_np(dist):
    n = np.maximum(np.asarray(dist, np.int32), 0)
    max_exact = NUM_BUCKETS // 2
    nf = np.maximum(n, max_exact).astype(np.float32)
    large = max_exact + (np.log(nf / np.float32(max_exact)) / np.float32(math.log(MAX_DISTANCE / max_exact))
                         * np.float32(NUM_BUCKETS - max_exact)).astype(np.int32)
    large = np.minimum(large, NUM_BUCKETS - 1)
    return np.where(n < max_exact, n, large).astype(np.int32)


def _ada_kernel(ct_ref, w_ref, b_ref, o_ref):
    ct = ct_ref[...]
    s = ct * jax.nn.sigmoid(ct)
    w = w_ref[...]
    for b in range(ct.shape[1]):
        o_ref[b:b + 1, :] = jnp.sum(w * s[:, b:b + 1], axis=0, keepdims=True) + b_ref[...]


def _ada(c, w_ada, b_ada, tn=1024):
    B, D = c.shape
    N = w_ada.shape[1]
    return pl.pallas_call(
        _ada_kernel,
        grid=(N // tn,),
        in_specs=[pl.BlockSpec((D, B), lambda j: (0, 0)),
                  pl.BlockSpec((D, tn), lambda j: (0, j)),
                  pl.BlockSpec((1, tn), lambda j: (0, j))],
        out_specs=pl.BlockSpec((B, tn), lambda j: (0, j)),
        out_shape=jax.ShapeDtypeStruct((B, N), F32),
        compiler_params=_cparams(1),
        name="ada",
    )(c.T, w_ada, b_ada.reshape(1, N))


def _modnorm(x, gain, sc, sh):
    ms = jnp.mean(x * x, axis=-1, keepdims=True)
    return (x * lax.rsqrt(ms + EPS) * gain) * (1.0 + sc) + sh


def _norm1_kernel(x_ref, g_ref, sc_ref, sh_ref, o_ref):
    o_ref[0] = _modnorm(x_ref[0], g_ref[...], sc_ref[0], sh_ref[0]).astype(o_ref.dtype)


def _norm1(x, gain, mod3, tm=512):
    B, T, D = x.shape
    return pl.pallas_call(
        _norm1_kernel,
        grid=(B, T // tm),
        in_specs=[pl.BlockSpec((1, tm, D), lambda b, i: (b, i, 0)),
                  pl.BlockSpec((1, D), lambda b, i: (0, 0)),
                  pl.BlockSpec((1, 1, D), lambda b, i: (b * 6 + 1, 0, 0)),
                  pl.BlockSpec((1, 1, D), lambda b, i: (b * 6 + 0, 0, 0))],
        out_specs=pl.BlockSpec((1, tm, D), lambda b, i: (b, i, 0)),
        out_shape=jax.ShapeDtypeStruct((B, T, D), BF16),
        compiler_params=_cparams(2),
        name="norm1",
    )(x, gain.reshape(1, D), mod3, mod3)


def _row_chunks(tm):
    return [slice(r, r + SUB_ROWS) for r in range(0, tm, SUB_ROWS)]


def _inproj_kernel(a_ref, wt_ref, g_ref, o_ref, wb_ref, *, heads, mode):
    @pl.when(pl.program_id(1) == 0)
    def _():
        wb_ref[...] = wt_ref[0].astype(BF16)

    for rows in _row_chunks(a_ref.shape[0]):
        acc = _qk(a_ref[rows, :], wb_ref[...])
        for k, is_head in enumerate(heads):
            lanes = slice(k * LANE, (k + 1) * LANE)
            y = acc[:, lanes]
            if is_head:
                ms = jnp.mean(y * y, axis=-1, keepdims=True)
                y = y * lax.rsqrt(ms + EPS) * g_ref[:, lanes]
            elif mode == "sigmoid":
                y = jax.nn.sigmoid(y)
            o_ref[rows, lanes] = y.astype(o_ref.dtype)


def _inproj(h2d, w_in_t, gains, *, src0, ncols, heads, mode, out_dtype, tm, tn, name):
    M, K = h2d.shape
    assert src0 % 8 == 0 and ncols % tn == 0 and len(heads) == tn // LANE
    return pl.pallas_call(
        functools.partial(_inproj_kernel, heads=heads, mode=mode),
        grid=(ncols // tn, M // tm),
        in_specs=[pl.BlockSpec((tm, K), lambda j, i: (i, 0)),
                  pl.BlockSpec((pl.Element(1), pl.Element(tn), pl.Element(K)),
                               lambda j, i: (0, pl.multiple_of(src0 + j * tn, 8), 0)),
                  pl.BlockSpec((1, tn), lambda j, i: (0, j))],
        out_specs=pl.BlockSpec((tm, tn), lambda j, i: (i, j)),
        out_shape=jax.ShapeDtypeStruct((M, ncols), out_dtype),
        scratch_shapes=[pltpu.VMEM((tn, K), BF16)],
        compiler_params=_cparams(2),
        name=name,
    )(h2d, w_in_t, gains)


def _inproj_cmp_gate_kernel(a_ref, wc_ref, wg_ref, oc_ref, og_ref, wcb_ref, wgb_ref):
    @pl.when(pl.program_id(0) == 0)
    def _():
        wcb_ref[...] = wc_ref[0].astype(BF16)
        wgb_ref[...] = wg_ref[0].astype(BF16)

    for rows in _row_chunks(a_ref.shape[0]):
        a = a_ref[rows, :]
        oc_ref[rows, :] = _qk(a, wcb_ref[...])
        og_ref[rows, :] = jax.nn.sigmoid(_qk(a, wgb_ref[...]))


def _inproj_cmp_gate(h2d, w_in_t, *, src_cmp, n_cmp, src_gate, tm):
    M, K = h2d.shape
    assert src_cmp % 8 == 0 and src_gate % 8 == 0

    def w_rows(start, n):
        return pl.BlockSpec((pl.Element(1), pl.Element(n), pl.Element(K)), lambda i: (0, start, 0))

    return pl.pallas_call(
        _inproj_cmp_gate_kernel,
        grid=(M // tm,),
        in_specs=[pl.BlockSpec((tm, K), lambda i: (i, 0)), w_rows(src_cmp, n_cmp), w_rows(src_gate, LANE)],
        out_specs=[pl.BlockSpec((tm, n_cmp), lambda i: (i, 0)), pl.BlockSpec((tm, LANE), lambda i: (i, 0))],
        out_shape=[jax.ShapeDtypeStruct((M, n_cmp), F32), jax.ShapeDtypeStruct((M, LANE), F32)],
        scratch_shapes=[pltpu.VMEM((n_cmp, K), BF16), pltpu.VMEM((LANE, K), BF16)],
        compiler_params=_cparams(1),
        name="inproj_cmp_gate",
    )(h2d, w_in_t, w_in_t)


def _bias_kernel(idx_ref, tab_ref, o_ref, *, head0, hpg, rel, mult):
    head = head0 + pl.program_id(0) * hpg + pl.program_id(2)
    idx = idx_ref[0]
    acc = jnp.zeros(idx.shape, F32)
    for b in range(NUM_BUCKETS):
        acc = jnp.where(idx == b, tab_ref[b, head], acc)
    if rel:
        acc = acc - tab_ref[NUM_BUCKETS - 1, head]
    o_ref[0, 0, 0] = jnp.where(idx < 0, NEG, acc * mult)


def _bias_tiles(idx, rel_bias, *, head0, groups, hpg, name, rel=False, mult=1.0):
    N, R, C = idx.shape
    return pl.pallas_call(
        functools.partial(_bias_kernel, head0=head0, hpg=hpg, rel=rel, mult=mult),
        grid=(groups, N, hpg),
        in_specs=[pl.BlockSpec((1, R, C), lambda g, n, h: (n, 0, 0)),
                  pl.BlockSpec(memory_space=pltpu.SMEM)],
        out_specs=pl.BlockSpec((1, 1, 1, R, C), lambda g, n, h: (g, n, h, 0, 0)),
        out_shape=jax.ShapeDtypeStruct((groups, N, hpg, R, C), F32),
        compiler_params=_cparams(3),
        name=name,
    )(jnp.asarray(idx), rel_bias)


def _causal_idx(R):
    r = np.arange(R)[:, None]
    c = np.arange(R)[None, :]
    return np.stack([np.where(r >= c, _t5_bucket_np(r - c), -1), _t5_bucket_np(R + r - c)]).astype(np.int32)


def _window_idx(R):
    r = np.arange(R)[:, None]
    c = np.arange(R)[None, :]
    edge = np.where(r < c, NUM_BUCKETS - 1, -1)
    return np.concatenate([_causal_idx(R), edge[None]]).astype(np.int32)


def _cmp_idx(T):
    per = QB // CMP_STRIDE
    u0 = (T // QB - 1) * per
    assert u0 + LANE <= 2 * LANE
    r = np.arange(QB)[:, None]
    end = (np.arange(2 * LANE)[None, :] - u0) * CMP_STRIDE + CMP_LEN - 1
    return np.where(end <= r, _t5_bucket_np(r - end), -1).astype(np.int32)[None], u0, per


def _compress_kernel(zk_ref, zv_ref, pe_ref, w1_ref, w2_ref, kg_ref, kc_ref, vc_ref):
    half = CMP_LEN // 2

    def one(z_ref, i):
        p1 = jnp.zeros((LANE, HEAD_DIM), F32)
        p2 = jnp.zeros((LANE, HEAD_DIM), F32)
        for l in range(half):
            z = z_ref[0, pl.ds(l, LANE, stride=CMP_STRIDE), :]
            w_lo = w1_ref[i, l * HEAD_DIM:(l + 1) * HEAD_DIM, :].astype(BF16)
            w_hi = w1_ref[i, (half + l) * HEAD_DIM:(half + l + 1) * HEAD_DIM, :].astype(BF16)
            p1 = p1 + jnp.dot((z + pe_ref[i, l:l + 1, :]).astype(BF16), w_lo, preferred_element_type=F32)
            p2 = p2 + jnp.dot((z + pe_ref[i, half + l:half + l + 1, :]).astype(BF16), w_hi,
                              preferred_element_type=F32)
        pre = p1 + pltpu.roll(p2, LANE - 1, axis=0)
        hid = jax.nn.gelu(pre)
        return jnp.dot(hid.astype(BF16), w2_ref[i].astype(BF16), preferred_element_type=F32)

    kc = one(zk_ref, 0)
    ms = jnp.mean(kc * kc, axis=-1, keepdims=True)
    kc_ref[0, 0] = (kc * lax.rsqrt(ms + EPS) * kg_ref[...]).astype(kc_ref.dtype)
    vc_ref[0, 0] = one(zv_ref, 1).astype(vc_ref.dtype)


def _compress(cmp_kv, pe, w1, w2, k_gain0):
    B, T, _ = cmp_kv.shape
    G, dk = NSA_KV_GROUPS, HEAD_DIM
    assert (T - CMP_LEN) // CMP_STRIDE + 1 == LANE - 1
    out = jax.ShapeDtypeStruct((B, G, LANE, dk), BF16)
    return pl.pallas_call(
        _compress_kernel,
        grid=(B, G),
        in_specs=[pl.BlockSpec((1, T, dk), lambda b, g: (b, 0, g)),
                  pl.BlockSpec((1, T, dk), lambda b, g: (b, 0, G + g)),
                  pl.BlockSpec((2, CMP_LEN, dk), lambda b, g: (0, 0, 0)),
                  pl.BlockSpec((2, CMP_LEN * dk, dk), lambda b, g: (0, 0, 0)),
                  pl.BlockSpec((2, dk, dk), lambda b, g: (0, 0, 0)),
                  pl.BlockSpec((1, dk), lambda b, g: (0, 0))],
        out_specs=[pl.BlockSpec((1, 1, LANE, dk), lambda b, g: (b, g, 0, 0))] * 2,
        out_shape=[out, out],
        compiler_params=_cparams(2),
        name="compress",
    )(cmp_kv, cmp_kv, pe, w1, w2, k_gain0.reshape(1, dk))


def _qk(q, k):
    return lax.dot_general(q, k, (((1,), (1,)), ((), ())), preferred_element_type=F32)


def _lane_fold(x, op):
    acc = x[..., :LANE]
    for t in range(1, x.shape[-1] // LANE):
        acc = op(acc, x[..., t * LANE:(t + 1) * LANE])
    return acc


def _softmax_parts(parts):
    m = jnp.max(functools.reduce(jnp.maximum, [_lane_fold(s, jnp.maximum) for s in parts]), axis=-1, keepdims=True)
    ps = [jnp.exp2(s - m) for s in parts]
    den = jnp.sum(functools.reduce(jnp.add, [_lane_fold(p, jnp.add) for p in ps]), axis=-1, keepdims=True)
    return ps, den


def _nsa_kernel(q_ref, ks_ref, vs_ref, kw_ref, vw_ref, kc_ref, vc_ref, gate_ref, cb_ref, nb_ref,
                ovt_ref, exp_ref, o_ref, *, cmp_u0, cmp_per):
    H, R = NSA_HPG, NSA_HPG * QB
    T = ks_ref.shape[1]
    ns = ovt_ref.shape[0]
    nwb = WINDOW // QB
    row = lax.broadcasted_iota(jnp.int32, (QB, LANE), 0)
    col = lax.broadcasted_iota(jnp.int32, (QB, LANE), 1)
    blk = lax.broadcasted_iota(jnp.int32, (ns, QB), 0)
    tl = lax.broadcasted_iota(jnp.int32, (ns, QB), 1)
    kc = kc_ref[0, 0]
    vc = vc_ref[0, 0]

    def attend_task(qs, k_ref, v_ref, spans_fn, out, key):
        spans = spans_fn()
        parts = []
        for a, b, add in spans:
            s = _qk(qs, k_ref[0, a:b, :]).reshape(H, QB, b - a)
            parts.append(s if add is None else s + add)
        yield
        ps, den = _softmax_parts(parts)
        yield
        o = None
        for (a, b, _), p in zip(spans, ps):
            pv = jnp.dot(p.reshape(R, b - a).astype(BF16), v_ref[0, a:b, :], preferred_element_type=F32)
            o = pv if o is None else o + pv
        out[key] = o / den.reshape(R, 1)
        yield

    def cmp_task(i, qs, out):
        lo = i * QB
        u = cmp_u0 - cmp_per * i
        s = _qk(qs, kc).reshape(H, QB, LANE) + cb_ref[0, 0][:, :, u:u + LANE]
        yield
        e = jnp.exp2(s - jnp.max(s, axis=-1, keepdims=True))
        if i == 0:
            valid = (col * CMP_STRIDE + (CMP_LEN - 1)) <= row
            e = jnp.where(valid[None], e, 0.0)
            den = jnp.sum(e, axis=-1, keepdims=True)
            p = e / jnp.where(den > 0.0, den, 1.0)
        else:
            p = e / jnp.sum(e, axis=-1, keepdims=True)
        if i >= 1:
            psum = p[0] + p[1] + p[2] + p[3]
            imp_t = lax.dot_general(ovt_ref[...], psum, (((1,), (1,)), ((), ())),
                                    precision=lax.Precision.HIGHEST, preferred_element_type=F32)
        yield
        out["cmp", i] = jnp.dot(p.reshape(R, LANE).astype(BF16), vc, preferred_element_type=F32)
        if i >= 1:
            cur = (lo + tl) // SEL_BLOCK
            forced = (blk == 0) | (blk == cur) | (blk == cur - 1)
            score = jnp.where(forced, 1e4, jnp.where(blk <= cur, imp_t, -1e4))
            rank = jnp.zeros((ns, QB), F32)
            for b in range(ns):
                other = score[b:b + 1, :]
                rank = rank + jnp.where(blk > b, jnp.where(other >= score, 1.0, 0.0),
                                        jnp.where(other > score, 1.0, 0.0))
            unsel_t = jnp.where(rank < float(min(N_SEL, ns)), 0.0, 1.0)
            unsel = jnp.concatenate([unsel_t, jnp.zeros((LANE - ns, QB), F32)], axis=0).T
            out["drop", i] = jnp.dot(unsel.astype(BF16), exp_ref[:, 0:lo], preferred_element_type=F32)
        yield

    def slc_spans(i, out):
        lo, hi = i * QB, (i + 1) * QB
        spans = [(lo, hi, nb_ref[0, 0])]
        if i >= 1:
            drop = out["drop", i]
            spans.insert(0, (lo - QB, lo, nb_ref[0, 1] + drop[None, :, lo - QB:lo]))
            if i >= 2:
                spans.insert(0, (0, lo - QB, drop[None, :, 0:lo - QB]))
        return spans

    def win_spans(i):
        lo, hi = i * QB, (i + 1) * QB
        spans = []
        if i >= nwb:
            spans.append(((i - nwb) * QB, (i - nwb + 1) * QB, nb_ref[0, 2]))
        mid_a, mid_b = max(i - nwb + 1, 0) * QB, (i - 1) * QB
        if mid_b > mid_a:
            spans.append((mid_a, mid_b, None))
        if i >= 1:
            spans.append((lo - QB, lo, nb_ref[0, 1]))
        spans.append((lo, hi, nb_ref[0, 0]))
        return spans

    def combine(i, out):
        lo, hi = i * QB, (i + 1) * QB
        gate = gate_ref[0, lo:hi, :]
        first = pl.program_id(1) == 0

        def gcol(h, br):
            c = 3 * h + br
            return jnp.where(first, gate[:, c:c + 1], gate[:, 3 * H + c:3 * H + c + 1])

        outs = []
        for h in range(H):
            sl = slice(h * QB, (h + 1) * QB)
            outs.append(gcol(h, 0) * out["cmp", i][sl] + gcol(h, 1) * out["slc", i][sl]
                        + gcol(h, 2) * out["win", i][sl])
        o_ref[0, lo:hi, :] = jnp.concatenate(outs, axis=1).astype(o_ref.dtype)

    nb = T // QB

    def pair(k):
        out = {}
        tasks = []
        blocks = (nb - 1 - k, k)
        qss = {}
        for i in blocks:
            q = q_ref[0, i * QB:(i + 1) * QB, :]
            qss[i] = jnp.concatenate([q[:, h * HEAD_DIM:(h + 1) * HEAD_DIM] for h in range(H)], axis=0)
            tasks.append(cmp_task(i, qss[i], out))
            tasks.append(attend_task(qss[i], kw_ref, vw_ref, functools.partial(win_spans, i), out, ("win", i)))
        for i in blocks:
            tasks.append(attend_task(qss[i], ks_ref, vs_ref, functools.partial(slc_spans, i, out), out, ("slc", i)))
        n_stage = 3
        for step in range(len(tasks) + n_stage - 1):
            for t in range(step - n_stage + 1, step + 1):
                if 0 <= t < len(tasks):
                    next(tasks[t])
        for i in blocks:
            combine(i, out)

    for k in range(nb // 2):
        pl.when(pl.program_id(2) == k)(functools.partial(pair, k))


def _nsa(qn, kv, kc, vc, gates, cbias, cmp_u0, cmp_per, nbias):
    B, T, _ = qn.shape
    G, H, dk = NSA_KV_GROUPS, NSA_HPG, HEAD_DIM
    nb = T // QB
    ns = T // SEL_BLOCK
    cstart = np.arange(LANE) * CMP_STRIDE
    sstart = np.arange(ns) * SEL_BLOCK
    overlap = np.clip(np.minimum(cstart[:, None] + CMP_LEN, sstart[None, :] + SEL_BLOCK)
                      - np.maximum(cstart[:, None], sstart[None, :]), 0, None) / CMP_STRIDE
    overlap[LANE - 1:] = 0.0
    ovt = jnp.asarray(overlap.T, F32)
    expand = np.zeros((LANE, T), np.float32)
    expand[np.arange(T) // SEL_BLOCK, np.arange(T)] = MASK_BIG
    return pl.pallas_call(
        functools.partial(_nsa_kernel, cmp_u0=cmp_u0, cmp_per=cmp_per),
        grid=(B, G, nb // 2),
        in_specs=[pl.BlockSpec((1, T, H * dk), lambda b, g, i: (b, 0, g)),
                  pl.BlockSpec((1, T, dk), lambda b, g, i: (b, 0, g)),
                  pl.BlockSpec((1, T, dk), lambda b, g, i: (b, 0, G + g)),
                  pl.BlockSpec((1, T, dk), lambda b, g, i: (b, 0, 2 * G + g)),
                  pl.BlockSpec((1, T, dk), lambda b, g, i: (b, 0, 3 * G + g)),
                  pl.BlockSpec((1, 1, LANE, dk), lambda b, g, i: (b, g, 0, 0)),
                  pl.BlockSpec((1, 1, LANE, dk), lambda b, g, i: (b, g, 0, 0)),
                  pl.BlockSpec((1, T, LANE), lambda b, g, i: (b, 0, 0)),
                  pl.BlockSpec((1, 1, H, QB, 2 * LANE), lambda b, g, i: (g, 0, 0, 0, 0)),
                  pl.BlockSpec((1, 3, H, QB, QB), lambda b, g, i: (g, 0, 0, 0, 0)),
                  pl.BlockSpec((ns, LANE), lambda b, g, i: (0, 0)),
                  pl.BlockSpec((LANE, T), lambda b, g, i: (0, 0))],
        out_specs=pl.BlockSpec((1, T, H * dk), lambda b, g, i: (b, 0, g)),
        out_shape=jax.ShapeDtypeStruct((B, T, NSA_HEADS * dk), BF16),
        compiler_params=_cparams(3),
        name="nsa",
    )(qn, kv, kv, kv, kv, kc, vc, gates, cbias, nbias, ovt, jnp.asarray(expand, BF16))


def _diff_kernel(q_ref, k_ref, v_ref, lq_ref, lk_ref, sg_ref, db_ref, o_ref):
    dk = HEAD_DIM
    T = k_ref.shape[1]
    lqk = lq_ref[...] * lk_ref[...]
    lam = (jnp.exp(jnp.sum(lqk[0:1], axis=-1, keepdims=True))
           - jnp.exp(jnp.sum(lqk[1:2], axis=-1, keepdims=True)) + LAM_INIT)
    def task(i, mm, out):
        lo, hi = i * DQB, (i + 1) * DQB
        cols = slice(mm * dk, (mm + 1) * dk)
        q = q_ref[0, lo:hi, cols]
        bounds, parts = [], []
        if i >= 2:
            bounds.append((0, lo - DQB))
            parts.append(_qk(q, k_ref[0, 0:lo - DQB, cols]))
        if i >= 1:
            bounds.append((lo - DQB, lo))
            parts.append(_qk(q, k_ref[0, lo - DQB:lo, cols]) + db_ref[0, 1, 0])
        bounds.append((lo, hi))
        parts.append(_qk(q, k_ref[0, lo:hi, cols]) + db_ref[0, 0, 0])
        yield
        ps, den = _softmax_parts(parts)
        yield
        o = None
        for (a, b), p in zip(bounds, ps):
            pv = jnp.dot(p.astype(BF16), v_ref[0, a:b, :], preferred_element_type=F32)
            o = pv if o is None else o + pv
        out[i, mm] = o / den
        yield

    out = {}
    tasks = [task(i, mm, out) for i in reversed(range(T // DQB)) for mm in range(2)]
    n_stage = 3
    for step in range(len(tasks) + n_stage - 1):
        for t in range(step - n_stage + 1, step + 1):
            if 0 <= t < len(tasks):
                next(tasks[t])

    for i in range(T // DQB):
        lo, hi = i * DQB, (i + 1) * DQB
        o = out[i, 0] - lam * out[i, 1]
        ms = jnp.mean(o * o, axis=-1, keepdims=True)
        o_ref[0, lo:hi, :] = ((o * lax.rsqrt(ms + EPS) * sg_ref[...]) * (1.0 - LAM_INIT)).astype(o_ref.dtype)


def _diff(dqk, dv, lam_q, lam_k, subln_gain, dbias):
    B, T, _ = dqk.shape
    Hd, dk = DIFF_HEADS, HEAD_DIM
    w = 2 * dk
    return pl.pallas_call(
        _diff_kernel,
        grid=(B, Hd),
        in_specs=[pl.BlockSpec((1, T, w), lambda b, h: (b, 0, h)),
                  pl.BlockSpec((1, T, w), lambda b, h: (b, 0, Hd + h)),
                  pl.BlockSpec((1, T, w), lambda b, h: (b, 0, h)),
                  pl.BlockSpec((2, dk), lambda b, h: (0, 0)),
                  pl.BlockSpec((2, dk), lambda b, h: (0, 0)),
                  pl.BlockSpec((1, w), lambda b, h: (0, 0)),
                  pl.BlockSpec((1, 2, 1, DQB, DQB), lambda b, h: (h, 0, 0, 0, 0))],
        out_specs=pl.BlockSpec((1, T, w), lambda b, h: (b, 0, h)),
        out_shape=jax.ShapeDtypeStruct((B, T, Hd * w), BF16),
        compiler_params=_cparams(2),
        name="diff",
    )(dqk, dqk, dv, lam_q, lam_k, subln_gain.reshape(1, w), dbias)


def _merge_kernel(an_ref, ad_ref, wn_ref, wd_ref, gn_ref, gd_ref, o_ref, wnb_ref, wdb_ref):
    @pl.when(pl.program_id(1) == 0)
    def _():
        wnb_ref[...] = wn_ref[...].astype(BF16)
        wdb_ref[...] = wd_ref[...].astype(BF16)

    for rows in _row_chunks(an_ref.shape[0]):
        yn = jnp.dot(an_ref[rows, :], wnb_ref[...], preferred_element_type=F32)
        yd = jnp.dot(ad_ref[rows, :], wdb_ref[...], preferred_element_type=F32)
        o_ref[rows, :] = (gn_ref[rows, :].astype(F32) * yn + gd_ref[rows, :].astype(F32) * yd).astype(o_ref.dtype)


def _merge(o_nsa, o_diff, w_n, w_d, mg, tm=1024, tn=1024):
    M, K = o_nsa.shape
    N = w_n.shape[1]
    nj = N // tn
    return pl.pallas_call(
        _merge_kernel,
        grid=(nj, M // tm),
        in_specs=[pl.BlockSpec((tm, K), lambda j, i: (i, 0)),
                  pl.BlockSpec((tm, K), lambda j, i: (i, 0)),
                  pl.BlockSpec((K, tn), lambda j, i: (0, j)),
                  pl.BlockSpec((K, tn), lambda j, i: (0, j)),
                  pl.BlockSpec((tm, tn), lambda j, i: (i, j)),
                  pl.BlockSpec((tm, tn), lambda j, i: (i, nj + j))],
        out_specs=pl.BlockSpec((tm, tn), lambda j, i: (i, j)),
        out_shape=jax.ShapeDtypeStruct((M, N), BF16),
        scratch_shapes=[pltpu.VMEM((K, tn), BF16), pltpu.VMEM((K, tn), BF16)],
        compiler_params=_cparams(2),
        name="merge",
    )(o_nsa, o_diff, w_n, w_d, mg, mg)


def _oproj_kernel(a_ref, w_ref, x_ref, g1_ref, gain_ref, sc_ref, sh_ref, x1_ref, h2_ref):
    for rows in _row_chunks(a_ref.shape[0]):
        y = jnp.dot(a_ref[rows, :], w_ref[...], preferred_element_type=F32)
        x1 = x_ref[rows, :] + g1_ref[0] * y
        x1_ref[rows, :] = x1
        h2_ref[rows, :] = _modnorm(x1, gain_ref[...], sc_ref[0], sh_ref[0]).astype(h2_ref.dtype)


def _oproj(merged, w_o, x2d, mod3, gain2, T, tm=512):
    M, D = x2d.shape
    per = T // tm
    return pl.pallas_call(
        _oproj_kernel,
        grid=(M // tm,),
        in_specs=[pl.BlockSpec((tm, D), lambda i: (i, 0)),
                  pl.BlockSpec((D, D), lambda i: (0, 0)),
                  pl.BlockSpec((tm, D), lambda i: (i, 0)),
                  pl.BlockSpec((1, 1, D), lambda i: ((i // per) * 6 + 2, 0, 0)),
                  pl.BlockSpec((1, D), lambda i: (0, 0)),
                  pl.BlockSpec((1, 1, D), lambda i: ((i // per) * 6 + 4, 0, 0)),
                  pl.BlockSpec((1, 1, D), lambda i: ((i // per) * 6 + 3, 0, 0))],
        out_specs=[pl.BlockSpec((tm, D), lambda i: (i, 0)),
                   pl.BlockSpec((tm, D), lambda i: (i, 0))],
        out_shape=[jax.ShapeDtypeStruct((M, D), F32), jax.ShapeDtypeStruct((M, D), BF16)],
        compiler_params=_cparams(1),
        name="oproj",
    )(merged, w_o, x2d, mod3, gain2.reshape(1, D), mod3, mod3)


def _ffn_up_kernel(h_ref, wa_ref, wv_ref, cwa_ref, cwv_ref, cba_ref, cbv_ref, o_ref, wab_ref, wvb_ref,
                   ca_ref, cv_ref, sa_ref, sv_ref, *, per):
    i = pl.program_id(1)

    @pl.when(i == 0)
    def _():
        wab_ref[...] = wa_ref[...].astype(BF16)
        wvb_ref[...] = wv_ref[...].astype(BF16)

    @pl.when(i % per == 0)
    def _():
        ca_ref[...] = jnp.zeros(ca_ref.shape, F32)
        cv_ref[...] = jnp.zeros(cv_ref.shape, F32)

    def conv(u, prev, cw_ref, cb_ref, s_ref):
        s_ref[0:8, :] = prev
        s_ref[8:8 + SUB_ROWS, :] = u
        u1 = s_ref[7:7 + SUB_ROWS, :]
        u2 = s_ref[6:6 + SUB_ROWS, :]
        return cb_ref[...] + cw_ref[0:1, :] * u2 + cw_ref[1:2, :] * u1 + cw_ref[2:3, :] * u

    prev_a, prev_v = ca_ref[...], cv_ref[...]
    for n, rows in enumerate(_row_chunks(h_ref.shape[0])):
        hs = h_ref[rows, :]
        ua = jnp.dot(hs, wab_ref[...], preferred_element_type=F32)
        uv = jnp.dot(hs, wvb_ref[...], preferred_element_type=F32)
        a = conv(ua, prev_a, cwa_ref, cba_ref, sa_ref.at[n % 2])
        val = conv(uv, prev_v, cwv_ref, cbv_ref, sv_ref.at[n % 2])
        o_ref[rows, :] = (a * jax.nn.sigmoid(a) * val).astype(o_ref.dtype)
        prev_a, prev_v = ua[SUB_ROWS - 8:, :], uv[SUB_ROWS - 8:, :]
    ca_ref[...] = prev_a
    cv_ref[...] = prev_v


def _ffn_up(h2, w_up, conv_w, conv_b, T, tm=2048, tn=512):
    M, D = h2.shape
    F = w_up.shape[1] // 2
    nj = F // tn
    cb = conv_b.reshape(1, 2 * F)
    return pl.pallas_call(
        functools.partial(_ffn_up_kernel, per=T // tm),
        grid=(nj, M // tm),
        in_specs=[pl.BlockSpec((tm, D), lambda j, i: (i, 0)),
                  pl.BlockSpec((D, tn), lambda j, i: (0, j)),
                  pl.BlockSpec((D, tn), lambda j, i: (0, nj + j)),
                  pl.BlockSpec((3, tn), lambda j, i: (0, j)),
                  pl.BlockSpec((3, tn), lambda j, i: (0, nj + j)),
                  pl.BlockSpec((1, tn), lambda j, i: (0, j)),
                  pl.BlockSpec((1, tn), lambda j, i: (0, nj + j))],
        out_specs=pl.BlockSpec((tm, tn), lambda j, i: (i, j)),
        out_shape=jax.ShapeDtypeStruct((M, F), BF16),
        scratch_shapes=[pltpu.VMEM((D, tn), BF16), pltpu.VMEM((D, tn), BF16),
                        pltpu.VMEM((8, tn), F32), pltpu.VMEM((8, tn), F32),
                        pltpu.VMEM((2, SUB_ROWS + 8, tn), F32), pltpu.VMEM((2, SUB_ROWS + 8, tn), F32)],
        compiler_params=_cparams(2),
        name="ffn_up",
    )(h2, w_up, w_up, conv_w, conv_w, cb, cb)


def _ffn_down_kernel(a_ref, w_ref, x_ref, g2_ref, o_ref, wb_ref):
    @pl.when(pl.program_id(1) == 0)
    def _():
        wb_ref[...] = w_ref[...].astype(BF16)

    for rows in _row_chunks(a_ref.shape[0]):
        y = jnp.dot(a_ref[rows, :], wb_ref[...], preferred_element_type=F32)
        o_ref[rows, :] = x_ref[rows, :] + g2_ref[0] * y


def _ffn_down(act, w_down, x1, mod3, T, tm=512, tn=512):
    M, F = act.shape
    D = w_down.shape[1]
    per = T // tm
    return pl.pallas_call(
        _ffn_down_kernel,
        grid=(D // tn, M // tm),
        in_specs=[pl.BlockSpec((tm, F), lambda j, i: (i, 0)),
                  pl.BlockSpec((F, tn), lambda j, i: (0, j)),
                  pl.BlockSpec((tm, tn), lambda j, i: (i, j)),
                  pl.BlockSpec((1, 1, tn), lambda j, i: ((i // per) * 6 + 5, 0, j))],
        out_specs=pl.BlockSpec((tm, tn), lambda j, i: (i, j)),
        out_shape=jax.ShapeDtypeStruct((M, D), F32),
        scratch_shapes=[pltpu.VMEM((F, tn), BF16)],
        compiler_params=_cparams(2),
        name="ffn_down",
    )(act, w_down, x1, mod3)


def _layer(x, c, w_ada, b_ada, norm1_gain, norm2_gain, w_in, nsa_q_gain, nsa_k_gain, cmp_pe, cmp_w1, cmp_w2,
           diff_q_gain, diff_k_gain, diff_lambda_q, diff_lambda_k, diff_subln_gain, w_nsa_out, w_diff_out, w_o,
           w_ffn_up, ffn_conv_w, ffn_conv_b, w_ffn_down, rel_bias):
    B, T, D = x.shape
    dk, G = HEAD_DIM, NSA_KV_GROUPS
    M = B * T
    scale = dk ** -0.5

    n_q = NSA_HEADS * dk
    o_kv = n_q
    o_g = o_kv + 3 * 2 * G * dk
    o_dq = o_g + NSA_HEADS * 3
    o_dk = o_dq + DIFF_HEADS * 2 * dk
    o_dv = o_dk + DIFF_HEADS * 2 * dk
    o_mg = o_dv + DIFF_HEADS * 2 * dk
    n_kv = 2 * G * dk
    n_dqk = 2 * DIFF_HEADS * 2 * dk

    mod3 = _ada(c, w_ada, b_ada).reshape(B * 6, 1, D)
    h = _norm1(x, norm1_gain, mod3).reshape(M, D)

    ones = jnp.ones((n_kv // 2,), F32)
    g_q = jnp.tile(nsa_q_gain * (scale * LOG2E), NSA_HEADS).reshape(1, n_q)
    g_kv = jnp.concatenate([jnp.tile(nsa_k_gain[1], G), ones, jnp.tile(nsa_k_gain[2], G), ones]).reshape(1, 2 * n_kv)
    g_dqk = jnp.concatenate([jnp.tile(diff_q_gain * (scale * LOG2E), 2 * DIFF_HEADS),
                             jnp.tile(diff_k_gain, 2 * DIFF_HEADS)]).reshape(1, n_dqk)
    g_one = jnp.ones((1, 2 * D), F32)
    proj = functools.partial(_inproj, h, jnp.swapaxes(w_in, 1, 2), tm=INPROJ_TM)
    wide = INPROJ_TN
    yes, no = (True,) * (wide // LANE), (False,) * (wide // LANE)
    qn = proj(g_q, src0=0, ncols=n_q, heads=yes, mode="raw", out_dtype=BF16, tn=wide, name="inproj_q")
    cmpkv, gates = _inproj_cmp_gate(h, jnp.swapaxes(w_in, 1, 2), src_cmp=o_kv, n_cmp=n_kv, src_gate=o_g,
                                    tm=INPROJ_TM)
    kv_heads = ((True,) * G + (False,) * G) * 2
    kv = proj(g_kv, src0=o_kv + n_kv, ncols=2 * n_kv, heads=kv_heads, mode="raw", out_dtype=BF16, tn=2 * n_kv,
              name="inproj_kv")
    dqk = proj(g_dqk, src0=o_dq, ncols=n_dqk, heads=yes, mode="raw", out_dtype=BF16, tn=wide, name="inproj_dqk")
    dv = proj(g_one, src0=o_dv, ncols=o_mg - o_dv, heads=no, mode="raw", out_dtype=BF16, tn=wide, name="inproj_dv")
    mgate = proj(g_one, src0=o_mg, ncols=2 * D, heads=no, mode="sigmoid", out_dtype=BF16, tn=wide, name="inproj_mg")

    nbias = _bias_tiles(_window_idx(QB), rel_bias, head0=0, groups=G, hpg=NSA_HPG, name="bias_nsa",
                        rel=True, mult=LOG2E)
    cmp_idx, cmp_u0, cmp_per = _cmp_idx(T)
    cbias = _bias_tiles(cmp_idx, rel_bias, head0=0, groups=G, hpg=NSA_HPG, name="bias_cmp", mult=LOG2E)
    dbias = _bias_tiles(_causal_idx(DQB), rel_bias, head0=NSA_HEADS, groups=DIFF_HEADS, hpg=1, name="bias_diff",
                        rel=True, mult=LOG2E)

    kc, vc = _compress(cmpkv.reshape(B, T, n_kv), cmp_pe, cmp_w1, cmp_w2, nsa_k_gain[0])
    o_nsa = _nsa(qn.reshape(B, T, n_q), kv.reshape(B, T, 2 * n_kv), kc, vc, gates.reshape(B, T, LANE), cbias,
                 cmp_u0, cmp_per, nbias)
    o_diff = _diff(dqk.reshape(B, T, n_dqk), dv.reshape(B, T, -1), diff_lambda_q, diff_lambda_k, diff_subln_gain,
                   dbias)

    merged = _merge(o_nsa.reshape(M, -1), o_diff.reshape(M, -1), w_nsa_out, w_diff_out, mgate)
    x1, h2 = _oproj(merged, w_o.astype(BF16), x.reshape(M, D), mod3, norm2_gain, T)
    act = _ffn_up(h2, w_ffn_up, ffn_conv_w, ffn_conv_b, T)
    out = _ffn_down(act, w_ffn_down, x1, mod3, T)
    return out.reshape(B, T, D)


def kernel(x, c, w_ada, b_ada, norm1_gain, norm2_gain, w_in, nsa_q_gain, nsa_k_gain, cmp_pe, cmp_w1, cmp_w2,
           diff_q_gain, diff_k_gain, diff_lambda_q, diff_lambda_k, diff_subln_gain, w_nsa_out, w_diff_out, w_o,
           w_ffn_up, ffn_conv_w, ffn_conv_b, w_ffn_down, rel_bias):
    return _layer(x, c, w_ada[0], b_ada[0], norm1_gain[0], norm2_gain[0], w_in, nsa_q_gain[0], nsa_k_gain[0],
                  cmp_pe[0], cmp_w1[0], cmp_w2[0], diff_q_gain[0], diff_k_gain[0], diff_lambda_q[0],
                  diff_lambda_k[0], diff_subln_gain[0], w_nsa_out[0], w_diff_out[0], w_o[0], w_ffn_up[0],
                  ffn_conv_w[0], ffn_conv_b[0], w_ffn_down[0], rel_bias)
```

```python
import functools
import math

import numpy as np
import jax
import jax.numpy as jnp
from jax import lax
from jax.experimental import pallas as pl
from jax.experimental.pallas import tpu as pltpu

F32 = jnp.float32
BF16 = jnp.bfloat16

HEAD_DIM = 128
NSA_HEADS = 8
NSA_KV_GROUPS = 2
NSA_HPG = NSA_HEADS // NSA_KV_GROUPS
CMP_LEN = 32
CMP_STRIDE = 16
SEL_BLOCK = 64
N_SEL = 16
WINDOW = 512
DIFF_HEADS = 4
NUM_BUCKETS = 32
MAX_DISTANCE = 128
EPS = 1e-6
NEG = -1e30
LAM_INIT = 0.8 - 0.6 * math.exp(-0.3 * 0)
LOG2E = math.log2(math.e)
MASK_BIG = -(2.0 ** 100)

LANE = 128
QB = 128
DQB = 256
SUB_ROWS = 256
INPROJ_TM = 2048
INPROJ_TN = 1024
VMEM_LIMIT = 56 * 1024 * 1024


def _cparams(n_axes):
    return pltpu.CompilerParams(dimension_semantics=("arbitrary",) * n_axes,
                                vmem_limit_bytes=VMEM_LIMIT)


def _t5_bucket_np(dist):
    n = np.maximum(np.asarray(dist, np.int32), 0)
    max_exact = NUM_BUCKETS // 2
    nf = np.maximum(n, max_exact).astype(np.float32)
    large = max_exact + (np.log(nf / np.float32(max_exact)) / np.float32(math.log(MAX_DISTANCE / max_exact))
                         * np.float32(NUM_BUCKETS - max_exact)).astype(np.int32)
    large = np.minimum(large, NUM_BUCKETS - 1)
    return np.where(n < max_exact, n, large).astype(np.int32)


def _ada_kernel(ct_ref, w_ref, b_ref, o_ref):
    ct = ct_ref[...]
    s = ct * jax.nn.sigmoid(ct)
    w = w_ref[...]
    for b in range(ct.shape[1]):
        o_ref[b:b + 1, :] = jnp.sum(w * s[:, b:b + 1], axis=0, keepdims=True) + b_ref[...]


def _ada(c, w_ada, b_ada, tn=1024):
    B, D = c.shape
    N = w_ada.shape[1]
    return pl.pallas_call(
        _ada_kernel,
        grid=(N // tn,),
        in_specs=[pl.BlockSpec((D, B), lambda j: (0, 0)),
                  pl.BlockSpec((D, tn), lambda j: (0, j)),
                  pl.BlockSpec((1, tn), lambda j: (0, j))],
        out_specs=pl.BlockSpec((B, tn), lambda j: (0, j)),
        out_shape=jax.ShapeDtypeStruct((B, N), F32),
        compiler_params=_cparams(1),
        name="ada",
    )(c.T, w_ada, b_ada.reshape(1, N))


def _modnorm(x, gain, sc, sh):
    ms = jnp.mean(x * x, axis=-1, keepdims=True)
    return (x * lax.rsqrt(ms + EPS) * gain) * (1.0 + sc) + sh


def _norm1_kernel(x_ref, g_ref, sc_ref, sh_ref, o_ref):
    o_ref[0] = _modnorm(x_ref[0], g_ref[...], sc_ref[0], sh_ref[0]).astype(o_ref.dtype)


def _norm1(x, gain, mod3, tm=512):
    B, T, D = x.shape
    return pl.pallas_call(
        _norm1_kernel,
        grid=(B, T // tm),
        in_specs=[pl.BlockSpec((1, tm, D), lambda b, i: (b, i, 0)),
                  pl.BlockSpec((1, D), lambda b, i: (0, 0)),
                  pl.BlockSpec((1, 1, D), lambda b, i: (b * 6 + 1, 0, 0)),
                  pl.BlockSpec((1, 1, D), lambda b, i: (b * 6 + 0, 0, 0))],
        out_specs=pl.BlockSpec((1, tm, D), lambda b, i: (b, i, 0)),
        out_shape=jax.ShapeDtypeStruct((B, T, D), BF16),
        compiler_params=_cparams(2),
        name="norm1",
    )(x, gain.reshape(1, D), mod3, mod3)


def _row_chunks(tm):
    return [slice(r, r + SUB_ROWS) for r in range(0, tm, SUB_ROWS)]


def _inproj_kernel(a_ref, wt_ref, g_ref, o_ref, wb_ref, *, heads, mode):
    @pl.when(pl.program_id(1) == 0)
    def _():
        wb_ref[...] = wt_ref[0].astype(BF16)

    for rows in _row_chunks(a_ref.shape[0]):
        acc = _qk(a_ref[rows, :], wb_ref[...])
        for k, is_head in enumerate(heads):
            lanes = slice(k * LANE, (k + 1) * LANE)
            y = acc[:, lanes]
            if is_head:
                ms = jnp.mean(y * y, axis=-1, keepdims=True)
                y = y * lax.rsqrt(ms + EPS) * g_ref[:, lanes]
            elif mode == "sigmoid":
                y = jax.nn.sigmoid(y)
            o_ref[rows, lanes] = y.astype(o_ref.dtype)


def _inproj(h2d, w_in_t, gains, *, src0, ncols, heads, mode, out_dtype, tm, tn, name):
    M, K = h2d.shape
    assert src0 % 8 == 0 and ncols % tn == 0 and len(heads) == tn // LANE
    return pl.pallas_call(
        functools.partial(_inproj_kernel, heads=heads, mode=mode),
        grid=(ncols // tn, M // tm),
        in_specs=[pl.BlockSpec((tm, K), lambda j, i: (i, 0)),
                  pl.BlockSpec((pl.Element(1), pl.Element(tn), pl.Element(K)),
                               lambda j, i: (0, pl.multiple_of(src0 + j * tn, 8), 0)),
                  pl.BlockSpec((1, tn), lambda j, i: (0, j))],
        out_specs=pl.BlockSpec((tm, tn), lambda j, i: (i, j)),
        out_shape=jax.ShapeDtypeStruct((M, ncols), out_dtype),
        scratch_shapes=[pltpu.VMEM((tn, K), BF16)],
        compiler_params=_cparams(2),
        name=name,
    )(h2d, w_in_t, gains)


def _inproj_cmp_gate_kernel(a_ref, wc_ref, wg_ref, oc_ref, og_ref, wcb_ref, wgb_ref):
    @pl.when(pl.program_id(0) == 0)
    def _():
        wcb_ref[...] = wc_ref[0].astype(BF16)
        wgb_ref[...] = wg_ref[0].astype(BF16)

    for rows in _row_chunks(a_ref.shape[0]):
        a = a_ref[rows, :]
        oc_ref[rows, :] = _qk(a, wcb_ref[...])
        og_ref[rows, :] = jax.nn.sigmoid(_qk(a, wgb_ref[...]))


def _inproj_cmp_gate(h2d, w_in_t, *, src_cmp, n_cmp, src_gate, tm):
    M, K = h2d.shape
    assert src_cmp % 8 == 0 and src_gate % 8 == 0

    def w_rows(start, n):
        return pl.BlockSpec((pl.Element(1), pl.Element(n), pl.Element(K)), lambda i: (0, start, 0))

    return pl.pallas_call(
        _inproj_cmp_gate_kernel,
        grid=(M // tm,),
        in_specs=[pl.BlockSpec((tm, K), lambda i: (i, 0)), w_rows(src_cmp, n_cmp), w_rows(src_gate, LANE)],
        out_specs=[pl.BlockSpec((tm, n_cmp), lambda i: (i, 0)), pl.BlockSpec((tm, LANE), lambda i: (i, 0))],
        out_shape=[jax.ShapeDtypeStruct((M, n_cmp), F32), jax.ShapeDtypeStruct((M, LANE), F32)],
        scratch_shapes=[pltpu.VMEM((n_cmp, K), BF16), pltpu.VMEM((LANE, K), BF16)],
        compiler_params=_cparams(1),
        name="inproj_cmp_gate",
    )(h2d, w_in_t, w_in_t)


def _bias_kernel(idx_ref, tab_ref, o_ref, *, head0, hpg, rel, mult):
    head = head0 + pl.program_id(0) * hpg + pl.program_id(2)
    idx = idx_ref[0]
    acc = jnp.zeros(idx.shape, F32)
    for b in range(NUM_BUCKETS):
        acc = jnp.where(idx == b, tab_ref[b, head], acc)
    if rel:
        acc = acc - tab_ref[NUM_BUCKETS - 1, head]
    o_ref[0, 0] = jnp.where(idx < 0, NEG, acc * mult)


def _bias_tiles(idx, rel_bias, *, head0, groups, hpg, name, rel=False, mult=1.0):
    N, R, C = idx.shape
    return pl.pallas_call(
        functools.partial(_bias_kernel, head0=head0, hpg=hpg, rel=rel, mult=mult),
        grid=(groups, N, hpg),
        in_specs=[pl.BlockSpec((1, R, C), lambda g, n, h: (n, 0, 0)),
                  pl.BlockSpec(memory_space=pltpu.SMEM)],
        out_specs=pl.BlockSpec((1, 1, R, C), lambda g, n, h: (g, n, 0, h)),
        out_shape=jax.ShapeDtypeStruct((groups, N, R, hpg * C), F32),
        compiler_params=_cparams(3),
        name=name,
    )(jnp.asarray(idx), rel_bias)


def _causal_idx(R):
    r = np.arange(R)[:, None]
    c = np.arange(R)[None, :]
    return np.stack([np.where(r >= c, _t5_bucket_np(r - c), -1), _t5_bucket_np(R + r - c)]).astype(np.int32)


def _window_idx(R):
    r = np.arange(R)[:, None]
    c = np.arange(R)[None, :]
    edge = np.where(r < c, NUM_BUCKETS - 1, -1)
    return np.concatenate([_causal_idx(R), edge[None]]).astype(np.int32)


def _cmp_idx(T):
    per = QB // CMP_STRIDE
    u0 = (T // QB - 1) * per
    assert u0 + LANE <= 2 * LANE
    r = np.arange(QB)[:, None]
    end = (np.arange(2 * LANE)[None, :] - u0) * CMP_STRIDE + CMP_LEN - 1
    return np.where(end <= r, _t5_bucket_np(r - end), -1).astype(np.int32)[None], u0, per


def _compress_kernel(zk_ref, zv_ref, pe_ref, w1_ref, w2_ref, kg_ref, kc_ref, vc_ref):
    half = CMP_LEN // 2

    def one(z_ref, i):
        p1 = jnp.zeros((LANE, HEAD_DIM), F32)
        p2 = jnp.zeros((LANE, HEAD_DIM), F32)
        for l in range(half):
            z = z_ref[0, pl.ds(l, LANE, stride=CMP_STRIDE), :]
            w_lo = w1_ref[i, l * HEAD_DIM:(l + 1) * HEAD_DIM, :].astype(BF16)
            w_hi = w1_ref[i, (half + l) * HEAD_DIM:(half + l + 1) * HEAD_DIM, :].astype(BF16)
            p1 = p1 + jnp.dot((z + pe_ref[i, l:l + 1, :]).astype(BF16), w_lo, preferred_element_type=F32)
            p2 = p2 + jnp.dot((z + pe_ref[i, half + l:half + l + 1, :]).astype(BF16), w_hi,
                              preferred_element_type=F32)
        pre = p1 + pltpu.roll(p2, LANE - 1, axis=0)
        hid = jax.nn.gelu(pre)
        return jnp.dot(hid.astype(BF16), w2_ref[i].astype(BF16), preferred_element_type=F32)

    kc = one(zk_ref, 0)
    ms = jnp.mean(kc * kc, axis=-1, keepdims=True)
    kc_ref[0, 0] = (kc * lax.rsqrt(ms + EPS) * kg_ref[...]).astype(kc_ref.dtype)
    vc_ref[0, 0] = one(zv_ref, 1).astype(vc_ref.dtype)


def _compress(cmp_kv, pe, w1, w2, k_gain0):
    B, T, _ = cmp_kv.shape
    G, dk = NSA_KV_GROUPS, HEAD_DIM
    assert (T - CMP_LEN) // CMP_STRIDE + 1 == LANE - 1
    out = jax.ShapeDtypeStruct((B, G, LANE, dk), BF16)
    return pl.pallas_call(
        _compress_kernel,
        grid=(B, G),
        in_specs=[pl.BlockSpec((1, T, dk), lambda b, g: (b, 0, g)),
                  pl.BlockSpec((1, T, dk), lambda b, g: (b, 0, G + g)),
                  pl.BlockSpec((2, CMP_LEN, dk), lambda b, g: (0, 0, 0)),
                  pl.BlockSpec((2, CMP_LEN * dk, dk), lambda b, g: (0, 0, 0)),
                  pl.BlockSpec((2, dk, dk), lambda b, g: (0, 0, 0)),
                  pl.BlockSpec((1, dk), lambda b, g: (0, 0))],
        out_specs=[pl.BlockSpec((1, 1, LANE, dk), lambda b, g: (b, g, 0, 0))] * 2,
        out_shape=[out, out],
        compiler_params=_cparams(2),
        name="compress",
    )(cmp_kv, cmp_kv, pe, w1, w2, k_gain0.reshape(1, dk))


def _qk(q, k):
    return lax.dot_general(q, k, (((1,), (1,)), ((), ())), preferred_element_type=F32)


def _lane_fold(x, op):
    acc = x[..., :LANE]
    for t in range(1, x.shape[-1] // LANE):
        acc = op(acc, x[..., t * LANE:(t + 1) * LANE])
    return acc


def _softmax_parts(parts):
    m = jnp.max(functools.reduce(jnp.maximum, [_lane_fold(s, jnp.maximum) for s in parts]), axis=-1, keepdims=True)
    ps = [jnp.exp2(s - m) for s in parts]
    den = jnp.sum(functools.reduce(jnp.add, [_lane_fold(p, jnp.add) for p in ps]), axis=-1, keepdims=True)
    return ps, den


def _nsa_kernel(q_ref, ks_ref, vs_ref, kw_ref, vw_ref, kc_ref, vc_ref, gate_ref, cb_ref, nb_ref,
                ovt_ref, exp_ref, o_ref, *, cmp_u0, cmp_per):
    H, R = NSA_HPG, NSA_HPG * QB
    T = ks_ref.shape[1]
    ns = ovt_ref.shape[0]
    nwb = WINDOW // QB
    row = lax.broadcasted_iota(jnp.int32, (QB, LANE), 0)
    col = lax.broadcasted_iota(jnp.int32, (QB, LANE), 1)
    blk = lax.broadcasted_iota(jnp.int32, (ns, QB), 0)
    tl = lax.broadcasted_iota(jnp.int32, (ns, QB), 1)
    kc = kc_ref[0, 0]
    vc = vc_ref[0, 0]

    def attend_task(qs, k_ref, v_ref, spans_fn, out, key):
        spans = spans_fn()
        parts = []
        for a, b, add in spans:
            s = _qk(qs, k_ref[0, a:b, :]).reshape(H, QB, b - a)
            parts.append(s if add is None else s + add)
        yield
        ps, den = _softmax_parts(parts)
        yield
        o = None
        for (a, b, _), p in zip(spans, ps):
            pv = jnp.dot(p.reshape(R, b - a).astype(BF16), v_ref[0, a:b, :], preferred_element_type=F32)
            o = pv if o is None else o + pv
        out[key] = o / den.reshape(R, 1)
        yield

    def cmp_task(i, qs, out):
        lo = i * QB
        u = cmp_u0 - cmp_per * i
        s = _qk(qs, kc).reshape(H, QB, LANE) + cb_ref[0, 0][:, :, u:u + LANE]
        yield
        e = jnp.exp2(s - jnp.max(s, axis=-1, keepdims=True))
        if i == 0:
            valid = (col * CMP_STRIDE + (CMP_LEN - 1)) <= row
            e = jnp.where(valid[None], e, 0.0)
            den = jnp.sum(e, axis=-1, keepdims=True)
            p = e / jnp.where(den > 0.0, den, 1.0)
        else:
            p = e / jnp.sum(e, axis=-1, keepdims=True)
        if i >= 1:
            psum = p[0] + p[1] + p[2] + p[3]
            imp_t = lax.dot_general(ovt_ref[...], psum, (((1,), (1,)), ((), ())),
                                    precision=lax.Precision.HIGHEST, preferred_element_type=F32)
        yield
        out["cmp", i] = jnp.dot(p.reshape(R, LANE).astype(BF16), vc, preferred_element_type=F32)
        if i >= 1:
            cur = (lo + tl) // SEL_BLOCK
            forced = (blk == 0) | (blk == cur) | (blk == cur - 1)
            score = jnp.where(forced, 1e4, jnp.where(blk <= cur, imp_t, -1e4))
            rank = jnp.zeros((ns, QB), F32)
            for b in range(ns):
                other = score[b:b + 1, :]
                rank = rank + jnp.where(blk > b, jnp.where(other >= score, 1.0, 0.0),
                                        jnp.where(other > score, 1.0, 0.0))
            unsel_t = jnp.where(rank < float(min(N_SEL, ns)), 0.0, 1.0)
            unsel = jnp.concatenate([unsel_t, jnp.zeros((LANE - ns, QB), F32)], axis=0).T
            out["drop", i] = jnp.dot(unsel.astype(BF16), exp_ref[:, 0:lo], preferred_element_type=F32)
        yield

    def slc_spans(i, out):
        lo, hi = i * QB, (i + 1) * QB
        spans = [(lo, hi, nb_ref[0, 0])]
        if i >= 1:
            drop = out["drop", i]
            spans.insert(0, (lo - QB, lo, nb_ref[0, 1] + drop[None, :, lo - QB:lo]))
            if i >= 2:
                spans.insert(0, (0, lo - QB, drop[None, :, 0:lo - QB]))
        return spans

    def win_spans(i):
        lo, hi = i * QB, (i + 1) * QB
        spans = []
        if i >= nwb:
            spans.append(((i - nwb) * QB, (i - nwb + 1) * QB, nb_ref[0, 2]))
        mid_a, mid_b = max(i - nwb + 1, 0) * QB, (i - 1) * QB
        if mid_b > mid_a:
            spans.append((mid_a, mid_b, None))
        if i >= 1:
            spans.append((lo - QB, lo, nb_ref[0, 1]))
        spans.append((lo, hi, nb_ref[0, 0]))
        return spans

    def combine(i, out):
        lo, hi = i * QB, (i + 1) * QB
        gate = gate_ref[0, lo:hi, :]
        first = pl.program_id(1) == 0

        def gcol(h, br):
            c = 3 * h + br
            return jnp.where(first, gate[:, c:c + 1], gate[:, 3 * H + c:3 * H + c + 1])

        outs = []
        for h in range(H):
            sl = slice(h * QB, (h + 1) * QB)
            outs.append(gcol(h, 0) * out["cmp", i][sl] + gcol(h, 1) * out["slc", i][sl]
                        + gcol(h, 2) * out["win", i][sl])
        o_ref[0, lo:hi, :] = jnp.concatenate(outs, axis=1).astype(o_ref.dtype)

    nb = T // QB

    def pair(k):
        out = {}
        tasks = []
        blocks = (nb - 1 - k, k)
        qss = {}
        for i in blocks:
            q = q_ref[0, i * QB:(i + 1) * QB, :]
            qss[i] = jnp.concatenate([q[:, h * HEAD_DIM:(h + 1) * HEAD_DIM] for h in range(H)], axis=0)
            tasks.append(cmp_task(i, qss[i], out))
            tasks.append(attend_task(qss[i], kw_ref, vw_ref, functools.partial(win_spans, i), out, ("win", i)))
        for i in blocks:
            tasks.append(attend_task(qss[i], ks_ref, vs_ref, functools.partial(slc_spans, i, out), out, ("slc", i)))
        n_stage = 3
        for step in range(len(tasks) + n_stage - 1):
            for t in range(step - n_stage + 1, step + 1):
                if 0 <= t < len(tasks):
                    next(tasks[t])
        for i in blocks:
            combine(i, out)

    for k in range(nb // 2):
        pl.when(pl.program_id(2) == k)(functools.partial(pair, k))


def _nsa(qn, kv, kc, vc, gates, cbias, cmp_u0, cmp_per, nbias):
    B, T, _ = qn.shape
    G, H, dk = NSA_KV_GROUPS, NSA_HPG, HEAD_DIM
    nb = T // QB
    ns = T // SEL_BLOCK
    cstart = np.arange(LANE) * CMP_STRIDE
    sstart = np.arange(ns) * SEL_BLOCK
    overlap = np.clip(np.minimum(cstart[:, None] + CMP_LEN, sstart[None, :] + SEL_BLOCK)
                      - np.maximum(cstart[:, None], sstart[None, :]), 0, None) / CMP_STRIDE
    overlap[LANE - 1:] = 0.0
    ovt = jnp.asarray(overlap.T, F32)
    expand = np.zeros((LANE, T), np.float32)
    expand[np.arange(T) // SEL_BLOCK, np.arange(T)] = MASK_BIG
    return pl.pallas_call(
        functools.partial(_nsa_kernel, cmp_u0=cmp_u0, cmp_per=cmp_per),
        grid=(B, G, nb // 2),
        in_specs=[pl.BlockSpec((1, T, H * dk), lambda b, g, i: (b, 0, g)),
                  pl.BlockSpec((1, T, dk), lambda b, g, i: (b, 0, g)),
                  pl.BlockSpec((1, T, dk), lambda b, g, i: (b, 0, G + g)),
                  pl.BlockSpec((1, T, dk), lambda b, g, i: (b, 0, 2 * G + g)),
                  pl.BlockSpec((1, T, dk), lambda b, g, i: (b, 0, 3 * G + g)),
                  pl.BlockSpec((1, 1, LANE, dk), lambda b, g, i: (b, g, 0, 0)),
                  pl.BlockSpec((1, 1, LANE, dk), lambda b, g, i: (b, g, 0, 0)),
                  pl.BlockSpec((1, T, LANE), lambda b, g, i: (b, 0, 0)),
                  pl.BlockSpec((1, 1, H, QB, 2 * LANE), lambda b, g, i: (g, 0, 0, 0, 0)),
                  pl.BlockSpec((1, 3, H, QB, QB), lambda b, g, i: (g, 0, 0, 0, 0)),
                  pl.BlockSpec((ns, LANE), lambda b, g, i: (0, 0)),
                  pl.BlockSpec((LANE, T), lambda b, g, i: (0, 0))],
        out_specs=pl.BlockSpec((1, T, H * dk), lambda b, g, i: (b, 0, g)),
        out_shape=jax.ShapeDtypeStruct((B, T, NSA_HEADS * dk), BF16),
        compiler_params=_cparams(3),
        name="nsa",
    )(qn, kv, kv, kv, kv, kc, vc, gates, cbias, nbias, ovt, jnp.asarray(expand, BF16))


def _nsa_t_kernel(q_ref, ks_ref, vs_ref, kw_ref, vw_ref, kc_ref, vc_ref, gate_ref, cb_ref, nb_ref,
                  ovt_ref, expt_ref, o_ref, vst_ref, vwt_ref, vct_ref, *, cmp_u0, cmp_per):
    H, W = NSA_HPG, NSA_HPG * QB
    T = ks_ref.shape[1]
    ns = ovt_ref.shape[0]
    nwb = WINDOW // QB
    nb = T // QB
    kc = kc_ref[0, 0]

    @pl.when(pl.program_id(2) == 0)
    def _():
        vst_ref[...] = vs_ref[0].astype(F32).T.astype(BF16)
        vwt_ref[...] = vw_ref[0].astype(F32).T.astype(BF16)
        vct_ref[...] = vc_ref[0, 0].astype(F32).T.astype(BF16)

    def softmax_t(parts):
        m = functools.reduce(jnp.maximum, [jnp.max(s, axis=0, keepdims=True) for s in parts])
        ps = [jnp.exp2(s - m) for s in parts]
        den = functools.reduce(jnp.add, [jnp.sum(p, axis=0, keepdims=True) for p in ps])
        return [p.astype(BF16) for p in ps], den

    def attend_task(qs, k_ref, vt_ref, spans_fn, out, key):
        spans = spans_fn()
        parts = []
        for a, b, add in spans:
            s = _qk(k_ref[0, a:b, :], qs)
            parts.append(s if add is None else s + add)
        yield
        ps, den = softmax_t(parts)
        yield
        o = None
        for (a, b, _), p in zip(spans, ps):
            pv = jnp.dot(vt_ref[:, a:b], p, preferred_element_type=F32)
            o = pv if o is None else o + pv
        out[key] = o / den
        yield

    def cmp_task(i, qs, out):
        lo = i * QB
        u = cmp_u0 - cmp_per * i
        s = _qk(kc, qs) + cb_ref[0, 0, u:u + LANE, :]
        yield
        e = jnp.exp2(s - jnp.max(s, axis=0, keepdims=True))
        if i == 0:
            key_end = lax.broadcasted_iota(jnp.int32, (LANE, QB), 0) * CMP_STRIDE + (CMP_LEN - 1)
            valid = key_end <= lax.broadcasted_iota(jnp.int32, (LANE, QB), 1)
            e = jnp.where(pltpu.repeat(valid.astype(F32), H, axis=1) > 0.5, e, 0.0)
            den = jnp.sum(e, axis=0, keepdims=True)
            p = e / jnp.where(den > 0.0, den, 1.0)
        else:
            p = e / jnp.sum(e, axis=0, keepdims=True)
        if i >= 1:
            psum = functools.reduce(jnp.add, [p[:, h * QB:(h + 1) * QB] for h in range(H)])
            imp_t = jnp.dot(ovt_ref[...], psum, precision=lax.Precision.HIGHEST, preferred_element_type=F32)
        yield
        out["cmp", i] = jnp.dot(vct_ref[...], p.astype(BF16), preferred_element_type=F32)
        if i >= 1:
            blk = lax.broadcasted_iota(jnp.int32, (ns, QB), 0)
            cur = (lo + lax.broadcasted_iota(jnp.int32, (ns, QB), 1)) // SEL_BLOCK
            forced = (blk == 0) | (blk == cur) | (blk == cur - 1)
            score = jnp.where(forced, 1e4, jnp.where(blk <= cur, imp_t, -1e4))
            rank = jnp.zeros((ns, QB), F32)
            for b in range(ns):
                other = score[b:b + 1, :]
                rank = rank + jnp.where(blk > b, jnp.where(other >= score, 1.0, 0.0),
                                        jnp.where(other > score, 1.0, 0.0))
            unsel = jnp.where(rank < float(min(N_SEL, ns)), 0.0, 1.0)
            unsel = jnp.concatenate([unsel, jnp.zeros((LANE - ns, QB), F32)], axis=0).astype(BF16)
            drop = jnp.dot(expt_ref[0:lo, :], unsel, preferred_element_type=F32)
            out["drop", i] = pltpu.repeat(drop, H, axis=1)
        yield

    def slc_spans(i, out):
        lo, hi = i * QB, (i + 1) * QB
        spans = [(lo, hi, nb_ref[0, 0])]
        if i >= 1:
            drop = out["drop", i]
            spans.insert(0, (lo - QB, lo, nb_ref[0, 1] + drop[lo - QB:lo, :]))
            if i >= 2:
                spans.insert(0, (0, lo - QB, drop[0:lo - QB, :]))
        return spans

    def win_spans(i):
        lo, hi = i * QB, (i + 1) * QB
        spans = []
        if i >= nwb:
            spans.append(((i - nwb) * QB, (i - nwb + 1) * QB, nb_ref[0, 2]))
        mid_a, mid_b = max(i - nwb + 1, 0) * QB, (i - 1) * QB
        if mid_b > mid_a:
            spans.append((mid_a, mid_b, None))
        if i >= 1:
            spans.append((lo - QB, lo, nb_ref[0, 1]))
        spans.append((lo, hi, nb_ref[0, 0]))
        return spans

    def combine(i, out):
        lo, hi = i * QB, (i + 1) * QB
        gate_t = gate_ref[0, lo:hi, :].T
        first = pl.program_id(1) == 0

        def grow(br):
            rows = []
            for h in range(H):
                c = 3 * h + br
                rows.append(jnp.where(first, gate_t[c:c + 1, :], gate_t[3 * H + c:3 * H + c + 1, :]))
            return jnp.concatenate(rows, axis=1)

        o_t = grow(0) * out["cmp", i] + grow(1) * out["slc", i] + grow(2) * out["win", i]
        o_ref[0, lo:hi, :] = jnp.concatenate([o_t[:, h * QB:(h + 1) * QB].T for h in range(H)],
                                             axis=1).astype(o_ref.dtype)

    def pair(k):
        out = {}
        tasks = []
        blocks = (nb - 1 - k, k)
        qss = {}
        for i in blocks:
            q = q_ref[0, i * QB:(i + 1) * QB, :]
            qss[i] = jnp.concatenate([q[:, h * HEAD_DIM:(h + 1) * HEAD_DIM] for h in range(H)], axis=0)
            tasks.append(cmp_task(i, qss[i], out))
            tasks.append(attend_task(qss[i], kw_ref, vwt_ref, functools.partial(win_spans, i), out, ("win", i)))
        for i in blocks:
            tasks.append(attend_task(qss[i], ks_ref, vst_ref, functools.partial(slc_spans, i, out), out, ("slc", i)))
        n_stage = 3
        for step in range(len(tasks) + n_stage - 1):
            for t in range(step - n_stage + 1, step + 1):
                if 0 <= t < len(tasks):
                    next(tasks[t])
        for i in blocks:
            combine(i, out)

    for k in range(nb // 2):
        pl.when(pl.program_id(2) == k)(functools.partial(pair, k))


def _nsa_t(qn, kv, kc, vc, gates, cbias, cmp_u0, cmp_per, nbias):
    B, T, _ = qn.shape
    G, H, dk = NSA_KV_GROUPS, NSA_HPG, HEAD_DIM
    nb = T // QB
    ns = T // SEL_BLOCK
    cstart = np.arange(LANE) * CMP_STRIDE
    sstart = np.arange(ns) * SEL_BLOCK
    overlap = np.clip(np.minimum(cstart[:, None] + CMP_LEN, sstart[None, :] + SEL_BLOCK)
                      - np.maximum(cstart[:, None], sstart[None, :]), 0, None) / CMP_STRIDE
    overlap[LANE - 1:] = 0.0
    ovt = jnp.asarray(overlap.T, F32)
    expand_t = np.zeros((T, LANE), np.float32)
    expand_t[np.arange(T), np.arange(T) // SEL_BLOCK] = MASK_BIG
    return pl.pallas_call(
        functools.partial(_nsa_t_kernel, cmp_u0=cmp_u0, cmp_per=cmp_per),
        grid=(B, G, nb // 2),
        in_specs=[pl.BlockSpec((1, T, H * dk), lambda b, g, i: (b, 0, g)),
                  pl.BlockSpec((1, T, dk), lambda b, g, i: (b, 0, g)),
                  pl.BlockSpec((1, T, dk), lambda b, g, i: (b, 0, G + g)),
                  pl.BlockSpec((1, T, dk), lambda b, g, i: (b, 0, 2 * G + g)),
                  pl.BlockSpec((1, T, dk), lambda b, g, i: (b, 0, 3 * G + g)),
                  pl.BlockSpec((1, 1, LANE, dk), lambda b, g, i: (b, g, 0, 0)),
                  pl.BlockSpec((1, 1, LANE, dk), lambda b, g, i: (b, g, 0, 0)),
                  pl.BlockSpec((1, T, LANE), lambda b, g, i: (b, 0, 0)),
                  pl.BlockSpec((1, 1, 2 * LANE, H * QB), lambda b, g, i: (g, 0, 0, 0)),
                  pl.BlockSpec((1, 3, QB, H * QB), lambda b, g, i: (g, 0, 0, 0)),
                  pl.BlockSpec((ns, LANE), lambda b, g, i: (0, 0)),
                  pl.BlockSpec((T, LANE), lambda b, g, i: (0, 0))],
        out_specs=pl.BlockSpec((1, T, H * dk), lambda b, g, i: (b, 0, g)),
        out_shape=jax.ShapeDtypeStruct((B, T, NSA_HEADS * dk), BF16),
        scratch_shapes=[pltpu.VMEM((dk, T), BF16), pltpu.VMEM((dk, T), BF16), pltpu.VMEM((dk, LANE), BF16)],
        compiler_params=_cparams(3),
        name="nsa",
    )(qn, kv, kv, kv, kv, kc, vc, gates, cbias, nbias, ovt, jnp.asarray(expand_t, BF16))


def _diff_kernel(q_ref, k_ref, v_ref, lq_ref, lk_ref, sg_ref, db_ref, o_ref):
    dk = HEAD_DIM
    T = k_ref.shape[1]
    lqk = lq_ref[...] * lk_ref[...]
    lam = (jnp.exp(jnp.sum(lqk[0:1], axis=-1, keepdims=True))
           - jnp.exp(jnp.sum(lqk[1:2], axis=-1, keepdims=True)) + LAM_INIT)
    def task(i, mm, out):
        lo, hi = i * DQB, (i + 1) * DQB
        cols = slice(mm * dk, (mm + 1) * dk)
        q = q_ref[0, lo:hi, cols]
        bounds, parts = [], []
        if i >= 2:
            bounds.append((0, lo - DQB))
            parts.append(_qk(q, k_ref[0, 0:lo - DQB, cols]))
        if i >= 1:
            bounds.append((lo - DQB, lo))
            parts.append(_qk(q, k_ref[0, lo - DQB:lo, cols]) + db_ref[0, 1])
        bounds.append((lo, hi))
        parts.append(_qk(q, k_ref[0, lo:hi, cols]) + db_ref[0, 0])
        yield
        ps, den = _softmax_parts(parts)
        yield
        o = None
        for (a, b), p in zip(bounds, ps):
            pv = jnp.dot(p.astype(BF16), v_ref[0, a:b, :], preferred_element_type=F32)
            o = pv if o is None else o + pv
        out[i, mm] = o / den
        yield

    out = {}
    tasks = [task(i, mm, out) for i in reversed(range(T // DQB)) for mm in range(2)]
    n_stage = 3
    for step in range(len(tasks) + n_stage - 1):
        for t in range(step - n_stage + 1, step + 1):
            if 0 <= t < len(tasks):
                next(tasks[t])

    for i in range(T // DQB):
        lo, hi = i * DQB, (i + 1) * DQB
        o = out[i, 0] - lam * out[i, 1]
        ms = jnp.mean(o * o, axis=-1, keepdims=True)
        o_ref[0, lo:hi, :] = ((o * lax.rsqrt(ms + EPS) * sg_ref[...]) * (1.0 - LAM_INIT)).astype(o_ref.dtype)


def _diff(dqk, dv, lam_q, lam_k, subln_gain, dbias):
    B, T, _ = dqk.shape
    Hd, dk = DIFF_HEADS, HEAD_DIM
    w = 2 * dk
    return pl.pallas_call(
        _diff_kernel,
        grid=(B, Hd),
        in_specs=[pl.BlockSpec((1, T, w), lambda b, h: (b, 0, h)),
                  pl.BlockSpec((1, T, w), lambda b, h: (b, 0, Hd + h)),
                  pl.BlockSpec((1, T, w), lambda b, h: (b, 0, h)),
                  pl.BlockSpec((2, dk), lambda b, h: (0, 0)),
                  pl.BlockSpec((2, dk), lambda b, h: (0, 0)),
                  pl.BlockSpec((1, w), lambda b, h: (0, 0)),
                  pl.BlockSpec((1, 2, DQB, DQB), lambda b, h: (h, 0, 0, 0))],
        out_specs=pl.BlockSpec((1, T, w), lambda b, h: (b, 0, h)),
        out_shape=jax.ShapeDtypeStruct((B, T, Hd * w), BF16),
        compiler_params=_cparams(2),
        name="diff",
    )(dqk, dqk, dv, lam_q, lam_k, subln_gain.reshape(1, w), dbias)


def _merge_kernel(an_ref, ad_ref, wn_ref, wd_ref, gn_ref, gd_ref, o_ref, wnb_ref, wdb_ref):
    @pl.when(pl.program_id(1) == 0)
    def _():
        wnb_ref[...] = wn_ref[...].astype(BF16)
        wdb_ref[...] = wd_ref[...].astype(BF16)

    for rows in _row_chunks(an_ref.shape[0]):
        yn = jnp.dot(an_ref[rows, :], wnb_ref[...], preferred_element_type=F32)
        yd = jnp.dot(ad_ref[rows, :], wdb_ref[...], preferred_element_type=F32)
        o_ref[rows, :] = (gn_ref[rows, :].astype(F32) * yn + gd_ref[rows, :].astype(F32) * yd).astype(o_ref.dtype)


def _merge(o_nsa, o_diff, w_n, w_d, mg, tm=1024, tn=1024):
    M, K = o_nsa.shape
    N = w_n.shape[1]
    nj = N // tn
    return pl.pallas_call(
        _merge_kernel,
        grid=(nj, M // tm),
        in_specs=[pl.BlockSpec((tm, K), lambda j, i: (i, 0)),
                  pl.BlockSpec((tm, K), lambda j, i: (i, 0)),
                  pl.BlockSpec((K, tn), lambda j, i: (0, j)),
                  pl.BlockSpec((K, tn), lambda j, i: (0, j)),
                  pl.BlockSpec((tm, tn), lambda j, i: (i, j)),
                  pl.BlockSpec((tm, tn), lambda j, i: (i, nj + j))],
        out_specs=pl.BlockSpec((tm, tn), lambda j, i: (i, j)),
        out_shape=jax.ShapeDtypeStruct((M, N), BF16),
        scratch_shapes=[pltpu.VMEM((K, tn), BF16), pltpu.VMEM((K, tn), BF16)],
        compiler_params=_cparams(2),
        name="merge",
    )(o_nsa, o_diff, w_n, w_d, mg, mg)


def _oproj_kernel(a_ref, w_ref, x_ref, g1_ref, gain_ref, sc_ref, sh_ref, x1_ref, h2_ref):
    for rows in _row_chunks(a_ref.shape[0]):
        y = jnp.dot(a_ref[rows, :], w_ref[...], preferred_element_type=F32)
        x1 = x_ref[rows, :] + g1_ref[0] * y
        x1_ref[rows, :] = x1
        h2_ref[rows, :] = _modnorm(x1, gain_ref[...], sc_ref[0], sh_ref[0]).astype(h2_ref.dtype)


def _oproj(merged, w_o, x2d, mod3, gain2, T, tm=512):
    M, D = x2d.shape
    per = T // tm
    return pl.pallas_call(
        _oproj_kernel,
        grid=(M // tm,),
        in_specs=[pl.BlockSpec((tm, D), lambda i: (i, 0)),
                  pl.BlockSpec((D, D), lambda i: (0, 0)),
                  pl.BlockSpec((tm, D), lambda i: (i, 0)),
                  pl.BlockSpec((1, 1, D), lambda i: ((i // per) * 6 + 2, 0, 0)),
                  pl.BlockSpec((1, D), lambda i: (0, 0)),
                  pl.BlockSpec((1, 1, D), lambda i: ((i // per) * 6 + 4, 0, 0)),
                  pl.BlockSpec((1, 1, D), lambda i: ((i // per) * 6 + 3, 0, 0))],
        out_specs=[pl.BlockSpec((tm, D), lambda i: (i, 0)),
                   pl.BlockSpec((tm, D), lambda i: (i, 0))],
        out_shape=[jax.ShapeDtypeStruct((M, D), F32), jax.ShapeDtypeStruct((M, D), BF16)],
        compiler_params=_cparams(1),
        name="oproj",
    )(merged, w_o, x2d, mod3, gain2.reshape(1, D), mod3, mod3)


def _ffn_up_kernel(h_ref, wa_ref, wv_ref, cwa_ref, cwv_ref, cba_ref, cbv_ref, o_ref, wab_ref, wvb_ref,
                   ca_ref, cv_ref, sa_ref, sv_ref, *, per):
    i = pl.program_id(1)

    @pl.when(i == 0)
    def _():
        wab_ref[...] = wa_ref[...].astype(BF16)
        wvb_ref[...] = wv_ref[...].astype(BF16)

    @pl.when(i % per == 0)
    def _():
        ca_ref[...] = jnp.zeros(ca_ref.shape, F32)
        cv_ref[...] = jnp.zeros(cv_ref.shape, F32)

    def conv(u, prev, cw_ref, cb_ref, s_ref):
        s_ref[0:8, :] = prev
        s_ref[8:8 + SUB_ROWS, :] = u
        u1 = s_ref[7:7 + SUB_ROWS, :]
        u2 = s_ref[6:6 + SUB_ROWS, :]
        return cb_ref[...] + cw_ref[0:1, :] * u2 + cw_ref[1:2, :] * u1 + cw_ref[2:3, :] * u

    prev_a, prev_v = ca_ref[...], cv_ref[...]
    for n, rows in enumerate(_row_chunks(h_ref.shape[0])):
        hs = h_ref[rows, :]
        ua = jnp.dot(hs, wab_ref[...], preferred_element_type=F32)
        uv = jnp.dot(hs, wvb_ref[...], preferred_element_type=F32)
        a = conv(ua, prev_a, cwa_ref, cba_ref, sa_ref.at[n % 2])
        val = conv(uv, prev_v, cwv_ref, cbv_ref, sv_ref.at[n % 2])
        o_ref[rows, :] = (a * jax.nn.sigmoid(a) * val).astype(o_ref.dtype)
        prev_a, prev_v = ua[SUB_ROWS - 8:, :], uv[SUB_ROWS - 8:, :]
    ca_ref[...] = prev_a
    cv_ref[...] = prev_v


def _ffn_up(h2, w_up, conv_w, conv_b, T, tm=2048, tn=512):
    M, D = h2.shape
    F = w_up.shape[1] // 2
    nj = F // tn
    cb = conv_b.reshape(1, 2 * F)
    return pl.pallas_call(
        functools.partial(_ffn_up_kernel, per=T // tm),
        grid=(nj, M // tm),
        in_specs=[pl.BlockSpec((tm, D), lambda j, i: (i, 0)),
                  pl.BlockSpec((D, tn), lambda j, i: (0, j)),
                  pl.BlockSpec((D, tn), lambda j, i: (0, nj + j)),
                  pl.BlockSpec((3, tn), lambda j, i: (0, j)),
                  pl.BlockSpec((3, tn), lambda j, i: (0, nj + j)),
                  pl.BlockSpec((1, tn), lambda j, i: (0, j)),
                  pl.BlockSpec((1, tn), lambda j, i: (0, nj + j))],
        out_specs=pl.BlockSpec((tm, tn), lambda j, i: (i, j)),
        out_shape=jax.ShapeDtypeStruct((M, F), BF16),
        scratch_shapes=[pltpu.VMEM((D, tn), BF16), pltpu.VMEM((D, tn), BF16),
                        pltpu.VMEM((8, tn), F32), pltpu.VMEM((8, tn), F32),
                        pltpu.VMEM((2, SUB_ROWS + 8, tn), F32), pltpu.VMEM((2, SUB_ROWS + 8, tn), F32)],
        compiler_params=_cparams(2),
        name="ffn_up",
    )(h2, w_up, w_up, conv_w, conv_w, cb, cb)


def _ffn_down_kernel(a_ref, w_ref, x_ref, g2_ref, o_ref, wb_ref):
    @pl.when(pl.program_id(1) == 0)
    def _():
        wb_ref[...] = w_ref[...].astype(BF16)

    for rows in _row_chunks(a_ref.shape[0]):
        y = jnp.dot(a_ref[rows, :], wb_ref[...], preferred_element_type=F32)
        o_ref[rows, :] = x_ref[rows, :] + g2_ref[0] * y


def _ffn_down(act, w_down, x1, mod3, T, tm=512, tn=512):
    M, F = act.shape
    D = w_down.shape[1]
    per = T // tm
    return pl.pallas_call(
        _ffn_down_kernel,
        grid=(D // tn, M // tm),
        in_specs=[pl.BlockSpec((tm, F), lambda j, i: (i, 0)),
                  pl.BlockSpec((F, tn), lambda j, i: (0, j)),
                  pl.BlockSpec((tm, tn), lambda j, i: (i, j)),
                  pl.BlockSpec((1, 1, tn), lambda j, i: ((i // per) * 6 + 5, 0, j))],
        out_specs=pl.BlockSpec((tm, tn), lambda j, i: (i, j)),
        out_shape=jax.ShapeDtypeStruct((M, D), F32),
        scratch_shapes=[pltpu.VMEM((F, tn), BF16)],
        compiler_params=_cparams(2),
        name="ffn_down",
    )(act, w_down, x1, mod3)


def _layer(x, c, w_ada, b_ada, norm1_gain, norm2_gain, w_in, nsa_q_gain, nsa_k_gain, cmp_pe, cmp_w1, cmp_w2,
           diff_q_gain, diff_k_gain, diff_lambda_q, diff_lambda_k, diff_subln_gain, w_nsa_out, w_diff_out, w_o,
           w_ffn_up, ffn_conv_w, ffn_conv_b, w_ffn_down, rel_bias):
    B, T, D = x.shape
    dk, G = HEAD_DIM, NSA_KV_GROUPS
    M = B * T
    scale = dk ** -0.5

    n_q = NSA_HEADS * dk
    o_kv = n_q
    o_g = o_kv + 3 * 2 * G * dk
    o_dq = o_g + NSA_HEADS * 3
    o_dk = o_dq + DIFF_HEADS * 2 * dk
    o_dv = o_dk + DIFF_HEADS * 2 * dk
    o_mg = o_dv + DIFF_HEADS * 2 * dk
    n_kv = 2 * G * dk
    n_dqk = 2 * DIFF_HEADS * 2 * dk

    mod3 = _ada(c, w_ada, b_ada).reshape(B * 6, 1, D)
    h = _norm1(x, norm1_gain, mod3).reshape(M, D)

    ones = jnp.ones((n_kv // 2,), F32)
    g_q = jnp.tile(nsa_q_gain * (scale * LOG2E), NSA_HEADS).reshape(1, n_q)
    g_kv = jnp.concatenate([jnp.tile(nsa_k_gain[1], G), ones, jnp.tile(nsa_k_gain[2], G), ones]).reshape(1, 2 * n_kv)
    g_dqk = jnp.concatenate([jnp.tile(diff_q_gain * (scale * LOG2E), 2 * DIFF_HEADS),
                             jnp.tile(diff_k_gain, 2 * DIFF_HEADS)]).reshape(1, n_dqk)
    g_one = jnp.ones((1, 2 * D), F32)
    proj = functools.partial(_inproj, h, jnp.swapaxes(w_in, 1, 2), tm=INPROJ_TM)
    wide = INPROJ_TN
    yes, no = (True,) * (wide // LANE), (False,) * (wide // LANE)
    qn = proj(g_q, src0=0, ncols=n_q, heads=yes, mode="raw", out_dtype=BF16, tn=wide, name="inproj_q")
    cmpkv, gates = _inproj_cmp_gate(h, jnp.swapaxes(w_in, 1, 2), src_cmp=o_kv, n_cmp=n_kv, src_gate=o_g,
                                    tm=INPROJ_TM)
    kv_heads = ((True,) * G + (False,) * G) * 2
    kv = proj(g_kv, src0=o_kv + n_kv, ncols=2 * n_kv, heads=kv_heads, mode="raw", out_dtype=BF16, tn=2 * n_kv,
              name="inproj_kv")
    dqk = proj(g_dqk, src0=o_dq, ncols=n_dqk, heads=yes, mode="raw", out_dtype=BF16, tn=wide, name="inproj_dqk")
    dv = proj(g_one, src0=o_dv, ncols=o_mg - o_dv, heads=no, mode="raw", out_dtype=BF16, tn=wide, name="inproj_dv")
    mgate = proj(g_one, src0=o_mg, ncols=2 * D, heads=no, mode="sigmoid", out_dtype=BF16, tn=wide, name="inproj_mg")

    nbias = _bias_tiles(_window_idx(QB).transpose(0, 2, 1), rel_bias, head0=0, groups=G, hpg=NSA_HPG,
                        name="bias_nsa", rel=True, mult=LOG2E)
    cmp_idx, cmp_u0, cmp_per = _cmp_idx(T)
    cbias = _bias_tiles(cmp_idx.transpose(0, 2, 1), rel_bias, head0=0, groups=G, hpg=NSA_HPG, name="bias_cmp",
                        mult=LOG2E)
    dbias = _bias_tiles(_causal_idx(DQB), rel_bias, head0=NSA_HEADS, groups=DIFF_HEADS, hpg=1, name="bias_diff",
                        rel=True, mult=LOG2E)

    kc, vc = _compress(cmpkv.reshape(B, T, n_kv), cmp_pe, cmp_w1, cmp_w2, nsa_k_gain[0])
    o_nsa = _nsa_t(qn.reshape(B, T, n_q), kv.reshape(B, T, 2 * n_kv), kc, vc, gates.reshape(B, T, LANE), cbias,
                   cmp_u0, cmp_per, nbias)
    o_diff = _diff(dqk.reshape(B, T, n_dqk), dv.reshape(B, T, -1), diff_lambda_q, diff_lambda_k, diff_subln_gain,
                   dbias)

    merged = _merge(o_nsa.reshape(M, -1), o_diff.reshape(M, -1), w_nsa_out, w_diff_out, mgate)
    x1, h2 = _oproj(merged, w_o.astype(BF16), x.reshape(M, D), mod3, norm2_gain, T)
    act = _ffn_up(h2, w_ffn_up, ffn_conv_w, ffn_conv_b, T)
    out = _ffn_down(act, w_ffn_down, x1, mod3, T)
    return out.reshape(B, T, D)


def kernel(x, c, w_ada, b_ada, norm1_gain, norm2_gain, w_in, nsa_q_gain, nsa_k_gain, cmp_pe, cmp_w1, cmp_w2,
           diff_q_gain, diff_k_gain, diff_lambda_q, diff_lambda_k, diff_subln_gain, w_nsa_out, w_diff_out, w_o,
           w_ffn_up, ffn_conv_w, ffn_conv_b, w_ffn_down, rel_bias):
    return _layer(x, c, w_ada[0], b_ada[0], norm1_gain[0], norm2_gain[0], w_in, nsa_q_gain[0], nsa_k_gain[0],
                  cmp_pe[0], cmp_w1[0], cmp_w2[0], diff_q_gain[0], diff_k_gain[0], diff_lambda_q[0],
                  diff_lambda_k[0], diff_subln_gain[0], w_nsa_out[0], w_diff_out[0], w_o[0], w_ffn_up[0],
                  ffn_conv_w[0], ffn_conv_b[0], w_ffn_down[0], rel_bias)
```

```python
import functools
import math

import numpy as np
import jax
import jax.numpy as jnp
from jax import lax
from jax.experimental import pallas as pl
from jax.experimental.pallas import tpu as pltpu

F32 = jnp.float32
BF16 = jnp.bfloat16

HEAD_DIM = 128
NSA_HEADS = 8
NSA_KV_GROUPS = 2
NSA_HPG = NSA_HEADS // NSA_KV_GROUPS
CMP_LEN = 32
CMP_STRIDE = 16
SEL_BLOCK = 64
N_SEL = 16
WINDOW = 512
DIFF_HEADS = 4
NUM_BUCKETS = 32
MAX_DISTANCE = 128
EPS = 1e-6
NEG = -1e30
LAM_INIT = 0.8 - 0.6 * math.exp(-0.3 * 0)
LOG2E = math.log2(math.e)
MASK_BIG = -(2.0 ** 100)

LANE = 128
QB = 128
DQB = 256
SUB_ROWS = 256
INPROJ_TM = 2048
INPROJ_TN = 1024
VMEM_LIMIT = 56 * 1024 * 1024


def _cparams(n_axes):
    return pltpu.CompilerParams(dimension_semantics=("arbitrary",) * n_axes,
                                vmem_limit_bytes=VMEM_LIMIT)


def _t5_bucket_np(dist):
    n = np.maximum(np.asarray(dist, np.int32), 0)
    max_exact = NUM_BUCKETS // 2
    nf = np.maximum(n, max_exact).astype(np.float32)
    large = max_exact + (np.log(nf / np.float32(max_exact)) / np.float32(math.log(MAX_DISTANCE / max_exact))
                         * np.float32(NUM_BUCKETS - max_exact)).astype(np.int32)
    large = np.minimum(large, NUM_BUCKETS - 1)
    return np.where(n < max_exact, n, large).astype(np.int32)


def _ada_kernel(ct_ref, w_ref, b_ref, o_ref):
    ct = ct_ref[...]
    s = ct * jax.nn.sigmoid(ct)
    w = w_ref[...]
    for b in range(ct.shape[1]):
        o_ref[b:b + 1, :] = jnp.sum(w * s[:, b:b + 1], axis=0, keepdims=True) + b_ref[...]


def _ada(c, w_ada, b_ada, tn=1024):
    B, D = c.shape
    N = w_ada.shape[1]
    return pl.pallas_call(
        _ada_kernel,
        grid=(N // tn,),
        in_specs=[pl.BlockSpec((D, B), lambda j: (0, 0)),
                  pl.BlockSpec((D, tn), lambda j: (0, j)),
                  pl.BlockSpec((1, tn), lambda j: (0, j))],
        out_specs=pl.BlockSpec((B, tn), lambda j: (0, j)),
        out_shape=jax.ShapeDtypeStruct((B, N), F32),
        compiler_params=_cparams(1),
        name="ada",
    )(c.T, w_ada, b_ada.reshape(1, N))


def _modnorm(x, gain, sc, sh):
    ms = jnp.mean(x * x, axis=-1, keepdims=True)
    return (x * lax.rsqrt(ms + EPS) * gain) * (1.0 + sc) + sh


def _norm1_kernel(x_ref, g_ref, sc_ref, sh_ref, o_ref):
    o_ref[0] = _modnorm(x_ref[0], g_ref[...], sc_ref[0], sh_ref[0]).astype(o_ref.dtype)


def _norm1(x, gain, mod3, tm=512):
    B, T, D = x.shape
    return pl.pallas_call(
        _norm1_kernel,
        grid=(B, T // tm),
        in_specs=[pl.BlockSpec((1, tm, D), lambda b, i: (b, i, 0)),
                  pl.BlockSpec((1, D), lambda b, i: (0, 0)),
                  pl.BlockSpec((1, 1, D), lambda b, i: (b * 6 + 1, 0, 0)),
                  pl.BlockSpec((1, 1, D), lambda b, i: (b * 6 + 0, 0, 0))],
        out_specs=pl.BlockSpec((1, tm, D), lambda b, i: (b, i, 0)),
        out_shape=jax.ShapeDtypeStruct((B, T, D), BF16),
        compiler_params=_cparams(2),
        name="norm1",
    )(x, gain.reshape(1, D), mod3, mod3)


def _row_chunks(tm):
    return [slice(r, r + SUB_ROWS) for r in range(0, tm, SUB_ROWS)]


def _inproj_kernel(a_ref, wt_ref, g_ref, o_ref, wb_ref, *, heads, mode):
    @pl.when(pl.program_id(1) == 0)
    def _():
        wb_ref[...] = wt_ref[0].astype(BF16)

    for rows in _row_chunks(a_ref.shape[0]):
        acc = _qk(a_ref[rows, :], wb_ref[...])
        for k, is_head in enumerate(heads):
            lanes = slice(k * LANE, (k + 1) * LANE)
            y = acc[:, lanes]
            if is_head:
                ms = jnp.mean(y * y, axis=-1, keepdims=True)
                y = y * lax.rsqrt(ms + EPS) * g_ref[:, lanes]
            elif mode == "sigmoid":
                y = jax.nn.sigmoid(y)
            o_ref[rows, lanes] = y.astype(o_ref.dtype)


def _inproj(h2d, w_in_t, gains, *, src0, ncols, heads, mode, out_dtype, tm, tn, name):
    M, K = h2d.shape
    assert src0 % 8 == 0 and ncols % tn == 0 and len(heads) == tn // LANE
    return pl.pallas_call(
        functools.partial(_inproj_kernel, heads=heads, mode=mode),
        grid=(ncols // tn, M // tm),
        in_specs=[pl.BlockSpec((tm, K), lambda j, i: (i, 0)),
                  pl.BlockSpec((pl.Element(1), pl.Element(tn), pl.Element(K)),
                               lambda j, i: (0, pl.multiple_of(src0 + j * tn, 8), 0)),
                  pl.BlockSpec((1, tn), lambda j, i: (0, j))],
        out_specs=pl.BlockSpec((tm, tn), lambda j, i: (i, j)),
        out_shape=jax.ShapeDtypeStruct((M, ncols), out_dtype),
        scratch_shapes=[pltpu.VMEM((tn, K), BF16)],
        compiler_params=_cparams(2),
        name=name,
    )(h2d, w_in_t, gains)


def _inproj_cmp_gate_kernel(a_ref, wc_ref, wg_ref, oc_ref, og_ref, wcb_ref, wgb_ref):
    @pl.when(pl.program_id(0) == 0)
    def _():
        wcb_ref[...] = wc_ref[0].astype(BF16)
        wgb_ref[...] = wg_ref[0].astype(BF16)

    for rows in _row_chunks(a_ref.shape[0]):
        a = a_ref[rows, :]
        oc_ref[rows, :] = _qk(a, wcb_ref[...])
        og_ref[rows, :] = jax.nn.sigmoid(_qk(a, wgb_ref[...]))


def _inproj_cmp_gate(h2d, w_in_t, *, src_cmp, n_cmp, src_gate, tm):
    M, K = h2d.shape
    assert src_cmp % 8 == 0 and src_gate % 8 == 0

    def w_rows(start, n):
        return pl.BlockSpec((pl.Element(1), pl.Element(n), pl.Element(K)), lambda i: (0, start, 0))

    return pl.pallas_call(
        _inproj_cmp_gate_kernel,
        grid=(M // tm,),
        in_specs=[pl.BlockSpec((tm, K), lambda i: (i, 0)), w_rows(src_cmp, n_cmp), w_rows(src_gate, LANE)],
        out_specs=[pl.BlockSpec((tm, n_cmp), lambda i: (i, 0)), pl.BlockSpec((tm, LANE), lambda i: (i, 0))],
        out_shape=[jax.ShapeDtypeStruct((M, n_cmp), F32), jax.ShapeDtypeStruct((M, LANE), F32)],
        scratch_shapes=[pltpu.VMEM((n_cmp, K), BF16), pltpu.VMEM((LANE, K), BF16)],
        compiler_params=_cparams(1),
        name="inproj_cmp_gate",
    )(h2d, w_in_t, w_in_t)


def _bias_kernel(idx_ref, tab_ref, o_ref, *, head0, hpg, rel, mult):
    head = head0 + pl.program_id(0) * hpg + pl.program_id(2)
    idx = idx_ref[0]
    acc = jnp.zeros(idx.shape, F32)
    for b in range(NUM_BUCKETS):
        acc = jnp.where(idx == b, tab_ref[b, head], acc)
    if rel:
        acc = acc - tab_ref[NUM_BUCKETS - 1, head]
    o_ref[0, 0] = jnp.where(idx < 0, NEG, acc * mult)


def _bias_tiles(idx, rel_bias, *, head0, groups, hpg, name, rel=False, mult=1.0):
    N, R, C = idx.shape
    return pl.pallas_call(
        functools.partial(_bias_kernel, head0=head0, hpg=hpg, rel=rel, mult=mult),
        grid=(groups, N, hpg),
        in_specs=[pl.BlockSpec((1, R, C), lambda g, n, h: (n, 0, 0)),
                  pl.BlockSpec(memory_space=pltpu.SMEM)],
        out_specs=pl.BlockSpec((1, 1, R, C), lambda g, n, h: (g, n, 0, h)),
        out_shape=jax.ShapeDtypeStruct((groups, N, R, hpg * C), F32),
        compiler_params=_cparams(3),
        name=name,
    )(jnp.asarray(idx), rel_bias)


def _causal_idx(R):
    r = np.arange(R)[:, None]
    c = np.arange(R)[None, :]
    return np.stack([np.where(r >= c, _t5_bucket_np(r - c), -1), _t5_bucket_np(R + r - c)]).astype(np.int32)


def _window_idx(R):
    r = np.arange(R)[:, None]
    c = np.arange(R)[None, :]
    edge = np.where(r < c, NUM_BUCKETS - 1, -1)
    return np.concatenate([_causal_idx(R), edge[None]]).astype(np.int32)


def _cmp_idx(T):
    per = QB // CMP_STRIDE
    u0 = (T // QB - 1) * per
    assert u0 + LANE <= 2 * LANE
    r = np.arange(QB)[:, None]
    end = (np.arange(2 * LANE)[None, :] - u0) * CMP_STRIDE + CMP_LEN - 1
    return np.where(end <= r, _t5_bucket_np(r - end), -1).astype(np.int32)[None], u0, per


def _compress_kernel(zk_ref, zv_ref, pe_ref, w1_ref, w2_ref, kg_ref, kc_ref, vc_ref):
    half = CMP_LEN // 2

    def one(z_ref, i):
        p1 = jnp.zeros((LANE, HEAD_DIM), F32)
        p2 = jnp.zeros((LANE, HEAD_DIM), F32)
        for l in range(half):
            z = z_ref[0, pl.ds(l, LANE, stride=CMP_STRIDE), :]
            w_lo = w1_ref[i, l * HEAD_DIM:(l + 1) * HEAD_DIM, :].astype(BF16)
            w_hi = w1_ref[i, (half + l) * HEAD_DIM:(half + l + 1) * HEAD_DIM, :].astype(BF16)
            p1 = p1 + jnp.dot((z + pe_ref[i, l:l + 1, :]).astype(BF16), w_lo, preferred_element_type=F32)
            p2 = p2 + jnp.dot((z + pe_ref[i, half + l:half + l + 1, :]).astype(BF16), w_hi,
                              preferred_element_type=F32)
        pre = p1 + pltpu.roll(p2, LANE - 1, axis=0)
        hid = jax.nn.gelu(pre)
        return jnp.dot(hid.astype(BF16), w2_ref[i].astype(BF16), preferred_element_type=F32)

    kc = one(zk_ref, 0)
    ms = jnp.mean(kc * kc, axis=-1, keepdims=True)
    kc_ref[0, 0] = (kc * lax.rsqrt(ms + EPS) * kg_ref[...]).astype(kc_ref.dtype)
    vc_ref[0, 0] = one(zv_ref, 1).astype(vc_ref.dtype)


def _compress(cmp_kv, pe, w1, w2, k_gain0):
    B, T, _ = cmp_kv.shape
    G, dk = NSA_KV_GROUPS, HEAD_DIM
    assert (T - CMP_LEN) // CMP_STRIDE + 1 == LANE - 1
    out = jax.ShapeDtypeStruct((B, G, LANE, dk), BF16)
    return pl.pallas_call(
        _compress_kernel,
        grid=(B, G),
        in_specs=[pl.BlockSpec((1, T, dk), lambda b, g: (b, 0, g)),
                  pl.BlockSpec((1, T, dk), lambda b, g: (b, 0, G + g)),
                  pl.BlockSpec((2, CMP_LEN, dk), lambda b, g: (0, 0, 0)),
                  pl.BlockSpec((2, CMP_LEN * dk, dk), lambda b, g: (0, 0, 0)),
                  pl.BlockSpec((2, dk, dk), lambda b, g: (0, 0, 0)),
                  pl.BlockSpec((1, dk), lambda b, g: (0, 0))],
        out_specs=[pl.BlockSpec((1, 1, LANE, dk), lambda b, g: (b, g, 0, 0))] * 2,
        out_shape=[out, out],
        compiler_params=_cparams(2),
        name="compress",
    )(cmp_kv, cmp_kv, pe, w1, w2, k_gain0.reshape(1, dk))


def _qk(q, k):
    return lax.dot_general(q, k, (((1,), (1,)), ((), ())), preferred_element_type=F32)


def _lane_fold(x, op):
    acc = x[..., :LANE]
    for t in range(1, x.shape[-1] // LANE):
        acc = op(acc, x[..., t * LANE:(t + 1) * LANE])
    return acc


def _softmax_parts(parts):
    m = jnp.max(functools.reduce(jnp.maximum, [_lane_fold(s, jnp.maximum) for s in parts]), axis=-1, keepdims=True)
    ps = [jnp.exp2(s - m) for s in parts]
    den = jnp.sum(functools.reduce(jnp.add, [_lane_fold(p, jnp.add) for p in ps]), axis=-1, keepdims=True)
    return ps, den


def _nsa_t_kernel(q_ref, ks_ref, vs_ref, kw_ref, vw_ref, kc_ref, vc_ref, gate_ref, cb_ref, nb_ref,
                  ovt_ref, expt_ref, o_ref, vst_ref, vwt_ref, vct_ref, ksa_ref, *, cmp_u0, cmp_per):
    H, W = NSA_HPG, NSA_HPG * QB
    T = ks_ref.shape[1]
    ns = ovt_ref.shape[0]
    nwb = WINDOW // QB
    nb = T // QB
    kc = kc_ref[0, 0]

    @pl.when(pl.program_id(2) == 0)
    def _():
        ksa_ref[:, :HEAD_DIM] = ks_ref[0]
        ksa_ref[:, HEAD_DIM:] = expt_ref[...]
        vst_ref[...] = vs_ref[0].astype(F32).T.astype(BF16)
        vwt_ref[...] = vw_ref[0].astype(F32).T.astype(BF16)
        vct_ref[...] = vc_ref[0, 0].astype(F32).T.astype(BF16)

    def softmax_t(parts):
        m = functools.reduce(jnp.maximum, [jnp.max(s, axis=0, keepdims=True) for s in parts])
        ps = [jnp.exp2(s - m) for s in parts]
        den = functools.reduce(jnp.add, [jnp.sum(p, axis=0, keepdims=True) for p in ps])
        return [p.astype(BF16) for p in ps], den

    def attend_task(vt_ref, spans_fn, out, key):
        spans = spans_fn()
        parts = []
        for _, _, k_slab, q_op, add in spans:
            s = _qk(k_slab, q_op)
            parts.append(s if add is None else s + add)
        yield
        ps, den = softmax_t(parts)
        yield
        o = None
        for (a, b, _, _, _), p in zip(spans, ps):
            pv = jnp.dot(vt_ref[:, a:b], p, preferred_element_type=F32)
            o = pv if o is None else o + pv
        out[key] = o / den
        yield

    def cmp_task(i, qs, out):
        lo = i * QB
        u = cmp_u0 - cmp_per * i
        s = _qk(kc, qs) + cb_ref[0, 0, u:u + LANE, :]
        yield
        e = jnp.exp2(s - jnp.max(s, axis=0, keepdims=True))
        if i == 0:
            key_end = lax.broadcasted_iota(jnp.int32, (LANE, W), 0) * CMP_STRIDE + (CMP_LEN - 1)
            query = lax.broadcasted_iota(jnp.int32, (LANE, W), 1) % QB
            e = jnp.where(key_end <= query, e, 0.0)
            den = jnp.sum(e, axis=0, keepdims=True)
            p = e / jnp.where(den > 0.0, den, 1.0)
        else:
            p = e / jnp.sum(e, axis=0, keepdims=True)
        if i >= 1:
            psum = functools.reduce(jnp.add, [p[:, h * QB:(h + 1) * QB] for h in range(H)])
            imp_t = jnp.dot(ovt_ref[...], psum, precision=lax.Precision.HIGHEST, preferred_element_type=F32)
        yield
        out["cmp", i] = jnp.dot(vct_ref[...], p.astype(BF16), preferred_element_type=F32)
        if i >= 1:
            blk = lax.broadcasted_iota(jnp.int32, (ns, QB), 0)
            cur = (lo + lax.broadcasted_iota(jnp.int32, (ns, QB), 1)) // SEL_BLOCK
            forced = (blk == 0) | (blk == cur) | (blk == cur - 1)
            score = jnp.where(forced, 1e4, jnp.where(blk <= cur, imp_t, -1e4))
            rank = jnp.zeros((ns, QB), F32)
            for b in range(ns):
                other = score[b:b + 1, :]
                rank = rank + jnp.where(blk > b, jnp.where(other >= score, 1.0, 0.0),
                                        jnp.where(other > score, 1.0, 0.0))
            unsel = jnp.where(rank < float(min(N_SEL, ns)), 0.0, 1.0)
            unsel_q = jnp.concatenate([unsel, jnp.zeros((LANE - ns, QB), F32)], axis=0).T.astype(BF16)
            out["qs_aug", i] = jnp.concatenate([qs, jnp.concatenate([unsel_q] * H, axis=0)], axis=1)
        yield

    def slc_spans(i, qs, out):
        lo, hi = i * QB, (i + 1) * QB
        spans = [(lo, hi, ks_ref[0, lo:hi, :], qs, nb_ref[0, 0])]
        if i >= 1:
            qs_aug = out["qs_aug", i]
            spans.insert(0, (lo - QB, lo, ksa_ref[lo - QB:lo, :], qs_aug, nb_ref[0, 1]))
            if i >= 2:
                spans.insert(0, (0, lo - QB, ksa_ref[0:lo - QB, :], qs_aug, None))
        return spans

    def win_spans(i, qs):
        lo, hi = i * QB, (i + 1) * QB

        def span(a, b, add):
            return (a, b, kw_ref[0, a:b, :], qs, add)

        spans = []
        if i >= nwb:
            spans.append(span((i - nwb) * QB, (i - nwb + 1) * QB, nb_ref[0, 2]))
        mid_a, mid_b = max(i - nwb + 1, 0) * QB, (i - 1) * QB
        if mid_b > mid_a:
            spans.append(span(mid_a, mid_b, None))
        if i >= 1:
            spans.append(span(lo - QB, lo, nb_ref[0, 1]))
        spans.append(span(lo, hi, nb_ref[0, 0]))
        return spans

    def combine(i, out):
        lo, hi = i * QB, (i + 1) * QB
        gate_t = gate_ref[0, lo:hi, :].T
        first = pl.program_id(1) == 0

        def grow(br):
            rows = []
            for h in range(H):
                c = 3 * h + br
                rows.append(jnp.where(first, gate_t[c:c + 1, :], gate_t[3 * H + c:3 * H + c + 1, :]))
            return jnp.concatenate(rows, axis=1)

        o_t = grow(0) * out["cmp", i] + grow(1) * out["slc", i] + grow(2) * out["win", i]
        o_ref[0, lo:hi, :] = jnp.concatenate([o_t[:, h * QB:(h + 1) * QB].T for h in range(H)],
                                             axis=1).astype(o_ref.dtype)

    def pair(k):
        out = {}
        tasks = []
        blocks = (nb - 1 - k, k)
        qss = {}
        for i in blocks:
            q = q_ref[0, i * QB:(i + 1) * QB, :]
            qss[i] = jnp.concatenate([q[:, h * HEAD_DIM:(h + 1) * HEAD_DIM] for h in range(H)], axis=0)
            tasks.append(cmp_task(i, qss[i], out))
            tasks.append(attend_task(vwt_ref, functools.partial(win_spans, i, qss[i]), out, ("win", i)))
        for i in blocks:
            tasks.append(attend_task(vst_ref, functools.partial(slc_spans, i, qss[i], out), out, ("slc", i)))
        n_stage = 3
        for step in range(len(tasks) + n_stage - 1):
            for t in range(step - n_stage + 1, step + 1):
                if 0 <= t < len(tasks):
                    next(tasks[t])
        for i in blocks:
            combine(i, out)

    for k in range(nb // 2):
        pl.when(pl.program_id(2) == k)(functools.partial(pair, k))


def _nsa_t(qn, kv, kc, vc, gates, cbias, cmp_u0, cmp_per, nbias):
    B, T, _ = qn.shape
    G, H, dk = NSA_KV_GROUPS, NSA_HPG, HEAD_DIM
    nb = T // QB
    ns = T // SEL_BLOCK
    cstart = np.arange(LANE) * CMP_STRIDE
    sstart = np.arange(ns) * SEL_BLOCK
    overlap = np.clip(np.minimum(cstart[:, None] + CMP_LEN, sstart[None, :] + SEL_BLOCK)
                      - np.maximum(cstart[:, None], sstart[None, :]), 0, None) / CMP_STRIDE
    overlap[LANE - 1:] = 0.0
    ovt = jnp.asarray(overlap.T, F32)
    expand_t = np.zeros((T, LANE), np.float32)
    expand_t[np.arange(T), np.arange(T) // SEL_BLOCK] = MASK_BIG
    return pl.pallas_call(
        functools.partial(_nsa_t_kernel, cmp_u0=cmp_u0, cmp_per=cmp_per),
        grid=(B, G, nb // 2),
        in_specs=[pl.BlockSpec((1, T, H * dk), lambda b, g, i: (b, 0, g)),
                  pl.BlockSpec((1, T, dk), lambda b, g, i: (b, 0, g)),
                  pl.BlockSpec((1, T, dk), lambda b, g, i: (b, 0, G + g)),
                  pl.BlockSpec((1, T, dk), lambda b, g, i: (b, 0, 2 * G + g)),
                  pl.BlockSpec((1, T, dk), lambda b, g, i: (b, 0, 3 * G + g)),
                  pl.BlockSpec((1, 1, LANE, dk), lambda b, g, i: (b, g, 0, 0)),
                  pl.BlockSpec((1, 1, LANE, dk), lambda b, g, i: (b, g, 0, 0)),
                  pl.BlockSpec((1, T, LANE), lambda b, g, i: (b, 0, 0)),
                  pl.BlockSpec((1, 1, 2 * LANE, H * QB), lambda b, g, i: (g, 0, 0, 0)),
                  pl.BlockSpec((1, 3, QB, H * QB), lambda b, g, i: (g, 0, 0, 0)),
                  pl.BlockSpec((ns, LANE), lambda b, g, i: (0, 0)),
                  pl.BlockSpec((T, LANE), lambda b, g, i: (0, 0))],
        out_specs=pl.BlockSpec((1, T, H * dk), lambda b, g, i: (b, 0, g)),
        out_shape=jax.ShapeDtypeStruct((B, T, NSA_HEADS * dk), BF16),
        scratch_shapes=[pltpu.VMEM((dk, T), BF16), pltpu.VMEM((dk, T), BF16), pltpu.VMEM((dk, LANE), BF16),
                        pltpu.VMEM((T, 2 * dk), BF16)],
        compiler_params=_cparams(3),
        name="nsa",
    )(qn, kv, kv, kv, kv, kc, vc, gates, cbias, nbias, ovt, jnp.asarray(expand_t, BF16))


def _diff_kernel(q_ref, k_ref, v_ref, lq_ref, lk_ref, sg_ref, db_ref, o_ref):
    dk = HEAD_DIM
    T = k_ref.shape[1]
    lqk = lq_ref[...] * lk_ref[...]
    lam = (jnp.exp(jnp.sum(lqk[0:1], axis=-1, keepdims=True))
           - jnp.exp(jnp.sum(lqk[1:2], axis=-1, keepdims=True)) + LAM_INIT)
    def task(i, mm, out):
        lo, hi = i * DQB, (i + 1) * DQB
        cols = slice(mm * dk, (mm + 1) * dk)
        q = q_ref[0, lo:hi, cols]
        bounds, parts = [], []
        if i >= 2:
            bounds.append((0, lo - DQB))
            parts.append(_qk(q, k_ref[0, 0:lo - DQB, cols]))
        if i >= 1:
            bounds.append((lo - DQB, lo))
            parts.append(_qk(q, k_ref[0, lo - DQB:lo, cols]) + db_ref[0, 1])
        bounds.append((lo, hi))
        parts.append(_qk(q, k_ref[0, lo:hi, cols]) + db_ref[0, 0])
        yield
        ps, den = _softmax_parts(parts)
        yield
        o = None
        for (a, b), p in zip(bounds, ps):
            pv = jnp.dot(p.astype(BF16), v_ref[0, a:b, :], preferred_element_type=F32)
            o = pv if o is None else o + pv
        out[i, mm] = o / den
        yield

    out = {}
    tasks = [task(i, mm, out) for i in reversed(range(T // DQB)) for mm in range(2)]
    n_stage = 3
    for step in range(len(tasks) + n_stage - 1):
        for t in range(step - n_stage + 1, step + 1):
            if 0 <= t < len(tasks):
                next(tasks[t])

    for i in range(T // DQB):
        lo, hi = i * DQB, (i + 1) * DQB
        o = out[i, 0] - lam * out[i, 1]
        ms = jnp.mean(o * o, axis=-1, keepdims=True)
        o_ref[0, lo:hi, :] = ((o * lax.rsqrt(ms + EPS) * sg_ref[...]) * (1.0 - LAM_INIT)).astype(o_ref.dtype)


def _diff(dqk, dv, lam_q, lam_k, subln_gain, dbias):
    B, T, _ = dqk.shape
    Hd, dk = DIFF_HEADS, HEAD_DIM
    w = 2 * dk
    return pl.pallas_call(
        _diff_kernel,
        grid=(B, Hd),
        in_specs=[pl.BlockSpec((1, T, w), lambda b, h: (b, 0, h)),
                  pl.BlockSpec((1, T, w), lambda b, h: (b, 0, Hd + h)),
                  pl.BlockSpec((1, T, w), lambda b, h: (b, 0, h)),
                  pl.BlockSpec((2, dk), lambda b, h: (0, 0)),
                  pl.BlockSpec((2, dk), lambda b, h: (0, 0)),
                  pl.BlockSpec((1, w), lambda b, h: (0, 0)),
                  pl.BlockSpec((1, 2, DQB, DQB), lambda b, h: (h, 0, 0, 0))],
        out_specs=pl.BlockSpec((1, T, w), lambda b, h: (b, 0, h)),
        out_shape=jax.ShapeDtypeStruct((B, T, Hd * w), BF16),
        compiler_params=_cparams(2),
        name="diff",
    )(dqk, dqk, dv, lam_q, lam_k, subln_gain.reshape(1, w), dbias)


def _merge_kernel(an_ref, ad_ref, wn_ref, wd_ref, gn_ref, gd_ref, o_ref, wnb_ref, wdb_ref):
    @pl.when(pl.program_id(1) == 0)
    def _():
        wnb_ref[...] = wn_ref[...].astype(BF16)
        wdb_ref[...] = wd_ref[...].astype(BF16)

    for rows in _row_chunks(an_ref.shape[0]):
        yn = jnp.dot(an_ref[rows, :], wnb_ref[...], preferred_element_type=F32)
        yd = jnp.dot(ad_ref[rows, :], wdb_ref[...], preferred_element_type=F32)
        o_ref[rows, :] = (gn_ref[rows, :].astype(F32) * yn + gd_ref[rows, :].astype(F32) * yd).astype(o_ref.dtype)


def _merge(o_nsa, o_diff, w_n, w_d, mg, tm=1024, tn=1024):
    M, K = o_nsa.shape
    N = w_n.shape[1]
    nj = N // tn
    return pl.pallas_call(
        _merge_kernel,
        grid=(nj, M // tm),
        in_specs=[pl.BlockSpec((tm, K), lambda j, i: (i, 0)),
                  pl.BlockSpec((tm, K), lambda j, i: (i, 0)),
                  pl.BlockSpec((K, tn), lambda j, i: (0, j)),
                  pl.BlockSpec((K, tn), lambda j, i: (0, j)),
                  pl.BlockSpec((tm, tn), lambda j, i: (i, j)),
                  pl.BlockSpec((tm, tn), lambda j, i: (i, nj + j))],
        out_specs=pl.BlockSpec((tm, tn), lambda j, i: (i, j)),
        out_shape=jax.ShapeDtypeStruct((M, N), BF16),
        scratch_shapes=[pltpu.VMEM((K, tn), BF16), pltpu.VMEM((K, tn), BF16)],
        compiler_params=_cparams(2),
        name="merge",
    )(o_nsa, o_diff, w_n, w_d, mg, mg)


def _oproj_kernel(a_ref, w_ref, x_ref, g1_ref, gain_ref, sc_ref, sh_ref, x1_ref, h2_ref):
    for rows in _row_chunks(a_ref.shape[0]):
        y = jnp.dot(a_ref[rows, :], w_ref[...], preferred_element_type=F32)
        x1 = x_ref[rows, :] + g1_ref[0] * y
        x1_ref[rows, :] = x1
        h2_ref[rows, :] = _modnorm(x1, gain_ref[...], sc_ref[0], sh_ref[0]).astype(h2_ref.dtype)


def _oproj(merged, w_o, x2d, mod3, gain2, T, tm=512):
    M, D = x2d.shape
    per = T // tm
    return pl.pallas_call(
        _oproj_kernel,
        grid=(M // tm,),
        in_specs=[pl.BlockSpec((tm, D), lambda i: (i, 0)),
                  pl.BlockSpec((D, D), lambda i: (0, 0)),
                  pl.BlockSpec((tm, D), lambda i: (i, 0)),
                  pl.BlockSpec((1, 1, D), lambda i: ((i // per) * 6 + 2, 0, 0)),
                  pl.BlockSpec((1, D), lambda i: (0, 0)),
                  pl.BlockSpec((1, 1, D), lambda i: ((i // per) * 6 + 4, 0, 0)),
                  pl.BlockSpec((1, 1, D), lambda i: ((i // per) * 6 + 3, 0, 0))],
        out_specs=[pl.BlockSpec((tm, D), lambda i: (i, 0)),
                   pl.BlockSpec((tm, D), lambda i: (i, 0))],
        out_shape=[jax.ShapeDtypeStruct((M, D), F32), jax.ShapeDtypeStruct((M, D), BF16)],
        compiler_params=_cparams(1),
        name="oproj",
    )(merged, w_o, x2d, mod3, gain2.reshape(1, D), mod3, mod3)


def _ffn_up_kernel(h_ref, wa_ref, wv_ref, cwa_ref, cwv_ref, cba_ref, cbv_ref, o_ref, wab_ref, wvb_ref,
                   ca_ref, cv_ref, sa_ref, sv_ref, *, per):
    i = pl.program_id(1)

    @pl.when(i == 0)
    def _():
        wab_ref[...] = wa_ref[...].astype(BF16)
        wvb_ref[...] = wv_ref[...].astype(BF16)

    @pl.when(i % per == 0)
    def _():
        ca_ref[...] = jnp.zeros(ca_ref.shape, F32)
        cv_ref[...] = jnp.zeros(cv_ref.shape, F32)

    def conv(u, prev, cw_ref, cb_ref, s_ref):
        s_ref[0:8, :] = prev
        s_ref[8:8 + SUB_ROWS, :] = u
        u1 = s_ref[7:7 + SUB_ROWS, :]
        u2 = s_ref[6:6 + SUB_ROWS, :]
        return cb_ref[...] + cw_ref[0:1, :] * u2 + cw_ref[1:2, :] * u1 + cw_ref[2:3, :] * u

    prev_a, prev_v = ca_ref[...], cv_ref[...]
    for n, rows in enumerate(_row_chunks(h_ref.shape[0])):
        hs = h_ref[rows, :]
        ua = jnp.dot(hs, wab_ref[...], preferred_element_type=F32)
        uv = jnp.dot(hs, wvb_ref[...], preferred_element_type=F32)
        a = conv(ua, prev_a, cwa_ref, cba_ref, sa_ref.at[n % 2])
        val = conv(uv, prev_v, cwv_ref, cbv_ref, sv_ref.at[n % 2])
        o_ref[rows, :] = (a * jax.nn.sigmoid(a) * val).astype(o_ref.dtype)
        prev_a, prev_v = ua[SUB_ROWS - 8:, :], uv[SUB_ROWS - 8:, :]
    ca_ref[...] = prev_a
    cv_ref[...] = prev_v


def _ffn_up(h2, w_up, conv_w, conv_b, T, tm=2048, tn=512):
    M, D = h2.shape
    F = w_up.shape[1] // 2
    nj = F // tn
    cb = conv_b.reshape(1, 2 * F)
    return pl.pallas_call(
        functools.partial(_ffn_up_kernel, per=T // tm),
        grid=(nj, M // tm),
        in_specs=[pl.BlockSpec((tm, D), lambda j, i: (i, 0)),
                  pl.BlockSpec((D, tn), lambda j, i: (0, j)),
                  pl.BlockSpec((D, tn), lambda j, i: (0, nj + j)),
                  pl.BlockSpec((3, tn), lambda j, i: (0, j)),
                  pl.BlockSpec((3, tn), lambda j, i: (0, nj + j)),
                  pl.BlockSpec((1, tn), lambda j, i: (0, j)),
                  pl.BlockSpec((1, tn), lambda j, i: (0, nj + j))],
        out_specs=pl.BlockSpec((tm, tn), lambda j, i: (i, j)),
        out_shape=jax.ShapeDtypeStruct((M, F), BF16),
        scratch_shapes=[pltpu.VMEM((D, tn), BF16), pltpu.VMEM((D, tn), BF16),
                        pltpu.VMEM((8, tn), F32), pltpu.VMEM((8, tn), F32),
                        pltpu.VMEM((2, SUB_ROWS + 8, tn), F32), pltpu.VMEM((2, SUB_ROWS + 8, tn), F32)],
        compiler_params=_cparams(2),
        name="ffn_up",
    )(h2, w_up, w_up, conv_w, conv_w, cb, cb)


def _ffn_down_kernel(a_ref, w_ref, x_ref, g2_ref, o_ref, wb_ref):
    @pl.when(pl.program_id(1) == 0)
    def _():
        wb_ref[...] = w_ref[...].astype(BF16)

    for rows in _row_chunks(a_ref.shape[0]):
        y = jnp.dot(a_ref[rows, :], wb_ref[...], preferred_element_type=F32)
        o_ref[rows, :] = x_ref[rows, :] + g2_ref[0] * y


def _ffn_down(act, w_down, x1, mod3, T, tm=512, tn=512):
    M, F = act.shape
    D = w_down.shape[1]
    per = T // tm
    return pl.pallas_call(
        _ffn_down_kernel,
        grid=(D // tn, M // tm),
        in_specs=[pl.BlockSpec((tm, F), lambda j, i: (i, 0)),
                  pl.BlockSpec((F, tn), lambda j, i: (0, j)),
                  pl.BlockSpec((tm, tn), lambda j, i: (i, j)),
                  pl.BlockSpec((1, 1, tn), lambda j, i: ((i // per) * 6 + 5, 0, j))],
        out_specs=pl.BlockSpec((tm, tn), lambda j, i: (i, j)),
        out_shape=jax.ShapeDtypeStruct((M, D), F32),
        scratch_shapes=[pltpu.VMEM((F, tn), BF16)],
        compiler_params=_cparams(2),
        name="ffn_down",
    )(act, w_down, x1, mod3)


def _layer(x, c, w_ada, b_ada, norm1_gain, norm2_gain, w_in, nsa_q_gain, nsa_k_gain, cmp_pe, cmp_w1, cmp_w2,
           diff_q_gain, diff_k_gain, diff_lambda_q, diff_lambda_k, diff_subln_gain, w_nsa_out, w_diff_out, w_o,
           w_ffn_up, ffn_conv_w, ffn_conv_b, w_ffn_down, rel_bias):
    B, T, D = x.shape
    dk, G = HEAD_DIM, NSA_KV_GROUPS
    M = B * T
    scale = dk ** -0.5

    n_q = NSA_HEADS * dk
    o_kv = n_q
    o_g = o_kv + 3 * 2 * G * dk
    o_dq = o_g + NSA_HEADS * 3
    o_dk = o_dq + DIFF_HEADS * 2 * dk
    o_dv = o_dk + DIFF_HEADS * 2 * dk
    o_mg = o_dv + DIFF_HEADS * 2 * dk
    n_kv = 2 * G * dk
    n_dqk = 2 * DIFF_HEADS * 2 * dk

    mod3 = _ada(c, w_ada, b_ada).reshape(B * 6, 1, D)
    h = _norm1(x, norm1_gain, mod3).reshape(M, D)

    ones = jnp.ones((n_kv // 2,), F32)
    g_q = jnp.tile(nsa_q_gain * (scale * LOG2E), NSA_HEADS).reshape(1, n_q)
    g_kv = jnp.concatenate([jnp.tile(nsa_k_gain[1], G), ones, jnp.tile(nsa_k_gain[2], G), ones]).reshape(1, 2 * n_kv)
    g_dqk = jnp.concatenate([jnp.tile(diff_q_gain * (scale * LOG2E), 2 * DIFF_HEADS),
                             jnp.tile(diff_k_gain, 2 * DIFF_HEADS)]).reshape(1, n_dqk)
    g_one = jnp.ones((1, 2 * D), F32)
    proj = functools.partial(_inproj, h, jnp.swapaxes(w_in, 1, 2), tm=INPROJ_TM)
    wide = INPROJ_TN
    yes, no = (True,) * (wide // LANE), (False,) * (wide // LANE)
    qn = proj(g_q, src0=0, ncols=n_q, heads=yes, mode="raw", out_dtype=BF16, tn=wide, name="inproj_q")
    cmpkv, gates = _inproj_cmp_gate(h, jnp.swapaxes(w_in, 1, 2), src_cmp=o_kv, n_cmp=n_kv, src_gate=o_g,
                                    tm=INPROJ_TM)
    kv_heads = ((True,) * G + (False,) * G) * 2
    kv = proj(g_kv, src0=o_kv + n_kv, ncols=2 * n_kv, heads=kv_heads, mode="raw", out_dtype=BF16, tn=2 * n_kv,
              name="inproj_kv")
    dqk = proj(g_dqk, src0=o_dq, ncols=n_dqk, heads=yes, mode="raw", out_dtype=BF16, tn=wide, name="inproj_dqk")
    dv = proj(g_one, src0=o_dv, ncols=o_mg - o_dv, heads=no, mode="raw", out_dtype=BF16, tn=wide, name="inproj_dv")
    mgate = proj(g_one, src0=o_mg, ncols=2 * D, heads=no, mode="sigmoid", out_dtype=BF16, tn=wide, name="inproj_mg")

    nbias = _bias_tiles(_window_idx(QB).transpose(0, 2, 1), rel_bias, head0=0, groups=G, hpg=NSA_HPG,
                        name="bias_nsa", rel=True, mult=LOG2E)
    cmp_idx, cmp_u0, cmp_per = _cmp_idx(T)
    cbias = _bias_tiles(cmp_idx.transpose(0, 2, 1), rel_bias, head0=0, groups=G, hpg=NSA_HPG, name="bias_cmp",
                        mult=LOG2E)
    dbias = _bias_tiles(_causal_idx(DQB), rel_bias, head0=NSA_HEADS, groups=DIFF_HEADS, hpg=1, name="bias_diff",
                        rel=True, mult=LOG2E)

    kc, vc = _compress(cmpkv.reshape(B, T, n_kv), cmp_pe, cmp_w1, cmp_w2, nsa_k_gain[0])
    o_nsa = _nsa_t(qn.reshape(B, T, n_q), kv.reshape(B, T, 2 * n_kv), kc, vc, gates.reshape(B, T, LANE), cbias,
                   cmp_u0, cmp_per, nbias)
    o_diff = _diff(dqk.reshape(B, T, n_dqk), dv.reshape(B, T, -1), diff_lambda_q, diff_lambda_k, diff_subln_gain,
                   dbias)

    merged = _merge(o_nsa.reshape(M, -1), o_diff.reshape(M, -1), w_nsa_out, w_diff_out, mgate)
    x1, h2 = _oproj(merged, w_o.astype(BF16), x.reshape(M, D), mod3, norm2_gain, T)
    act = _ffn_up(h2, w_ffn_up, ffn_conv_w, ffn_conv_b, T)
    out = _ffn_down(act, w_ffn_down, x1, mod3, T)
    return out.reshape(B, T, D)


def kernel(x, c, w_ada, b_ada, norm1_gain, norm2_gain, w_in, nsa_q_gain, nsa_k_gain, cmp_pe, cmp_w1, cmp_w2,
           diff_q_gain, diff_k_gain, diff_lambda_q, diff_lambda_k, diff_subln_gain, w_nsa_out, w_diff_out, w_o,
           w_ffn_up, ffn_conv_w, ffn_conv_b, w_ffn_down, rel_bias):
    return _layer(x, c, w_ada[0], b_ada[0], norm1_gain[0], norm2_gain[0], w_in, nsa_q_gain[0], nsa_k_gain[0],
                  cmp_pe[0], cmp_w1[0], cmp_w2[0], diff_q_gain[0], diff_k_gain[0], diff_lambda_q[0],
                  diff_lambda_k[0], diff_subln_gain[0], w_nsa_out[0], w_diff_out[0], w_o[0], w_ffn_up[0],
                  ffn_conv_w[0], ffn_conv_b[0], w_ffn_down[0], rel_bias)
```

```python
import functools
import math

import numpy as np
import jax
import jax.numpy as jnp
from jax import lax
from jax.experimental import pallas as pl
from jax.experimental.pallas import tpu as pltpu

F32 = jnp.float32
BF16 = jnp.bfloat16

HEAD_DIM = 128
NSA_HEADS = 8
NSA_KV_GROUPS = 2
NSA_HPG = NSA_HEADS // NSA_KV_GROUPS
CMP_LEN = 32
CMP_STRIDE = 16
SEL_BLOCK = 64
N_SEL = 16
WINDOW = 512
DIFF_HEADS = 4
NUM_BUCKETS = 32
MAX_DISTANCE = 128
EPS = 1e-6
NEG = -1e30
LAM_INIT = 0.8 - 0.6 * math.exp(-0.3 * 0)
LOG2E = math.log2(math.e)
MASK_BIG = -(2.0 ** 100)

LANE = 128
QB = 128
DQB = 256
SUB_ROWS = 256
INPROJ_TM = 2048
INPROJ_TN = 1024
VMEM_LIMIT = 56 * 1024 * 1024


def _cparams(n_axes):
    return pltpu.CompilerParams(dimension_semantics=("arbitrary",) * n_axes,
                                vmem_limit_bytes=VMEM_LIMIT)


def _t5_bucket_np(dist):
    n = np.maximum(np.asarray(dist, np.int32), 0)
    max_exact = NUM_BUCKETS // 2
    nf = np.maximum(n, max_exact).astype(np.float32)
    large = max_exact + (np.log(nf / np.float32(max_exact)) / np.float32(math.log(MAX_DISTANCE / max_exact))
                         * np.float32(NUM_BUCKETS - max_exact)).astype(np.int32)
    large = np.minimum(large, NUM_BUCKETS - 1)
    return np.where(n < max_exact, n, large).astype(np.int32)


def _ada_kernel(ct_ref, w_ref, b_ref, o_ref):
    ct = ct_ref[...]
    s = ct * jax.nn.sigmoid(ct)
    w = w_ref[...]
    for b in range(ct.shape[1]):
        o_ref[b:b + 1, :] = jnp.sum(w * s[:, b:b + 1], axis=0, keepdims=True) + b_ref[...]


def _ada(c, w_ada, b_ada, tn=1024):
    B, D = c.shape
    N = w_ada.shape[1]
    return pl.pallas_call(
        _ada_kernel,
        grid=(N // tn,),
        in_specs=[pl.BlockSpec((D, B), lambda j: (0, 0)),
                  pl.BlockSpec((D, tn), lambda j: (0, j)),
                  pl.BlockSpec((1, tn), lambda j: (0, j))],
        out_specs=pl.BlockSpec((B, tn), lambda j: (0, j)),
        out_shape=jax.ShapeDtypeStruct((B, N), F32),
        compiler_params=_cparams(1),
        name="ada",
    )(c.T, w_ada, b_ada.reshape(1, N))


def _modnorm(x, gain, sc, sh):
    ms = jnp.mean(x * x, axis=-1, keepdims=True)
    return (x * lax.rsqrt(ms + EPS) * gain) * (1.0 + sc) + sh


def _row_chunks(tm):
    return [slice(r, r + SUB_ROWS) for r in range(0, tm, SUB_ROWS)]


def _inproj_epilogue(acc, g_ref, o_ref, rows, heads, mode):
    for k, is_head in enumerate(heads):
        lanes = slice(k * LANE, (k + 1) * LANE)
        y = acc[:, lanes]
        if is_head:
            ms = jnp.mean(y * y, axis=-1, keepdims=True)
            y = y * lax.rsqrt(ms + EPS) * g_ref[:, lanes]
        elif mode == "sigmoid":
            y = jax.nn.sigmoid(y)
        o_ref[rows, lanes] = y.astype(o_ref.dtype)


def _inproj_kernel(a_ref, wt_ref, g_ref, o_ref, wb_ref, *, heads, mode):
    @pl.when(pl.program_id(1) == 0)
    def _():
        wb_ref[...] = wt_ref[0].astype(BF16)

    for rows in _row_chunks(a_ref.shape[0]):
        acc = _qk(a_ref[rows, :], wb_ref[...])
        _inproj_epilogue(acc, g_ref, o_ref, rows, heads, mode)


def _norm1_inproj_kernel(x_ref, n1g_ref, sc_ref, sh_ref, wt_ref, g_ref, o_ref, h_ref, wb_ref, *, heads, mode):
    @pl.when(pl.program_id(0) == 0)
    def _():
        wb_ref[...] = wt_ref[0].astype(BF16)

    for rows in _row_chunks(x_ref.shape[0]):
        h = _modnorm(x_ref[rows, :], n1g_ref[...], sc_ref[0], sh_ref[0]).astype(BF16)
        h_ref[rows, :] = h
        _inproj_epilogue(_qk(h, wb_ref[...]), g_ref, o_ref, rows, heads, mode)


def _norm1_inproj(x2d, n1_gain, mod3, w_in_t, gains, *, T, src0, ncols, heads, mode, out_dtype, tm, name):
    M, K = x2d.shape
    per = T // tm
    assert src0 % 8 == 0 and len(heads) == ncols // LANE
    return pl.pallas_call(
        functools.partial(_norm1_inproj_kernel, heads=heads, mode=mode),
        grid=(M // tm,),
        in_specs=[pl.BlockSpec((tm, K), lambda i: (i, 0)),
                  pl.BlockSpec((1, K), lambda i: (0, 0)),
                  pl.BlockSpec((1, 1, K), lambda i: ((i // per) * 6 + 1, 0, 0)),
                  pl.BlockSpec((1, 1, K), lambda i: ((i // per) * 6 + 0, 0, 0)),
                  pl.BlockSpec((pl.Element(1), pl.Element(ncols), pl.Element(K)), lambda i: (0, src0, 0)),
                  pl.BlockSpec((1, ncols), lambda i: (0, 0))],
        out_specs=[pl.BlockSpec((tm, ncols), lambda i: (i, 0)), pl.BlockSpec((tm, K), lambda i: (i, 0))],
        out_shape=[jax.ShapeDtypeStruct((M, ncols), out_dtype), jax.ShapeDtypeStruct((M, K), BF16)],
        scratch_shapes=[pltpu.VMEM((ncols, K), BF16)],
        compiler_params=_cparams(1),
        name=name,
    )(x2d, n1_gain.reshape(1, K), mod3, mod3, w_in_t, gains)


def _inproj(h2d, w_in_t, gains, *, src0, ncols, heads, mode, out_dtype, tm, tn, name):
    M, K = h2d.shape
    assert src0 % 8 == 0 and ncols % tn == 0 and len(heads) == tn // LANE
    return pl.pallas_call(
        functools.partial(_inproj_kernel, heads=heads, mode=mode),
        grid=(ncols // tn, M // tm),
        in_specs=[pl.BlockSpec((tm, K), lambda j, i: (i, 0)),
                  pl.BlockSpec((pl.Element(1), pl.Element(tn), pl.Element(K)),
                               lambda j, i: (0, pl.multiple_of(src0 + j * tn, 8), 0)),
                  pl.BlockSpec((1, tn), lambda j, i: (0, j))],
        out_specs=pl.BlockSpec((tm, tn), lambda j, i: (i, j)),
        out_shape=jax.ShapeDtypeStruct((M, ncols), out_dtype),
        scratch_shapes=[pltpu.VMEM((tn, K), BF16)],
        compiler_params=_cparams(2),
        name=name,
    )(h2d, w_in_t, gains)


def _inproj_cmp_gate_kernel(a_ref, wc_ref, wg_ref, oc_ref, og_ref, wcb_ref, wgb_ref):
    @pl.when(pl.program_id(0) == 0)
    def _():
        wcb_ref[...] = wc_ref[0].astype(BF16)
        wgb_ref[...] = wg_ref[0].astype(BF16)

    for rows in _row_chunks(a_ref.shape[0]):
        a = a_ref[rows, :]
        oc_ref[rows, :] = _qk(a, wcb_ref[...])
        og_ref[rows, :] = jax.nn.sigmoid(_qk(a, wgb_ref[...]))


def _inproj_cmp_gate(h2d, w_in_t, *, src_cmp, n_cmp, src_gate, tm):
    M, K = h2d.shape
    assert src_cmp % 8 == 0 and src_gate % 8 == 0

    def w_rows(start, n):
        return pl.BlockSpec((pl.Element(1), pl.Element(n), pl.Element(K)), lambda i: (0, start, 0))

    return pl.pallas_call(
        _inproj_cmp_gate_kernel,
        grid=(M // tm,),
        in_specs=[pl.BlockSpec((tm, K), lambda i: (i, 0)), w_rows(src_cmp, n_cmp), w_rows(src_gate, LANE)],
        out_specs=[pl.BlockSpec((tm, n_cmp), lambda i: (i, 0)), pl.BlockSpec((tm, LANE), lambda i: (i, 0))],
        out_shape=[jax.ShapeDtypeStruct((M, n_cmp), F32), jax.ShapeDtypeStruct((M, LANE), F32)],
        scratch_shapes=[pltpu.VMEM((n_cmp, K), BF16), pltpu.VMEM((LANE, K), BF16)],
        compiler_params=_cparams(1),
        name="inproj_cmp_gate",
    )(h2d, w_in_t, w_in_t)


def _bias_kernel(idx_ref, tab_ref, o_ref, *, head0, hpg, rel, mult):
    head = head0 + pl.program_id(0) * hpg + pl.program_id(2)
    idx = idx_ref[0]
    acc = jnp.zeros(idx.shape, F32)
    for b in range(NUM_BUCKETS):
        acc = jnp.where(idx == b, tab_ref[b, head], acc)
    if rel:
        acc = acc - tab_ref[NUM_BUCKETS - 1, head]
    o_ref[0, 0] = jnp.where(idx < 0, NEG, acc * mult)


def _bias_tiles(idx, rel_bias, *, head0, groups, hpg, name, rel=False, mult=1.0):
    N, R, C = idx.shape
    return pl.pallas_call(
        functools.partial(_bias_kernel, head0=head0, hpg=hpg, rel=rel, mult=mult),
        grid=(groups, N, hpg),
        in_specs=[pl.BlockSpec((1, R, C), lambda g, n, h: (n, 0, 0)),
                  pl.BlockSpec(memory_space=pltpu.SMEM)],
        out_specs=pl.BlockSpec((1, 1, R, C), lambda g, n, h: (g, n, 0, h)),
        out_shape=jax.ShapeDtypeStruct((groups, N, R, hpg * C), F32),
        compiler_params=_cparams(3),
        name=name,
    )(jnp.asarray(idx), rel_bias)


def _causal_idx(R):
    r = np.arange(R)[:, None]
    c = np.arange(R)[None, :]
    return np.stack([np.where(r >= c, _t5_bucket_np(r - c), -1), _t5_bucket_np(R + r - c)]).astype(np.int32)


def _window_idx(R):
    r = np.arange(R)[:, None]
    c = np.arange(R)[None, :]
    edge = np.where(r < c, NUM_BUCKETS - 1, -1)
    return np.concatenate([_causal_idx(R), edge[None]]).astype(np.int32)


def _cmp_idx(T):
    per = QB // CMP_STRIDE
    u0 = (T // QB - 1) * per
    assert u0 + LANE <= 2 * LANE
    r = np.arange(QB)[:, None]
    end = (np.arange(2 * LANE)[None, :] - u0) * CMP_STRIDE + CMP_LEN - 1
    return np.where(end <= r, _t5_bucket_np(r - end), -1).astype(np.int32)[None], u0, per


def _compress_kernel(zk_ref, zv_ref, pe_ref, w1_ref, w2_ref, kg_ref, kc_ref, vc_ref):
    half = CMP_LEN // 2

    def one(z_ref, i):
        p1 = jnp.zeros((LANE, HEAD_DIM), F32)
        p2 = jnp.zeros((LANE, HEAD_DIM), F32)
        for l in range(half):
            z = z_ref[0, pl.ds(l, LANE, stride=CMP_STRIDE), :]
            w_lo = w1_ref[i, l * HEAD_DIM:(l + 1) * HEAD_DIM, :].astype(BF16)
            w_hi = w1_ref[i, (half + l) * HEAD_DIM:(half + l + 1) * HEAD_DIM, :].astype(BF16)
            p1 = p1 + jnp.dot((z + pe_ref[i, l:l + 1, :]).astype(BF16), w_lo, preferred_element_type=F32)
            p2 = p2 + jnp.dot((z + pe_ref[i, half + l:half + l + 1, :]).astype(BF16), w_hi,
                              preferred_element_type=F32)
        pre = p1 + pltpu.roll(p2, LANE - 1, axis=0)
        hid = jax.nn.gelu(pre)
        return jnp.dot(hid.astype(BF16), w2_ref[i].astype(BF16), preferred_element_type=F32)

    kc = one(zk_ref, 0)
    ms = jnp.mean(kc * kc, axis=-1, keepdims=True)
    kc_ref[0, 0] = (kc * lax.rsqrt(ms + EPS) * kg_ref[...]).astype(kc_ref.dtype)
    vc_ref[0, 0] = one(zv_ref, 1).astype(vc_ref.dtype)


def _compress(cmp_kv, pe, w1, w2, k_gain0):
    B, T, _ = cmp_kv.shape
    G, dk = NSA_KV_GROUPS, HEAD_DIM
    assert (T - CMP_LEN) // CMP_STRIDE + 1 == LANE - 1
    out = jax.ShapeDtypeStruct((B, G, LANE, dk), BF16)
    return pl.pallas_call(
        _compress_kernel,
        grid=(B, G),
        in_specs=[pl.BlockSpec((1, T, dk), lambda b, g: (b, 0, g)),
                  pl.BlockSpec((1, T, dk), lambda b, g: (b, 0, G + g)),
                  pl.BlockSpec((2, CMP_LEN, dk), lambda b, g: (0, 0, 0)),
                  pl.BlockSpec((2, CMP_LEN * dk, dk), lambda b, g: (0, 0, 0)),
                  pl.BlockSpec((2, dk, dk), lambda b, g: (0, 0, 0)),
                  pl.BlockSpec((1, dk), lambda b, g: (0, 0))],
        out_specs=[pl.BlockSpec((1, 1, LANE, dk), lambda b, g: (b, g, 0, 0))] * 2,
        out_shape=[out, out],
        compiler_params=_cparams(2),
        name="compress",
    )(cmp_kv, cmp_kv, pe, w1, w2, k_gain0.reshape(1, dk))


def _qk(q, k):
    return lax.dot_general(q, k, (((1,), (1,)), ((), ())), preferred_element_type=F32)


def _lane_fold(x, op):
    acc = x[..., :LANE]
    for t in range(1, x.shape[-1] // LANE):
        acc = op(acc, x[..., t * LANE:(t + 1) * LANE])
    return acc


def _softmax_parts(parts):
    m = jnp.max(functools.reduce(jnp.maximum, [_lane_fold(s, jnp.maximum) for s in parts]), axis=-1, keepdims=True)
    ps = [jnp.exp2(s - m) for s in parts]
    den = jnp.sum(functools.reduce(jnp.add, [_lane_fold(p, jnp.add) for p in ps]), axis=-1, keepdims=True)
    return ps, den


def _nsa_t_kernel(q_ref, ks_ref, vs_ref, kw_ref, vw_ref, kc_ref, vc_ref, gate_ref, cb_ref, nb_ref,
                  ovt_ref, expt_ref, o_ref, vst_ref, vwt_ref, vct_ref, ksa_ref, *, cmp_u0, cmp_per):
    H, W = NSA_HPG, NSA_HPG * QB
    T = ks_ref.shape[1]
    ns = ovt_ref.shape[0]
    nwb = WINDOW // QB
    nb = T // QB
    kc = kc_ref[0, 0]

    @pl.when(pl.program_id(2) == 0)
    def _():
        ksa_ref[:, :HEAD_DIM] = ks_ref[0]
        ksa_ref[:, HEAD_DIM:] = expt_ref[...]
        vst_ref[...] = vs_ref[0].astype(F32).T.astype(BF16)
        vwt_ref[...] = vw_ref[0].astype(F32).T.astype(BF16)
        vct_ref[...] = vc_ref[0, 0].astype(F32).T.astype(BF16)

    def softmax_t(parts):
        m = functools.reduce(jnp.maximum, [jnp.max(s, axis=0, keepdims=True) for s in parts])
        ps = [jnp.exp2(s - m) for s in parts]
        den = functools.reduce(jnp.add, [jnp.sum(p, axis=0, keepdims=True) for p in ps])
        return [p.astype(BF16) for p in ps], den

    def attend_task(vt_ref, spans_fn, out, key):
        spans = spans_fn()
        parts = []
        for _, _, k_slab, q_op, add in spans:
            s = _qk(k_slab, q_op)
            parts.append(s if add is None else s + add)
        yield
        ps, den = softmax_t(parts)
        yield
        o = None
        for (a, b, _, _, _), p in zip(spans, ps):
            pv = jnp.dot(vt_ref[:, a:b], p, preferred_element_type=F32)
            o = pv if o is None else o + pv
        out[key] = o / den
        yield

    def cmp_task(i, qs, out):
        lo = i * QB
        u = cmp_u0 - cmp_per * i
        s = _qk(kc, qs) + cb_ref[0, 0, u:u + LANE, :]
        yield
        e = jnp.exp2(s - jnp.max(s, axis=0, keepdims=True))
        if i == 0:
            key_end = lax.broadcasted_iota(jnp.int32, (LANE, W), 0) * CMP_STRIDE + (CMP_LEN - 1)
            query = lax.broadcasted_iota(jnp.int32, (LANE, W), 1) % QB
            e = jnp.where(key_end <= query, e, 0.0)
            den = jnp.sum(e, axis=0, keepdims=True)
            p = e / jnp.where(den > 0.0, den, 1.0)
        else:
            p = e / jnp.sum(e, axis=0, keepdims=True)
        if i >= 1:
            psum = functools.reduce(jnp.add, [p[:, h * QB:(h + 1) * QB] for h in range(H)])
            imp_t = jnp.dot(ovt_ref[...], psum, precision=lax.Precision.HIGHEST, preferred_element_type=F32)
        yield
        out["cmp", i] = jnp.dot(vct_ref[...], p.astype(BF16), preferred_element_type=F32)
        if i >= 1:
            blk = lax.broadcasted_iota(jnp.int32, (ns, QB), 0)
            cur = (lo + lax.broadcasted_iota(jnp.int32, (ns, QB), 1)) // SEL_BLOCK
            forced = (blk == 0) | (blk == cur) | (blk == cur - 1)
            score = jnp.where(forced, 1e4, jnp.where(blk <= cur, imp_t, -1e4))
            rank = jnp.zeros((ns, QB), F32)
            for b in range(ns):
                other = score[b:b + 1, :]
                rank = rank + jnp.where(blk > b, jnp.where(other >= score, 1.0, 0.0),
                                        jnp.where(other > score, 1.0, 0.0))
            unsel = jnp.where(rank < float(min(N_SEL, ns)), 0.0, 1.0)
            unsel_q = jnp.concatenate([unsel, jnp.zeros((LANE - ns, QB), F32)], axis=0).T.astype(BF16)
            out["qs_aug", i] = jnp.concatenate([qs, jnp.concatenate([unsel_q] * H, axis=0)], axis=1)
        yield

    def slc_spans(i, qs, out):
        lo, hi = i * QB, (i + 1) * QB
        spans = [(lo, hi, ks_ref[0, lo:hi, :], qs, nb_ref[0, 0])]
        if i >= 1:
            qs_aug = out["qs_aug", i]
            spans.insert(0, (lo - QB, lo, ksa_ref[lo - QB:lo, :], qs_aug, nb_ref[0, 1]))
            if i >= 2:
                spans.insert(0, (0, lo - QB, ksa_ref[0:lo - QB, :], qs_aug, None))
        return spans

    def win_spans(i, qs):
        lo, hi = i * QB, (i + 1) * QB

        def span(a, b, add):
            return (a, b, kw_ref[0, a:b, :], qs, add)

        spans = []
        if i >= nwb:
            spans.append(span((i - nwb) * QB, (i - nwb + 1) * QB, nb_ref[0, 2]))
        mid_a, mid_b = max(i - nwb + 1, 0) * QB, (i - 1) * QB
        if mid_b > mid_a:
            spans.append(span(mid_a, mid_b, None))
        if i >= 1:
            spans.append(span(lo - QB, lo, nb_ref[0, 1]))
        spans.append(span(lo, hi, nb_ref[0, 0]))
        return spans

    def combine(i, out):
        lo, hi = i * QB, (i + 1) * QB
        gate_t = gate_ref[0, lo:hi, :].T
        first = pl.program_id(1) == 0

        def grow(br):
            rows = []
            for h in range(H):
                c = 3 * h + br
                rows.append(jnp.where(first, gate_t[c:c + 1, :], gate_t[3 * H + c:3 * H + c + 1, :]))
            return jnp.concatenate(rows, axis=1)

        o_t = grow(0) * out["cmp", i] + grow(1) * out["slc", i] + grow(2) * out["win", i]
        o_ref[0, lo:hi, :] = jnp.concatenate([o_t[:, h * QB:(h + 1) * QB].T for h in range(H)],
                                             axis=1).astype(o_ref.dtype)

    def pair(k):
        out = {}
        tasks = []
        blocks = (nb - 1 - k, k)
        qss = {}
        for i in blocks:
            q = q_ref[0, i * QB:(i + 1) * QB, :]
            qss[i] = jnp.concatenate([q[:, h * HEAD_DIM:(h + 1) * HEAD_DIM] for h in range(H)], axis=0)
            tasks.append(cmp_task(i, qss[i], out))
            tasks.append(attend_task(vwt_ref, functools.partial(win_spans, i, qss[i]), out, ("win", i)))
        for i in blocks:
            tasks.append(attend_task(vst_ref, functools.partial(slc_spans, i, qss[i], out), out, ("slc", i)))
        n_stage = 3
        for step in range(len(tasks) + n_stage - 1):
            for t in range(step - n_stage + 1, step + 1):
                if 0 <= t < len(tasks):
                    next(tasks[t])
        for i in blocks:
            combine(i, out)

    for k in range(nb // 2):
        pl.when(pl.program_id(2) == k)(functools.partial(pair, k))


def _nsa_t(qn, kv, kc, vc, gates, cbias, cmp_u0, cmp_per, nbias):
    B, T, _ = qn.shape
    G, H, dk = NSA_KV_GROUPS, NSA_HPG, HEAD_DIM
    nb = T // QB
    ns = T // SEL_BLOCK
    cstart = np.arange(LANE) * CMP_STRIDE
    sstart = np.arange(ns) * SEL_BLOCK
    overlap = np.clip(np.minimum(cstart[:, None] + CMP_LEN, sstart[None, :] + SEL_BLOCK)
                      - np.maximum(cstart[:, None], sstart[None, :]), 0, None) / CMP_STRIDE
    overlap[LANE - 1:] = 0.0
    ovt = jnp.asarray(overlap.T, F32)
    expand_t = np.zeros((T, LANE), np.float32)
    expand_t[np.arange(T), np.arange(T) // SEL_BLOCK] = MASK_BIG
    return pl.pallas_call(
        functools.partial(_nsa_t_kernel, cmp_u0=cmp_u0, cmp_per=cmp_per),
        grid=(B, G, nb // 2),
        in_specs=[pl.BlockSpec((1, T, H * dk), lambda b, g, i: (b, 0, g)),
                  pl.BlockSpec((1, T, dk), lambda b, g, i: (b, 0, g)),
                  pl.BlockSpec((1, T, dk), lambda b, g, i: (b, 0, G + g)),
                  pl.BlockSpec((1, T, dk), lambda b, g, i: (b, 0, 2 * G + g)),
                  pl.BlockSpec((1, T, dk), lambda b, g, i: (b, 0, 3 * G + g)),
                  pl.BlockSpec((1, 1, LANE, dk), lambda b, g, i: (b, g, 0, 0)),
                  pl.BlockSpec((1, 1, LANE, dk), lambda b, g, i: (b, g, 0, 0)),
                  pl.BlockSpec((1, T, LANE), lambda b, g, i: (b, 0, 0)),
                  pl.BlockSpec((1, 1, 2 * LANE, H * QB), lambda b, g, i: (g, 0, 0, 0)),
                  pl.BlockSpec((1, 3, QB, H * QB), lambda b, g, i: (g, 0, 0, 0)),
                  pl.BlockSpec((ns, LANE), lambda b, g, i: (0, 0)),
                  pl.BlockSpec((T, LANE), lambda b, g, i: (0, 0))],
        out_specs=pl.BlockSpec((1, T, H * dk), lambda b, g, i: (b, 0, g)),
        out_shape=jax.ShapeDtypeStruct((B, T, NSA_HEADS * dk), BF16),
        scratch_shapes=[pltpu.VMEM((dk, T), BF16), pltpu.VMEM((dk, T), BF16), pltpu.VMEM((dk, LANE), BF16),
                        pltpu.VMEM((T, 2 * dk), BF16)],
        compiler_params=_cparams(3),
        name="nsa",
    )(qn, kv, kv, kv, kv, kc, vc, gates, cbias, nbias, ovt, jnp.asarray(expand_t, BF16))


def _diff_kernel(q_ref, k_ref, v_ref, lq_ref, lk_ref, sg_ref, db_ref, o_ref):
    dk = HEAD_DIM
    T = k_ref.shape[1]
    lqk = lq_ref[...] * lk_ref[...]
    lam = (jnp.exp(jnp.sum(lqk[0:1], axis=-1, keepdims=True))
           - jnp.exp(jnp.sum(lqk[1:2], axis=-1, keepdims=True)) + LAM_INIT)
    def task(i, mm, out):
        lo, hi = i * DQB, (i + 1) * DQB
        cols = slice(mm * dk, (mm + 1) * dk)
        q = q_ref[0, lo:hi, cols]
        bounds, parts = [], []
        if i >= 2:
            bounds.append((0, lo - DQB))
            parts.append(_qk(q, k_ref[0, 0:lo - DQB, cols]))
        if i >= 1:
            bounds.append((lo - DQB, lo))
            parts.append(_qk(q, k_ref[0, lo - DQB:lo, cols]) + db_ref[0, 1])
        bounds.append((lo, hi))
        parts.append(_qk(q, k_ref[0, lo:hi, cols]) + db_ref[0, 0])
        yield
        ps, den = _softmax_parts(parts)
        yield
        o = None
        for (a, b), p in zip(bounds, ps):
            pv = jnp.dot(p.astype(BF16), v_ref[0, a:b, :], preferred_element_type=F32)
            o = pv if o is None else o + pv
        out[i, mm] = o / den
        yield

    out = {}
    tasks = [task(i, mm, out) for i in reversed(range(T // DQB)) for mm in range(2)]
    n_stage = 3
    for step in range(len(tasks) + n_stage - 1):
        for t in range(step - n_stage + 1, step + 1):
            if 0 <= t < len(tasks):
                next(tasks[t])

    for i in range(T // DQB):
        lo, hi = i * DQB, (i + 1) * DQB
        o = out[i, 0] - lam * out[i, 1]
        ms = jnp.mean(o * o, axis=-1, keepdims=True)
        o_ref[0, lo:hi, :] = ((o * lax.rsqrt(ms + EPS) * sg_ref[...]) * (1.0 - LAM_INIT)).astype(o_ref.dtype)


def _diff(dqk, dv, lam_q, lam_k, subln_gain, dbias):
    B, T, _ = dqk.shape
    Hd, dk = DIFF_HEADS, HEAD_DIM
    w = 2 * dk
    return pl.pallas_call(
        _diff_kernel,
        grid=(B, Hd),
        in_specs=[pl.BlockSpec((1, T, w), lambda b, h: (b, 0, h)),
                  pl.BlockSpec((1, T, w), lambda b, h: (b, 0, Hd + h)),
                  pl.BlockSpec((1, T, w), lambda b, h: (b, 0, h)),
                  pl.BlockSpec((2, dk), lambda b, h: (0, 0)),
                  pl.BlockSpec((2, dk), lambda b, h: (0, 0)),
                  pl.BlockSpec((1, w), lambda b, h: (0, 0)),
                  pl.BlockSpec((1, 2, DQB, DQB), lambda b, h: (h, 0, 0, 0))],
        out_specs=pl.BlockSpec((1, T, w), lambda b, h: (b, 0, h)),
        out_shape=jax.ShapeDtypeStruct((B, T, Hd * w), BF16),
        compiler_params=_cparams(2),
        name="diff",
    )(dqk, dqk, dv, lam_q, lam_k, subln_gain.reshape(1, w), dbias)


def _merge_kernel(an_ref, ad_ref, wn_ref, wd_ref, gn_ref, gd_ref, o_ref, wnb_ref, wdb_ref):
    @pl.when(pl.program_id(1) == 0)
    def _():
        wnb_ref[...] = wn_ref[...].astype(BF16)
        wdb_ref[...] = wd_ref[...].astype(BF16)

    for rows in _row_chunks(an_ref.shape[0]):
        yn = jnp.dot(an_ref[rows, :], wnb_ref[...], preferred_element_type=F32)
        yd = jnp.dot(ad_ref[rows, :], wdb_ref[...], preferred_element_type=F32)
        o_ref[rows, :] = (gn_ref[rows, :].astype(F32) * yn + gd_ref[rows, :].astype(F32) * yd).astype(o_ref.dtype)


def _merge(o_nsa, o_diff, w_n, w_d, mg, tm=1024, tn=1024):
    M, K = o_nsa.shape
    N = w_n.shape[1]
    nj = N // tn
    return pl.pallas_call(
        _merge_kernel,
        grid=(nj, M // tm),
        in_specs=[pl.BlockSpec((tm, K), lambda j, i: (i, 0)),
                  pl.BlockSpec((tm, K), lambda j, i: (i, 0)),
                  pl.BlockSpec((K, tn), lambda j, i: (0, j)),
                  pl.BlockSpec((K, tn), lambda j, i: (0, j)),
                  pl.BlockSpec((tm, tn), lambda j, i: (i, j)),
                  pl.BlockSpec((tm, tn), lambda j, i: (i, nj + j))],
        out_specs=pl.BlockSpec((tm, tn), lambda j, i: (i, j)),
        out_shape=jax.ShapeDtypeStruct((M, N), BF16),
        scratch_shapes=[pltpu.VMEM((K, tn), BF16), pltpu.VMEM((K, tn), BF16)],
        compiler_params=_cparams(2),
        name="merge",
    )(o_nsa, o_diff, w_n, w_d, mg, mg)


def _oproj_kernel(a_ref, w_ref, x_ref, g1_ref, gain_ref, sc_ref, sh_ref, x1_ref, h2_ref):
    for rows in _row_chunks(a_ref.shape[0]):
        y = jnp.dot(a_ref[rows, :], w_ref[...], preferred_element_type=F32)
        x1 = x_ref[rows, :] + g1_ref[0] * y
        x1_ref[rows, :] = x1
        h2_ref[rows, :] = _modnorm(x1, gain_ref[...], sc_ref[0], sh_ref[0]).astype(h2_ref.dtype)


def _oproj(merged, w_o, x2d, mod3, gain2, T, tm=512):
    M, D = x2d.shape
    per = T // tm
    return pl.pallas_call(
        _oproj_kernel,
        grid=(M // tm,),
        in_specs=[pl.BlockSpec((tm, D), lambda i: (i, 0)),
                  pl.BlockSpec((D, D), lambda i: (0, 0)),
                  pl.BlockSpec((tm, D), lambda i: (i, 0)),
                  pl.BlockSpec((1, 1, D), lambda i: ((i // per) * 6 + 2, 0, 0)),
                  pl.BlockSpec((1, D), lambda i: (0, 0)),
                  pl.BlockSpec((1, 1, D), lambda i: ((i // per) * 6 + 4, 0, 0)),
                  pl.BlockSpec((1, 1, D), lambda i: ((i // per) * 6 + 3, 0, 0))],
        out_specs=[pl.BlockSpec((tm, D), lambda i: (i, 0)),
                   pl.BlockSpec((tm, D), lambda i: (i, 0))],
        out_shape=[jax.ShapeDtypeStruct((M, D), F32), jax.ShapeDtypeStruct((M, D), BF16)],
        compiler_params=_cparams(1),
        name="oproj",
    )(merged, w_o, x2d, mod3, gain2.reshape(1, D), mod3, mod3)


def _ffn_up_kernel(h_ref, wa_ref, wv_ref, cwa_ref, cwv_ref, cba_ref, cbv_ref, o_ref, wab_ref, wvb_ref,
                   ca_ref, cv_ref, sa_ref, sv_ref, *, per):
    i = pl.program_id(1)

    @pl.when(i == 0)
    def _():
        wab_ref[...] = wa_ref[...].astype(BF16)
        wvb_ref[...] = wv_ref[...].astype(BF16)

    @pl.when(i % per == 0)
    def _():
        ca_ref[...] = jnp.zeros(ca_ref.shape, F32)
        cv_ref[...] = jnp.zeros(cv_ref.shape, F32)

    def conv(u, prev, cw_ref, cb_ref, s_ref):
        s_ref[0:8, :] = prev
        s_ref[8:8 + SUB_ROWS, :] = u
        u1 = s_ref[7:7 + SUB_ROWS, :]
        u2 = s_ref[6:6 + SUB_ROWS, :]
        return cb_ref[...] + cw_ref[0:1, :] * u2 + cw_ref[1:2, :] * u1 + cw_ref[2:3, :] * u

    prev_a, prev_v = ca_ref[...], cv_ref[...]
    for n, rows in enumerate(_row_chunks(h_ref.shape[0])):
        hs = h_ref[rows, :]
        ua = jnp.dot(hs, wab_ref[...], preferred_element_type=F32)
        uv = jnp.dot(hs, wvb_ref[...], preferred_element_type=F32)
        a = conv(ua, prev_a, cwa_ref, cba_ref, sa_ref.at[n % 2])
        val = conv(uv, prev_v, cwv_ref, cbv_ref, sv_ref.at[n % 2])
        o_ref[rows, :] = (a * jax.nn.sigmoid(a) * val).astype(o_ref.dtype)
        prev_a, prev_v = ua[SUB_ROWS - 8:, :], uv[SUB_ROWS - 8:, :]
    ca_ref[...] = prev_a
    cv_ref[...] = prev_v


def _ffn_up(h2, w_up, conv_w, conv_b, T, tm=2048, tn=512):
    M, D = h2.shape
    F = w_up.shape[1] // 2
    nj = F // tn
    cb = conv_b.reshape(1, 2 * F)
    return pl.pallas_call(
        functools.partial(_ffn_up_kernel, per=T // tm),
        grid=(nj, M // tm),
        in_specs=[pl.BlockSpec((tm, D), lambda j, i: (i, 0)),
                  pl.BlockSpec((D, tn), lambda j, i: (0, j)),
                  pl.BlockSpec((D, tn), lambda j, i: (0, nj + j)),
                  pl.BlockSpec((3, tn), lambda j, i: (0, j)),
                  pl.BlockSpec((3, tn), lambda j, i: (0, nj + j)),
                  pl.BlockSpec((1, tn), lambda j, i: (0, j)),
                  pl.BlockSpec((1, tn), lambda j, i: (0, nj + j))],
        out_specs=pl.BlockSpec((tm, tn), lambda j, i: (i, j)),
        out_shape=jax.ShapeDtypeStruct((M, F), BF16),
        scratch_shapes=[pltpu.VMEM((D, tn), BF16), pltpu.VMEM((D, tn), BF16),
                        pltpu.VMEM((8, tn), F32), pltpu.VMEM((8, tn), F32),
                        pltpu.VMEM((2, SUB_ROWS + 8, tn), F32), pltpu.VMEM((2, SUB_ROWS + 8, tn), F32)],
        compiler_params=_cparams(2),
        name="ffn_up",
    )(h2, w_up, w_up, conv_w, conv_w, cb, cb)


def _ffn_down_kernel(a_ref, w_ref, x_ref, g2_ref, o_ref, wb_ref):
    @pl.when(pl.program_id(1) == 0)
    def _():
        wb_ref[...] = w_ref[...].astype(BF16)

    for rows in _row_chunks(a_ref.shape[0]):
        y = jnp.dot(a_ref[rows, :], wb_ref[...], preferred_element_type=F32)
        o_ref[rows, :] = x_ref[rows, :] + g2_ref[0] * y


def _ffn_down(act, w_down, x1, mod3, T, tm=512, tn=512):
    M, F = act.shape
    D = w_down.shape[1]
    per = T // tm
    return pl.pallas_call(
        _ffn_down_kernel,
        grid=(D // tn, M // tm),
        in_specs=[pl.BlockSpec((tm, F), lambda j, i: (i, 0)),
                  pl.BlockSpec((F, tn), lambda j, i: (0, j)),
                  pl.BlockSpec((tm, tn), lambda j, i: (i, j)),
                  pl.BlockSpec((1, 1, tn), lambda j, i: ((i // per) * 6 + 5, 0, j))],
        out_specs=pl.BlockSpec((tm, tn), lambda j, i: (i, j)),
        out_shape=jax.ShapeDtypeStruct((M, D), F32),
        scratch_shapes=[pltpu.VMEM((F, tn), BF16)],
        compiler_params=_cparams(2),
        name="ffn_down",
    )(act, w_down, x1, mod3)


def _layer(x, c, w_ada, b_ada, norm1_gain, norm2_gain, w_in, nsa_q_gain, nsa_k_gain, cmp_pe, cmp_w1, cmp_w2,
           diff_q_gain, diff_k_gain, diff_lambda_q, diff_lambda_k, diff_subln_gain, w_nsa_out, w_diff_out, w_o,
           w_ffn_up, ffn_conv_w, ffn_conv_b, w_ffn_down, rel_bias):
    B, T, D = x.shape
    dk, G = HEAD_DIM, NSA_KV_GROUPS
    M = B * T
    scale = dk ** -0.5

    n_q = NSA_HEADS * dk
    o_kv = n_q
    o_g = o_kv + 3 * 2 * G * dk
    o_dq = o_g + NSA_HEADS * 3
    o_dk = o_dq + DIFF_HEADS * 2 * dk
    o_dv = o_dk + DIFF_HEADS * 2 * dk
    o_mg = o_dv + DIFF_HEADS * 2 * dk
    n_kv = 2 * G * dk
    n_dqk = 2 * DIFF_HEADS * 2 * dk

    mod3 = _ada(c, w_ada, b_ada).reshape(B * 6, 1, D)

    ones = jnp.ones((n_kv // 2,), F32)
    g_q = jnp.tile(nsa_q_gain * (scale * LOG2E), NSA_HEADS).reshape(1, n_q)
    g_kv = jnp.concatenate([jnp.tile(nsa_k_gain[1], G), ones, jnp.tile(nsa_k_gain[2], G), ones]).reshape(1, 2 * n_kv)
    g_dqk = jnp.concatenate([jnp.tile(diff_q_gain * (scale * LOG2E), 2 * DIFF_HEADS),
                             jnp.tile(diff_k_gain, 2 * DIFF_HEADS)]).reshape(1, n_dqk)
    g_one = jnp.ones((1, 2 * D), F32)
    w_in_t = jnp.swapaxes(w_in, 1, 2)
    wide = INPROJ_TN
    yes, no = (True,) * (wide // LANE), (False,) * (wide // LANE)
    qn, h = _norm1_inproj(x.reshape(M, D), norm1_gain, mod3, w_in_t, g_q, T=T, src0=0, ncols=n_q,
                          heads=(True,) * NSA_HEADS, mode="raw", out_dtype=BF16, tm=INPROJ_TM // 2, name="inproj_q")
    proj = functools.partial(_inproj, h, w_in_t, tm=INPROJ_TM)
    cmpkv, gates = _inproj_cmp_gate(h, w_in_t, src_cmp=o_kv, n_cmp=n_kv, src_gate=o_g, tm=INPROJ_TM)
    kv_heads = ((True,) * G + (False,) * G) * 2
    kv = proj(g_kv, src0=o_kv + n_kv, ncols=2 * n_kv, heads=kv_heads, mode="raw", out_dtype=BF16, tn=2 * n_kv,
              name="inproj_kv")
    dqk = proj(g_dqk, src0=o_dq, ncols=n_dqk, heads=yes, mode="raw", out_dtype=BF16, tn=wide, name="inproj_dqk")
    dv = proj(g_one, src0=o_dv, ncols=o_mg - o_dv, heads=no, mode="raw", out_dtype=BF16, tn=wide, name="inproj_dv")
    mgate = proj(g_one, src0=o_mg, ncols=2 * D, heads=no, mode="sigmoid", out_dtype=BF16, tn=wide, name="inproj_mg")

    nbias = _bias_tiles(_window_idx(QB).transpose(0, 2, 1), rel_bias, head0=0, groups=G, hpg=NSA_HPG,
                        name="bias_nsa", rel=True, mult=LOG2E)
    cmp_idx, cmp_u0, cmp_per = _cmp_idx(T)
    cbias = _bias_tiles(cmp_idx.transpose(0, 2, 1), rel_bias, head0=0, groups=G, hpg=NSA_HPG, name="bias_cmp",
                        mult=LOG2E)
    dbias = _bias_tiles(_causal_idx(DQB), rel_bias, head0=NSA_HEADS, groups=DIFF_HEADS, hpg=1, name="bias_diff",
                        rel=True, mult=LOG2E)

    kc, vc = _compress(cmpkv.reshape(B, T, n_kv), cmp_pe, cmp_w1, cmp_w2, nsa_k_gain[0])
    o_nsa = _nsa_t(qn.reshape(B, T, n_q), kv.reshape(B, T, 2 * n_kv), kc, vc, gates.reshape(B, T, LANE), cbias,
                   cmp_u0, cmp_per, nbias)
    o_diff = _diff(dqk.reshape(B, T, n_dqk), dv.reshape(B, T, -1), diff_lambda_q, diff_lambda_k, diff_subln_gain,
                   dbias)

    merged = _merge(o_nsa.reshape(M, -1), o_diff.reshape(M, -1), w_nsa_out, w_diff_out, mgate)
    x1, h2 = _oproj(merged, w_o.astype(BF16), x.reshape(M, D), mod3, norm2_gain, T)
    act = _ffn_up(h2, w_ffn_up, ffn_conv_w, ffn_conv_b, T)
    out = _ffn_down(act, w_ffn_down, x1, mod3, T)
    return out.reshape(B, T, D)


def kernel(x, c, w_ada, b_ada, norm1_gain, norm2_gain, w_in, nsa_q_gain, nsa_k_gain, cmp_pe, cmp_w1, cmp_w2,
           diff_q_gain, diff_k_gain, diff_lambda_q, diff_lambda_k, diff_subln_gain, w_nsa_out, w_diff_out, w_o,
           w_ffn_up, ffn_conv_w, ffn_conv_b, w_ffn_down, rel_bias):
    return _layer(x, c, w_ada[0], b_ada[0], norm1_gain[0], norm2_gain[0], w_in, nsa_q_gain[0], nsa_k_gain[0],
                  cmp_pe[0], cmp_w1[0], cmp_w2[0], diff_q_gain[0], diff_k_gain[0], diff_lambda_q[0],
                  diff_lambda_k[0], diff_subln_gain[0], w_nsa_out[0], w_diff_out[0], w_o[0], w_ffn_up[0],
                  ffn_conv_w[0], ffn_conv_b[0], w_ffn_down[0], rel_bias)
```

```python
import functools
import math

import numpy as np
import jax
import jax.numpy as jnp
from jax import lax
from jax.experimental import pallas as pl
from jax.experimental.pallas import tpu as pltpu

F32 = jnp.float32
BF16 = jnp.bfloat16

HEAD_DIM = 128
NSA_HEADS = 8
NSA_KV_GROUPS = 2
NSA_HPG = NSA_HEADS // NSA_KV_GROUPS
CMP_LEN = 32
CMP_STRIDE = 16
SEL_BLOCK = 64
N_SEL = 16
WINDOW = 512
DIFF_HEADS = 4
NUM_BUCKETS = 32
MAX_DISTANCE = 128
EPS = 1e-6
NEG = -1e30
LAM_INIT = 0.8 - 0.6 * math.exp(-0.3 * 0)
LOG2E = math.log2(math.e)
MASK_BIG = -(2.0 ** 100)

LANE = 128
QB = 128
DQB = 256
SUB_ROWS = 256
INPROJ_TM = 2048
INPROJ_TN = 1024
VMEM_LIMIT = 56 * 1024 * 1024


def _cparams(n_axes):
    return pltpu.CompilerParams(dimension_semantics=("arbitrary",) * n_axes,
                                vmem_limit_bytes=VMEM_LIMIT)


def _t5_bucket_np(dist):
    n = np.maximum(np.asarray(dist, np.int32), 0)
    max_exact = NUM_BUCKETS // 2
    nf = np.maximum(n, max_exact).astype(np.float32)
    large = max_exact + (np.log(nf / np.float32(max_exact)) / np.float32(math.log(MAX_DISTANCE / max_exact))
                         * np.float32(NUM_BUCKETS - max_exact)).astype(np.int32)
    large = np.minimum(large, NUM_BUCKETS - 1)
    return np.where(n < max_exact, n, large).astype(np.int32)


def _ada_kernel(ct_ref, w_ref, b_ref, o_ref):
    ct = ct_ref[...]
    s = ct * jax.nn.sigmoid(ct)
    w = w_ref[...]
    for b in range(ct.shape[1]):
        o_ref[b:b + 1, :] = jnp.sum(w * s[:, b:b + 1], axis=0, keepdims=True) + b_ref[...]


def _ada(c, w_ada, b_ada, tn=1024):
    B, D = c.shape
    N = w_ada.shape[1]
    return pl.pallas_call(
        _ada_kernel,
        grid=(N // tn,),
        in_specs=[pl.BlockSpec((D, B), lambda j: (0, 0)),
                  pl.BlockSpec((D, tn), lambda j: (0, j)),
                  pl.BlockSpec((1, tn), lambda j: (0, j))],
        out_specs=pl.BlockSpec((B, tn), lambda j: (0, j)),
        out_shape=jax.ShapeDtypeStruct((B, N), F32),
        compiler_params=_cparams(1),
        name="ada",
    )(c.T, w_ada, b_ada.reshape(1, N))


def _modnorm(x, gain, sc, sh):
    ms = jnp.mean(x * x, axis=-1, keepdims=True)
    return (x * lax.rsqrt(ms + EPS) * gain) * (1.0 + sc) + sh


def _row_chunks(tm):
    return [slice(r, r + SUB_ROWS) for r in range(0, tm, SUB_ROWS)]


def _inproj_epilogue(acc, g_ref, o_ref, rows, heads, mode):
    for k, is_head in enumerate(heads):
        lanes = slice(k * LANE, (k + 1) * LANE)
        y = acc[:, lanes]
        if is_head:
            ms = jnp.mean(y * y, axis=-1, keepdims=True)
            y = y * lax.rsqrt(ms + EPS) * g_ref[:, lanes]
        elif mode == "sigmoid":
            y = jax.nn.sigmoid(y)
        o_ref[rows, lanes] = y.astype(o_ref.dtype)


def _inproj_kernel(a_ref, wt_ref, g_ref, o_ref, wb_ref, *, heads, mode):
    @pl.when(pl.program_id(1) == 0)
    def _():
        wb_ref[...] = wt_ref[0].astype(BF16)

    def tile(tile_heads):
        for rows in _row_chunks(a_ref.shape[0]):
            acc = _qk(a_ref[rows, :], wb_ref[...])
            _inproj_epilogue(acc, g_ref, o_ref, rows, tile_heads, mode)

    if len(set(heads)) == 1:
        tile(heads[0])
    else:
        for j, tile_heads in enumerate(heads):
            pl.when(pl.program_id(0) == j)(functools.partial(tile, tile_heads))


def _norm1_inproj_kernel(x_ref, n1g_ref, sc_ref, sh_ref, wt_ref, g_ref, o_ref, h_ref, wb_ref, *, heads, mode):
    @pl.when(pl.program_id(0) == 0)
    def _():
        wb_ref[...] = wt_ref[0].astype(BF16)

    for rows in _row_chunks(x_ref.shape[0]):
        h = _modnorm(x_ref[rows, :], n1g_ref[...], sc_ref[0], sh_ref[0]).astype(BF16)
        h_ref[rows, :] = h
        _inproj_epilogue(_qk(h, wb_ref[...]), g_ref, o_ref, rows, heads, mode)


def _norm1_inproj(x2d, n1_gain, mod3, w_in_t, gains, *, T, src0, ncols, heads, mode, out_dtype, tm, name):
    M, K = x2d.shape
    per = T // tm
    assert src0 % 8 == 0 and len(heads) == ncols // LANE
    return pl.pallas_call(
        functools.partial(_norm1_inproj_kernel, heads=heads, mode=mode),
        grid=(M // tm,),
        in_specs=[pl.BlockSpec((tm, K), lambda i: (i, 0)),
                  pl.BlockSpec((1, K), lambda i: (0, 0)),
                  pl.BlockSpec((1, 1, K), lambda i: ((i // per) * 6 + 1, 0, 0)),
                  pl.BlockSpec((1, 1, K), lambda i: ((i // per) * 6 + 0, 0, 0)),
                  pl.BlockSpec((pl.Element(1), pl.Element(ncols), pl.Element(K)), lambda i: (0, src0, 0)),
                  pl.BlockSpec((1, ncols), lambda i: (0, 0))],
        out_specs=[pl.BlockSpec((tm, ncols), lambda i: (i, 0)), pl.BlockSpec((tm, K), lambda i: (i, 0))],
        out_shape=[jax.ShapeDtypeStruct((M, ncols), out_dtype), jax.ShapeDtypeStruct((M, K), BF16)],
        scratch_shapes=[pltpu.VMEM((ncols, K), BF16)],
        compiler_params=_cparams(1),
        name=name,
    )(x2d, n1_gain.reshape(1, K), mod3, mod3, w_in_t, gains)


def _inproj(h2d, w_in_t, gains, *, src0, ncols, heads, mode, out_dtype, tm, tn, name):
    M, K = h2d.shape
    if not isinstance(heads[0], tuple):
        heads = (heads,) * (ncols // tn)
    assert src0 % 8 == 0 and ncols % tn == 0 and len(heads) == ncols // tn
    assert all(len(t) == tn // LANE for t in heads)
    return pl.pallas_call(
        functools.partial(_inproj_kernel, heads=tuple(heads), mode=mode),
        grid=(ncols // tn, M // tm),
        in_specs=[pl.BlockSpec((tm, K), lambda j, i: (i, 0)),
                  pl.BlockSpec((pl.Element(1), pl.Element(tn), pl.Element(K)),
                               lambda j, i: (0, pl.multiple_of(src0 + j * tn, 8), 0)),
                  pl.BlockSpec((1, tn), lambda j, i: (0, j))],
        out_specs=pl.BlockSpec((tm, tn), lambda j, i: (i, j)),
        out_shape=jax.ShapeDtypeStruct((M, ncols), out_dtype),
        scratch_shapes=[pltpu.VMEM((tn, K), BF16)],
        compiler_params=_cparams(2),
        name=name,
    )(h2d, w_in_t, gains)


def _inproj_cmp_gate_kernel(a_ref, wc_ref, wg_ref, oc_ref, og_ref, wcb_ref, wgb_ref):
    @pl.when(pl.program_id(0) == 0)
    def _():
        wcb_ref[...] = wc_ref[0].astype(BF16)
        wgb_ref[...] = wg_ref[0].astype(BF16)

    for rows in _row_chunks(a_ref.shape[0]):
        a = a_ref[rows, :]
        oc_ref[rows, :] = _qk(a, wcb_ref[...])
        og_ref[rows, :] = jax.nn.sigmoid(_qk(a, wgb_ref[...]))


def _inproj_cmp_gate(h2d, w_in_t, *, src_cmp, n_cmp, src_gate, tm):
    M, K = h2d.shape
    assert src_cmp % 8 == 0 and src_gate % 8 == 0

    def w_rows(start, n):
        return pl.BlockSpec((pl.Element(1), pl.Element(n), pl.Element(K)), lambda i: (0, start, 0))

    return pl.pallas_call(
        _inproj_cmp_gate_kernel,
        grid=(M // tm,),
        in_specs=[pl.BlockSpec((tm, K), lambda i: (i, 0)), w_rows(src_cmp, n_cmp), w_rows(src_gate, LANE)],
        out_specs=[pl.BlockSpec((tm, n_cmp), lambda i: (i, 0)), pl.BlockSpec((tm, LANE), lambda i: (i, 0))],
        out_shape=[jax.ShapeDtypeStruct((M, n_cmp), F32), jax.ShapeDtypeStruct((M, LANE), F32)],
        scratch_shapes=[pltpu.VMEM((n_cmp, K), BF16), pltpu.VMEM((LANE, K), BF16)],
        compiler_params=_cparams(1),
        name="inproj_cmp_gate",
    )(h2d, w_in_t, w_in_t)


def _bias_kernel(idx_ref, tab_ref, o_ref, *, head0, hpg, rel, mult):
    n_tiles, _, C = idx_ref.shape
    for n in range(n_tiles):
        idx = idx_ref[n]
        for h in range(hpg):
            head = head0 + pl.program_id(0) * hpg + h
            acc = jnp.zeros(idx.shape, F32)
            for b in range(NUM_BUCKETS):
                acc = jnp.where(idx == b, tab_ref[b, head], acc)
            if rel:
                acc = acc - tab_ref[NUM_BUCKETS - 1, head]
            o_ref[0, n, :, h * C:(h + 1) * C] = jnp.where(idx < 0, NEG, acc * mult)


def _bias_tiles(idx, rel_bias, *, head0, groups, hpg, name, rel=False, mult=1.0):
    N, R, C = idx.shape
    return pl.pallas_call(
        functools.partial(_bias_kernel, head0=head0, hpg=hpg, rel=rel, mult=mult),
        grid=(groups,),
        in_specs=[pl.BlockSpec((N, R, C), lambda g: (0, 0, 0)),
                  pl.BlockSpec(memory_space=pltpu.SMEM)],
        out_specs=pl.BlockSpec((1, N, R, hpg * C), lambda g: (g, 0, 0, 0)),
        out_shape=jax.ShapeDtypeStruct((groups, N, R, hpg * C), F32),
        compiler_params=_cparams(1),
        name=name,
    )(jnp.asarray(idx), rel_bias)


def _causal_idx(R):
    r = np.arange(R)[:, None]
    c = np.arange(R)[None, :]
    return np.stack([np.where(r >= c, _t5_bucket_np(r - c), -1), _t5_bucket_np(R + r - c)]).astype(np.int32)


def _window_idx(R):
    r = np.arange(R)[:, None]
    c = np.arange(R)[None, :]
    edge = np.where(r < c, NUM_BUCKETS - 1, -1)
    return np.concatenate([_causal_idx(R), edge[None]]).astype(np.int32)


def _cmp_idx(T):
    per = QB // CMP_STRIDE
    u0 = (T // QB - 1) * per
    assert u0 + LANE <= 2 * LANE
    r = np.arange(QB)[:, None]
    end = (np.arange(2 * LANE)[None, :] - u0) * CMP_STRIDE + CMP_LEN - 1
    return np.where(end <= r, _t5_bucket_np(r - end), -1).astype(np.int32)[None], u0, per


def _compress_kernel(zk_ref, zv_ref, pe_ref, w1_ref, w2_ref, kg_ref, kc_ref, vc_ref):
    half = CMP_LEN // 2

    def one(z_ref, i):
        p1 = jnp.zeros((LANE, HEAD_DIM), F32)
        p2 = jnp.zeros((LANE, HEAD_DIM), F32)
        for l in range(half):
            z = z_ref[0, pl.ds(l, LANE, stride=CMP_STRIDE), :]
            w_lo = w1_ref[i, l * HEAD_DIM:(l + 1) * HEAD_DIM, :].astype(BF16)
            w_hi = w1_ref[i, (half + l) * HEAD_DIM:(half + l + 1) * HEAD_DIM, :].astype(BF16)
            p1 = p1 + jnp.dot((z + pe_ref[i, l:l + 1, :]).astype(BF16), w_lo, preferred_element_type=F32)
            p2 = p2 + jnp.dot((z + pe_ref[i, half + l:half + l + 1, :]).astype(BF16), w_hi,
                              preferred_element_type=F32)
        pre = p1 + pltpu.roll(p2, LANE - 1, axis=0)
        hid = jax.nn.gelu(pre)
        return jnp.dot(hid.astype(BF16), w2_ref[i].astype(BF16), preferred_element_type=F32)

    kc = one(zk_ref, 0)
    ms = jnp.mean(kc * kc, axis=-1, keepdims=True)
    kc_ref[0, 0] = (kc * lax.rsqrt(ms + EPS) * kg_ref[...]).astype(kc_ref.dtype)
    vc_ref[0, 0] = one(zv_ref, 1).astype(vc_ref.dtype)


def _compress(cmp_kv, pe, w1, w2, k_gain0):
    B, T, _ = cmp_kv.shape
    G, dk = NSA_KV_GROUPS, HEAD_DIM
    assert (T - CMP_LEN) // CMP_STRIDE + 1 == LANE - 1
    out = jax.ShapeDtypeStruct((B, G, LANE, dk), BF16)
    return pl.pallas_call(
        _compress_kernel,
        grid=(B, G),
        in_specs=[pl.BlockSpec((1, T, dk), lambda b, g: (b, 0, g)),
                  pl.BlockSpec((1, T, dk), lambda b, g: (b, 0, G + g)),
                  pl.BlockSpec((2, CMP_LEN, dk), lambda b, g: (0, 0, 0)),
                  pl.BlockSpec((2, CMP_LEN * dk, dk), lambda b, g: (0, 0, 0)),
                  pl.BlockSpec((2, dk, dk), lambda b, g: (0, 0, 0)),
                  pl.BlockSpec((1, dk), lambda b, g: (0, 0))],
        out_specs=[pl.BlockSpec((1, 1, LANE, dk), lambda b, g: (b, g, 0, 0))] * 2,
        out_shape=[out, out],
        compiler_params=_cparams(2),
        name="compress",
    )(cmp_kv, cmp_kv, pe, w1, w2, k_gain0.reshape(1, dk))


def _qk(q, k):
    return lax.dot_general(q, k, (((1,), (1,)), ((), ())), preferred_element_type=F32)


def _lane_fold(x, op):
    acc = x[..., :LANE]
    for t in range(1, x.shape[-1] // LANE):
        acc = op(acc, x[..., t * LANE:(t + 1) * LANE])
    return acc


def _softmax_parts(parts):
    m = jnp.max(functools.reduce(jnp.maximum, [_lane_fold(s, jnp.maximum) for s in parts]), axis=-1, keepdims=True)
    ps = [jnp.exp2(s - m) for s in parts]
    den = jnp.sum(functools.reduce(jnp.add, [_lane_fold(p, jnp.add) for p in ps]), axis=-1, keepdims=True)
    return ps, den


def _nsa_t_kernel(q_ref, ks_ref, vs_ref, kw_ref, vw_ref, kc_ref, vc_ref, gate_ref, cb_ref, nb_ref,
                  ovt_ref, expt_ref, o_ref, vst_ref, vwt_ref, vct_ref, ksa_ref, *, cmp_u0, cmp_per):
    H, W = NSA_HPG, NSA_HPG * QB
    T = ks_ref.shape[1]
    ns = ovt_ref.shape[0]
    nwb = WINDOW // QB
    nb = T // QB
    kc = kc_ref[0, 0]

    @pl.when(pl.program_id(2) == 0)
    def _():
        ksa_ref[:, :HEAD_DIM] = ks_ref[0]
        ksa_ref[:, HEAD_DIM:] = expt_ref[...]
        vst_ref[...] = vs_ref[0].astype(F32).T.astype(BF16)
        vwt_ref[...] = vw_ref[0].astype(F32).T.astype(BF16)
        vct_ref[...] = vc_ref[0, 0].astype(F32).T.astype(BF16)

    def softmax_t(parts):
        m = functools.reduce(jnp.maximum, [jnp.max(s, axis=0, keepdims=True) for s in parts])
        ps = [jnp.exp2(s - m) for s in parts]
        den = functools.reduce(jnp.add, [jnp.sum(p, axis=0, keepdims=True) for p in ps])
        return [p.astype(BF16) for p in ps], den

    def attend_task(vt_ref, spans_fn, out, key):
        spans = spans_fn()
        parts = []
        for _, _, k_slab, q_op, add in spans:
            s = _qk(k_slab, q_op)
            parts.append(s if add is None else s + add)
        yield
        ps, den = softmax_t(parts)
        yield
        o = None
        for (a, b, _, _, _), p in zip(spans, ps):
            pv = jnp.dot(vt_ref[:, a:b], p, preferred_element_type=F32)
            o = pv if o is None else o + pv
        out[key] = o / den
        yield

    def cmp_task(i, qs, out):
        lo = i * QB
        u = cmp_u0 - cmp_per * i
        s = _qk(kc, qs) + cb_ref[0, 0, u:u + LANE, :]
        yield
        e = jnp.exp2(s - jnp.max(s, axis=0, keepdims=True))
        if i == 0:
            key_end = lax.broadcasted_iota(jnp.int32, (LANE, W), 0) * CMP_STRIDE + (CMP_LEN - 1)
            query = lax.broadcasted_iota(jnp.int32, (LANE, W), 1) % QB
            e = jnp.where(key_end <= query, e, 0.0)
            den = jnp.sum(e, axis=0, keepdims=True)
            p = e / jnp.where(den > 0.0, den, 1.0)
        else:
            p = e / jnp.sum(e, axis=0, keepdims=True)
        if i >= 1:
            psum = functools.reduce(jnp.add, [p[:, h * QB:(h + 1) * QB] for h in range(H)])
            imp_t = jnp.dot(ovt_ref[...], psum, precision=lax.Precision.HIGHEST, preferred_element_type=F32)
        yield
        out["cmp", i] = jnp.dot(vct_ref[...], p.astype(BF16), preferred_element_type=F32)
        if i >= 1:
            blk = lax.broadcasted_iota(jnp.int32, (ns, QB), 0)
            cur = (lo + lax.broadcasted_iota(jnp.int32, (ns, QB), 1)) // SEL_BLOCK
            forced = (blk == 0) | (blk == cur) | (blk == cur - 1)
            score = jnp.where(forced, 1e4, jnp.where(blk <= cur, imp_t, -1e4))
            rank = jnp.zeros((ns, QB), F32)
            for b in range(ns):
                other = score[b:b + 1, :]
                rank = rank + jnp.where(blk > b, jnp.where(other >= score, 1.0, 0.0),
                                        jnp.where(other > score, 1.0, 0.0))
            unsel = jnp.where(rank < float(min(N_SEL, ns)), 0.0, 1.0)
            unsel_q = jnp.concatenate([unsel, jnp.zeros((LANE - ns, QB), F32)], axis=0).T.astype(BF16)
            out["qs_aug", i] = jnp.concatenate([qs, jnp.concatenate([unsel_q] * H, axis=0)], axis=1)
        yield

    def slc_spans(i, qs, out):
        lo, hi = i * QB, (i + 1) * QB
        spans = [(lo, hi, ks_ref[0, lo:hi, :], qs, nb_ref[0, 0])]
        if i >= 1:
            qs_aug = out["qs_aug", i]
            spans.insert(0, (lo - QB, lo, ksa_ref[lo - QB:lo, :], qs_aug, nb_ref[0, 1]))
            if i >= 2:
                spans.insert(0, (0, lo - QB, ksa_ref[0:lo - QB, :], qs_aug, None))
        return spans

    def win_spans(i, qs):
        lo, hi = i * QB, (i + 1) * QB

        def span(a, b, add):
            return (a, b, kw_ref[0, a:b, :], qs, add)

        spans = []
        if i >= nwb:
            spans.append(span((i - nwb) * QB, (i - nwb + 1) * QB, nb_ref[0, 2]))
        mid_a, mid_b = max(i - nwb + 1, 0) * QB, (i - 1) * QB
        if mid_b > mid_a:
            spans.append(span(mid_a, mid_b, None))
        if i >= 1:
            spans.append(span(lo - QB, lo, nb_ref[0, 1]))
        spans.append(span(lo, hi, nb_ref[0, 0]))
        return spans

    def combine(i, out):
        lo, hi = i * QB, (i + 1) * QB
        gate_t = gate_ref[0, lo:hi, :].T
        first = pl.program_id(1) == 0

        def grow(br):
            rows = []
            for h in range(H):
                c = 3 * h + br
                rows.append(jnp.where(first, gate_t[c:c + 1, :], gate_t[3 * H + c:3 * H + c + 1, :]))
            return jnp.concatenate(rows, axis=1)

        o_t = grow(0) * out["cmp", i] + grow(1) * out["slc", i] + grow(2) * out["win", i]
        o_ref[0, lo:hi, :] = jnp.concatenate([o_t[:, h * QB:(h + 1) * QB].T for h in range(H)],
                                             axis=1).astype(o_ref.dtype)

    def pair(k):
        out = {}
        tasks = []
        blocks = (nb - 1 - k, k)
        qss = {}
        for i in blocks:
            q = q_ref[0, i * QB:(i + 1) * QB, :]
            qss[i] = jnp.concatenate([q[:, h * HEAD_DIM:(h + 1) * HEAD_DIM] for h in range(H)], axis=0)
            tasks.append(cmp_task(i, qss[i], out))
            tasks.append(attend_task(vwt_ref, functools.partial(win_spans, i, qss[i]), out, ("win", i)))
        for i in blocks:
            tasks.append(attend_task(vst_ref, functools.partial(slc_spans, i, qss[i], out), out, ("slc", i)))
        n_stage = 3
        for step in range(len(tasks) + n_stage - 1):
            for t in range(step - n_stage + 1, step + 1):
                if 0 <= t < len(tasks):
                    next(tasks[t])
        for i in blocks:
            combine(i, out)

    for k in range(nb // 2):
        pl.when(pl.program_id(2) == k)(functools.partial(pair, k))


def _nsa_t(qn, kv, kc, vc, gates, cbias, cmp_u0, cmp_per, nbias):
    B, T, _ = qn.shape
    G, H, dk = NSA_KV_GROUPS, NSA_HPG, HEAD_DIM
    nb = T // QB
    ns = T // SEL_BLOCK
    cstart = np.arange(LANE) * CMP_STRIDE
    sstart = np.arange(ns) * SEL_BLOCK
    overlap = np.clip(np.minimum(cstart[:, None] + CMP_LEN, sstart[None, :] + SEL_BLOCK)
                      - np.maximum(cstart[:, None], sstart[None, :]), 0, None) / CMP_STRIDE
    overlap[LANE - 1:] = 0.0
    ovt = jnp.asarray(overlap.T, F32)
    expand_t = np.zeros((T, LANE), np.float32)
    expand_t[np.arange(T), np.arange(T) // SEL_BLOCK] = MASK_BIG
    return pl.pallas_call(
        functools.partial(_nsa_t_kernel, cmp_u0=cmp_u0, cmp_per=cmp_per),
        grid=(B, G, nb // 2),
        in_specs=[pl.BlockSpec((1, T, H * dk), lambda b, g, i: (b, 0, g)),
                  pl.BlockSpec((1, T, dk), lambda b, g, i: (b, 0, g)),
                  pl.BlockSpec((1, T, dk), lambda b, g, i: (b, 0, G + g)),
                  pl.BlockSpec((1, T, dk), lambda b, g, i: (b, 0, 2 * G + g)),
                  pl.BlockSpec((1, T, dk), lambda b, g, i: (b, 0, 3 * G + g)),
                  pl.BlockSpec((1, 1, LANE, dk), lambda b, g, i: (b, g, 0, 0)),
                  pl.BlockSpec((1, 1, LANE, dk), lambda b, g, i: (b, g, 0, 0)),
                  pl.BlockSpec((1, T, LANE), lambda b, g, i: (b, 0, 0)),
                  pl.BlockSpec((1, 1, 2 * LANE, H * QB), lambda b, g, i: (g, 0, 0, 0)),
                  pl.BlockSpec((1, 3, QB, H * QB), lambda b, g, i: (g, 0, 0, 0)),
                  pl.BlockSpec((ns, LANE), lambda b, g, i: (0, 0)),
                  pl.BlockSpec((T, LANE), lambda b, g, i: (0, 0))],
        out_specs=pl.BlockSpec((1, T, H * dk), lambda b, g, i: (b, 0, g)),
        out_shape=jax.ShapeDtypeStruct((B, T, NSA_HEADS * dk), BF16),
        scratch_shapes=[pltpu.VMEM((dk, T), BF16), pltpu.VMEM((dk, T), BF16), pltpu.VMEM((dk, LANE), BF16),
                        pltpu.VMEM((T, 2 * dk), BF16)],
        compiler_params=_cparams(3),
        name="nsa",
    )(qn, kv, kv, kv, kv, kc, vc, gates, cbias, nbias, ovt, jnp.asarray(expand_t, BF16))


def _diff_kernel(q_ref, k_ref, v_ref, lq_ref, lk_ref, sg_ref, db_ref, o_ref):
    dk = HEAD_DIM
    T = k_ref.shape[1]
    lqk = lq_ref[...] * lk_ref[...]
    lam = (jnp.exp(jnp.sum(lqk[0:1], axis=-1, keepdims=True))
           - jnp.exp(jnp.sum(lqk[1:2], axis=-1, keepdims=True)) + LAM_INIT)
    def task(i, mm, out):
        lo, hi = i * DQB, (i + 1) * DQB
        cols = slice(mm * dk, (mm + 1) * dk)
        q = q_ref[0, lo:hi, cols]
        bounds, parts = [], []
        if i >= 2:
            bounds.append((0, lo - DQB))
            parts.append(_qk(q, k_ref[0, 0:lo - DQB, cols]))
        if i >= 1:
            bounds.append((lo - DQB, lo))
            parts.append(_qk(q, k_ref[0, lo - DQB:lo, cols]) + db_ref[0, 1])
        bounds.append((lo, hi))
        parts.append(_qk(q, k_ref[0, lo:hi, cols]) + db_ref[0, 0])
        yield
        ps, den = _softmax_parts(parts)
        yield
        o = None
        for (a, b), p in zip(bounds, ps):
            pv = jnp.dot(p.astype(BF16), v_ref[0, a:b, :], preferred_element_type=F32)
            o = pv if o is None else o + pv
        out[i, mm] = o / den
        yield

    out = {}
    tasks = [task(i, mm, out) for i in reversed(range(T // DQB)) for mm in range(2)]
    n_stage = 3
    for step in range(len(tasks) + n_stage - 1):
        for t in range(step - n_stage + 1, step + 1):
            if 0 <= t < len(tasks):
                next(tasks[t])

    for i in range(T // DQB):
        lo, hi = i * DQB, (i + 1) * DQB
        o = out[i, 0] - lam * out[i, 1]
        ms = jnp.mean(o * o, axis=-1, keepdims=True)
        o_ref[0, lo:hi, :] = ((o * lax.rsqrt(ms + EPS) * sg_ref[...]) * (1.0 - LAM_INIT)).astype(o_ref.dtype)


def _diff(dqkv, lam_q, lam_k, subln_gain, dbias):
    B, T, _ = dqkv.shape
    Hd, dk = DIFF_HEADS, HEAD_DIM
    w = 2 * dk
    return pl.pallas_call(
        _diff_kernel,
        grid=(B, Hd),
        in_specs=[pl.BlockSpec((1, T, w), lambda b, h: (b, 0, h)),
                  pl.BlockSpec((1, T, w), lambda b, h: (b, 0, Hd + h)),
                  pl.BlockSpec((1, T, w), lambda b, h: (b, 0, 2 * Hd + h)),
                  pl.BlockSpec((2, dk), lambda b, h: (0, 0)),
                  pl.BlockSpec((2, dk), lambda b, h: (0, 0)),
                  pl.BlockSpec((1, w), lambda b, h: (0, 0)),
                  pl.BlockSpec((1, 2, DQB, DQB), lambda b, h: (h, 0, 0, 0))],
        out_specs=pl.BlockSpec((1, T, w), lambda b, h: (b, 0, h)),
        out_shape=jax.ShapeDtypeStruct((B, T, Hd * w), BF16),
        compiler_params=_cparams(2),
        name="diff",
    )(dqkv, dqkv, dqkv, lam_q, lam_k, subln_gain.reshape(1, w), dbias)


def _merge_kernel(an_ref, ad_ref, wn_ref, wd_ref, gn_ref, gd_ref, o_ref, wnb_ref, wdb_ref):
    @pl.when(pl.program_id(1) == 0)
    def _():
        wnb_ref[...] = wn_ref[...].astype(BF16)
        wdb_ref[...] = wd_ref[...].astype(BF16)

    for rows in _row_chunks(an_ref.shape[0]):
        yn = jnp.dot(an_ref[rows, :], wnb_ref[...], preferred_element_type=F32)
        yd = jnp.dot(ad_ref[rows, :], wdb_ref[...], preferred_element_type=F32)
        o_ref[rows, :] = (gn_ref[rows, :].astype(F32) * yn + gd_ref[rows, :].astype(F32) * yd).astype(o_ref.dtype)


def _merge(o_nsa, o_diff, w_n, w_d, mg, tm=1024, tn=1024):
    M, K = o_nsa.shape
    N = w_n.shape[1]
    nj = N // tn
    return pl.pallas_call(
        _merge_kernel,
        grid=(nj, M // tm),
        in_specs=[pl.BlockSpec((tm, K), lambda j, i: (i, 0)),
                  pl.BlockSpec((tm, K), lambda j, i: (i, 0)),
                  pl.BlockSpec((K, tn), lambda j, i: (0, j)),
                  pl.BlockSpec((K, tn), lambda j, i: (0, j)),
                  pl.BlockSpec((tm, tn), lambda j, i: (i, j)),
                  pl.BlockSpec((tm, tn), lambda j, i: (i, nj + j))],
        out_specs=pl.BlockSpec((tm, tn), lambda j, i: (i, j)),
        out_shape=jax.ShapeDtypeStruct((M, N), BF16),
        scratch_shapes=[pltpu.VMEM((K, tn), BF16), pltpu.VMEM((K, tn), BF16)],
        compiler_params=_cparams(2),
        name="merge",
    )(o_nsa, o_diff, w_n, w_d, mg, mg)


def _oproj_kernel(a_ref, w_ref, x_ref, g1_ref, gain_ref, sc_ref, sh_ref, x1_ref, h2_ref):
    for rows in _row_chunks(a_ref.shape[0]):
        y = jnp.dot(a_ref[rows, :], w_ref[...], preferred_element_type=F32)
        x1 = x_ref[rows, :] + g1_ref[0] * y
        x1_ref[rows, :] = x1
        h2_ref[rows, :] = _modnorm(x1, gain_ref[...], sc_ref[0], sh_ref[0]).astype(h2_ref.dtype)


def _oproj(merged, w_o, x2d, mod3, gain2, T, tm=512):
    M, D = x2d.shape
    per = T // tm
    return pl.pallas_call(
        _oproj_kernel,
        grid=(M // tm,),
        in_specs=[pl.BlockSpec((tm, D), lambda i: (i, 0)),
                  pl.BlockSpec((D, D), lambda i: (0, 0)),
                  pl.BlockSpec((tm, D), lambda i: (i, 0)),
                  pl.BlockSpec((1, 1, D), lambda i: ((i // per) * 6 + 2, 0, 0)),
                  pl.BlockSpec((1, D), lambda i: (0, 0)),
                  pl.BlockSpec((1, 1, D), lambda i: ((i // per) * 6 + 4, 0, 0)),
                  pl.BlockSpec((1, 1, D), lambda i: ((i // per) * 6 + 3, 0, 0))],
        out_specs=[pl.BlockSpec((tm, D), lambda i: (i, 0)),
                   pl.BlockSpec((tm, D), lambda i: (i, 0))],
        out_shape=[jax.ShapeDtypeStruct((M, D), F32), jax.ShapeDtypeStruct((M, D), BF16)],
        compiler_params=_cparams(1),
        name="oproj",
    )(merged, w_o, x2d, mod3, gain2.reshape(1, D), mod3, mod3)


def _ffn_up_kernel(h_ref, wa_ref, wv_ref, cwa_ref, cwv_ref, cba_ref, cbv_ref, o_ref, wab_ref, wvb_ref,
                   ca_ref, cv_ref, sa_ref, sv_ref, *, per):
    i = pl.program_id(1)

    @pl.when(i == 0)
    def _():
        wab_ref[...] = wa_ref[...].astype(BF16)
        wvb_ref[...] = wv_ref[...].astype(BF16)

    @pl.when(i % per == 0)
    def _():
        ca_ref[...] = jnp.zeros(ca_ref.shape, F32)
        cv_ref[...] = jnp.zeros(cv_ref.shape, F32)

    def conv(u, prev, cw_ref, cb_ref, s_ref):
        s_ref[0:8, :] = prev
        s_ref[8:8 + SUB_ROWS, :] = u
        u1 = s_ref[7:7 + SUB_ROWS, :]
        u2 = s_ref[6:6 + SUB_ROWS, :]
        return cb_ref[...] + cw_ref[0:1, :] * u2 + cw_ref[1:2, :] * u1 + cw_ref[2:3, :] * u

    prev_a, prev_v = ca_ref[...], cv_ref[...]
    for n, rows in enumerate(_row_chunks(h_ref.shape[0])):
        hs = h_ref[rows, :]
        ua = jnp.dot(hs, wab_ref[...], preferred_element_type=F32)
        uv = jnp.dot(hs, wvb_ref[...], preferred_element_type=F32)
        a = conv(ua, prev_a, cwa_ref, cba_ref, sa_ref.at[n % 2])
        val = conv(uv, prev_v, cwv_ref, cbv_ref, sv_ref.at[n % 2])
        o_ref[rows, :] = (a * jax.nn.sigmoid(a) * val).astype(o_ref.dtype)
        prev_a, prev_v = ua[SUB_ROWS - 8:, :], uv[SUB_ROWS - 8:, :]
    ca_ref[...] = prev_a
    cv_ref[...] = prev_v


def _ffn_up(h2, w_up, conv_w, conv_b, T, tm=2048, tn=512):
    M, D = h2.shape
    F = w_up.shape[1] // 2
    nj = F // tn
    cb = conv_b.reshape(1, 2 * F)
    return pl.pallas_call(
        functools.partial(_ffn_up_kernel, per=T // tm),
        grid=(nj, M // tm),
        in_specs=[pl.BlockSpec((tm, D), lambda j, i: (i, 0)),
                  pl.BlockSpec((D, tn), lambda j, i: (0, j)),
                  pl.BlockSpec((D, tn), lambda j, i: (0, nj + j)),
                  pl.BlockSpec((3, tn), lambda j, i: (0, j)),
                  pl.BlockSpec((3, tn), lambda j, i: (0, nj + j)),
                  pl.BlockSpec((1, tn), lambda j, i: (0, j)),
                  pl.BlockSpec((1, tn), lambda j, i: (0, nj + j))],
        out_specs=pl.BlockSpec((tm, tn), lambda j, i: (i, j)),
        out_shape=jax.ShapeDtypeStruct((M, F), BF16),
        scratch_shapes=[pltpu.VMEM((D, tn), BF16), pltpu.VMEM((D, tn), BF16),
                        pltpu.VMEM((8, tn), F32), pltpu.VMEM((8, tn), F32),
                        pltpu.VMEM((2, SUB_ROWS + 8, tn), F32), pltpu.VMEM((2, SUB_ROWS + 8, tn), F32)],
        compiler_params=_cparams(2),
        name="ffn_up",
    )(h2, w_up, w_up, conv_w, conv_w, cb, cb)


def _ffn_down_kernel(a_ref, w_ref, x_ref, g2_ref, o_ref, wb_ref):
    @pl.when(pl.program_id(1) == 0)
    def _():
        wb_ref[...] = w_ref[...].astype(BF16)

    for rows in _row_chunks(a_ref.shape[0]):
        y = jnp.dot(a_ref[rows, :], wb_ref[...], preferred_element_type=F32)
        o_ref[rows, :] = x_ref[rows, :] + g2_ref[0] * y


def _ffn_down(act, w_down, x1, mod3, T, tm=512, tn=512):
    M, F = act.shape
    D = w_down.shape[1]
    per = T // tm
    return pl.pallas_call(
        _ffn_down_kernel,
        grid=(D // tn, M // tm),
        in_specs=[pl.BlockSpec((tm, F), lambda j, i: (i, 0)),
                  pl.BlockSpec((F, tn), lambda j, i: (0, j)),
                  pl.BlockSpec((tm, tn), lambda j, i: (i, j)),
                  pl.BlockSpec((1, 1, tn), lambda j, i: ((i // per) * 6 + 5, 0, j))],
        out_specs=pl.BlockSpec((tm, tn), lambda j, i: (i, j)),
        out_shape=jax.ShapeDtypeStruct((M, D), F32),
        scratch_shapes=[pltpu.VMEM((F, tn), BF16)],
        compiler_params=_cparams(2),
        name="ffn_down",
    )(act, w_down, x1, mod3)


def _layer(x, c, w_ada, b_ada, norm1_gain, norm2_gain, w_in, nsa_q_gain, nsa_k_gain, cmp_pe, cmp_w1, cmp_w2,
           diff_q_gain, diff_k_gain, diff_lambda_q, diff_lambda_k, diff_subln_gain, w_nsa_out, w_diff_out, w_o,
           w_ffn_up, ffn_conv_w, ffn_conv_b, w_ffn_down, rel_bias):
    B, T, D = x.shape
    dk, G = HEAD_DIM, NSA_KV_GROUPS
    M = B * T
    scale = dk ** -0.5

    n_q = NSA_HEADS * dk
    o_kv = n_q
    o_g = o_kv + 3 * 2 * G * dk
    o_dq = o_g + NSA_HEADS * 3
    o_dk = o_dq + DIFF_HEADS * 2 * dk
    o_dv = o_dk + DIFF_HEADS * 2 * dk
    o_mg = o_dv + DIFF_HEADS * 2 * dk
    n_kv = 2 * G * dk
    n_dqk = 2 * DIFF_HEADS * 2 * dk

    mod3 = _ada(c, w_ada, b_ada).reshape(B * 6, 1, D)

    ones = jnp.ones((n_kv // 2,), F32)
    g_q = jnp.tile(nsa_q_gain * (scale * LOG2E), NSA_HEADS).reshape(1, n_q)
    g_kv = jnp.concatenate([jnp.tile(nsa_k_gain[1], G), ones, jnp.tile(nsa_k_gain[2], G), ones]).reshape(1, 2 * n_kv)
    g_dqk = jnp.concatenate([jnp.tile(diff_q_gain * (scale * LOG2E), 2 * DIFF_HEADS),
                             jnp.tile(diff_k_gain, 2 * DIFF_HEADS)]).reshape(1, n_dqk)
    g_one = jnp.ones((1, 2 * D), F32)
    w_in_t = jnp.swapaxes(w_in, 1, 2)
    wide = INPROJ_TN
    yes, no = (True,) * (wide // LANE), (False,) * (wide // LANE)
    qn, h = _norm1_inproj(x.reshape(M, D), norm1_gain, mod3, w_in_t, g_q, T=T, src0=0, ncols=n_q,
                          heads=(True,) * NSA_HEADS, mode="raw", out_dtype=BF16, tm=INPROJ_TM // 2, name="inproj_q")
    proj = functools.partial(_inproj, h, w_in_t, tm=INPROJ_TM)
    cmpkv, gates = _inproj_cmp_gate(h, w_in_t, src_cmp=o_kv, n_cmp=n_kv, src_gate=o_g, tm=INPROJ_TM)
    kv_heads = ((True,) * G + (False,) * G) * 2
    kv = proj(g_kv, src0=o_kv + n_kv, ncols=2 * n_kv, heads=kv_heads, mode="raw", out_dtype=BF16, tn=2 * n_kv,
              name="inproj_kv")
    n_dv = o_mg - o_dv
    g_dqkv = jnp.concatenate([g_dqk, g_one[:, :n_dv]], axis=1)
    dqkv = proj(g_dqkv, src0=o_dq, ncols=n_dqk + n_dv, heads=(yes,) * (n_dqk // wide) + (no,) * (n_dv // wide),
                mode="raw", out_dtype=BF16, tn=wide, name="inproj_dqkv")
    mgate = proj(g_one, src0=o_mg, ncols=2 * D, heads=no, mode="sigmoid", out_dtype=BF16, tn=wide, name="inproj_mg")

    nbias = _bias_tiles(_window_idx(QB).transpose(0, 2, 1), rel_bias, head0=0, groups=G, hpg=NSA_HPG,
                        name="bias_nsa", rel=True, mult=LOG2E)
    cmp_idx, cmp_u0, cmp_per = _cmp_idx(T)
    cbias = _bias_tiles(cmp_idx.transpose(0, 2, 1), rel_bias, head0=0, groups=G, hpg=NSA_HPG, name="bias_cmp",
                        mult=LOG2E)
    dbias = _bias_tiles(_causal_idx(DQB), rel_bias, head0=NSA_HEADS, groups=DIFF_HEADS, hpg=1, name="bias_diff",
                        rel=True, mult=LOG2E)

    kc, vc = _compress(cmpkv.reshape(B, T, n_kv), cmp_pe, cmp_w1, cmp_w2, nsa_k_gain[0])
    o_nsa = _nsa_t(qn.reshape(B, T, n_q), kv.reshape(B, T, 2 * n_kv), kc, vc, gates.reshape(B, T, LANE), cbias,
                   cmp_u0, cmp_per, nbias)
    o_diff = _diff(dqkv.reshape(B, T, n_dqk + n_dv), diff_lambda_q, diff_lambda_k, diff_subln_gain, dbias)

    merged = _merge(o_nsa.reshape(M, -1), o_diff.reshape(M, -1), w_nsa_out, w_diff_out, mgate)
    x1, h2 = _oproj(merged, w_o.astype(BF16), x.reshape(M, D), mod3, norm2_gain, T)
    act = _ffn_up(h2, w_ffn_up, ffn_conv_w, ffn_conv_b, T)
    out = _ffn_down(act, w_ffn_down, x1, mod3, T)
    return out.reshape(B, T, D)


def kernel(x, c, w_ada, b_ada, norm1_gain, norm2_gain, w_in, nsa_q_gain, nsa_k_gain, cmp_pe, cmp_w1, cmp_w2,
           diff_q_gain, diff_k_gain, diff_lambda_q, diff_lambda_k, diff_subln_gain, w_nsa_out, w_diff_out, w_o,
           w_ffn_up, ffn_conv_w, ffn_conv_b, w_ffn_down, rel_bias):
    return _layer(x, c, w_ada[0], b_ada[0], norm1_gain[0], norm2_gain[0], w_in, nsa_q_gain[0], nsa_k_gain[0],
                  cmp_pe[0], cmp_w1[0], cmp_w2[0], diff_q_gain[0], diff_k_gain[0], diff_lambda_q[0],
                  diff_lambda_k[0], diff_subln_gain[0], w_nsa_out[0], w_diff_out[0], w_o[0], w_ffn_up[0],
                  ffn_conv_w[0], ffn_conv_b[0], w_ffn_down[0], rel_bias)
```

```python
import functools
import math

import numpy as np
import jax
import jax.numpy as jnp
from jax import lax
from jax.experimental import pallas as pl
from jax.experimental.pallas import tpu as pltpu

F32 = jnp.float32
BF16 = jnp.bfloat16

HEAD_DIM = 128
NSA_HEADS = 8
NSA_KV_GROUPS = 2
NSA_HPG = NSA_HEADS // NSA_KV_GROUPS
CMP_LEN = 32
CMP_STRIDE = 16
SEL_BLOCK = 64
N_SEL = 16
WINDOW = 512
DIFF_HEADS = 4
NUM_BUCKETS = 32
MAX_DISTANCE = 128
EPS = 1e-6
NEG = -1e30
LAM_INIT = 0.8 - 0.6 * math.exp(-0.3 * 0)
LOG2E = math.log2(math.e)
MASK_BIG = -(2.0 ** 100)

LANE = 128
QB = 128
DQB = 256
SUB_ROWS = 256
FFN_SUB_ROWS = 128
INPROJ_TM = 2048
INPROJ_TN = 1024
VMEM_LIMIT = 56 * 1024 * 1024


def _cparams(n_axes):
    return pltpu.CompilerParams(dimension_semantics=("arbitrary",) * n_axes,
                                vmem_limit_bytes=VMEM_LIMIT)


def _t5_bucket_np(dist):
    n = np.maximum(np.asarray(dist, np.int32), 0)
    max_exact = NUM_BUCKETS // 2
    nf = np.maximum(n, max_exact).astype(np.float32)
    large = max_exact + (np.log(nf / np.float32(max_exact)) / np.float32(math.log(MAX_DISTANCE / max_exact))
                         * np.float32(NUM_BUCKETS - max_exact)).astype(np.int32)
    large = np.minimum(large, NUM_BUCKETS - 1)
    return np.where(n < max_exact, n, large).astype(np.int32)


def _ada_kernel(ct_ref, w_ref, b_ref, o_ref):
    ct = ct_ref[...]
    s = ct * jax.nn.sigmoid(ct)
    w = w_ref[...]
    for b in range(ct.shape[1]):
        o_ref[b:b + 1, :] = jnp.sum(w * s[:, b:b + 1], axis=0, keepdims=True) + b_ref[...]


def _ada(c, w_ada, b_ada, tn=1024):
    B, D = c.shape
    N = w_ada.shape[1]
    return pl.pallas_call(
        _ada_kernel,
        grid=(N // tn,),
        in_specs=[pl.BlockSpec((D, B), lambda j: (0, 0)),
                  pl.BlockSpec((D, tn), lambda j: (0, j)),
                  pl.BlockSpec((1, tn), lambda j: (0, j))],
        out_specs=pl.BlockSpec((B, tn), lambda j: (0, j)),
        out_shape=jax.ShapeDtypeStruct((B, N), F32),
        compiler_params=_cparams(1),
        name="ada",
    )(c.T, w_ada, b_ada.reshape(1, N))


def _modnorm(x, gain, sc, sh):
    ms = jnp.mean(x * x, axis=-1, keepdims=True)
    return (x * lax.rsqrt(ms + EPS) * gain) * (1.0 + sc) + sh


def _row_chunks(tm, sub=SUB_ROWS):
    return [slice(r, r + sub) for r in range(0, tm, sub)]


def _inproj_epilogue(acc, g_ref, o_ref, rows, heads, mode):
    for k, is_head in enumerate(heads):
        lanes = slice(k * LANE, (k + 1) * LANE)
        y = acc[:, lanes]
        if is_head:
            ms = jnp.mean(y * y, axis=-1, keepdims=True)
            y = y * lax.rsqrt(ms + EPS) * g_ref[:, lanes]
        elif mode == "sigmoid":
            y = jax.nn.sigmoid(y)
        o_ref[rows, lanes] = y.astype(o_ref.dtype)


def _inproj_kernel(a_ref, wt_ref, g_ref, o_ref, wb_ref, *, heads, mode):
    @pl.when(pl.program_id(1) == 0)
    def _():
        wb_ref[...] = wt_ref[0].astype(BF16)

    for rows in _row_chunks(a_ref.shape[0]):
        acc = _qk(a_ref[rows, :], wb_ref[...])
        _inproj_epilogue(acc, g_ref, o_ref, rows, heads, mode)


def _norm1_inproj_kernel(x_ref, n1g_ref, sc_ref, sh_ref, wt_ref, g_ref, o_ref, h_ref, wb_ref, *, heads, mode):
    @pl.when(pl.program_id(0) == 0)
    def _():
        wb_ref[...] = wt_ref[0].astype(BF16)

    for rows in _row_chunks(x_ref.shape[0]):
        h = _modnorm(x_ref[rows, :], n1g_ref[...], sc_ref[0], sh_ref[0]).astype(BF16)
        h_ref[rows, :] = h
        _inproj_epilogue(_qk(h, wb_ref[...]), g_ref, o_ref, rows, heads, mode)


def _norm1_inproj(x2d, n1_gain, mod3, w_in_t, gains, *, T, src0, ncols, heads, mode, out_dtype, tm, name):
    M, K = x2d.shape
    per = T // tm
    assert src0 % 8 == 0 and len(heads) == ncols // LANE
    return pl.pallas_call(
        functools.partial(_norm1_inproj_kernel, heads=heads, mode=mode),
        grid=(M // tm,),
        in_specs=[pl.BlockSpec((tm, K), lambda i: (i, 0)),
                  pl.BlockSpec((1, K), lambda i: (0, 0)),
                  pl.BlockSpec((1, 1, K), lambda i: ((i // per) * 6 + 1, 0, 0)),
                  pl.BlockSpec((1, 1, K), lambda i: ((i // per) * 6 + 0, 0, 0)),
                  pl.BlockSpec((pl.Element(1), pl.Element(ncols), pl.Element(K)), lambda i: (0, src0, 0)),
                  pl.BlockSpec((1, ncols), lambda i: (0, 0))],
        out_specs=[pl.BlockSpec((tm, ncols), lambda i: (i, 0)), pl.BlockSpec((tm, K), lambda i: (i, 0))],
        out_shape=[jax.ShapeDtypeStruct((M, ncols), out_dtype), jax.ShapeDtypeStruct((M, K), BF16)],
        scratch_shapes=[pltpu.VMEM((ncols, K), BF16)],
        compiler_params=_cparams(1),
        name=name,
    )(x2d, n1_gain.reshape(1, K), mod3, mod3, w_in_t, gains)


def _inproj(h2d, w_in_t, gains, *, src0, ncols, heads, mode, out_dtype, tm, tn, name):
    M, K = h2d.shape
    assert src0 % 8 == 0 and ncols % tn == 0 and len(heads) == tn // LANE
    return pl.pallas_call(
        functools.partial(_inproj_kernel, heads=heads, mode=mode),
        grid=(ncols // tn, M // tm),
        in_specs=[pl.BlockSpec((tm, K), lambda j, i: (i, 0)),
                  pl.BlockSpec((pl.Element(1), pl.Element(tn), pl.Element(K)),
                               lambda j, i: (0, pl.multiple_of(src0 + j * tn, 8), 0)),
                  pl.BlockSpec((1, tn), lambda j, i: (0, j))],
        out_specs=pl.BlockSpec((tm, tn), lambda j, i: (i, j)),
        out_shape=jax.ShapeDtypeStruct((M, ncols), out_dtype),
        scratch_shapes=[pltpu.VMEM((tn, K), BF16)],
        compiler_params=_cparams(2),
        name=name,
    )(h2d, w_in_t, gains)


def _inproj_cmp_gate_kernel(a_ref, wc_ref, wg_ref, oc_ref, og_ref, wcb_ref, wgb_ref):
    @pl.when(pl.program_id(0) == 0)
    def _():
        wcb_ref[...] = wc_ref[0].astype(BF16)
        wgb_ref[...] = wg_ref[0].astype(BF16)

    for rows in _row_chunks(a_ref.shape[0]):
        a = a_ref[rows, :]
        oc_ref[rows, :] = _qk(a, wcb_ref[...])
        og_ref[rows, :] = jax.nn.sigmoid(_qk(a, wgb_ref[...]))


def _inproj_cmp_gate(h2d, w_in_t, *, src_cmp, n_cmp, src_gate, tm):
    M, K = h2d.shape
    assert src_cmp % 8 == 0 and src_gate % 8 == 0

    def w_rows(start, n):
        return pl.BlockSpec((pl.Element(1), pl.Element(n), pl.Element(K)), lambda i: (0, start, 0))

    return pl.pallas_call(
        _inproj_cmp_gate_kernel,
        grid=(M // tm,),
        in_specs=[pl.BlockSpec((tm, K), lambda i: (i, 0)), w_rows(src_cmp, n_cmp), w_rows(src_gate, LANE)],
        out_specs=[pl.BlockSpec((tm, n_cmp), lambda i: (i, 0)), pl.BlockSpec((tm, LANE), lambda i: (i, 0))],
        out_shape=[jax.ShapeDtypeStruct((M, n_cmp), F32), jax.ShapeDtypeStruct((M, LANE), F32)],
        scratch_shapes=[pltpu.VMEM((n_cmp, K), BF16), pltpu.VMEM((LANE, K), BF16)],
        compiler_params=_cparams(1),
        name="inproj_cmp_gate",
    )(h2d, w_in_t, w_in_t)


def _bias_kernel(idx_ref, tab_ref, o_ref, *, head0, hpg, rel, mult):
    n_tiles, _, C = idx_ref.shape
    for n in range(n_tiles):
        idx = idx_ref[n]
        for h in range(hpg):
            head = head0 + pl.program_id(0) * hpg + h
            acc = jnp.zeros(idx.shape, F32)
            for b in range(NUM_BUCKETS):
                acc = jnp.where(idx == b, tab_ref[b, head], acc)
            if rel:
                acc = acc - tab_ref[NUM_BUCKETS - 1, head]
            o_ref[0, n, :, h * C:(h + 1) * C] = jnp.where(idx < 0, NEG, acc * mult)


def _bias_tiles(idx, rel_bias, *, head0, groups, hpg, name, rel=False, mult=1.0):
    N, R, C = idx.shape
    return pl.pallas_call(
        functools.partial(_bias_kernel, head0=head0, hpg=hpg, rel=rel, mult=mult),
        grid=(groups,),
        in_specs=[pl.BlockSpec((N, R, C), lambda g: (0, 0, 0)),
                  pl.BlockSpec(memory_space=pltpu.SMEM)],
        out_specs=pl.BlockSpec((1, N, R, hpg * C), lambda g: (g, 0, 0, 0)),
        out_shape=jax.ShapeDtypeStruct((groups, N, R, hpg * C), F32),
        compiler_params=_cparams(1),
        name=name,
    )(jnp.asarray(idx), rel_bias)


def _causal_idx(R):
    r = np.arange(R)[:, None]
    c = np.arange(R)[None, :]
    return np.stack([np.where(r >= c, _t5_bucket_np(r - c), -1), _t5_bucket_np(R + r - c)]).astype(np.int32)


def _window_idx(R):
    r = np.arange(R)[:, None]
    c = np.arange(R)[None, :]
    edge = np.where(r < c, NUM_BUCKETS - 1, -1)
    return np.concatenate([_causal_idx(R), edge[None]]).astype(np.int32)


def _cmp_idx(T):
    per = QB // CMP_STRIDE
    u0 = (T // QB - 1) * per
    assert u0 + LANE <= 2 * LANE
    r = np.arange(QB)[:, None]
    end = (np.arange(2 * LANE)[None, :] - u0) * CMP_STRIDE + CMP_LEN - 1
    return np.where(end <= r, _t5_bucket_np(r - end), -1).astype(np.int32)[None], u0, per


def _compress_kernel(zk_ref, zv_ref, pe_ref, w1_ref, w2_ref, kg_ref, kc_ref, vc_ref):
    half = CMP_LEN // 2

    def one(z_ref, i):
        p1 = jnp.zeros((LANE, HEAD_DIM), F32)
        p2 = jnp.zeros((LANE, HEAD_DIM), F32)
        for l in range(half):
            z = z_ref[0, pl.ds(l, LANE, stride=CMP_STRIDE), :]
            w_lo = w1_ref[i, l * HEAD_DIM:(l + 1) * HEAD_DIM, :].astype(BF16)
            w_hi = w1_ref[i, (half + l) * HEAD_DIM:(half + l + 1) * HEAD_DIM, :].astype(BF16)
            p1 = p1 + jnp.dot((z + pe_ref[i, l:l + 1, :]).astype(BF16), w_lo, preferred_element_type=F32)
            p2 = p2 + jnp.dot((z + pe_ref[i, half + l:half + l + 1, :]).astype(BF16), w_hi,
                              preferred_element_type=F32)
        pre = p1 + pltpu.roll(p2, LANE - 1, axis=0)
        hid = jax.nn.gelu(pre)
        return jnp.dot(hid.astype(BF16), w2_ref[i].astype(BF16), preferred_element_type=F32)

    kc = one(zk_ref, 0)
    ms = jnp.mean(kc * kc, axis=-1, keepdims=True)
    kc_ref[0, 0] = (kc * lax.rsqrt(ms + EPS) * kg_ref[...]).astype(kc_ref.dtype)
    vc_ref[0, 0] = one(zv_ref, 1).astype(vc_ref.dtype)


def _compress(cmp_kv, pe, w1, w2, k_gain0):
    B, T, _ = cmp_kv.shape
    G, dk = NSA_KV_GROUPS, HEAD_DIM
    assert (T - CMP_LEN) // CMP_STRIDE + 1 == LANE - 1
    out = jax.ShapeDtypeStruct((B, G, LANE, dk), BF16)
    return pl.pallas_call(
        _compress_kernel,
        grid=(B, G),
        in_specs=[pl.BlockSpec((1, T, dk), lambda b, g: (b, 0, g)),
                  pl.BlockSpec((1, T, dk), lambda b, g: (b, 0, G + g)),
                  pl.BlockSpec((2, CMP_LEN, dk), lambda b, g: (0, 0, 0)),
                  pl.BlockSpec((2, CMP_LEN * dk, dk), lambda b, g: (0, 0, 0)),
                  pl.BlockSpec((2, dk, dk), lambda b, g: (0, 0, 0)),
                  pl.BlockSpec((1, dk), lambda b, g: (0, 0))],
        out_specs=[pl.BlockSpec((1, 1, LANE, dk), lambda b, g: (b, g, 0, 0))] * 2,
        out_shape=[out, out],
        compiler_params=_cparams(2),
        name="compress",
    )(cmp_kv, cmp_kv, pe, w1, w2, k_gain0.reshape(1, dk))


def _qk(q, k):
    return lax.dot_general(q, k, (((1,), (1,)), ((), ())), preferred_element_type=F32)


def _lane_fold(x, op):
    acc = x[..., :LANE]
    for t in range(1, x.shape[-1] // LANE):
        acc = op(acc, x[..., t * LANE:(t + 1) * LANE])
    return acc


def _softmax_parts(parts):
    m = jnp.max(functools.reduce(jnp.maximum, [_lane_fold(s, jnp.maximum) for s in parts]), axis=-1, keepdims=True)
    ps = [jnp.exp2(s - m) for s in parts]
    den = jnp.sum(functools.reduce(jnp.add, [_lane_fold(p, jnp.add) for p in ps]), axis=-1, keepdims=True)
    return ps, den


def _nsa_t_kernel(q_ref, ks_ref, vs_ref, kw_ref, vw_ref, kc_ref, vc_ref, gate_ref, cb_ref, nb_ref,
                  ovt_ref, expt_ref, o_ref, vst_ref, vwt_ref, vct_ref, ksa_ref, *, cmp_u0, cmp_per):
    H, W = NSA_HPG, NSA_HPG * QB
    T = ks_ref.shape[1]
    ns = ovt_ref.shape[0]
    nwb = WINDOW // QB
    nb = T // QB
    kc = kc_ref[0, 0]

    @pl.when(pl.program_id(2) == 0)
    def _():
        ksa_ref[:, :HEAD_DIM] = ks_ref[0]
        ksa_ref[:, HEAD_DIM:] = expt_ref[...]
        vst_ref[...] = vs_ref[0].astype(F32).T.astype(BF16)
        vwt_ref[...] = vw_ref[0].astype(F32).T.astype(BF16)
        vct_ref[...] = vc_ref[0, 0].astype(F32).T.astype(BF16)

    def softmax_t(parts):
        m = functools.reduce(jnp.maximum, [jnp.max(s, axis=0, keepdims=True) for s in parts])
        ps = [jnp.exp2(s - m) for s in parts]
        den = functools.reduce(jnp.add, [jnp.sum(p, axis=0, keepdims=True) for p in ps])
        return [p.astype(BF16) for p in ps], den

    def attend_task(vt_ref, spans_fn, out, key):
        spans = spans_fn()
        parts = []
        for _, _, k_slab, q_op, add in spans:
            s = _qk(k_slab, q_op)
            parts.append(s if add is None else s + add)
        yield
        ps, den = softmax_t(parts)
        yield
        o = None
        for (a, b, _, _, _), p in zip(spans, ps):
            pv = jnp.dot(vt_ref[:, a:b], p, preferred_element_type=F32)
            o = pv if o is None else o + pv
        out[key] = o / den
        yield

    def cmp_task(i, qs, out):
        lo = i * QB
        u = cmp_u0 - cmp_per * i
        s = _qk(kc, qs) + cb_ref[0, 0, u:u + LANE, :]
        yield
        e = jnp.exp2(s - jnp.max(s, axis=0, keepdims=True))
        if i == 0:
            key_end = lax.broadcasted_iota(jnp.int32, (LANE, W), 0) * CMP_STRIDE + (CMP_LEN - 1)
            query = lax.broadcasted_iota(jnp.int32, (LANE, W), 1) % QB
            e = jnp.where(key_end <= query, e, 0.0)
            den = jnp.sum(e, axis=0, keepdims=True)
            p = e / jnp.where(den > 0.0, den, 1.0)
        else:
            p = e / jnp.sum(e, axis=0, keepdims=True)
        if i >= 1:
            psum = functools.reduce(jnp.add, [p[:, h * QB:(h + 1) * QB] for h in range(H)])
            imp_t = jnp.dot(ovt_ref[...], psum, precision=lax.Precision.HIGHEST, preferred_element_type=F32)
        yield
        out["cmp", i] = jnp.dot(vct_ref[...], p.astype(BF16), preferred_element_type=F32)
        if i >= 1:
            blk = lax.broadcasted_iota(jnp.int32, (ns, QB), 0)
            cur = (lo + lax.broadcasted_iota(jnp.int32, (ns, QB), 1)) // SEL_BLOCK
            forced = (blk == 0) | (blk == cur) | (blk == cur - 1)
            score = jnp.where(forced, 1e4, jnp.where(blk <= cur, imp_t, -1e4))
            rank = jnp.zeros((ns, QB), F32)
            for b in range(ns):
                other = score[b:b + 1, :]
                rank = rank + jnp.where(blk > b, jnp.where(other >= score, 1.0, 0.0),
                                        jnp.where(other > score, 1.0, 0.0))
            unsel = jnp.where(rank < float(min(N_SEL, ns)), 0.0, 1.0)
            unsel_q = jnp.concatenate([unsel, jnp.zeros((LANE - ns, QB), F32)], axis=0).T.astype(BF16)
            out["qs_aug", i] = jnp.concatenate([qs, jnp.concatenate([unsel_q] * H, axis=0)], axis=1)
        yield

    def slc_spans(i, qs, out):
        lo, hi = i * QB, (i + 1) * QB
        spans = [(lo, hi, ks_ref[0, lo:hi, :], qs, nb_ref[0, 0])]
        if i >= 1:
            qs_aug = out["qs_aug", i]
            spans.insert(0, (lo - QB, lo, ksa_ref[lo - QB:lo, :], qs_aug, nb_ref[0, 1]))
            if i >= 2:
                spans.insert(0, (0, lo - QB, ksa_ref[0:lo - QB, :], qs_aug, None))
        return spans

    def win_spans(i, qs):
        lo, hi = i * QB, (i + 1) * QB

        def span(a, b, add):
            return (a, b, kw_ref[0, a:b, :], qs, add)

        spans = []
        if i >= nwb:
            spans.append(span((i - nwb) * QB, (i - nwb + 1) * QB, nb_ref[0, 2]))
        mid_a, mid_b = max(i - nwb + 1, 0) * QB, (i - 1) * QB
        if mid_b > mid_a:
            spans.append(span(mid_a, mid_b, None))
        if i >= 1:
            spans.append(span(lo - QB, lo, nb_ref[0, 1]))
        spans.append(span(lo, hi, nb_ref[0, 0]))
        return spans

    def combine(i, out):
        lo, hi = i * QB, (i + 1) * QB
        gate_t = gate_ref[0, lo:hi, :].T
        first = pl.program_id(1) == 0

        def grow(br):
            rows = []
            for h in range(H):
                c = 3 * h + br
                rows.append(jnp.where(first, gate_t[c:c + 1, :], gate_t[3 * H + c:3 * H + c + 1, :]))
            return jnp.concatenate(rows, axis=1)

        o_t = grow(0) * out["cmp", i] + grow(1) * out["slc", i] + grow(2) * out["win", i]
        o_ref[0, lo:hi, :] = jnp.concatenate([o_t[:, h * QB:(h + 1) * QB].T for h in range(H)],
                                             axis=1).astype(o_ref.dtype)

    def pair(k):
        out = {}
        tasks = []
        blocks = (nb - 1 - k, k)
        qss = {}
        for i in blocks:
            q = q_ref[0, i * QB:(i + 1) * QB, :]
            qss[i] = jnp.concatenate([q[:, h * HEAD_DIM:(h + 1) * HEAD_DIM] for h in range(H)], axis=0)
            tasks.append(cmp_task(i, qss[i], out))
            tasks.append(attend_task(vwt_ref, functools.partial(win_spans, i, qss[i]), out, ("win", i)))
        for i in blocks:
            tasks.append(attend_task(vst_ref, functools.partial(slc_spans, i, qss[i], out), out, ("slc", i)))
        n_stage = 3
        for step in range(len(tasks) + n_stage - 1):
            for t in range(step - n_stage + 1, step + 1):
                if 0 <= t < len(tasks):
                    next(tasks[t])
        for i in blocks:
            combine(i, out)

    for k in range(nb // 2):
        pl.when(pl.program_id(2) == k)(functools.partial(pair, k))


def _nsa_t(qn, kv, kc, vc, gates, cbias, cmp_u0, cmp_per, nbias):
    B, T, _ = qn.shape
    G, H, dk = NSA_KV_GROUPS, NSA_HPG, HEAD_DIM
    nb = T // QB
    ns = T // SEL_BLOCK
    cstart = np.arange(LANE) * CMP_STRIDE
    sstart = np.arange(ns) * SEL_BLOCK
    overlap = np.clip(np.minimum(cstart[:, None] + CMP_LEN, sstart[None, :] + SEL_BLOCK)
                      - np.maximum(cstart[:, None], sstart[None, :]), 0, None) / CMP_STRIDE
    overlap[LANE - 1:] = 0.0
    ovt = jnp.asarray(overlap.T, F32)
    expand_t = np.zeros((T, LANE), np.float32)
    expand_t[np.arange(T), np.arange(T) // SEL_BLOCK] = MASK_BIG
    return pl.pallas_call(
        functools.partial(_nsa_t_kernel, cmp_u0=cmp_u0, cmp_per=cmp_per),
        grid=(B, G, nb // 2),
        in_specs=[pl.BlockSpec((1, T, H * dk), lambda b, g, i: (b, 0, g)),
                  pl.BlockSpec((1, T, dk), lambda b, g, i: (b, 0, g)),
                  pl.BlockSpec((1, T, dk), lambda b, g, i: (b, 0, G + g)),
                  pl.BlockSpec((1, T, dk), lambda b, g, i: (b, 0, 2 * G + g)),
                  pl.BlockSpec((1, T, dk), lambda b, g, i: (b, 0, 3 * G + g)),
                  pl.BlockSpec((1, 1, LANE, dk), lambda b, g, i: (b, g, 0, 0)),
                  pl.BlockSpec((1, 1, LANE, dk), lambda b, g, i: (b, g, 0, 0)),
                  pl.BlockSpec((1, T, LANE), lambda b, g, i: (b, 0, 0)),
                  pl.BlockSpec((1, 1, 2 * LANE, H * QB), lambda b, g, i: (g, 0, 0, 0)),
                  pl.BlockSpec((1, 3, QB, H * QB), lambda b, g, i: (g, 0, 0, 0)),
                  pl.BlockSpec((ns, LANE), lambda b, g, i: (0, 0)),
                  pl.BlockSpec((T, LANE), lambda b, g, i: (0, 0))],
        out_specs=pl.BlockSpec((1, T, H * dk), lambda b, g, i: (b, 0, g)),
        out_shape=jax.ShapeDtypeStruct((B, T, NSA_HEADS * dk), BF16),
        scratch_shapes=[pltpu.VMEM((dk, T), BF16), pltpu.VMEM((dk, T), BF16), pltpu.VMEM((dk, LANE), BF16),
                        pltpu.VMEM((T, 2 * dk), BF16)],
        compiler_params=_cparams(3),
        name="nsa",
    )(qn, kv, kv, kv, kv, kc, vc, gates, cbias, nbias, ovt, jnp.asarray(expand_t, BF16))


def _diff_kernel(q_ref, k_ref, v_ref, lq_ref, lk_ref, sg_ref, db_ref, o_ref):
    dk = HEAD_DIM
    T = k_ref.shape[1]
    lqk = lq_ref[...] * lk_ref[...]
    lam = (jnp.exp(jnp.sum(lqk[0:1], axis=-1, keepdims=True))
           - jnp.exp(jnp.sum(lqk[1:2], axis=-1, keepdims=True)) + LAM_INIT)
    def task(i, mm, out):
        lo, hi = i * DQB, (i + 1) * DQB
        cols = slice(mm * dk, (mm + 1) * dk)
        q = q_ref[0, lo:hi, cols]
        bounds, parts = [], []
        if i >= 2:
            bounds.append((0, lo - DQB))
            parts.append(_qk(q, k_ref[0, 0:lo - DQB, cols]))
        if i >= 1:
            bounds.append((lo - DQB, lo))
            parts.append(_qk(q, k_ref[0, lo - DQB:lo, cols]) + db_ref[0, 1])
        bounds.append((lo, hi))
        parts.append(_qk(q, k_ref[0, lo:hi, cols]) + db_ref[0, 0])
        yield
        ps, den = _softmax_parts(parts)
        yield
        o = None
        for (a, b), p in zip(bounds, ps):
            pv = jnp.dot(p.astype(BF16), v_ref[0, a:b, :], preferred_element_type=F32)
            o = pv if o is None else o + pv
        out[i, mm] = o / den
        yield

    out = {}
    tasks = [task(i, mm, out) for i in reversed(range(T // DQB)) for mm in range(2)]
    n_stage = 3
    for step in range(len(tasks) + n_stage - 1):
        for t in range(step - n_stage + 1, step + 1):
            if 0 <= t < len(tasks):
                next(tasks[t])

    for i in range(T // DQB):
        lo, hi = i * DQB, (i + 1) * DQB
        o = out[i, 0] - lam * out[i, 1]
        ms = jnp.mean(o * o, axis=-1, keepdims=True)
        o_ref[0, lo:hi, :] = ((o * lax.rsqrt(ms + EPS) * sg_ref[...]) * (1.0 - LAM_INIT)).astype(o_ref.dtype)


def _diff(dqk, dv, lam_q, lam_k, subln_gain, dbias):
    B, T, _ = dqk.shape
    Hd, dk = DIFF_HEADS, HEAD_DIM
    w = 2 * dk
    return pl.pallas_call(
        _diff_kernel,
        grid=(B, Hd),
        in_specs=[pl.BlockSpec((1, T, w), lambda b, h: (b, 0, h)),
                  pl.BlockSpec((1, T, w), lambda b, h: (b, 0, Hd + h)),
                  pl.BlockSpec((1, T, w), lambda b, h: (b, 0, h)),
                  pl.BlockSpec((2, dk), lambda b, h: (0, 0)),
                  pl.BlockSpec((2, dk), lambda b, h: (0, 0)),
                  pl.BlockSpec((1, w), lambda b, h: (0, 0)),
                  pl.BlockSpec((1, 2, DQB, DQB), lambda b, h: (h, 0, 0, 0))],
        out_specs=pl.BlockSpec((1, T, w), lambda b, h: (b, 0, h)),
        out_shape=jax.ShapeDtypeStruct((B, T, Hd * w), BF16),
        compiler_params=_cparams(2),
        name="diff",
    )(dqk, dqk, dv, lam_q, lam_k, subln_gain.reshape(1, w), dbias)


def _merge_kernel(an_ref, ad_ref, wn_ref, wd_ref, gn_ref, gd_ref, o_ref, wnb_ref, wdb_ref):
    @pl.when(pl.program_id(1) == 0)
    def _():
        wnb_ref[...] = wn_ref[...].astype(BF16)
        wdb_ref[...] = wd_ref[...].astype(BF16)

    for rows in _row_chunks(an_ref.shape[0]):
        yn = jnp.dot(an_ref[rows, :], wnb_ref[...], preferred_element_type=F32)
        yd = jnp.dot(ad_ref[rows, :], wdb_ref[...], preferred_element_type=F32)
        o_ref[rows, :] = (gn_ref[rows, :].astype(F32) * yn + gd_ref[rows, :].astype(F32) * yd).astype(o_ref.dtype)


def _merge(o_nsa, o_diff, w_n, w_d, mg, tm=1024, tn=1024):
    M, K = o_nsa.shape
    N = w_n.shape[1]
    nj = N // tn
    return pl.pallas_call(
        _merge_kernel,
        grid=(nj, M // tm),
        in_specs=[pl.BlockSpec((tm, K), lambda j, i: (i, 0)),
                  pl.BlockSpec((tm, K), lambda j, i: (i, 0)),
                  pl.BlockSpec((K, tn), lambda j, i: (0, j)),
                  pl.BlockSpec((K, tn), lambda j, i: (0, j)),
                  pl.BlockSpec((tm, tn), lambda j, i: (i, j)),
                  pl.BlockSpec((tm, tn), lambda j, i: (i, nj + j))],
        out_specs=pl.BlockSpec((tm, tn), lambda j, i: (i, j)),
        out_shape=jax.ShapeDtypeStruct((M, N), BF16),
        scratch_shapes=[pltpu.VMEM((K, tn), BF16), pltpu.VMEM((K, tn), BF16)],
        compiler_params=_cparams(2),
        name="merge",
    )(o_nsa, o_diff, w_n, w_d, mg, mg)


def _oproj_kernel(a_ref, w_ref, x_ref, g1_ref, gain_ref, sc_ref, sh_ref, x1_ref, h2_ref):
    for rows in _row_chunks(a_ref.shape[0]):
        y = jnp.dot(a_ref[rows, :], w_ref[...], preferred_element_type=F32)
        x1 = x_ref[rows, :] + g1_ref[0] * y
        x1_ref[rows, :] = x1
        h2_ref[rows, :] = _modnorm(x1, gain_ref[...], sc_ref[0], sh_ref[0]).astype(h2_ref.dtype)


def _oproj(merged, w_o, x2d, mod3, gain2, T, tm=512):
    M, D = x2d.shape
    per = T // tm
    return pl.pallas_call(
        _oproj_kernel,
        grid=(M // tm,),
        in_specs=[pl.BlockSpec((tm, D), lambda i: (i, 0)),
                  pl.BlockSpec((D, D), lambda i: (0, 0)),
                  pl.BlockSpec((tm, D), lambda i: (i, 0)),
                  pl.BlockSpec((1, 1, D), lambda i: ((i // per) * 6 + 2, 0, 0)),
                  pl.BlockSpec((1, D), lambda i: (0, 0)),
                  pl.BlockSpec((1, 1, D), lambda i: ((i // per) * 6 + 4, 0, 0)),
                  pl.BlockSpec((1, 1, D), lambda i: ((i // per) * 6 + 3, 0, 0))],
        out_specs=[pl.BlockSpec((tm, D), lambda i: (i, 0)),
                   pl.BlockSpec((tm, D), lambda i: (i, 0))],
        out_shape=[jax.ShapeDtypeStruct((M, D), F32), jax.ShapeDtypeStruct((M, D), BF16)],
        compiler_params=_cparams(1),
        name="oproj",
    )(merged, w_o, x2d, mod3, gain2.reshape(1, D), mod3, mod3)


def _ffn_up_kernel(h_ref, wa_ref, wv_ref, cwa_ref, cwv_ref, cba_ref, cbv_ref, o_ref, wab_ref, wvb_ref,
                   ca_ref, cv_ref, sa_ref, sv_ref, *, per):
    i = pl.program_id(1)

    @pl.when(i == 0)
    def _():
        wab_ref[...] = wa_ref[...].astype(BF16)
        wvb_ref[...] = wv_ref[...].astype(BF16)

    @pl.when(i % per == 0)
    def _():
        ca_ref[...] = jnp.zeros(ca_ref.shape, F32)
        cv_ref[...] = jnp.zeros(cv_ref.shape, F32)

    def conv(u, prev, cw_ref, cb_ref, s_ref):
        s_ref[0:8, :] = prev
        s_ref[8:8 + FFN_SUB_ROWS, :] = u
        u1 = s_ref[7:7 + FFN_SUB_ROWS, :]
        u2 = s_ref[6:6 + FFN_SUB_ROWS, :]
        return cb_ref[...] + cw_ref[0:1, :] * u2 + cw_ref[1:2, :] * u1 + cw_ref[2:3, :] * u

    prev_a, prev_v = ca_ref[...], cv_ref[...]
    for n, rows in enumerate(_row_chunks(h_ref.shape[0], FFN_SUB_ROWS)):
        hs = h_ref[rows, :]
        ua = jnp.dot(hs, wab_ref[...], preferred_element_type=F32)
        uv = jnp.dot(hs, wvb_ref[...], preferred_element_type=F32)
        a = conv(ua, prev_a, cwa_ref, cba_ref, sa_ref.at[n % 2])
        val = conv(uv, prev_v, cwv_ref, cbv_ref, sv_ref.at[n % 2])
        o_ref[rows, :] = (a * jax.nn.sigmoid(a) * val).astype(o_ref.dtype)
        prev_a, prev_v = ua[FFN_SUB_ROWS - 8:, :], uv[FFN_SUB_ROWS - 8:, :]
    ca_ref[...] = prev_a
    cv_ref[...] = prev_v


def _ffn_up(h2, w_up, conv_w, conv_b, T, tm=2048, tn=512):
    M, D = h2.shape
    F = w_up.shape[1] // 2
    nj = F // tn
    cb = conv_b.reshape(1, 2 * F)
    return pl.pallas_call(
        functools.partial(_ffn_up_kernel, per=T // tm),
        grid=(nj, M // tm),
        in_specs=[pl.BlockSpec((tm, D), lambda j, i: (i, 0)),
                  pl.BlockSpec((D, tn), lambda j, i: (0, j)),
                  pl.BlockSpec((D, tn), lambda j, i: (0, nj + j)),
                  pl.BlockSpec((3, tn), lambda j, i: (0, j)),
                  pl.BlockSpec((3, tn), lambda j, i: (0, nj + j)),
                  pl.BlockSpec((1, tn), lambda j, i: (0, j)),
                  pl.BlockSpec((1, tn), lambda j, i: (0, nj + j))],
        out_specs=pl.BlockSpec((tm, tn), lambda j, i: (i, j)),
        out_shape=jax.ShapeDtypeStruct((M, F), BF16),
        scratch_shapes=[pltpu.VMEM((D, tn), BF16), pltpu.VMEM((D, tn), BF16),
                        pltpu.VMEM((8, tn), F32), pltpu.VMEM((8, tn), F32),
                        pltpu.VMEM((2, FFN_SUB_ROWS + 8, tn), F32), pltpu.VMEM((2, FFN_SUB_ROWS + 8, tn), F32)],
        compiler_params=_cparams(2),
        name="ffn_up",
    )(h2, w_up, w_up, conv_w, conv_w, cb, cb)


def _ffn_down_kernel(a_ref, w_ref, x_ref, g2_ref, o_ref, wb_ref):
    @pl.when(pl.program_id(1) == 0)
    def _():
        wb_ref[...] = w_ref[...].astype(BF16)

    for rows in _row_chunks(a_ref.shape[0]):
        y = jnp.dot(a_ref[rows, :], wb_ref[...], preferred_element_type=F32)
        o_ref[rows, :] = x_ref[rows, :] + g2_ref[0] * y


def _ffn_down(act, w_down, x1, mod3, T, tm=512, tn=512):
    M, F = act.shape
    D = w_down.shape[1]
    per = T // tm
    return pl.pallas_call(
        _ffn_down_kernel,
        grid=(D // tn, M // tm),
        in_specs=[pl.BlockSpec((tm, F), lambda j, i: (i, 0)),
                  pl.BlockSpec((F, tn), lambda j, i: (0, j)),
                  pl.BlockSpec((tm, tn), lambda j, i: (i, j)),
                  pl.BlockSpec((1, 1, tn), lambda j, i: ((i // per) * 6 + 5, 0, j))],
        out_specs=pl.BlockSpec((tm, tn), lambda j, i: (i, j)),
        out_shape=jax.ShapeDtypeStruct((M, D), F32),
        scratch_shapes=[pltpu.VMEM((F, tn), BF16)],
        compiler_params=_cparams(2),
        name="ffn_down",
    )(act, w_down, x1, mod3)


def _layer(x, c, w_ada, b_ada, norm1_gain, norm2_gain, w_in, nsa_q_gain, nsa_k_gain, cmp_pe, cmp_w1, cmp_w2,
           diff_q_gain, diff_k_gain, diff_lambda_q, diff_lambda_k, diff_subln_gain, w_nsa_out, w_diff_out, w_o,
           w_ffn_up, ffn_conv_w, ffn_conv_b, w_ffn_down, rel_bias):
    B, T, D = x.shape
    dk, G = HEAD_DIM, NSA_KV_GROUPS
    M = B * T
    scale = dk ** -0.5

    n_q = NSA_HEADS * dk
    o_kv = n_q
    o_g = o_kv + 3 * 2 * G * dk
    o_dq = o_g + NSA_HEADS * 3
    o_dk = o_dq + DIFF_HEADS * 2 * dk
    o_dv = o_dk + DIFF_HEADS * 2 * dk
    o_mg = o_dv + DIFF_HEADS * 2 * dk
    n_kv = 2 * G * dk
    n_dqk = 2 * DIFF_HEADS * 2 * dk

    mod3 = _ada(c, w_ada, b_ada).reshape(B * 6, 1, D)

    ones = jnp.ones((n_kv // 2,), F32)
    g_q = jnp.tile(nsa_q_gain * (scale * LOG2E), NSA_HEADS).reshape(1, n_q)
    g_kv = jnp.concatenate([jnp.tile(nsa_k_gain[1], G), ones, jnp.tile(nsa_k_gain[2], G), ones]).reshape(1, 2 * n_kv)
    g_dqk = jnp.concatenate([jnp.tile(diff_q_gain * (scale * LOG2E), 2 * DIFF_HEADS),
                             jnp.tile(diff_k_gain, 2 * DIFF_HEADS)]).reshape(1, n_dqk)
    g_one = jnp.ones((1, 2 * D), F32)
    w_in_t = jnp.swapaxes(w_in, 1, 2)
    wide = INPROJ_TN
    yes, no = (True,) * (wide // LANE), (False,) * (wide // LANE)
    qn, h = _norm1_inproj(x.reshape(M, D), norm1_gain, mod3, w_in_t, g_q, T=T, src0=0, ncols=n_q,
                          heads=(True,) * NSA_HEADS, mode="raw", out_dtype=BF16, tm=INPROJ_TM // 2, name="inproj_q")
    proj = functools.partial(_inproj, h, w_in_t, tm=INPROJ_TM)
    cmpkv, gates = _inproj_cmp_gate(h, w_in_t, src_cmp=o_kv, n_cmp=n_kv, src_gate=o_g, tm=INPROJ_TM)
    kv_heads = ((True,) * G + (False,) * G) * 2
    kv = proj(g_kv, src0=o_kv + n_kv, ncols=2 * n_kv, heads=kv_heads, mode="raw", out_dtype=BF16, tn=2 * n_kv,
              name="inproj_kv")
    dqk = proj(g_dqk, src0=o_dq, ncols=n_dqk, heads=yes, mode="raw", out_dtype=BF16, tn=wide, name="inproj_dqk")
    dv = proj(g_one, src0=o_dv, ncols=o_mg - o_dv, heads=no, mode="raw", out_dtype=BF16, tn=wide, name="inproj_dv")
    mgate = proj(g_one, src0=o_mg, ncols=2 * D, heads=no, mode="sigmoid", out_dtype=BF16, tn=wide, name="inproj_mg")

    nbias = _bias_tiles(_window_idx(QB).transpose(0, 2, 1), rel_bias, head0=0, groups=G, hpg=NSA_HPG,
                        name="bias_nsa", rel=True, mult=LOG2E)
    cmp_idx, cmp_u0, cmp_per = _cmp_idx(T)
    cbias = _bias_tiles(cmp_idx.transpose(0, 2, 1), rel_bias, head0=0, groups=G, hpg=NSA_HPG, name="bias_cmp",
                        mult=LOG2E)
    dbias = _bias_tiles(_causal_idx(DQB), rel_bias, head0=NSA_HEADS, groups=DIFF_HEADS, hpg=1, name="bias_diff",
                        rel=True, mult=LOG2E)

    kc, vc = _compress(cmpkv.reshape(B, T, n_kv), cmp_pe, cmp_w1, cmp_w2, nsa_k_gain[0])
    o_nsa = _nsa_t(qn.reshape(B, T, n_q), kv.reshape(B, T, 2 * n_kv), kc, vc, gates.reshape(B, T, LANE), cbias,
                   cmp_u0, cmp_per, nbias)
    o_diff = _diff(dqk.reshape(B, T, n_dqk), dv.reshape(B, T, -1), diff_lambda_q, diff_lambda_k, diff_subln_gain,
                   dbias)

    merged = _merge(o_nsa.reshape(M, -1), o_diff.reshape(M, -1), w_nsa_out, w_diff_out, mgate)
    x1, h2 = _oproj(merged, w_o.astype(BF16), x.reshape(M, D), mod3, norm2_gain, T)
    act = _ffn_up(h2, w_ffn_up, ffn_conv_w, ffn_conv_b, T)
    out = _ffn_down(act, w_ffn_down, x1, mod3, T)
    return out.reshape(B, T, D)


def kernel(x, c, w_ada, b_ada, norm1_gain, norm2_gain, w_in, nsa_q_gain, nsa_k_gain, cmp_pe, cmp_w1, cmp_w2,
           diff_q_gain, diff_k_gain, diff_lambda_q, diff_lambda_k, diff_subln_gain, w_nsa_out, w_diff_out, w_o,
           w_ffn_up, ffn_conv_w, ffn_conv_b, w_ffn_down, rel_bias):
    return _layer(x, c, w_ada[0], b_ada[0], norm1_gain[0], norm2_gain[0], w_in, nsa_q_gain[0], nsa_k_gain[0],
                  cmp_pe[0], cmp_w1[0], cmp_w2[0], diff_q_gain[0], diff_k_gain[0], diff_lambda_q[0],
                  diff_lambda_k[0], diff_subln_gain[0], w_nsa_out[0], w_diff_out[0], w_o[0], w_ffn_up[0],
                  ffn_conv_w[0], ffn_conv_b[0], w_ffn_down[0], rel_bias)
```

```python
import functools
import math

import numpy as np
import jax
import jax.numpy as jnp
from jax import lax
from jax.experimental import pallas as pl
from jax.experimental.pallas import tpu as pltpu

F32 = jnp.float32
BF16 = jnp.bfloat16

HEAD_DIM = 128
NSA_HEADS = 8
NSA_KV_GROUPS = 2
NSA_HPG = NSA_HEADS // NSA_KV_GROUPS
CMP_LEN = 32
CMP_STRIDE = 16
SEL_BLOCK = 64
N_SEL = 16
WINDOW = 512
DIFF_HEADS = 4
NUM_BUCKETS = 32
MAX_DISTANCE = 128
EPS = 1e-6
NEG = -1e30
LAM_INIT = 0.8 - 0.6 * math.exp(-0.3 * 0)
LOG2E = math.log2(math.e)
MASK_BIG = -(2.0 ** 100)

LANE = 128
QB = 128
DQB = 256
SUB_ROWS = 256
FFN_SUB_ROWS = 256
INPROJ_TM = 2048
INPROJ_TN = 1024
VMEM_LIMIT = 56 * 1024 * 1024


def _cparams(n_axes):
    return pltpu.CompilerParams(dimension_semantics=("arbitrary",) * n_axes,
                                vmem_limit_bytes=VMEM_LIMIT)


def _t5_bucket_np(dist):
    n = np.maximum(np.asarray(dist, np.int32), 0)
    max_exact = NUM_BUCKETS // 2
    nf = np.maximum(n, max_exact).astype(np.float32)
    large = max_exact + (np.log(nf / np.float32(max_exact)) / np.float32(math.log(MAX_DISTANCE / max_exact))
                         * np.float32(NUM_BUCKETS - max_exact)).astype(np.int32)
    large = np.minimum(large, NUM_BUCKETS - 1)
    return np.where(n < max_exact, n, large).astype(np.int32)


def _ada_kernel(ct_ref, w_ref, b_ref, o_ref):
    ct = ct_ref[...]
    s = ct * jax.nn.sigmoid(ct)
    w = w_ref[...]
    for b in range(ct.shape[1]):
        o_ref[b:b + 1, :] = jnp.sum(w * s[:, b:b + 1], axis=0, keepdims=True) + b_ref[...]


def _ada(c, w_ada, b_ada, tn=1024):
    B, D = c.shape
    N = w_ada.shape[1]
    return pl.pallas_call(
        _ada_kernel,
        grid=(N // tn,),
        in_specs=[pl.BlockSpec((D, B), lambda j: (0, 0)),
                  pl.BlockSpec((D, tn), lambda j: (0, j)),
                  pl.BlockSpec((1, tn), lambda j: (0, j))],
        out_specs=pl.BlockSpec((B, tn), lambda j: (0, j)),
        out_shape=jax.ShapeDtypeStruct((B, N), F32),
        compiler_params=_cparams(1),
        name="ada",
    )(c.T, w_ada, b_ada.reshape(1, N))


def _modnorm(x, gain, sc, sh):
    ms = jnp.mean(x * x, axis=-1, keepdims=True)
    return (x * lax.rsqrt(ms + EPS) * gain) * (1.0 + sc) + sh


def _row_chunks(tm, sub=SUB_ROWS):
    return [slice(r, r + sub) for r in range(0, tm, sub)]


def _inproj_epilogue(acc, g_ref, o_ref, rows, heads, mode):
    for k, is_head in enumerate(heads):
        lanes = slice(k * LANE, (k + 1) * LANE)
        y = acc[:, lanes]
        if is_head:
            ms = jnp.mean(y * y, axis=-1, keepdims=True)
            y = y * lax.rsqrt(ms + EPS) * g_ref[:, lanes]
        elif mode == "sigmoid":
            y = jax.nn.sigmoid(y)
        o_ref[rows, lanes] = y.astype(o_ref.dtype)


def _inproj_kernel(a_ref, wt_ref, g_ref, o_ref, wb_ref, *, heads, mode):
    @pl.when(pl.program_id(1) == 0)
    def _():
        wb_ref[...] = wt_ref[0].astype(BF16)

    for rows in _row_chunks(a_ref.shape[0]):
        acc = _qk(a_ref[rows, :], wb_ref[...])
        _inproj_epilogue(acc, g_ref, o_ref, rows, heads, mode)


def _norm1_inproj_kernel(x_ref, n1g_ref, sc_ref, sh_ref, wt_ref, g_ref, o_ref, h_ref, wb_ref, *, heads, mode):
    @pl.when(pl.program_id(0) == 0)
    def _():
        wb_ref[...] = wt_ref[0].astype(BF16)

    for rows in _row_chunks(x_ref.shape[0]):
        h = _modnorm(x_ref[rows, :], n1g_ref[...], sc_ref[0], sh_ref[0]).astype(BF16)
        h_ref[rows, :] = h
        _inproj_epilogue(_qk(h, wb_ref[...]), g_ref, o_ref, rows, heads, mode)


def _norm1_inproj(x2d, n1_gain, mod3, w_in_t, gains, *, T, src0, ncols, heads, mode, out_dtype, tm, name):
    M, K = x2d.shape
    per = T // tm
    assert src0 % 8 == 0 and len(heads) == ncols // LANE
    return pl.pallas_call(
        functools.partial(_norm1_inproj_kernel, heads=heads, mode=mode),
        grid=(M // tm,),
        in_specs=[pl.BlockSpec((tm, K), lambda i: (i, 0)),
                  pl.BlockSpec((1, K), lambda i: (0, 0)),
                  pl.BlockSpec((1, 1, K), lambda i: ((i // per) * 6 + 1, 0, 0)),
                  pl.BlockSpec((1, 1, K), lambda i: ((i // per) * 6 + 0, 0, 0)),
                  pl.BlockSpec((pl.Element(1), pl.Element(ncols), pl.Element(K)), lambda i: (0, src0, 0)),
                  pl.BlockSpec((1, ncols), lambda i: (0, 0))],
        out_specs=[pl.BlockSpec((tm, ncols), lambda i: (i, 0)), pl.BlockSpec((tm, K), lambda i: (i, 0))],
        out_shape=[jax.ShapeDtypeStruct((M, ncols), out_dtype), jax.ShapeDtypeStruct((M, K), BF16)],
        scratch_shapes=[pltpu.VMEM((ncols, K), BF16)],
        compiler_params=_cparams(1),
        name=name,
    )(x2d, n1_gain.reshape(1, K), mod3, mod3, w_in_t, gains)


def _inproj(h2d, w_in_t, gains, *, src0, ncols, heads, mode, out_dtype, tm, tn, name):
    M, K = h2d.shape
    assert src0 % 8 == 0 and ncols % tn == 0 and len(heads) == tn // LANE
    return pl.pallas_call(
        functools.partial(_inproj_kernel, heads=heads, mode=mode),
        grid=(ncols // tn, M // tm),
        in_specs=[pl.BlockSpec((tm, K), lambda j, i: (i, 0)),
                  pl.BlockSpec((pl.Element(1), pl.Element(tn), pl.Element(K)),
                               lambda j, i: (0, pl.multiple_of(src0 + j * tn, 8), 0)),
                  pl.BlockSpec((1, tn), lambda j, i: (0, j))],
        out_specs=pl.BlockSpec((tm, tn), lambda j, i: (i, j)),
        out_shape=jax.ShapeDtypeStruct((M, ncols), out_dtype),
        scratch_shapes=[pltpu.VMEM((tn, K), BF16)],
        compiler_params=_cparams(2),
        name=name,
    )(h2d, w_in_t, gains)


def _inproj_cmp_gate_kernel(a_ref, wc_ref, wg_ref, oc_ref, og_ref, wcb_ref, wgb_ref):
    @pl.when(pl.program_id(0) == 0)
    def _():
        wcb_ref[...] = wc_ref[0].astype(BF16)
        wgb_ref[...] = wg_ref[0].astype(BF16)

    for rows in _row_chunks(a_ref.shape[0]):
        a = a_ref[rows, :]
        oc_ref[rows, :] = _qk(a, wcb_ref[...])
        og_ref[rows, :] = jax.nn.sigmoid(_qk(a, wgb_ref[...]))


def _inproj_cmp_gate(h2d, w_in_t, *, src_cmp, n_cmp, src_gate, tm):
    M, K = h2d.shape
    assert src_cmp % 8 == 0 and src_gate % 8 == 0

    def w_rows(start, n):
        return pl.BlockSpec((pl.Element(1), pl.Element(n), pl.Element(K)), lambda i: (0, start, 0))

    return pl.pallas_call(
        _inproj_cmp_gate_kernel,
        grid=(M // tm,),
        in_specs=[pl.BlockSpec((tm, K), lambda i: (i, 0)), w_rows(src_cmp, n_cmp), w_rows(src_gate, LANE)],
        out_specs=[pl.BlockSpec((tm, n_cmp), lambda i: (i, 0)), pl.BlockSpec((tm, LANE), lambda i: (i, 0))],
        out_shape=[jax.ShapeDtypeStruct((M, n_cmp), F32), jax.ShapeDtypeStruct((M, LANE), F32)],
        scratch_shapes=[pltpu.VMEM((n_cmp, K), BF16), pltpu.VMEM((LANE, K), BF16)],
        compiler_params=_cparams(1),
        name="inproj_cmp_gate",
    )(h2d, w_in_t, w_in_t)


def _bias_kernel(idx_ref, tab_ref, o_ref, *, head0, hpg, rel, mult):
    n_tiles, _, C = idx_ref.shape
    for n in range(n_tiles):
        idx = idx_ref[n]
        for h in range(hpg):
            head = head0 + pl.program_id(0) * hpg + h
            acc = jnp.zeros(idx.shape, F32)
            for b in range(NUM_BUCKETS):
                acc = jnp.where(idx == b, tab_ref[b, head], acc)
            if rel:
                acc = acc - tab_ref[NUM_BUCKETS - 1, head]
            o_ref[0, n, :, h * C:(h + 1) * C] = jnp.where(idx < 0, NEG, acc * mult)


def _bias_tiles(idx, rel_bias, *, head0, groups, hpg, name, rel=False, mult=1.0):
    N, R, C = idx.shape
    return pl.pallas_call(
        functools.partial(_bias_kernel, head0=head0, hpg=hpg, rel=rel, mult=mult),
        grid=(groups,),
        in_specs=[pl.BlockSpec((N, R, C), lambda g: (0, 0, 0)),
                  pl.BlockSpec(memory_space=pltpu.SMEM)],
        out_specs=pl.BlockSpec((1, N, R, hpg * C), lambda g: (g, 0, 0, 0)),
        out_shape=jax.ShapeDtypeStruct((groups, N, R, hpg * C), F32),
        compiler_params=_cparams(1),
        name=name,
    )(jnp.asarray(idx), rel_bias)


def _causal_idx(R):
    r = np.arange(R)[:, None]
    c = np.arange(R)[None, :]
    return np.stack([np.where(r >= c, _t5_bucket_np(r - c), -1), _t5_bucket_np(R + r - c)]).astype(np.int32)


def _window_idx(R):
    r = np.arange(R)[:, None]
    c = np.arange(R)[None, :]
    edge = np.where(r < c, NUM_BUCKETS - 1, -1)
    return np.concatenate([_causal_idx(R), edge[None]]).astype(np.int32)


def _cmp_idx(T):
    per = QB // CMP_STRIDE
    u0 = (T // QB - 1) * per
    assert u0 + LANE <= 2 * LANE
    r = np.arange(QB)[:, None]
    end = (np.arange(2 * LANE)[None, :] - u0) * CMP_STRIDE + CMP_LEN - 1
    return np.where(end <= r, _t5_bucket_np(r - end), -1).astype(np.int32)[None], u0, per


def _compress_kernel(zk_ref, zv_ref, pe_ref, w1_ref, w2_ref, kg_ref, kc_ref, vc_ref):
    half = CMP_LEN // 2

    def one(z_ref, i):
        p1 = jnp.zeros((LANE, HEAD_DIM), F32)
        p2 = jnp.zeros((LANE, HEAD_DIM), F32)
        for l in range(half):
            z = z_ref[0, pl.ds(l, LANE, stride=CMP_STRIDE), :]
            w_lo = w1_ref[i, l * HEAD_DIM:(l + 1) * HEAD_DIM, :].astype(BF16)
            w_hi = w1_ref[i, (half + l) * HEAD_DIM:(half + l + 1) * HEAD_DIM, :].astype(BF16)
            p1 = p1 + jnp.dot((z + pe_ref[i, l:l + 1, :]).astype(BF16), w_lo, preferred_element_type=F32)
            p2 = p2 + jnp.dot((z + pe_ref[i, half + l:half + l + 1, :]).astype(BF16), w_hi,
                              preferred_element_type=F32)
        pre = p1 + pltpu.roll(p2, LANE - 1, axis=0)
        hid = jax.nn.gelu(pre)
        return jnp.dot(hid.astype(BF16), w2_ref[i].astype(BF16), preferred_element_type=F32)

    kc = one(zk_ref, 0)
    ms = jnp.mean(kc * kc, axis=-1, keepdims=True)
    kc_ref[0, 0] = (kc * lax.rsqrt(ms + EPS) * kg_ref[...]).astype(kc_ref.dtype)
    vc_ref[0, 0] = one(zv_ref, 1).astype(vc_ref.dtype)


def _compress(cmp_kv, pe, w1, w2, k_gain0):
    B, T, _ = cmp_kv.shape
    G, dk = NSA_KV_GROUPS, HEAD_DIM
    assert (T - CMP_LEN) // CMP_STRIDE + 1 == LANE - 1
    out = jax.ShapeDtypeStruct((B, G, LANE, dk), BF16)
    return pl.pallas_call(
        _compress_kernel,
        grid=(B, G),
        in_specs=[pl.BlockSpec((1, T, dk), lambda b, g: (b, 0, g)),
                  pl.BlockSpec((1, T, dk), lambda b, g: (b, 0, G + g)),
                  pl.BlockSpec((2, CMP_LEN, dk), lambda b, g: (0, 0, 0)),
                  pl.BlockSpec((2, CMP_LEN * dk, dk), lambda b, g: (0, 0, 0)),
                  pl.BlockSpec((2, dk, dk), lambda b, g: (0, 0, 0)),
                  pl.BlockSpec((1, dk), lambda b, g: (0, 0))],
        out_specs=[pl.BlockSpec((1, 1, LANE, dk), lambda b, g: (b, g, 0, 0))] * 2,
        out_shape=[out, out],
        compiler_params=_cparams(2),
        name="compress",
    )(cmp_kv, cmp_kv, pe, w1, w2, k_gain0.reshape(1, dk))


def _qk(q, k):
    return lax.dot_general(q, k, (((1,), (1,)), ((), ())), preferred_element_type=F32)


def _lane_fold(x, op):
    acc = x[..., :LANE]
    for t in range(1, x.shape[-1] // LANE):
        acc = op(acc, x[..., t * LANE:(t + 1) * LANE])
    return acc


def _softmax_parts(parts):
    m = jnp.max(functools.reduce(jnp.maximum, [_lane_fold(s, jnp.maximum) for s in parts]), axis=-1, keepdims=True)
    ps = [jnp.exp2(s - m) for s in parts]
    den = jnp.sum(functools.reduce(jnp.add, [_lane_fold(p, jnp.add) for p in ps]), axis=-1, keepdims=True)
    return ps, den


def _nsa_t_kernel(q_ref, ks_ref, vs_ref, kw_ref, vw_ref, kc_ref, vc_ref, gate_ref, cb_ref, nb_ref,
                  ovt_ref, expt_ref, o_ref, vst_ref, vwt_ref, vct_ref, ksa_ref, *, cmp_u0, cmp_per):
    H, W = NSA_HPG, NSA_HPG * QB
    T = ks_ref.shape[1]
    ns = ovt_ref.shape[0]
    nwb = WINDOW // QB
    nb = T // QB
    kc = kc_ref[0, 0]

    @pl.when(pl.program_id(2) == 0)
    def _():
        ksa_ref[:, :HEAD_DIM] = ks_ref[0]
        ksa_ref[:, HEAD_DIM:] = expt_ref[...]
        vst_ref[...] = vs_ref[0].astype(F32).T.astype(BF16)
        vwt_ref[...] = vw_ref[0].astype(F32).T.astype(BF16)
        vct_ref[...] = vc_ref[0, 0].astype(F32).T.astype(BF16)

    def softmax_t(parts):
        m = functools.reduce(jnp.maximum, [jnp.max(s, axis=0, keepdims=True) for s in parts])
        ps = [jnp.exp2(s - m) for s in parts]
        den = functools.reduce(jnp.add, [jnp.sum(p, axis=0, keepdims=True) for p in ps])
        return [p.astype(BF16) for p in ps], den

    def attend_task(vt_ref, spans_fn, out, key):
        spans = spans_fn()
        parts = []
        for _, _, k_slab, q_op, add in spans:
            s = _qk(k_slab, q_op)
            parts.append(s if add is None else s + add)
        yield
        ps, den = softmax_t(parts)
        yield
        o = None
        for (a, b, _, _, _), p in zip(spans, ps):
            pv = jnp.dot(vt_ref[:, a:b], p, preferred_element_type=F32)
            o = pv if o is None else o + pv
        out[key] = o / den
        yield

    def cmp_task(i, qs, out):
        lo = i * QB
        u = cmp_u0 - cmp_per * i
        s = _qk(kc, qs) + cb_ref[0, 0, u:u + LANE, :]
        yield
        e = jnp.exp2(s - jnp.max(s, axis=0, keepdims=True))
        if i == 0:
            key_end = lax.broadcasted_iota(jnp.int32, (LANE, W), 0) * CMP_STRIDE + (CMP_LEN - 1)
            query = lax.broadcasted_iota(jnp.int32, (LANE, W), 1) % QB
            e = jnp.where(key_end <= query, e, 0.0)
            den = jnp.sum(e, axis=0, keepdims=True)
            p = e / jnp.where(den > 0.0, den, 1.0)
        else:
            p = e / jnp.sum(e, axis=0, keepdims=True)
        if i >= 1:
            psum = functools.reduce(jnp.add, [p[:, h * QB:(h + 1) * QB] for h in range(H)])
            imp_t = jnp.dot(ovt_ref[...], psum, precision=lax.Precision.HIGHEST, preferred_element_type=F32)
        yield
        out["cmp", i] = jnp.dot(vct_ref[...], p.astype(BF16), preferred_element_type=F32)
        if i >= 1:
            blk = lax.broadcasted_iota(jnp.int32, (ns, QB), 0)
            cur = (lo + lax.broadcasted_iota(jnp.int32, (ns, QB), 1)) // SEL_BLOCK
            forced = (blk == 0) | (blk == cur) | (blk == cur - 1)
            score = jnp.where(forced, 1e4, jnp.where(blk <= cur, imp_t, -1e4))
            rank = jnp.zeros((ns, QB), F32)
            for b in range(ns):
                other = score[b:b + 1, :]
                rank = rank + jnp.where(blk > b, jnp.where(other >= score, 1.0, 0.0),
                                        jnp.where(other > score, 1.0, 0.0))
            unsel = jnp.where(rank < float(min(N_SEL, ns)), 0.0, 1.0)
            unsel_q = jnp.concatenate([unsel, jnp.zeros((LANE - ns, QB), F32)], axis=0).T.astype(BF16)
            out["qs_aug", i] = jnp.concatenate([qs, jnp.concatenate([unsel_q] * H, axis=0)], axis=1)
        yield

    def slc_spans(i, qs, out):
        lo, hi = i * QB, (i + 1) * QB
        spans = [(lo, hi, ks_ref[0, lo:hi, :], qs, nb_ref[0, 0])]
        if i >= 1:
            qs_aug = out["qs_aug", i]
            spans.insert(0, (lo - QB, lo, ksa_ref[lo - QB:lo, :], qs_aug, nb_ref[0, 1]))
            if i >= 2:
                spans.insert(0, (0, lo - QB, ksa_ref[0:lo - QB, :], qs_aug, None))
        return spans

    def win_spans(i, qs):
        lo, hi = i * QB, (i + 1) * QB

        def span(a, b, add):
            return (a, b, kw_ref[0, a:b, :], qs, add)

        spans = []
        if i >= nwb:
            spans.append(span((i - nwb) * QB, (i - nwb + 1) * QB, nb_ref[0, 2]))
        mid_a, mid_b = max(i - nwb + 1, 0) * QB, (i - 1) * QB
        if mid_b > mid_a:
            spans.append(span(mid_a, mid_b, None))
        if i >= 1:
            spans.append(span(lo - QB, lo, nb_ref[0, 1]))
        spans.append(span(lo, hi, nb_ref[0, 0]))
        return spans

    def combine(i, out):
        lo, hi = i * QB, (i + 1) * QB
        gate_t = gate_ref[0, lo:hi, :].T
        first = pl.program_id(1) == 0

        def grow(br):
            rows = []
            for h in range(H):
                c = 3 * h + br
                rows.append(jnp.where(first, gate_t[c:c + 1, :], gate_t[3 * H + c:3 * H + c + 1, :]))
            return jnp.concatenate(rows, axis=1)

        o_t = grow(0) * out["cmp", i] + grow(1) * out["slc", i] + grow(2) * out["win", i]
        o_ref[0, lo:hi, :] = jnp.concatenate([o_t[:, h * QB:(h + 1) * QB].T for h in range(H)],
                                             axis=1).astype(o_ref.dtype)

    def pair(k):
        out = {}
        tasks = []
        blocks = (nb - 1 - k, k)
        qss = {}
        for i in blocks:
            q = q_ref[0, i * QB:(i + 1) * QB, :]
            qss[i] = jnp.concatenate([q[:, h * HEAD_DIM:(h + 1) * HEAD_DIM] for h in range(H)], axis=0)
            tasks.append(cmp_task(i, qss[i], out))
            tasks.append(attend_task(vwt_ref, functools.partial(win_spans, i, qss[i]), out, ("win", i)))
        for i in blocks:
            tasks.append(attend_task(vst_ref, functools.partial(slc_spans, i, qss[i], out), out, ("slc", i)))
        n_stage = 3
        for step in range(len(tasks) + n_stage - 1):
            for t in range(step - n_stage + 1, step + 1):
                if 0 <= t < len(tasks):
                    next(tasks[t])
        for i in blocks:
            combine(i, out)

    for k in range(nb // 2):
        pl.when(pl.program_id(2) == k)(functools.partial(pair, k))


def _nsa_t(qn, kv, kc, vc, gates, cbias, cmp_u0, cmp_per, nbias):
    B, T, _ = qn.shape
    G, H, dk = NSA_KV_GROUPS, NSA_HPG, HEAD_DIM
    nb = T // QB
    ns = T // SEL_BLOCK
    cstart = np.arange(LANE) * CMP_STRIDE
    sstart = np.arange(ns) * SEL_BLOCK
    overlap = np.clip(np.minimum(cstart[:, None] + CMP_LEN, sstart[None, :] + SEL_BLOCK)
                      - np.maximum(cstart[:, None], sstart[None, :]), 0, None) / CMP_STRIDE
    overlap[LANE - 1:] = 0.0
    ovt = jnp.asarray(overlap.T, F32)
    expand_t = np.zeros((T, LANE), np.float32)
    expand_t[np.arange(T), np.arange(T) // SEL_BLOCK] = MASK_BIG
    return pl.pallas_call(
        functools.partial(_nsa_t_kernel, cmp_u0=cmp_u0, cmp_per=cmp_per),
        grid=(B, G, nb // 2),
        in_specs=[pl.BlockSpec((1, T, H * dk), lambda b, g, i: (b, 0, g)),
                  pl.BlockSpec((1, T, dk), lambda b, g, i: (b, 0, g)),
                  pl.BlockSpec((1, T, dk), lambda b, g, i: (b, 0, G + g)),
                  pl.BlockSpec((1, T, dk), lambda b, g, i: (b, 0, 2 * G + g)),
                  pl.BlockSpec((1, T, dk), lambda b, g, i: (b, 0, 3 * G + g)),
                  pl.BlockSpec((1, 1, LANE, dk), lambda b, g, i: (b, g, 0, 0)),
                  pl.BlockSpec((1, 1, LANE, dk), lambda b, g, i: (b, g, 0, 0)),
                  pl.BlockSpec((1, T, LANE), lambda b, g, i: (b, 0, 0)),
                  pl.BlockSpec((1, 1, 2 * LANE, H * QB), lambda b, g, i: (g, 0, 0, 0)),
                  pl.BlockSpec((1, 3, QB, H * QB), lambda b, g, i: (g, 0, 0, 0)),
                  pl.BlockSpec((ns, LANE), lambda b, g, i: (0, 0)),
                  pl.BlockSpec((T, LANE), lambda b, g, i: (0, 0))],
        out_specs=pl.BlockSpec((1, T, H * dk), lambda b, g, i: (b, 0, g)),
        out_shape=jax.ShapeDtypeStruct((B, T, NSA_HEADS * dk), BF16),
        scratch_shapes=[pltpu.VMEM((dk, T), BF16), pltpu.VMEM((dk, T), BF16), pltpu.VMEM((dk, LANE), BF16),
                        pltpu.VMEM((T, 2 * dk), BF16)],
        compiler_params=_cparams(3),
        name="nsa",
    )(qn, kv, kv, kv, kv, kc, vc, gates, cbias, nbias, ovt, jnp.asarray(expand_t, BF16))


def _diff_kernel(q_ref, k_ref, v_ref, lq_ref, lk_ref, sg_ref, db_ref, o_ref):
    dk = HEAD_DIM
    T = k_ref.shape[1]
    lqk = lq_ref[...] * lk_ref[...]
    lam = (jnp.exp(jnp.sum(lqk[0:1], axis=-1, keepdims=True))
           - jnp.exp(jnp.sum(lqk[1:2], axis=-1, keepdims=True)) + LAM_INIT)
    def task(i, mm, out):
        lo, hi = i * DQB, (i + 1) * DQB
        cols = slice(mm * dk, (mm + 1) * dk)
        q = q_ref[0, lo:hi, cols]
        bounds, parts = [], []
        if i >= 2:
            bounds.append((0, lo - DQB))
            parts.append(_qk(q, k_ref[0, 0:lo - DQB, cols]))
        if i >= 1:
            bounds.append((lo - DQB, lo))
            parts.append(_qk(q, k_ref[0, lo - DQB:lo, cols]) + db_ref[0, 1])
        bounds.append((lo, hi))
        parts.append(_qk(q, k_ref[0, lo:hi, cols]) + db_ref[0, 0])
        yield
        ps, den = _softmax_parts(parts)
        yield
        o = None
        for (a, b), p in zip(bounds, ps):
            pv = jnp.dot(p.astype(BF16), v_ref[0, a:b, :], preferred_element_type=F32)
            o = pv if o is None else o + pv
        out[i, mm] = o / den
        yield

    out = {}
    tasks = [task(i, mm, out) for i in reversed(range(T // DQB)) for mm in range(2)]
    n_stage = 3
    for step in range(len(tasks) + n_stage - 1):
        for t in range(step - n_stage + 1, step + 1):
            if 0 <= t < len(tasks):
                next(tasks[t])

    for i in range(T // DQB):
        lo, hi = i * DQB, (i + 1) * DQB
        o = out[i, 0] - lam * out[i, 1]
        ms = jnp.mean(o * o, axis=-1, keepdims=True)
        o_ref[0, lo:hi, :] = ((o * lax.rsqrt(ms + EPS) * sg_ref[...]) * (1.0 - LAM_INIT)).astype(o_ref.dtype)


def _diff(dqk, dv, lam_q, lam_k, subln_gain, dbias):
    B, T, _ = dqk.shape
    Hd, dk = DIFF_HEADS, HEAD_DIM
    w = 2 * dk
    return pl.pallas_call(
        _diff_kernel,
        grid=(B, Hd),
        in_specs=[pl.BlockSpec((1, T, w), lambda b, h: (b, 0, h)),
                  pl.BlockSpec((1, T, w), lambda b, h: (b, 0, Hd + h)),
                  pl.BlockSpec((1, T, w), lambda b, h: (b, 0, h)),
                  pl.BlockSpec((2, dk), lambda b, h: (0, 0)),
                  pl.BlockSpec((2, dk), lambda b, h: (0, 0)),
                  pl.BlockSpec((1, w), lambda b, h: (0, 0)),
                  pl.BlockSpec((1, 2, DQB, DQB), lambda b, h: (h, 0, 0, 0))],
        out_specs=pl.BlockSpec((1, T, w), lambda b, h: (b, 0, h)),
        out_shape=jax.ShapeDtypeStruct((B, T, Hd * w), BF16),
        compiler_params=_cparams(2),
        name="diff",
    )(dqk, dqk, dv, lam_q, lam_k, subln_gain.reshape(1, w), dbias)


def _merge_kernel(an_ref, ad_ref, wn_ref, wd_ref, gn_ref, gd_ref, o_ref, wnb_ref, wdb_ref):
    @pl.when(pl.program_id(1) == 0)
    def _():
        wnb_ref[...] = wn_ref[...].astype(BF16)
        wdb_ref[...] = wd_ref[...].astype(BF16)

    for rows in _row_chunks(an_ref.shape[0]):
        yn = jnp.dot(an_ref[rows, :], wnb_ref[...], preferred_element_type=F32)
        yd = jnp.dot(ad_ref[rows, :], wdb_ref[...], preferred_element_type=F32)
        o_ref[rows, :] = (gn_ref[rows, :].astype(F32) * yn + gd_ref[rows, :].astype(F32) * yd).astype(o_ref.dtype)


def _merge(o_nsa, o_diff, w_n, w_d, mg, tm=1024, tn=1024):
    M, K = o_nsa.shape
    N = w_n.shape[1]
    nj = N // tn
    return pl.pallas_call(
        _merge_kernel,
        grid=(nj, M // tm),
        in_specs=[pl.BlockSpec((tm, K), lambda j, i: (i, 0)),
                  pl.BlockSpec((tm, K), lambda j, i: (i, 0)),
                  pl.BlockSpec((K, tn), lambda j, i: (0, j)),
                  pl.BlockSpec((K, tn), lambda j, i: (0, j)),
                  pl.BlockSpec((tm, tn), lambda j, i: (i, j)),
                  pl.BlockSpec((tm, tn), lambda j, i: (i, nj + j))],
        out_specs=pl.BlockSpec((tm, tn), lambda j, i: (i, j)),
        out_shape=jax.ShapeDtypeStruct((M, N), BF16),
        scratch_shapes=[pltpu.VMEM((K, tn), BF16), pltpu.VMEM((K, tn), BF16)],
        compiler_params=_cparams(2),
        name="merge",
    )(o_nsa, o_diff, w_n, w_d, mg, mg)


def _oproj_kernel(a_ref, w_ref, x_ref, g1_ref, gain_ref, sc_ref, sh_ref, x1_ref, h2_ref):
    for rows in _row_chunks(a_ref.shape[0]):
        y = jnp.dot(a_ref[rows, :], w_ref[...], preferred_element_type=F32)
        x1 = x_ref[rows, :] + g1_ref[0] * y
        x1_ref[rows, :] = x1
        h2_ref[rows, :] = _modnorm(x1, gain_ref[...], sc_ref[0], sh_ref[0]).astype(h2_ref.dtype)


def _oproj(merged, w_o, x2d, mod3, gain2, T, tm=512):
    M, D = x2d.shape
    per = T // tm
    return pl.pallas_call(
        _oproj_kernel,
        grid=(M // tm,),
        in_specs=[pl.BlockSpec((tm, D), lambda i: (i, 0)),
                  pl.BlockSpec((D, D), lambda i: (0, 0)),
                  pl.BlockSpec((tm, D), lambda i: (i, 0)),
                  pl.BlockSpec((1, 1, D), lambda i: ((i // per) * 6 + 2, 0, 0)),
                  pl.BlockSpec((1, D), lambda i: (0, 0)),
                  pl.BlockSpec((1, 1, D), lambda i: ((i // per) * 6 + 4, 0, 0)),
                  pl.BlockSpec((1, 1, D), lambda i: ((i // per) * 6 + 3, 0, 0))],
        out_specs=[pl.BlockSpec((tm, D), lambda i: (i, 0)),
                   pl.BlockSpec((tm, D), lambda i: (i, 0))],
        out_shape=[jax.ShapeDtypeStruct((M, D), F32), jax.ShapeDtypeStruct((M, D), BF16)],
        compiler_params=_cparams(1),
        name="oproj",
    )(merged, w_o, x2d, mod3, gain2.reshape(1, D), mod3, mod3)


def _ffn_up_kernel(h_ref, wa_ref, wv_ref, cwa_ref, cwv_ref, cba_ref, cbv_ref, o_ref, wab_ref, wvb_ref,
                   ca_ref, cv_ref, sa_ref, sv_ref, *, per):
    i = pl.program_id(1)

    @pl.when(i == 0)
    def _():
        wab_ref[...] = wa_ref[...].astype(BF16)
        wvb_ref[...] = wv_ref[...].astype(BF16)

    @pl.when(i % per == 0)
    def _():
        ca_ref[...] = jnp.zeros(ca_ref.shape, F32)
        cv_ref[...] = jnp.zeros(cv_ref.shape, F32)

    def conv(u, prev, cw_ref, cb_ref, s_ref):
        s_ref[0:8, :] = prev
        s_ref[8:8 + FFN_SUB_ROWS, :] = u
        u1 = s_ref[7:7 + FFN_SUB_ROWS, :]
        u2 = s_ref[6:6 + FFN_SUB_ROWS, :]
        return cb_ref[...] + cw_ref[0:1, :] * u2 + cw_ref[1:2, :] * u1 + cw_ref[2:3, :] * u

    prev_a, prev_v = ca_ref[...], cv_ref[...]
    for n, rows in enumerate(_row_chunks(h_ref.shape[0], FFN_SUB_ROWS)):
        hs = h_ref[rows, :]
        ua = jnp.dot(hs, wab_ref[...], preferred_element_type=F32)
        uv = jnp.dot(hs, wvb_ref[...], preferred_element_type=F32)
        a = conv(ua, prev_a, cwa_ref, cba_ref, sa_ref.at[n % 2])
        val = conv(uv, prev_v, cwv_ref, cbv_ref, sv_ref.at[n % 2])
        o_ref[rows, :] = (a * jax.nn.sigmoid(a) * val).astype(o_ref.dtype)
        prev_a, prev_v = ua[FFN_SUB_ROWS - 8:, :], uv[FFN_SUB_ROWS - 8:, :]
    ca_ref[...] = prev_a
    cv_ref[...] = prev_v


def _ffn_up(h2, w_up, conv_w, conv_b, T, tm=2048, tn=512):
    M, D = h2.shape
    F = w_up.shape[1] // 2
    nj = F // tn
    cb = conv_b.reshape(1, 2 * F)
    return pl.pallas_call(
        functools.partial(_ffn_up_kernel, per=T // tm),
        grid=(nj, M // tm),
        in_specs=[pl.BlockSpec((tm, D), lambda j, i: (i, 0)),
                  pl.BlockSpec((D, tn), lambda j, i: (0, j)),
                  pl.BlockSpec((D, tn), lambda j, i: (0, nj + j)),
                  pl.BlockSpec((3, tn), lambda j, i: (0, j)),
                  pl.BlockSpec((3, tn), lambda j, i: (0, nj + j)),
                  pl.BlockSpec((1, tn), lambda j, i: (0, j)),
                  pl.BlockSpec((1, tn), lambda j, i: (0, nj + j))],
        out_specs=pl.BlockSpec((tm, tn), lambda j, i: (i, j)),
        out_shape=jax.ShapeDtypeStruct((M, F), BF16),
        scratch_shapes=[pltpu.VMEM((D, tn), BF16), pltpu.VMEM((D, tn), BF16),
                        pltpu.VMEM((8, tn), F32), pltpu.VMEM((8, tn), F32),
                        pltpu.VMEM((2, FFN_SUB_ROWS + 8, tn), F32), pltpu.VMEM((2, FFN_SUB_ROWS + 8, tn), F32)],
        compiler_params=_cparams(2),
        name="ffn_up",
    )(h2, w_up, w_up, conv_w, conv_w, cb, cb)


def _ffn_down_kernel(a_ref, w_ref, x_ref, g2_ref, o_ref, wb_ref):
    @pl.when(pl.program_id(1) == 0)
    def _():
        wb_ref[...] = w_ref[...].astype(BF16)

    for rows in _row_chunks(a_ref.shape[0]):
        y = jnp.dot(a_ref[rows, :], wb_ref[...], preferred_element_type=F32)
        o_ref[rows, :] = x_ref[rows, :] + g2_ref[0] * y


def _ffn_down(act, w_down, x1, mod3, T, tm=512, tn=512):
    M, F = act.shape
    D = w_down.shape[1]
    per = T // tm
    return pl.pallas_call(
        _ffn_down_kernel,
        grid=(D // tn, M // tm),
        in_specs=[pl.BlockSpec((tm, F), lambda j, i: (i, 0)),
                  pl.BlockSpec((F, tn), lambda j, i: (0, j)),
                  pl.BlockSpec((tm, tn), lambda j, i: (i, j)),
                  pl.BlockSpec((1, 1, tn), lambda j, i: ((i // per) * 6 + 5, 0, j))],
        out_specs=pl.BlockSpec((tm, tn), lambda j, i: (i, j)),
        out_shape=jax.ShapeDtypeStruct((M, D), F32),
        scratch_shapes=[pltpu.VMEM((F, tn), BF16)],
        compiler_params=_cparams(2),
        name="ffn_down",
    )(act, w_down, x1, mod3)


def _layer(x, c, w_ada, b_ada, norm1_gain, norm2_gain, w_in, nsa_q_gain, nsa_k_gain, cmp_pe, cmp_w1, cmp_w2,
           diff_q_gain, diff_k_gain, diff_lambda_q, diff_lambda_k, diff_subln_gain, w_nsa_out, w_diff_out, w_o,
           w_ffn_up, ffn_conv_w, ffn_conv_b, w_ffn_down, rel_bias):
    B, T, D = x.shape
    dk, G = HEAD_DIM, NSA_KV_GROUPS
    M = B * T
    scale = dk ** -0.5

    n_q = NSA_HEADS * dk
    o_kv = n_q
    o_g = o_kv + 3 * 2 * G * dk
    o_dq = o_g + NSA_HEADS * 3
    o_dk = o_dq + DIFF_HEADS * 2 * dk
    o_dv = o_dk + DIFF_HEADS * 2 * dk
    o_mg = o_dv + DIFF_HEADS * 2 * dk
    n_kv = 2 * G * dk
    n_dqk = 2 * DIFF_HEADS * 2 * dk

    mod3 = _ada(c, w_ada, b_ada).reshape(B * 6, 1, D)

    ones = jnp.ones((n_kv // 2,), F32)
    g_q = jnp.tile(nsa_q_gain * (scale * LOG2E), NSA_HEADS).reshape(1, n_q)
    g_kv = jnp.concatenate([jnp.tile(nsa_k_gain[1], G), ones, jnp.tile(nsa_k_gain[2], G), ones]).reshape(1, 2 * n_kv)
    g_dqk = jnp.concatenate([jnp.tile(diff_q_gain * (scale * LOG2E), 2 * DIFF_HEADS),
                             jnp.tile(diff_k_gain, 2 * DIFF_HEADS)]).reshape(1, n_dqk)
    g_one = jnp.ones((1, 2 * D), F32)
    w_in_t = jnp.swapaxes(w_in, 1, 2)
    wide = INPROJ_TN
    yes, no = (True,) * (wide // LANE), (False,) * (wide // LANE)
    qn, h = _norm1_inproj(x.reshape(M, D), norm1_gain, mod3, w_in_t, g_q, T=T, src0=0, ncols=n_q,
                          heads=(True,) * NSA_HEADS, mode="raw", out_dtype=BF16, tm=INPROJ_TM // 2, name="inproj_q")
    proj = functools.partial(_inproj, h, w_in_t, tm=INPROJ_TM)
    cmpkv, gates = _inproj_cmp_gate(h, w_in_t, src_cmp=o_kv, n_cmp=n_kv, src_gate=o_g, tm=INPROJ_TM)
    kv_heads = ((True,) * G + (False,) * G) * 2
    kv = proj(g_kv, src0=o_kv + n_kv, ncols=2 * n_kv, heads=kv_heads, mode="raw", out_dtype=BF16, tn=2 * n_kv,
              name="inproj_kv")
    dqk = proj(g_dqk, src0=o_dq, ncols=n_dqk, heads=yes, mode="raw", out_dtype=BF16, tn=wide, name="inproj_dqk")
    dv = proj(g_one, src0=o_dv, ncols=o_mg - o_dv, heads=no, mode="raw", out_dtype=BF16, tn=wide, name="inproj_dv")
    mgate = proj(g_one, src0=o_mg, ncols=2 * D, heads=no, mode="sigmoid", out_dtype=BF16, tn=wide, name="inproj_mg")

    nbias = _bias_tiles(_window_idx(QB).transpose(0, 2, 1), rel_bias, head0=0, groups=G, hpg=NSA_HPG,
                        name="bias_nsa", rel=True, mult=LOG2E)
    cmp_idx, cmp_u0, cmp_per = _cmp_idx(T)
    cbias = _bias_tiles(cmp_idx.transpose(0, 2, 1), rel_bias, head0=0, groups=G, hpg=NSA_HPG, name="bias_cmp",
                        mult=LOG2E)
    dbias = _bias_tiles(_causal_idx(DQB), rel_bias, head0=NSA_HEADS, groups=DIFF_HEADS, hpg=1, name="bias_diff",
                        rel=True, mult=LOG2E)

    kc, vc = _compress(cmpkv.reshape(B, T, n_kv), cmp_pe, cmp_w1, cmp_w2, nsa_k_gain[0])
    o_nsa = _nsa_t(qn.reshape(B, T, n_q), kv.reshape(B, T, 2 * n_kv), kc, vc, gates.reshape(B, T, LANE), cbias,
                   cmp_u0, cmp_per, nbias)
    o_diff = _diff(dqk.reshape(B, T, n_dqk), dv.reshape(B, T, -1), diff_lambda_q, diff_lambda_k, diff_subln_gain,
                   dbias)

    merged = _merge(o_nsa.reshape(M, -1), o_diff.reshape(M, -1), w_nsa_out, w_diff_out, mgate)
    x1, h2 = _oproj(merged, w_o.astype(BF16), x.reshape(M, D), mod3, norm2_gain, T)
    act = _ffn_up(h2, w_ffn_up, ffn_conv_w, ffn_conv_b, T)
    out = _ffn_down(act, w_ffn_down, x1, mod3, T)
    return out.reshape(B, T, D)


def kernel(x, c, w_ada, b_ada, norm1_gain, norm2_gain, w_in, nsa_q_gain, nsa_k_gain, cmp_pe, cmp_w1, cmp_w2,
           diff_q_gain, diff_k_gain, diff_lambda_q, diff_lambda_k, diff_subln_gain, w_nsa_out, w_diff_out, w_o,
           w_ffn_up, ffn_conv_w, ffn_conv_b, w_ffn_down, rel_bias):
    return _layer(x, c, w_ada[0], b_ada[0], norm1_gain[0], norm2_gain[0], w_in, nsa_q_gain[0], nsa_k_gain[0],
                  cmp_pe[0], cmp_w1[0], cmp_w2[0], diff_q_gain[0], diff_k_gain[0], diff_lambda_q[0],
                  diff_lambda_k[0], diff_subln_gain[0], w_nsa_out[0], w_diff_out[0], w_o[0], w_ffn_up[0],
                  ffn_conv_w[0], ffn_conv_b[0], w_ffn_down[0], rel_bias)
```

```python
import functools
import math

import numpy as np
import jax
import jax.numpy as jnp
from jax import lax
from jax.experimental import pallas as pl
from jax.experimental.pallas import tpu as pltpu

F32 = jnp.float32
BF16 = jnp.bfloat16

HEAD_DIM = 128
NSA_HEADS = 8
NSA_KV_GROUPS = 2
NSA_HPG = NSA_HEADS // NSA_KV_GROUPS
CMP_LEN = 32
CMP_STRIDE = 16
SEL_BLOCK = 64
N_SEL = 16
WINDOW = 512
DIFF_HEADS = 4
NUM_BUCKETS = 32
MAX_DISTANCE = 128
EPS = 1e-6
NEG = -1e30
LAM_INIT = 0.8 - 0.6 * math.exp(-0.3 * 0)
LOG2E = math.log2(math.e)
MASK_BIG = -(2.0 ** 100)

LANE = 128
QB = 128
DQB = 256
SUB_ROWS = 256
FFN_SUB_ROWS = 512
WIDE_SUB_ROWS = 512
INPROJ_TM = 2048
INPROJ_TN = 1024
VMEM_LIMIT = 56 * 1024 * 1024


def _cparams(n_axes):
    return pltpu.CompilerParams(dimension_semantics=("arbitrary",) * n_axes,
                                vmem_limit_bytes=VMEM_LIMIT)


def _t5_bucket_np(dist):
    n = np.maximum(np.asarray(dist, np.int32), 0)
    max_exact = NUM_BUCKETS // 2
    nf = np.maximum(n, max_exact).astype(np.float32)
    large = max_exact + (np.log(nf / np.float32(max_exact)) / np.float32(math.log(MAX_DISTANCE / max_exact))
                         * np.float32(NUM_BUCKETS - max_exact)).astype(np.int32)
    large = np.minimum(large, NUM_BUCKETS - 1)
    return np.where(n < max_exact, n, large).astype(np.int32)


def _ada_kernel(ct_ref, w_ref, b_ref, o_ref):
    ct = ct_ref[...]
    s = ct * jax.nn.sigmoid(ct)
    w = w_ref[...]
    for b in range(ct.shape[1]):
        o_ref[b:b + 1, :] = jnp.sum(w * s[:, b:b + 1], axis=0, keepdims=True) + b_ref[...]


def _ada(c, w_ada, b_ada, tn=1024):
    B, D = c.shape
    N = w_ada.shape[1]
    return pl.pallas_call(
        _ada_kernel,
        grid=(N // tn,),
        in_specs=[pl.BlockSpec((D, B), lambda j: (0, 0)),
                  pl.BlockSpec((D, tn), lambda j: (0, j)),
                  pl.BlockSpec((1, tn), lambda j: (0, j))],
        out_specs=pl.BlockSpec((B, tn), lambda j: (0, j)),
        out_shape=jax.ShapeDtypeStruct((B, N), F32),
        compiler_params=_cparams(1),
        name="ada",
    )(c.T, w_ada, b_ada.reshape(1, N))


def _modnorm(x, gain, sc, sh):
    ms = jnp.mean(x * x, axis=-1, keepdims=True)
    return (x * lax.rsqrt(ms + EPS) * gain) * (1.0 + sc) + sh


def _row_chunks(tm, sub=SUB_ROWS):
    return [slice(r, r + sub) for r in range(0, tm, sub)]


def _inproj_epilogue(acc, g_ref, o_ref, rows, heads, mode):
    for k, is_head in enumerate(heads):
        lanes = slice(k * LANE, (k + 1) * LANE)
        y = acc[:, lanes]
        if is_head:
            ms = jnp.mean(y * y, axis=-1, keepdims=True)
            y = y * lax.rsqrt(ms + EPS) * g_ref[:, lanes]
        elif mode == "sigmoid":
            y = jax.nn.sigmoid(y)
        o_ref[rows, lanes] = y.astype(o_ref.dtype)


def _inproj_kernel(a_ref, wt_ref, g_ref, o_ref, wb_ref, *, heads, mode):
    @pl.when(pl.program_id(1) == 0)
    def _():
        wb_ref[...] = wt_ref[0].astype(BF16)

    for rows in _row_chunks(a_ref.shape[0], WIDE_SUB_ROWS):
        acc = _qk(a_ref[rows, :], wb_ref[...])
        _inproj_epilogue(acc, g_ref, o_ref, rows, heads, mode)


def _norm1_inproj_kernel(x_ref, n1g_ref, sc_ref, sh_ref, wt_ref, g_ref, o_ref, h_ref, wb_ref, *, heads, mode):
    @pl.when(pl.program_id(0) == 0)
    def _():
        wb_ref[...] = wt_ref[0].astype(BF16)

    for rows in _row_chunks(x_ref.shape[0]):
        h = _modnorm(x_ref[rows, :], n1g_ref[...], sc_ref[0], sh_ref[0]).astype(BF16)
        h_ref[rows, :] = h
        _inproj_epilogue(_qk(h, wb_ref[...]), g_ref, o_ref, rows, heads, mode)


def _norm1_inproj(x2d, n1_gain, mod3, w_in_t, gains, *, T, src0, ncols, heads, mode, out_dtype, tm, name):
    M, K = x2d.shape
    per = T // tm
    assert src0 % 8 == 0 and len(heads) == ncols // LANE
    return pl.pallas_call(
        functools.partial(_norm1_inproj_kernel, heads=heads, mode=mode),
        grid=(M // tm,),
        in_specs=[pl.BlockSpec((tm, K), lambda i: (i, 0)),
                  pl.BlockSpec((1, K), lambda i: (0, 0)),
                  pl.BlockSpec((1, 1, K), lambda i: ((i // per) * 6 + 1, 0, 0)),
                  pl.BlockSpec((1, 1, K), lambda i: ((i // per) * 6 + 0, 0, 0)),
                  pl.BlockSpec((pl.Element(1), pl.Element(ncols), pl.Element(K)), lambda i: (0, src0, 0)),
                  pl.BlockSpec((1, ncols), lambda i: (0, 0))],
        out_specs=[pl.BlockSpec((tm, ncols), lambda i: (i, 0)), pl.BlockSpec((tm, K), lambda i: (i, 0))],
        out_shape=[jax.ShapeDtypeStruct((M, ncols), out_dtype), jax.ShapeDtypeStruct((M, K), BF16)],
        scratch_shapes=[pltpu.VMEM((ncols, K), BF16)],
        compiler_params=_cparams(1),
        name=name,
    )(x2d, n1_gain.reshape(1, K), mod3, mod3, w_in_t, gains)


def _inproj(h2d, w_in_t, gains, *, src0, ncols, heads, mode, out_dtype, tm, tn, name):
    M, K = h2d.shape
    assert src0 % 8 == 0 and ncols % tn == 0 and len(heads) == tn // LANE
    return pl.pallas_call(
        functools.partial(_inproj_kernel, heads=heads, mode=mode),
        grid=(ncols // tn, M // tm),
        in_specs=[pl.BlockSpec((tm, K), lambda j, i: (i, 0)),
                  pl.BlockSpec((pl.Element(1), pl.Element(tn), pl.Element(K)),
                               lambda j, i: (0, pl.multiple_of(src0 + j * tn, 8), 0)),
                  pl.BlockSpec((1, tn), lambda j, i: (0, j))],
        out_specs=pl.BlockSpec((tm, tn), lambda j, i: (i, j)),
        out_shape=jax.ShapeDtypeStruct((M, ncols), out_dtype),
        scratch_shapes=[pltpu.VMEM((tn, K), BF16)],
        compiler_params=_cparams(2),
        name=name,
    )(h2d, w_in_t, gains)


def _inproj_cmp_gate_kernel(a_ref, wc_ref, wg_ref, oc_ref, og_ref, wcb_ref, wgb_ref):
    @pl.when(pl.program_id(0) == 0)
    def _():
        wcb_ref[...] = wc_ref[0].astype(BF16)
        wgb_ref[...] = wg_ref[0].astype(BF16)

    for rows in _row_chunks(a_ref.shape[0]):
        a = a_ref[rows, :]
        oc_ref[rows, :] = _qk(a, wcb_ref[...])
        og_ref[rows, :] = jax.nn.sigmoid(_qk(a, wgb_ref[...]))


def _inproj_cmp_gate(h2d, w_in_t, *, src_cmp, n_cmp, src_gate, tm):
    M, K = h2d.shape
    assert src_cmp % 8 == 0 and src_gate % 8 == 0

    def w_rows(start, n):
        return pl.BlockSpec((pl.Element(1), pl.Element(n), pl.Element(K)), lambda i: (0, start, 0))

    return pl.pallas_call(
        _inproj_cmp_gate_kernel,
        grid=(M // tm,),
        in_specs=[pl.BlockSpec((tm, K), lambda i: (i, 0)), w_rows(src_cmp, n_cmp), w_rows(src_gate, LANE)],
        out_specs=[pl.BlockSpec((tm, n_cmp), lambda i: (i, 0)), pl.BlockSpec((tm, LANE), lambda i: (i, 0))],
        out_shape=[jax.ShapeDtypeStruct((M, n_cmp), F32), jax.ShapeDtypeStruct((M, LANE), F32)],
        scratch_shapes=[pltpu.VMEM((n_cmp, K), BF16), pltpu.VMEM((LANE, K), BF16)],
        compiler_params=_cparams(1),
        name="inproj_cmp_gate",
    )(h2d, w_in_t, w_in_t)


def _bias_kernel(idx_ref, tab_ref, o_ref, *, head0, hpg, rel, mult):
    n_tiles, _, C = idx_ref.shape
    for n in range(n_tiles):
        idx = idx_ref[n]
        for h in range(hpg):
            head = head0 + pl.program_id(0) * hpg + h
            acc = jnp.zeros(idx.shape, F32)
            for b in range(NUM_BUCKETS):
                acc = jnp.where(idx == b, tab_ref[b, head], acc)
            if rel:
                acc = acc - tab_ref[NUM_BUCKETS - 1, head]
            o_ref[0, n, :, h * C:(h + 1) * C] = jnp.where(idx < 0, NEG, acc * mult)


def _bias_tiles(idx, rel_bias, *, head0, groups, hpg, name, rel=False, mult=1.0):
    N, R, C = idx.shape
    return pl.pallas_call(
        functools.partial(_bias_kernel, head0=head0, hpg=hpg, rel=rel, mult=mult),
        grid=(groups,),
        in_specs=[pl.BlockSpec((N, R, C), lambda g: (0, 0, 0)),
                  pl.BlockSpec(memory_space=pltpu.SMEM)],
        out_specs=pl.BlockSpec((1, N, R, hpg * C), lambda g: (g, 0, 0, 0)),
        out_shape=jax.ShapeDtypeStruct((groups, N, R, hpg * C), F32),
        compiler_params=_cparams(1),
        name=name,
    )(jnp.asarray(idx), rel_bias)


def _causal_idx(R):
    r = np.arange(R)[:, None]
    c = np.arange(R)[None, :]
    return np.stack([np.where(r >= c, _t5_bucket_np(r - c), -1), _t5_bucket_np(R + r - c)]).astype(np.int32)


def _window_idx(R):
    r = np.arange(R)[:, None]
    c = np.arange(R)[None, :]
    edge = np.where(r < c, NUM_BUCKETS - 1, -1)
    return np.concatenate([_causal_idx(R), edge[None]]).astype(np.int32)


def _cmp_idx(T):
    per = QB // CMP_STRIDE
    u0 = (T // QB - 1) * per
    assert u0 + LANE <= 2 * LANE
    r = np.arange(QB)[:, None]
    end = (np.arange(2 * LANE)[None, :] - u0) * CMP_STRIDE + CMP_LEN - 1
    return np.where(end <= r, _t5_bucket_np(r - end), -1).astype(np.int32)[None], u0, per


def _compress_kernel(zk_ref, zv_ref, pe_ref, w1_ref, w2_ref, kg_ref, kc_ref, vc_ref):
    half = CMP_LEN // 2

    def one(z_ref, i):
        p1 = jnp.zeros((LANE, HEAD_DIM), F32)
        p2 = jnp.zeros((LANE, HEAD_DIM), F32)
        for l in range(half):
            z = z_ref[0, pl.ds(l, LANE, stride=CMP_STRIDE), :]
            w_lo = w1_ref[i, l * HEAD_DIM:(l + 1) * HEAD_DIM, :].astype(BF16)
            w_hi = w1_ref[i, (half + l) * HEAD_DIM:(half + l + 1) * HEAD_DIM, :].astype(BF16)
            p1 = p1 + jnp.dot((z + pe_ref[i, l:l + 1, :]).astype(BF16), w_lo, preferred_element_type=F32)
            p2 = p2 + jnp.dot((z + pe_ref[i, half + l:half + l + 1, :]).astype(BF16), w_hi,
                              preferred_element_type=F32)
        pre = p1 + pltpu.roll(p2, LANE - 1, axis=0)
        hid = jax.nn.gelu(pre)
        return jnp.dot(hid.astype(BF16), w2_ref[i].astype(BF16), preferred_element_type=F32)

    kc = one(zk_ref, 0)
    ms = jnp.mean(kc * kc, axis=-1, keepdims=True)
    kc_ref[0, 0] = (kc * lax.rsqrt(ms + EPS) * kg_ref[...]).astype(kc_ref.dtype)
    vc_ref[0, 0] = one(zv_ref, 1).astype(vc_ref.dtype)


def _compress(cmp_kv, pe, w1, w2, k_gain0):
    B, T, _ = cmp_kv.shape
    G, dk = NSA_KV_GROUPS, HEAD_DIM
    assert (T - CMP_LEN) // CMP_STRIDE + 1 == LANE - 1
    out = jax.ShapeDtypeStruct((B, G, LANE, dk), BF16)
    return pl.pallas_call(
        _compress_kernel,
        grid=(B, G),
        in_specs=[pl.BlockSpec((1, T, dk), lambda b, g: (b, 0, g)),
                  pl.BlockSpec((1, T, dk), lambda b, g: (b, 0, G + g)),
                  pl.BlockSpec((2, CMP_LEN, dk), lambda b, g: (0, 0, 0)),
                  pl.BlockSpec((2, CMP_LEN * dk, dk), lambda b, g: (0, 0, 0)),
                  pl.BlockSpec((2, dk, dk), lambda b, g: (0, 0, 0)),
                  pl.BlockSpec((1, dk), lambda b, g: (0, 0))],
        out_specs=[pl.BlockSpec((1, 1, LANE, dk), lambda b, g: (b, g, 0, 0))] * 2,
        out_shape=[out, out],
        compiler_params=_cparams(2),
        name="compress",
    )(cmp_kv, cmp_kv, pe, w1, w2, k_gain0.reshape(1, dk))


def _qk(q, k):
    return lax.dot_general(q, k, (((1,), (1,)), ((), ())), preferred_element_type=F32)


def _lane_fold(x, op):
    acc = x[..., :LANE]
    for t in range(1, x.shape[-1] // LANE):
        acc = op(acc, x[..., t * LANE:(t + 1) * LANE])
    return acc


def _softmax_parts(parts):
    m = jnp.max(functools.reduce(jnp.maximum, [_lane_fold(s, jnp.maximum) for s in parts]), axis=-1, keepdims=True)
    ps = [jnp.exp2(s - m) for s in parts]
    den = jnp.sum(functools.reduce(jnp.add, [_lane_fold(p, jnp.add) for p in ps]), axis=-1, keepdims=True)
    return ps, den


def _nsa_t_kernel(q_ref, ks_ref, vs_ref, kw_ref, vw_ref, kc_ref, vc_ref, gate_ref, cb_ref, nb_ref,
                  ovt_ref, expt_ref, o_ref, vst_ref, vwt_ref, vct_ref, ksa_ref, *, cmp_u0, cmp_per):
    H, W = NSA_HPG, NSA_HPG * QB
    T = ks_ref.shape[1]
    ns = ovt_ref.shape[0]
    nwb = WINDOW // QB
    nb = T // QB
    kc = kc_ref[0, 0]

    @pl.when(pl.program_id(2) == 0)
    def _():
        ksa_ref[:, :HEAD_DIM] = ks_ref[0]
        ksa_ref[:, HEAD_DIM:] = expt_ref[...]
        vst_ref[...] = vs_ref[0].astype(F32).T.astype(BF16)
        vwt_ref[...] = vw_ref[0].astype(F32).T.astype(BF16)
        vct_ref[...] = vc_ref[0, 0].astype(F32).T.astype(BF16)

    def softmax_t(parts):
        m = functools.reduce(jnp.maximum, [jnp.max(s, axis=0, keepdims=True) for s in parts])
        ps = [jnp.exp2(s - m) for s in parts]
        den = functools.reduce(jnp.add, [jnp.sum(p, axis=0, keepdims=True) for p in ps])
        return [p.astype(BF16) for p in ps], den

    def attend_task(vt_ref, spans_fn, out, key):
        spans = spans_fn()
        parts = []
        for _, _, k_slab, q_op, add in spans:
            s = _qk(k_slab, q_op)
            parts.append(s if add is None else s + add)
        yield
        ps, den = softmax_t(parts)
        yield
        o = None
        for (a, b, _, _, _), p in zip(spans, ps):
            pv = jnp.dot(vt_ref[:, a:b], p, preferred_element_type=F32)
            o = pv if o is None else o + pv
        out[key] = o / den
        yield

    def cmp_task(i, qs, out):
        lo = i * QB
        u = cmp_u0 - cmp_per * i
        s = _qk(kc, qs) + cb_ref[0, 0, u:u + LANE, :]
        yield
        e = jnp.exp2(s - jnp.max(s, axis=0, keepdims=True))
        if i == 0:
            key_end = lax.broadcasted_iota(jnp.int32, (LANE, W), 0) * CMP_STRIDE + (CMP_LEN - 1)
            query = lax.broadcasted_iota(jnp.int32, (LANE, W), 1) % QB
            e = jnp.where(key_end <= query, e, 0.0)
            den = jnp.sum(e, axis=0, keepdims=True)
            p = e / jnp.where(den > 0.0, den, 1.0)
        else:
            p = e / jnp.sum(e, axis=0, keepdims=True)
        if i >= 1:
            psum = functools.reduce(jnp.add, [p[:, h * QB:(h + 1) * QB] for h in range(H)])
            imp_t = jnp.dot(ovt_ref[...], psum, precision=lax.Precision.HIGHEST, preferred_element_type=F32)
        yield
        out["cmp", i] = jnp.dot(vct_ref[...], p.astype(BF16), preferred_element_type=F32)
        if i >= 1:
            blk = lax.broadcasted_iota(jnp.int32, (ns, QB), 0)
            cur = (lo + lax.broadcasted_iota(jnp.int32, (ns, QB), 1)) // SEL_BLOCK
            forced = (blk == 0) | (blk == cur) | (blk == cur - 1)
            score = jnp.where(forced, 1e4, jnp.where(blk <= cur, imp_t, -1e4))
            rank = jnp.zeros((ns, QB), F32)
            for b in range(ns):
                other = score[b:b + 1, :]
                rank = rank + jnp.where(blk > b, jnp.where(other >= score, 1.0, 0.0),
                                        jnp.where(other > score, 1.0, 0.0))
            unsel = jnp.where(rank < float(min(N_SEL, ns)), 0.0, 1.0)
            unsel_q = jnp.concatenate([unsel, jnp.zeros((LANE - ns, QB), F32)], axis=0).T.astype(BF16)
            out["qs_aug", i] = jnp.concatenate([qs, jnp.concatenate([unsel_q] * H, axis=0)], axis=1)
        yield

    def slc_spans(i, qs, out):
        lo, hi = i * QB, (i + 1) * QB
        spans = [(lo, hi, ks_ref[0, lo:hi, :], qs, nb_ref[0, 0])]
        if i >= 1:
            qs_aug = out["qs_aug", i]
            spans.insert(0, (lo - QB, lo, ksa_ref[lo - QB:lo, :], qs_aug, nb_ref[0, 1]))
            if i >= 2:
                spans.insert(0, (0, lo - QB, ksa_ref[0:lo - QB, :], qs_aug, None))
        return spans

    def win_spans(i, qs):
        lo, hi = i * QB, (i + 1) * QB

        def span(a, b, add):
            return (a, b, kw_ref[0, a:b, :], qs, add)

        spans = []
        if i >= nwb:
            spans.append(span((i - nwb) * QB, (i - nwb + 1) * QB, nb_ref[0, 2]))
        mid_a, mid_b = max(i - nwb + 1, 0) * QB, (i - 1) * QB
        if mid_b > mid_a:
            spans.append(span(mid_a, mid_b, None))
        if i >= 1:
            spans.append(span(lo - QB, lo, nb_ref[0, 1]))
        spans.append(span(lo, hi, nb_ref[0, 0]))
        return spans

    def combine(i, out):
        lo, hi = i * QB, (i + 1) * QB
        gate_t = gate_ref[0, lo:hi, :].T
        first = pl.program_id(1) == 0

        def grow(br):
            rows = []
            for h in range(H):
                c = 3 * h + br
                rows.append(jnp.where(first, gate_t[c:c + 1, :], gate_t[3 * H + c:3 * H + c + 1, :]))
            return jnp.concatenate(rows, axis=1)

        o_t = grow(0) * out["cmp", i] + grow(1) * out["slc", i] + grow(2) * out["win", i]
        o_ref[0, lo:hi, :] = jnp.concatenate([o_t[:, h * QB:(h + 1) * QB].T for h in range(H)],
                                             axis=1).astype(o_ref.dtype)

    def pair(k):
        out = {}
        tasks = []
        blocks = (nb - 1 - k, k)
        qss = {}
        for i in blocks:
            q = q_ref[0, i * QB:(i + 1) * QB, :]
            qss[i] = jnp.concatenate([q[:, h * HEAD_DIM:(h + 1) * HEAD_DIM] for h in range(H)], axis=0)
            tasks.append(cmp_task(i, qss[i], out))
            tasks.append(attend_task(vwt_ref, functools.partial(win_spans, i, qss[i]), out, ("win", i)))
        for i in blocks:
            tasks.append(attend_task(vst_ref, functools.partial(slc_spans, i, qss[i], out), out, ("slc", i)))
        n_stage = 3
        for step in range(len(tasks) + n_stage - 1):
            for t in range(step - n_stage + 1, step + 1):
                if 0 <= t < len(tasks):
                    next(tasks[t])
        for i in blocks:
            combine(i, out)

    for k in range(nb // 2):
        pl.when(pl.program_id(2) == k)(functools.partial(pair, k))


def _nsa_t(qn, kv, kc, vc, gates, cbias, cmp_u0, cmp_per, nbias):
    B, T, _ = qn.shape
    G, H, dk = NSA_KV_GROUPS, NSA_HPG, HEAD_DIM
    nb = T // QB
    ns = T // SEL_BLOCK
    cstart = np.arange(LANE) * CMP_STRIDE
    sstart = np.arange(ns) * SEL_BLOCK
    overlap = np.clip(np.minimum(cstart[:, None] + CMP_LEN, sstart[None, :] + SEL_BLOCK)
                      - np.maximum(cstart[:, None], sstart[None, :]), 0, None) / CMP_STRIDE
    overlap[LANE - 1:] = 0.0
    ovt = jnp.asarray(overlap.T, F32)
    expand_t = np.zeros((T, LANE), np.float32)
    expand_t[np.arange(T), np.arange(T) // SEL_BLOCK] = MASK_BIG
    return pl.pallas_call(
        functools.partial(_nsa_t_kernel, cmp_u0=cmp_u0, cmp_per=cmp_per),
        grid=(B, G, nb // 2),
        in_specs=[pl.BlockSpec((1, T, H * dk), lambda b, g, i: (b, 0, g)),
                  pl.BlockSpec((1, T, dk), lambda b, g, i: (b, 0, g)),
                  pl.BlockSpec((1, T, dk), lambda b, g, i: (b, 0, G + g)),
                  pl.BlockSpec((1, T, dk), lambda b, g, i: (b, 0, 2 * G + g)),
                  pl.BlockSpec((1, T, dk), lambda b, g, i: (b, 0, 3 * G + g)),
                  pl.BlockSpec((1, 1, LANE, dk), lambda b, g, i: (b, g, 0, 0)),
                  pl.BlockSpec((1, 1, LANE, dk), lambda b, g, i: (b, g, 0, 0)),
                  pl.BlockSpec((1, T, LANE), lambda b, g, i: (b, 0, 0)),
                  pl.BlockSpec((1, 1, 2 * LANE, H * QB), lambda b, g, i: (g, 0, 0, 0)),
                  pl.BlockSpec((1, 3, QB, H * QB), lambda b, g, i: (g, 0, 0, 0)),
                  pl.BlockSpec((ns, LANE), lambda b, g, i: (0, 0)),
                  pl.BlockSpec((T, LANE), lambda b, g, i: (0, 0))],
        out_specs=pl.BlockSpec((1, T, H * dk), lambda b, g, i: (b, 0, g)),
        out_shape=jax.ShapeDtypeStruct((B, T, NSA_HEADS * dk), BF16),
        scratch_shapes=[pltpu.VMEM((dk, T), BF16), pltpu.VMEM((dk, T), BF16), pltpu.VMEM((dk, LANE), BF16),
                        pltpu.VMEM((T, 2 * dk), BF16)],
        compiler_params=_cparams(3),
        name="nsa",
    )(qn, kv, kv, kv, kv, kc, vc, gates, cbias, nbias, ovt, jnp.asarray(expand_t, BF16))


def _diff_kernel(q_ref, k_ref, v_ref, lq_ref, lk_ref, sg_ref, db_ref, o_ref):
    dk = HEAD_DIM
    T = k_ref.shape[1]
    lqk = lq_ref[...] * lk_ref[...]
    lam = (jnp.exp(jnp.sum(lqk[0:1], axis=-1, keepdims=True))
           - jnp.exp(jnp.sum(lqk[1:2], axis=-1, keepdims=True)) + LAM_INIT)
    def task(i, mm, out):
        lo, hi = i * DQB, (i + 1) * DQB
        cols = slice(mm * dk, (mm + 1) * dk)
        q = q_ref[0, lo:hi, cols]
        bounds, parts = [], []
        if i >= 2:
            bounds.append((0, lo - DQB))
            parts.append(_qk(q, k_ref[0, 0:lo - DQB, cols]))
        if i >= 1:
            bounds.append((lo - DQB, lo))
            parts.append(_qk(q, k_ref[0, lo - DQB:lo, cols]) + db_ref[0, 1])
        bounds.append((lo, hi))
        parts.append(_qk(q, k_ref[0, lo:hi, cols]) + db_ref[0, 0])
        yield
        ps, den = _softmax_parts(parts)
        yield
        o = None
        for (a, b), p in zip(bounds, ps):
            pv = jnp.dot(p.astype(BF16), v_ref[0, a:b, :], preferred_element_type=F32)
            o = pv if o is None else o + pv
        out[i, mm] = o / den
        yield

    out = {}
    tasks = [task(i, mm, out) for i in reversed(range(T // DQB)) for mm in range(2)]
    n_stage = 3
    for step in range(len(tasks) + n_stage - 1):
        for t in range(step - n_stage + 1, step + 1):
            if 0 <= t < len(tasks):
                next(tasks[t])

    for i in range(T // DQB):
        lo, hi = i * DQB, (i + 1) * DQB
        o = out[i, 0] - lam * out[i, 1]
        ms = jnp.mean(o * o, axis=-1, keepdims=True)
        o_ref[0, lo:hi, :] = ((o * lax.rsqrt(ms + EPS) * sg_ref[...]) * (1.0 - LAM_INIT)).astype(o_ref.dtype)


def _diff(dqk, dv, lam_q, lam_k, subln_gain, dbias):
    B, T, _ = dqk.shape
    Hd, dk = DIFF_HEADS, HEAD_DIM
    w = 2 * dk
    return pl.pallas_call(
        _diff_kernel,
        grid=(B, Hd),
        in_specs=[pl.BlockSpec((1, T, w), lambda b, h: (b, 0, h)),
                  pl.BlockSpec((1, T, w), lambda b, h: (b, 0, Hd + h)),
                  pl.BlockSpec((1, T, w), lambda b, h: (b, 0, h)),
                  pl.BlockSpec((2, dk), lambda b, h: (0, 0)),
                  pl.BlockSpec((2, dk), lambda b, h: (0, 0)),
                  pl.BlockSpec((1, w), lambda b, h: (0, 0)),
                  pl.BlockSpec((1, 2, DQB, DQB), lambda b, h: (h, 0, 0, 0))],
        out_specs=pl.BlockSpec((1, T, w), lambda b, h: (b, 0, h)),
        out_shape=jax.ShapeDtypeStruct((B, T, Hd * w), BF16),
        compiler_params=_cparams(2),
        name="diff",
    )(dqk, dqk, dv, lam_q, lam_k, subln_gain.reshape(1, w), dbias)


def _merge_kernel(an_ref, ad_ref, wn_ref, wd_ref, gn_ref, gd_ref, o_ref, wnb_ref, wdb_ref):
    @pl.when(pl.program_id(1) == 0)
    def _():
        wnb_ref[...] = wn_ref[...].astype(BF16)
        wdb_ref[...] = wd_ref[...].astype(BF16)

    for rows in _row_chunks(an_ref.shape[0], WIDE_SUB_ROWS):
        yn = jnp.dot(an_ref[rows, :], wnb_ref[...], preferred_element_type=F32)
        yd = jnp.dot(ad_ref[rows, :], wdb_ref[...], preferred_element_type=F32)
        o_ref[rows, :] = (gn_ref[rows, :].astype(F32) * yn + gd_ref[rows, :].astype(F32) * yd).astype(o_ref.dtype)


def _merge(o_nsa, o_diff, w_n, w_d, mg, tm=1024, tn=1024):
    M, K = o_nsa.shape
    N = w_n.shape[1]
    nj = N // tn
    return pl.pallas_call(
        _merge_kernel,
        grid=(nj, M // tm),
        in_specs=[pl.BlockSpec((tm, K), lambda j, i: (i, 0)),
                  pl.BlockSpec((tm, K), lambda j, i: (i, 0)),
                  pl.BlockSpec((K, tn), lambda j, i: (0, j)),
                  pl.BlockSpec((K, tn), lambda j, i: (0, j)),
                  pl.BlockSpec((tm, tn), lambda j, i: (i, j)),
                  pl.BlockSpec((tm, tn), lambda j, i: (i, nj + j))],
        out_specs=pl.BlockSpec((tm, tn), lambda j, i: (i, j)),
        out_shape=jax.ShapeDtypeStruct((M, N), BF16),
        scratch_shapes=[pltpu.VMEM((K, tn), BF16), pltpu.VMEM((K, tn), BF16)],
        compiler_params=_cparams(2),
        name="merge",
    )(o_nsa, o_diff, w_n, w_d, mg, mg)


def _oproj_kernel(a_ref, w_ref, x_ref, g1_ref, gain_ref, sc_ref, sh_ref, x1_ref, h2_ref):
    for rows in _row_chunks(a_ref.shape[0]):
        y = jnp.dot(a_ref[rows, :], w_ref[...], preferred_element_type=F32)
        x1 = x_ref[rows, :] + g1_ref[0] * y
        x1_ref[rows, :] = x1
        h2_ref[rows, :] = _modnorm(x1, gain_ref[...], sc_ref[0], sh_ref[0]).astype(h2_ref.dtype)


def _oproj(merged, w_o, x2d, mod3, gain2, T, tm=512):
    M, D = x2d.shape
    per = T // tm
    return pl.pallas_call(
        _oproj_kernel,
        grid=(M // tm,),
        in_specs=[pl.BlockSpec((tm, D), lambda i: (i, 0)),
                  pl.BlockSpec((D, D), lambda i: (0, 0)),
                  pl.BlockSpec((tm, D), lambda i: (i, 0)),
                  pl.BlockSpec((1, 1, D), lambda i: ((i // per) * 6 + 2, 0, 0)),
                  pl.BlockSpec((1, D), lambda i: (0, 0)),
                  pl.BlockSpec((1, 1, D), lambda i: ((i // per) * 6 + 4, 0, 0)),
                  pl.BlockSpec((1, 1, D), lambda i: ((i // per) * 6 + 3, 0, 0))],
        out_specs=[pl.BlockSpec((tm, D), lambda i: (i, 0)),
                   pl.BlockSpec((tm, D), lambda i: (i, 0))],
        out_shape=[jax.ShapeDtypeStruct((M, D), F32), jax.ShapeDtypeStruct((M, D), BF16)],
        compiler_params=_cparams(1),
        name="oproj",
    )(merged, w_o, x2d, mod3, gain2.reshape(1, D), mod3, mod3)


def _ffn_up_kernel(h_ref, wa_ref, wv_ref, cwa_ref, cwv_ref, cba_ref, cbv_ref, o_ref, wab_ref, wvb_ref,
                   ca_ref, cv_ref, sa_ref, sv_ref, *, per):
    i = pl.program_id(1)

    @pl.when(i == 0)
    def _():
        wab_ref[...] = wa_ref[...].astype(BF16)
        wvb_ref[...] = wv_ref[...].astype(BF16)

    @pl.when(i % per == 0)
    def _():
        ca_ref[...] = jnp.zeros(ca_ref.shape, F32)
        cv_ref[...] = jnp.zeros(cv_ref.shape, F32)

    def conv(u, prev, cw_ref, cb_ref, s_ref):
        s_ref[0:8, :] = prev
        s_ref[8:8 + FFN_SUB_ROWS, :] = u
        u1 = s_ref[7:7 + FFN_SUB_ROWS, :]
        u2 = s_ref[6:6 + FFN_SUB_ROWS, :]
        return cb_ref[...] + cw_ref[0:1, :] * u2 + cw_ref[1:2, :] * u1 + cw_ref[2:3, :] * u

    prev_a, prev_v = ca_ref[...], cv_ref[...]
    for n, rows in enumerate(_row_chunks(h_ref.shape[0], FFN_SUB_ROWS)):
        hs = h_ref[rows, :]
        ua = jnp.dot(hs, wab_ref[...], preferred_element_type=F32)
        uv = jnp.dot(hs, wvb_ref[...], preferred_element_type=F32)
        a = conv(ua, prev_a, cwa_ref, cba_ref, sa_ref.at[n % 2])
        val = conv(uv, prev_v, cwv_ref, cbv_ref, sv_ref.at[n % 2])
        o_ref[rows, :] = (a * jax.nn.sigmoid(a) * val).astype(o_ref.dtype)
        prev_a, prev_v = ua[FFN_SUB_ROWS - 8:, :], uv[FFN_SUB_ROWS - 8:, :]
    ca_ref[...] = prev_a
    cv_ref[...] = prev_v


def _ffn_up(h2, w_up, conv_w, conv_b, T, tm=2048, tn=512):
    M, D = h2.shape
    F = w_up.shape[1] // 2
    nj = F // tn
    cb = conv_b.reshape(1, 2 * F)
    return pl.pallas_call(
        functools.partial(_ffn_up_kernel, per=T // tm),
        grid=(nj, M // tm),
        in_specs=[pl.BlockSpec((tm, D), lambda j, i: (i, 0)),
                  pl.BlockSpec((D, tn), lambda j, i: (0, j)),
                  pl.BlockSpec((D, tn), lambda j, i: (0, nj + j)),
                  pl.BlockSpec((3, tn), lambda j, i: (0, j)),
                  pl.BlockSpec((3, tn), lambda j, i: (0, nj + j)),
                  pl.BlockSpec((1, tn), lambda j, i: (0, j)),
                  pl.BlockSpec((1, tn), lambda j, i: (0, nj + j))],
        out_specs=pl.BlockSpec((tm, tn), lambda j, i: (i, j)),
        out_shape=jax.ShapeDtypeStruct((M, F), BF16),
        scratch_shapes=[pltpu.VMEM((D, tn), BF16), pltpu.VMEM((D, tn), BF16),
                        pltpu.VMEM((8, tn), F32), pltpu.VMEM((8, tn), F32),
                        pltpu.VMEM((2, FFN_SUB_ROWS + 8, tn), F32), pltpu.VMEM((2, FFN_SUB_ROWS + 8, tn), F32)],
        compiler_params=_cparams(2),
        name="ffn_up",
    )(h2, w_up, w_up, conv_w, conv_w, cb, cb)


def _ffn_down_kernel(a_ref, w_ref, x_ref, g2_ref, o_ref, wb_ref):
    @pl.when(pl.program_id(1) == 0)
    def _():
        wb_ref[...] = w_ref[...].astype(BF16)

    for rows in _row_chunks(a_ref.shape[0]):
        y = jnp.dot(a_ref[rows, :], wb_ref[...], preferred_element_type=F32)
        o_ref[rows, :] = x_ref[rows, :] + g2_ref[0] * y


def _ffn_down(act, w_down, x1, mod3, T, tm=512, tn=512):
    M, F = act.shape
    D = w_down.shape[1]
    per = T // tm
    return pl.pallas_call(
        _ffn_down_kernel,
        grid=(D // tn, M // tm),
        in_specs=[pl.BlockSpec((tm, F), lambda j, i: (i, 0)),
                  pl.BlockSpec((F, tn), lambda j, i: (0, j)),
                  pl.BlockSpec((tm, tn), lambda j, i: (i, j)),
                  pl.BlockSpec((1, 1, tn), lambda j, i: ((i // per) * 6 + 5, 0, j))],
        out_specs=pl.BlockSpec((tm, tn), lambda j, i: (i, j)),
        out_shape=jax.ShapeDtypeStruct((M, D), F32),
        scratch_shapes=[pltpu.VMEM((F, tn), BF16)],
        compiler_params=_cparams(2),
        name="ffn_down",
    )(act, w_down, x1, mod3)


def _layer(x, c, w_ada, b_ada, norm1_gain, norm2_gain, w_in, nsa_q_gain, nsa_k_gain, cmp_pe, cmp_w1, cmp_w2,
           diff_q_gain, diff_k_gain, diff_lambda_q, diff_lambda_k, diff_subln_gain, w_nsa_out, w_diff_out, w_o,
           w_ffn_up, ffn_conv_w, ffn_conv_b, w_ffn_down, rel_bias):
    B, T, D = x.shape
    dk, G = HEAD_DIM, NSA_KV_GROUPS
    M = B * T
    scale = dk ** -0.5

    n_q = NSA_HEADS * dk
    o_kv = n_q
    o_g = o_kv + 3 * 2 * G * dk
    o_dq = o_g + NSA_HEADS * 3
    o_dk = o_dq + DIFF_HEADS * 2 * dk
    o_dv = o_dk + DIFF_HEADS * 2 * dk
    o_mg = o_dv + DIFF_HEADS * 2 * dk
    n_kv = 2 * G * dk
    n_dqk = 2 * DIFF_HEADS * 2 * dk

    mod3 = _ada(c, w_ada, b_ada).reshape(B * 6, 1, D)

    ones = jnp.ones((n_kv // 2,), F32)
    g_q = jnp.tile(nsa_q_gain * (scale * LOG2E), NSA_HEADS).reshape(1, n_q)
    g_kv = jnp.concatenate([jnp.tile(nsa_k_gain[1], G), ones, jnp.tile(nsa_k_gain[2], G), ones]).reshape(1, 2 * n_kv)
    g_dqk = jnp.concatenate([jnp.tile(diff_q_gain * (scale * LOG2E), 2 * DIFF_HEADS),
                             jnp.tile(diff_k_gain, 2 * DIFF_HEADS)]).reshape(1, n_dqk)
    g_one = jnp.ones((1, 2 * D), F32)
    w_in_t = jnp.swapaxes(w_in, 1, 2)
    wide = INPROJ_TN
    yes, no = (True,) * (wide // LANE), (False,) * (wide // LANE)
    qn, h = _norm1_inproj(x.reshape(M, D), norm1_gain, mod3, w_in_t, g_q, T=T, src0=0, ncols=n_q,
                          heads=(True,) * NSA_HEADS, mode="raw", out_dtype=BF16, tm=INPROJ_TM // 2, name="inproj_q")
    proj = functools.partial(_inproj, h, w_in_t, tm=INPROJ_TM)
    cmpkv, gates = _inproj_cmp_gate(h, w_in_t, src_cmp=o_kv, n_cmp=n_kv, src_gate=o_g, tm=INPROJ_TM)
    kv_heads = ((True,) * G + (False,) * G) * 2
    kv = proj(g_kv, src0=o_kv + n_kv, ncols=2 * n_kv, heads=kv_heads, mode="raw", out_dtype=BF16, tn=2 * n_kv,
              name="inproj_kv")
    dqk = proj(g_dqk, src0=o_dq, ncols=n_dqk, heads=yes, mode="raw", out_dtype=BF16, tn=wide, name="inproj_dqk")
    dv = proj(g_one, src0=o_dv, ncols=o_mg - o_dv, heads=no, mode="raw", out_dtype=BF16, tn=wide, name="inproj_dv")
    mgate = proj(g_one, src0=o_mg, ncols=2 * D, heads=no, mode="sigmoid", out_dtype=BF16, tn=wide, name="inproj_mg")

    nbias = _bias_tiles(_window_idx(QB).transpose(0, 2, 1), rel_bias, head0=0, groups=G, hpg=NSA_HPG,
                        name="bias_nsa", rel=True, mult=LOG2E)
    cmp_idx, cmp_u0, cmp_per = _cmp_idx(T)
    cbias = _bias_tiles(cmp_idx.transpose(0, 2, 1), rel_bias, head0=0, groups=G, hpg=NSA_HPG, name="bias_cmp",
                        mult=LOG2E)
    dbias = _bias_tiles(_causal_idx(DQB), rel_bias, head0=NSA_HEADS, groups=DIFF_HEADS, hpg=1, name="bias_diff",
                        rel=True, mult=LOG2E)

    kc, vc = _compress(cmpkv.reshape(B, T, n_kv), cmp_pe, cmp_w1, cmp_w2, nsa_k_gain[0])
    o_nsa = _nsa_t(qn.reshape(B, T, n_q), kv.reshape(B, T, 2 * n_kv), kc, vc, gates.reshape(B, T, LANE), cbias,
                   cmp_u0, cmp_per, nbias)
    o_diff = _diff(dqk.reshape(B, T, n_dqk), dv.reshape(B, T, -1), diff_lambda_q, diff_lambda_k, diff_subln_gain,
                   dbias)

    merged = _merge(o_nsa.reshape(M, -1), o_diff.reshape(M, -1), w_nsa_out, w_diff_out, mgate)
    x1, h2 = _oproj(merged, w_o.astype(BF16), x.reshape(M, D), mod3, norm2_gain, T)
    act = _ffn_up(h2, w_ffn_up, ffn_conv_w, ffn_conv_b, T)
    out = _ffn_down(act, w_ffn_down, x1, mod3, T)
    return out.reshape(B, T, D)


def kernel(x, c, w_ada, b_ada, norm1_gain, norm2_gain, w_in, nsa_q_gain, nsa_k_gain, cmp_pe, cmp_w1, cmp_w2,
           diff_q_gain, diff_k_gain, diff_lambda_q, diff_lambda_k, diff_subln_gain, w_nsa_out, w_diff_out, w_o,
           w_ffn_up, ffn_conv_w, ffn_conv_b, w_ffn_down, rel_bias):
    return _layer(x, c, w_ada[0], b_ada[0], norm1_gain[0], norm2_gain[0], w_in, nsa_q_gain[0], nsa_k_gain[0],
                  cmp_pe[0], cmp_w1[0], cmp_w2[0], diff_q_gain[0], diff_k_gain[0], diff_lambda_q[0],
                  diff_lambda_k[0], diff_subln_gain[0], w_nsa_out[0], w_diff_out[0], w_o[0], w_ffn_up[0],
                  ffn_conv_w[0], ffn_conv_b[0], w_ffn_down[0], rel_bias)
```

```python
import functools
import math

import numpy as np
import jax
import jax.numpy as jnp
from jax import lax
from jax.experimental import pallas as pl
from jax.experimental.pallas import tpu as pltpu

F32 = jnp.float32
BF16 = jnp.bfloat16

HEAD_DIM = 128
NSA_HEADS = 8
NSA_KV_GROUPS = 2
NSA_HPG = NSA_HEADS // NSA_KV_GROUPS
CMP_LEN = 32
CMP_STRIDE = 16
SEL_BLOCK = 64
N_SEL = 16
WINDOW = 512
DIFF_HEADS = 4
NUM_BUCKETS = 32
MAX_DISTANCE = 128
EPS = 1e-6
NEG = -1e30
LAM_INIT = 0.8 - 0.6 * math.exp(-0.3 * 0)
LOG2E = math.log2(math.e)
MASK_BIG = -(2.0 ** 100)

LANE = 128
QB = 128
DQB = 256
SUB_ROWS = 256
MERGE_SUB_ROWS = 512
INPROJ_TM = 2048
INPROJ_TN = 1024
VMEM_LIMIT = 56 * 1024 * 1024


def _cparams(n_axes):
    return pltpu.CompilerParams(dimension_semantics=("arbitrary",) * n_axes,
                                vmem_limit_bytes=VMEM_LIMIT)


def _t5_bucket_np(dist):
    n = np.maximum(np.asarray(dist, np.int32), 0)
    max_exact = NUM_BUCKETS // 2
    nf = np.maximum(n, max_exact).astype(np.float32)
    large = max_exact + (np.log(nf / np.float32(max_exact)) / np.float32(math.log(MAX_DISTANCE / max_exact))
                         * np.float32(NUM_BUCKETS - max_exact)).astype(np.int32)
    large = np.minimum(large, NUM_BUCKETS - 1)
    return np.where(n < max_exact, n, large).astype(np.int32)


def _ada_kernel(ct_ref, w_ref, b_ref, o_ref):
    ct = ct_ref[...]
    s = ct * jax.nn.sigmoid(ct)
    w = w_ref[...]
    for b in range(ct.shape[1]):
        o_ref[b:b + 1, :] = jnp.sum(w * s[:, b:b + 1], axis=0, keepdims=True) + b_ref[...]


def _ada(c, w_ada, b_ada, tn=1024):
    B, D = c.shape
    N = w_ada.shape[1]
    return pl.pallas_call(
        _ada_kernel,
        grid=(N // tn,),
        in_specs=[pl.BlockSpec((D, B), lambda j: (0, 0)),
                  pl.BlockSpec((D, tn), lambda j: (0, j)),
                  pl.BlockSpec((1, tn), lambda j: (0, j))],
        out_specs=pl.BlockSpec((B, tn), lambda j: (0, j)),
        out_shape=jax.ShapeDtypeStruct((B, N), F32),
        compiler_params=_cparams(1),
        name="ada",
    )(c.T, w_ada, b_ada.reshape(1, N))


def _modnorm(x, gain, sc, sh):
    ms = jnp.mean(x * x, axis=-1, keepdims=True)
    return (x * lax.rsqrt(ms + EPS) * gain) * (1.0 + sc) + sh


def _row_chunks(tm, sub=SUB_ROWS):
    return [slice(r, r + sub) for r in range(0, tm, sub)]


def _inproj_epilogue(acc, g_ref, o_ref, rows, heads, mode):
    for k, is_head in enumerate(heads):
        lanes = slice(k * LANE, (k + 1) * LANE)
        y = acc[:, lanes]
        if is_head:
            ms = jnp.mean(y * y, axis=-1, keepdims=True)
            y = y * lax.rsqrt(ms + EPS) * g_ref[:, lanes]
        elif mode == "sigmoid":
            y = jax.nn.sigmoid(y)
        o_ref[rows, lanes] = y.astype(o_ref.dtype)


def _inproj_kernel(a_ref, wt_ref, g_ref, o_ref, wb_ref, *, heads, mode):
    @pl.when(pl.program_id(1) == 0)
    def _():
        wb_ref[...] = wt_ref[0].astype(BF16)

    for rows in _row_chunks(a_ref.shape[0]):
        acc = _qk(a_ref[rows, :], wb_ref[...])
        _inproj_epilogue(acc, g_ref, o_ref, rows, heads, mode)


def _norm1_inproj_kernel(x_ref, n1g_ref, sc_ref, sh_ref, wt_ref, g_ref, o_ref, h_ref, wb_ref, *, heads, mode):
    @pl.when(pl.program_id(0) == 0)
    def _():
        wb_ref[...] = wt_ref[0].astype(BF16)

    for rows in _row_chunks(x_ref.shape[0]):
        h = _modnorm(x_ref[rows, :], n1g_ref[...], sc_ref[0], sh_ref[0]).astype(BF16)
        h_ref[rows, :] = h
        _inproj_epilogue(_qk(h, wb_ref[...]), g_ref, o_ref, rows, heads, mode)


def _norm1_inproj(x2d, n1_gain, mod3, w_in_t, gains, *, T, src0, ncols, heads, mode, out_dtype, tm, name):
    M, K = x2d.shape
    per = T // tm
    assert src0 % 8 == 0 and len(heads) == ncols // LANE
    return pl.pallas_call(
        functools.partial(_norm1_inproj_kernel, heads=heads, mode=mode),
        grid=(M // tm,),
        in_specs=[pl.BlockSpec((tm, K), lambda i: (i, 0)),
                  pl.BlockSpec((1, K), lambda i: (0, 0)),
                  pl.BlockSpec((1, 1, K), lambda i: ((i // per) * 6 + 1, 0, 0)),
                  pl.BlockSpec((1, 1, K), lambda i: ((i // per) * 6 + 0, 0, 0)),
                  pl.BlockSpec((pl.Element(1), pl.Element(ncols), pl.Element(K)), lambda i: (0, src0, 0)),
                  pl.BlockSpec((1, ncols), lambda i: (0, 0))],
        out_specs=[pl.BlockSpec((tm, ncols), lambda i: (i, 0)), pl.BlockSpec((tm, K), lambda i: (i, 0))],
        out_shape=[jax.ShapeDtypeStruct((M, ncols), out_dtype), jax.ShapeDtypeStruct((M, K), BF16)],
        scratch_shapes=[pltpu.VMEM((ncols, K), BF16)],
        compiler_params=_cparams(1),
        name=name,
    )(x2d, n1_gain.reshape(1, K), mod3, mod3, w_in_t, gains)


def _inproj(h2d, w_in_t, gains, *, src0, ncols, heads, mode, out_dtype, tm, tn, name):
    M, K = h2d.shape
    assert src0 % 8 == 0 and ncols % tn == 0 and len(heads) == tn // LANE
    return pl.pallas_call(
        functools.partial(_inproj_kernel, heads=heads, mode=mode),
        grid=(ncols // tn, M // tm),
        in_specs=[pl.BlockSpec((tm, K), lambda j, i: (i, 0)),
                  pl.BlockSpec((pl.Element(1), pl.Element(tn), pl.Element(K)),
                               lambda j, i: (0, pl.multiple_of(src0 + j * tn, 8), 0)),
                  pl.BlockSpec((1, tn), lambda j, i: (0, j))],
        out_specs=pl.BlockSpec((tm, tn), lambda j, i: (i, j)),
        out_shape=jax.ShapeDtypeStruct((M, ncols), out_dtype),
        scratch_shapes=[pltpu.VMEM((tn, K), BF16)],
        compiler_params=_cparams(2),
        name=name,
    )(h2d, w_in_t, gains)


def _inproj_cmp_gate_kernel(a_ref, wc_ref, wg_ref, oc_ref, og_ref, wcb_ref, wgb_ref):
    @pl.when(pl.program_id(0) == 0)
    def _():
        wcb_ref[...] = wc_ref[0].astype(BF16)
        wgb_ref[...] = wg_ref[0].astype(BF16)

    for rows in _row_chunks(a_ref.shape[0]):
        a = a_ref[rows, :]
        oc_ref[rows, :] = _qk(a, wcb_ref[...])
        og_ref[rows, :] = jax.nn.sigmoid(_qk(a, wgb_ref[...]))


def _inproj_cmp_gate(h2d, w_in_t, *, src_cmp, n_cmp, src_gate, tm):
    M, K = h2d.shape
    assert src_cmp % 8 == 0 and src_gate % 8 == 0

    def w_rows(start, n):
        return pl.BlockSpec((pl.Element(1), pl.Element(n), pl.Element(K)), lambda i: (0, start, 0))

    return pl.pallas_call(
        _inproj_cmp_gate_kernel,
        grid=(M // tm,),
        in_specs=[pl.BlockSpec((tm, K), lambda i: (i, 0)), w_rows(src_cmp, n_cmp), w_rows(src_gate, LANE)],
        out_specs=[pl.BlockSpec((tm, n_cmp), lambda i: (i, 0)), pl.BlockSpec((tm, LANE), lambda i: (i, 0))],
        out_shape=[jax.ShapeDtypeStruct((M, n_cmp), F32), jax.ShapeDtypeStruct((M, LANE), F32)],
        scratch_shapes=[pltpu.VMEM((n_cmp, K), BF16), pltpu.VMEM((LANE, K), BF16)],
        compiler_params=_cparams(1),
        name="inproj_cmp_gate",
    )(h2d, w_in_t, w_in_t)


def _bias_kernel(idx_ref, tab_ref, o_ref, *, head0, hpg, rel, mult):
    n_tiles, _, C = idx_ref.shape
    for n in range(n_tiles):
        idx = idx_ref[n]
        for h in range(hpg):
            head = head0 + pl.program_id(0) * hpg + h
            acc = jnp.zeros(idx.shape, F32)
            for b in range(NUM_BUCKETS):
                acc = jnp.where(idx == b, tab_ref[b, head], acc)
            if rel:
                acc = acc - tab_ref[NUM_BUCKETS - 1, head]
            o_ref[0, n, :, h * C:(h + 1) * C] = jnp.where(idx < 0, NEG, acc * mult)


def _bias_tiles(idx, rel_bias, *, head0, groups, hpg, name, rel=False, mult=1.0):
    N, R, C = idx.shape
    return pl.pallas_call(
        functools.partial(_bias_kernel, head0=head0, hpg=hpg, rel=rel, mult=mult),
        grid=(groups,),
        in_specs=[pl.BlockSpec((N, R, C), lambda g: (0, 0, 0)),
                  pl.BlockSpec(memory_space=pltpu.SMEM)],
        out_specs=pl.BlockSpec((1, N, R, hpg * C), lambda g: (g, 0, 0, 0)),
        out_shape=jax.ShapeDtypeStruct((groups, N, R, hpg * C), F32),
        compiler_params=_cparams(1),
        name=name,
    )(jnp.asarray(idx), rel_bias)


def _causal_idx(R):
    r = np.arange(R)[:, None]
    c = np.arange(R)[None, :]
    return np.stack([np.where(r >= c, _t5_bucket_np(r - c), -1), _t5_bucket_np(R + r - c)]).astype(np.int32)


def _window_idx(R):
    r = np.arange(R)[:, None]
    c = np.arange(R)[None, :]
    edge = np.where(r < c, NUM_BUCKETS - 1, -1)
    return np.concatenate([_causal_idx(R), edge[None]]).astype(np.int32)


def _cmp_idx(T):
    per = QB // CMP_STRIDE
    u0 = (T // QB - 1) * per
    assert u0 + LANE <= 2 * LANE
    r = np.arange(QB)[:, None]
    end = (np.arange(2 * LANE)[None, :] - u0) * CMP_STRIDE + CMP_LEN - 1
    return np.where(end <= r, _t5_bucket_np(r - end), -1).astype(np.int32)[None], u0, per


def _compress_kernel(zk_ref, zv_ref, pe_ref, w1_ref, w2_ref, kg_ref, kc_ref, vc_ref):
    half = CMP_LEN // 2

    def one(z_ref, i):
        p1 = jnp.zeros((LANE, HEAD_DIM), F32)
        p2 = jnp.zeros((LANE, HEAD_DIM), F32)
        for l in range(half):
            z = z_ref[0, pl.ds(l, LANE, stride=CMP_STRIDE), :]
            w_lo = w1_ref[i, l * HEAD_DIM:(l + 1) * HEAD_DIM, :].astype(BF16)
            w_hi = w1_ref[i, (half + l) * HEAD_DIM:(half + l + 1) * HEAD_DIM, :].astype(BF16)
            p1 = p1 + jnp.dot((z + pe_ref[i, l:l + 1, :]).astype(BF16), w_lo, preferred_element_type=F32)
            p2 = p2 + jnp.dot((z + pe_ref[i, half + l:half + l + 1, :]).astype(BF16), w_hi,
                              preferred_element_type=F32)
        pre = p1 + pltpu.roll(p2, LANE - 1, axis=0)
        hid = jax.nn.gelu(pre)
        return jnp.dot(hid.astype(BF16), w2_ref[i].astype(BF16), preferred_element_type=F32)

    kc = one(zk_ref, 0)
    ms = jnp.mean(kc * kc, axis=-1, keepdims=True)
    kc_ref[0, 0] = (kc * lax.rsqrt(ms + EPS) * kg_ref[...]).astype(kc_ref.dtype)
    vc_ref[0, 0] = one(zv_ref, 1).astype(vc_ref.dtype)


def _compress(cmp_kv, pe, w1, w2, k_gain0):
    B, T, _ = cmp_kv.shape
    G, dk = NSA_KV_GROUPS, HEAD_DIM
    assert (T - CMP_LEN) // CMP_STRIDE + 1 == LANE - 1
    out = jax.ShapeDtypeStruct((B, G, LANE, dk), BF16)
    return pl.pallas_call(
        _compress_kernel,
        grid=(B, G),
        in_specs=[pl.BlockSpec((1, T, dk), lambda b, g: (b, 0, g)),
                  pl.BlockSpec((1, T, dk), lambda b, g: (b, 0, G + g)),
                  pl.BlockSpec((2, CMP_LEN, dk), lambda b, g: (0, 0, 0)),
                  pl.BlockSpec((2, CMP_LEN * dk, dk), lambda b, g: (0, 0, 0)),
                  pl.BlockSpec((2, dk, dk), lambda b, g: (0, 0, 0)),
                  pl.BlockSpec((1, dk), lambda b, g: (0, 0))],
        out_specs=[pl.BlockSpec((1, 1, LANE, dk), lambda b, g: (b, g, 0, 0))] * 2,
        out_shape=[out, out],
        compiler_params=_cparams(2),
        name="compress",
    )(cmp_kv, cmp_kv, pe, w1, w2, k_gain0.reshape(1, dk))


def _qk(q, k):
    return lax.dot_general(q, k, (((1,), (1,)), ((), ())), preferred_element_type=F32)


def _lane_fold(x, op):
    acc = x[..., :LANE]
    for t in range(1, x.shape[-1] // LANE):
        acc = op(acc, x[..., t * LANE:(t + 1) * LANE])
    return acc


def _softmax_parts(parts):
    m = jnp.max(functools.reduce(jnp.maximum, [_lane_fold(s, jnp.maximum) for s in parts]), axis=-1, keepdims=True)
    ps = [jnp.exp2(s - m) for s in parts]
    den = jnp.sum(functools.reduce(jnp.add, [_lane_fold(p, jnp.add) for p in ps]), axis=-1, keepdims=True)
    return ps, den


def _nsa_t_kernel(q_ref, ks_ref, vs_ref, kw_ref, vw_ref, kc_ref, vc_ref, gate_ref, cb_ref, nb_ref,
                  ovt_ref, expt_ref, o_ref, vst_ref, vwt_ref, vct_ref, ksa_ref, *, cmp_u0, cmp_per):
    H, W = NSA_HPG, NSA_HPG * QB
    T = ks_ref.shape[1]
    ns = ovt_ref.shape[0]
    nwb = WINDOW // QB
    nb = T // QB
    kc = kc_ref[0, 0]

    @pl.when(pl.program_id(2) == 0)
    def _():
        ksa_ref[:, :HEAD_DIM] = ks_ref[0]
        ksa_ref[:, HEAD_DIM:] = expt_ref[...]
        vst_ref[...] = vs_ref[0].astype(F32).T.astype(BF16)
        vwt_ref[...] = vw_ref[0].astype(F32).T.astype(BF16)
        vct_ref[...] = vc_ref[0, 0].astype(F32).T.astype(BF16)

    def softmax_t(parts):
        m = functools.reduce(jnp.maximum, [jnp.max(s, axis=0, keepdims=True) for s in parts])
        ps = [jnp.exp2(s - m) for s in parts]
        den = functools.reduce(jnp.add, [jnp.sum(p, axis=0, keepdims=True) for p in ps])
        return [p.astype(BF16) for p in ps], den

    def attend_task(vt_ref, spans_fn, out, key):
        spans = spans_fn()
        parts = []
        for _, _, k_slab, q_op, add in spans:
            s = _qk(k_slab, q_op)
            parts.append(s if add is None else s + add)
        yield
        ps, den = softmax_t(parts)
        yield
        o = None
        for (a, b, _, _, _), p in zip(spans, ps):
            pv = jnp.dot(vt_ref[:, a:b], p, preferred_element_type=F32)
            o = pv if o is None else o + pv
        out[key] = o / den
        yield

    def cmp_task(i, qs, out):
        lo = i * QB
        u = cmp_u0 - cmp_per * i
        s = _qk(kc, qs) + cb_ref[0, 0, u:u + LANE, :]
        yield
        e = jnp.exp2(s - jnp.max(s, axis=0, keepdims=True))
        if i == 0:
            key_end = lax.broadcasted_iota(jnp.int32, (LANE, W), 0) * CMP_STRIDE + (CMP_LEN - 1)
            query = lax.broadcasted_iota(jnp.int32, (LANE, W), 1) % QB
            e = jnp.where(key_end <= query, e, 0.0)
            den = jnp.sum(e, axis=0, keepdims=True)
            p = e / jnp.where(den > 0.0, den, 1.0)
        else:
            p = e / jnp.sum(e, axis=0, keepdims=True)
        if i >= 1:
            psum = functools.reduce(jnp.add, [p[:, h * QB:(h + 1) * QB] for h in range(H)])
            imp_t = jnp.dot(ovt_ref[...], psum, precision=lax.Precision.HIGHEST, preferred_element_type=F32)
        yield
        out["cmp", i] = jnp.dot(vct_ref[...], p.astype(BF16), preferred_element_type=F32)
        if i >= 1:
            blk = lax.broadcasted_iota(jnp.int32, (ns, QB), 0)
            cur = (lo + lax.broadcasted_iota(jnp.int32, (ns, QB), 1)) // SEL_BLOCK
            forced = (blk == 0) | (blk == cur) | (blk == cur - 1)
            score = jnp.where(forced, 1e4, jnp.where(blk <= cur, imp_t, -1e4))
            rank = jnp.zeros((ns, QB), F32)
            for b in range(ns):
                other = score[b:b + 1, :]
                rank = rank + jnp.where(blk > b, jnp.where(other >= score, 1.0, 0.0),
                                        jnp.where(other > score, 1.0, 0.0))
            unsel = jnp.where(rank < float(min(N_SEL, ns)), 0.0, 1.0)
            unsel_q = jnp.concatenate([unsel, jnp.zeros((LANE - ns, QB), F32)], axis=0).T.astype(BF16)
            out["qs_aug", i] = jnp.concatenate([qs, jnp.concatenate([unsel_q] * H, axis=0)], axis=1)
        yield

    def slc_spans(i, qs, out):
        lo, hi = i * QB, (i + 1) * QB
        spans = [(lo, hi, ks_ref[0, lo:hi, :], qs, nb_ref[0, 0])]
        if i >= 1:
            qs_aug = out["qs_aug", i]
            spans.insert(0, (lo - QB, lo, ksa_ref[lo - QB:lo, :], qs_aug, nb_ref[0, 1]))
            if i >= 2:
                spans.insert(0, (0, lo - QB, ksa_ref[0:lo - QB, :], qs_aug, None))
        return spans

    def win_spans(i, qs):
        lo, hi = i * QB, (i + 1) * QB

        def span(a, b, add):
            return (a, b, kw_ref[0, a:b, :], qs, add)

        spans = []
        if i >= nwb:
            spans.append(span((i - nwb) * QB, (i - nwb + 1) * QB, nb_ref[0, 2]))
        mid_a, mid_b = max(i - nwb + 1, 0) * QB, (i - 1) * QB
        if mid_b > mid_a:
            spans.append(span(mid_a, mid_b, None))
        if i >= 1:
            spans.append(span(lo - QB, lo, nb_ref[0, 1]))
        spans.append(span(lo, hi, nb_ref[0, 0]))
        return spans

    def combine(i, out):
        lo, hi = i * QB, (i + 1) * QB
        gate_t = gate_ref[0, lo:hi, :].T
        first = pl.program_id(1) == 0

        def grow(br):
            rows = []
            for h in range(H):
                c = 3 * h + br
                rows.append(jnp.where(first, gate_t[c:c + 1, :], gate_t[3 * H + c:3 * H + c + 1, :]))
            return jnp.concatenate(rows, axis=1)

        o_t = grow(0) * out["cmp", i] + grow(1) * out["slc", i] + grow(2) * out["win", i]
        o_ref[0, lo:hi, :] = jnp.concatenate([o_t[:, h * QB:(h + 1) * QB].T for h in range(H)],
                                             axis=1).astype(o_ref.dtype)

    def pair(k):
        out = {}
        tasks = []
        blocks = (nb - 1 - k, k)
        qss = {}
        for i in blocks:
            q = q_ref[0, i * QB:(i + 1) * QB, :]
            qss[i] = jnp.concatenate([q[:, h * HEAD_DIM:(h + 1) * HEAD_DIM] for h in range(H)], axis=0)
            tasks.append(cmp_task(i, qss[i], out))
            tasks.append(attend_task(vwt_ref, functools.partial(win_spans, i, qss[i]), out, ("win", i)))
        for i in blocks:
            tasks.append(attend_task(vst_ref, functools.partial(slc_spans, i, qss[i], out), out, ("slc", i)))
        n_stage = 3
        for step in range(len(tasks) + n_stage - 1):
            for t in range(step - n_stage + 1, step + 1):
                if 0 <= t < len(tasks):
                    next(tasks[t])
        for i in blocks:
            combine(i, out)

    for k in range(nb // 2):
        pl.when(pl.program_id(2) == k)(functools.partial(pair, k))


def _nsa_t(qn, kv, kc, vc, gates, cbias, cmp_u0, cmp_per, nbias):
    B, T, _ = qn.shape
    G, H, dk = NSA_KV_GROUPS, NSA_HPG, HEAD_DIM
    nb = T // QB
    ns = T // SEL_BLOCK
    cstart = np.arange(LANE) * CMP_STRIDE
    sstart = np.arange(ns) * SEL_BLOCK
    overlap = np.clip(np.minimum(cstart[:, None] + CMP_LEN, sstart[None, :] + SEL_BLOCK)
                      - np.maximum(cstart[:, None], sstart[None, :]), 0, None) / CMP_STRIDE
    overlap[LANE - 1:] = 0.0
    ovt = jnp.asarray(overlap.T, F32)
    expand_t = np.zeros((T, LANE), np.float32)
    expand_t[np.arange(T), np.arange(T) // SEL_BLOCK] = MASK_BIG
    return pl.pallas_call(
        functools.partial(_nsa_t_kernel, cmp_u0=cmp_u0, cmp_per=cmp_per),
        grid=(B, G, nb // 2),
        in_specs=[pl.BlockSpec((1, T, H * dk), lambda b, g, i: (b, 0, g)),
                  pl.BlockSpec((1, T, dk), lambda b, g, i: (b, 0, g)),
                  pl.BlockSpec((1, T, dk), lambda b, g, i: (b, 0, G + g)),
                  pl.BlockSpec((1, T, dk), lambda b, g, i: (b, 0, 2 * G + g)),
                  pl.BlockSpec((1, T, dk), lambda b, g, i: (b, 0, 3 * G + g)),
                  pl.BlockSpec((1, 1, LANE, dk), lambda b, g, i: (b, g, 0, 0)),
                  pl.BlockSpec((1, 1, LANE, dk), lambda b, g, i: (b, g, 0, 0)),
                  pl.BlockSpec((1, T, LANE), lambda b, g, i: (b, 0, 0)),
                  pl.BlockSpec((1, 1, 2 * LANE, H * QB), lambda b, g, i: (g, 0, 0, 0)),
                  pl.BlockSpec((1, 3, QB, H * QB), lambda b, g, i: (g, 0, 0, 0)),
                  pl.BlockSpec((ns, LANE), lambda b, g, i: (0, 0)),
                  pl.BlockSpec((T, LANE), lambda b, g, i: (0, 0))],
        out_specs=pl.BlockSpec((1, T, H * dk), lambda b, g, i: (b, 0, g)),
        out_shape=jax.ShapeDtypeStruct((B, T, NSA_HEADS * dk), BF16),
        scratch_shapes=[pltpu.VMEM((dk, T), BF16), pltpu.VMEM((dk, T), BF16), pltpu.VMEM((dk, LANE), BF16),
                        pltpu.VMEM((T, 2 * dk), BF16)],
        compiler_params=_cparams(3),
        name="nsa",
    )(qn, kv, kv, kv, kv, kc, vc, gates, cbias, nbias, ovt, jnp.asarray(expand_t, BF16))


def _diff_kernel(q_ref, k_ref, v_ref, lq_ref, lk_ref, sg_ref, db_ref, o_ref):
    dk = HEAD_DIM
    T = k_ref.shape[1]
    lqk = lq_ref[...] * lk_ref[...]
    lam = (jnp.exp(jnp.sum(lqk[0:1], axis=-1, keepdims=True))
           - jnp.exp(jnp.sum(lqk[1:2], axis=-1, keepdims=True)) + LAM_INIT)
    def task(i, mm, out):
        lo, hi = i * DQB, (i + 1) * DQB
        cols = slice(mm * dk, (mm + 1) * dk)
        q = q_ref[0, lo:hi, cols]
        bounds, parts = [], []
        if i >= 2:
            bounds.append((0, lo - DQB))
            parts.append(_qk(q, k_ref[0, 0:lo - DQB, cols]))
        if i >= 1:
            bounds.append((lo - DQB, lo))
            parts.append(_qk(q, k_ref[0, lo - DQB:lo, cols]) + db_ref[0, 1])
        bounds.append((lo, hi))
        parts.append(_qk(q, k_ref[0, lo:hi, cols]) + db_ref[0, 0])
        yield
        ps, den = _softmax_parts(parts)
        yield
        o = None
        for (a, b), p in zip(bounds, ps):
            pv = jnp.dot(p.astype(BF16), v_ref[0, a:b, :], preferred_element_type=F32)
            o = pv if o is None else o + pv
        out[i, mm] = o / den
        yield

    out = {}
    tasks = [task(i, mm, out) for i in reversed(range(T // DQB)) for mm in range(2)]
    n_stage = 3
    for step in range(len(tasks) + n_stage - 1):
        for t in range(step - n_stage + 1, step + 1):
            if 0 <= t < len(tasks):
                next(tasks[t])

    for i in range(T // DQB):
        lo, hi = i * DQB, (i + 1) * DQB
        o = out[i, 0] - lam * out[i, 1]
        ms = jnp.mean(o * o, axis=-1, keepdims=True)
        o_ref[0, lo:hi, :] = ((o * lax.rsqrt(ms + EPS) * sg_ref[...]) * (1.0 - LAM_INIT)).astype(o_ref.dtype)


def _diff(dqk, dv, lam_q, lam_k, subln_gain, dbias):
    B, T, _ = dqk.shape
    Hd, dk = DIFF_HEADS, HEAD_DIM
    w = 2 * dk
    return pl.pallas_call(
        _diff_kernel,
        grid=(B, Hd),
        in_specs=[pl.BlockSpec((1, T, w), lambda b, h: (b, 0, h)),
                  pl.BlockSpec((1, T, w), lambda b, h: (b, 0, Hd + h)),
                  pl.BlockSpec((1, T, w), lambda b, h: (b, 0, h)),
                  pl.BlockSpec((2, dk), lambda b, h: (0, 0)),
                  pl.BlockSpec((2, dk), lambda b, h: (0, 0)),
                  pl.BlockSpec((1, w), lambda b, h: (0, 0)),
                  pl.BlockSpec((1, 2, DQB, DQB), lambda b, h: (h, 0, 0, 0))],
        out_specs=pl.BlockSpec((1, T, w), lambda b, h: (b, 0, h)),
        out_shape=jax.ShapeDtypeStruct((B, T, Hd * w), BF16),
        compiler_params=_cparams(2),
        name="diff",
    )(dqk, dqk, dv, lam_q, lam_k, subln_gain.reshape(1, w), dbias)


def _merge_kernel(an_ref, ad_ref, wn_ref, wd_ref, gn_ref, gd_ref, o_ref, wnb_ref, wdb_ref):
    @pl.when(pl.program_id(1) == 0)
    def _():
        wnb_ref[...] = wn_ref[...].astype(BF16)
        wdb_ref[...] = wd_ref[...].astype(BF16)

    for rows in _row_chunks(an_ref.shape[0], MERGE_SUB_ROWS):
        yn = jnp.dot(an_ref[rows, :], wnb_ref[...], preferred_element_type=F32)
        yd = jnp.dot(ad_ref[rows, :], wdb_ref[...], preferred_element_type=F32)
        o_ref[rows, :] = (gn_ref[rows, :].astype(F32) * yn + gd_ref[rows, :].astype(F32) * yd).astype(o_ref.dtype)


def _merge(o_nsa, o_diff, w_n, w_d, mg, tm=1024, tn=1024):
    M, K = o_nsa.shape
    N = w_n.shape[1]
    nj = N // tn
    return pl.pallas_call(
        _merge_kernel,
        grid=(nj, M // tm),
        in_specs=[pl.BlockSpec((tm, K), lambda j, i: (i, 0)),
                  pl.BlockSpec((tm, K), lambda j, i: (i, 0)),
                  pl.BlockSpec((K, tn), lambda j, i: (0, j)),
                  pl.BlockSpec((K, tn), lambda j, i: (0, j)),
                  pl.BlockSpec((tm, tn), lambda j, i: (i, j)),
                  pl.BlockSpec((tm, tn), lambda j, i: (i, nj + j))],
        out_specs=pl.BlockSpec((tm, tn), lambda j, i: (i, j)),
        out_shape=jax.ShapeDtypeStruct((M, N), BF16),
        scratch_shapes=[pltpu.VMEM((K, tn), BF16), pltpu.VMEM((K, tn), BF16)],
        compiler_params=_cparams(2),
        name="merge",
    )(o_nsa, o_diff, w_n, w_d, mg, mg)


def _oproj_kernel(a_ref, w_ref, x_ref, g1_ref, gain_ref, sc_ref, sh_ref, x1_ref, h2_ref):
    for rows in _row_chunks(a_ref.shape[0]):
        y = jnp.dot(a_ref[rows, :], w_ref[...], preferred_element_type=F32)
        x1 = x_ref[rows, :] + g1_ref[0] * y
        x1_ref[rows, :] = x1
        h2_ref[rows, :] = _modnorm(x1, gain_ref[...], sc_ref[0], sh_ref[0]).astype(h2_ref.dtype)


def _oproj(merged, w_o, x2d, mod3, gain2, T, tm=512):
    M, D = x2d.shape
    per = T // tm
    return pl.pallas_call(
        _oproj_kernel,
        grid=(M // tm,),
        in_specs=[pl.BlockSpec((tm, D), lambda i: (i, 0)),
                  pl.BlockSpec((D, D), lambda i: (0, 0)),
                  pl.BlockSpec((tm, D), lambda i: (i, 0)),
                  pl.BlockSpec((1, 1, D), lambda i: ((i // per) * 6 + 2, 0, 0)),
                  pl.BlockSpec((1, D), lambda i: (0, 0)),
                  pl.BlockSpec((1, 1, D), lambda i: ((i // per) * 6 + 4, 0, 0)),
                  pl.BlockSpec((1, 1, D), lambda i: ((i // per) * 6 + 3, 0, 0))],
        out_specs=[pl.BlockSpec((tm, D), lambda i: (i, 0)),
                   pl.BlockSpec((tm, D), lambda i: (i, 0))],
        out_shape=[jax.ShapeDtypeStruct((M, D), F32), jax.ShapeDtypeStruct((M, D), BF16)],
        compiler_params=_cparams(1),
        name="oproj",
    )(merged, w_o, x2d, mod3, gain2.reshape(1, D), mod3, mod3)


def _ffn_up_kernel(h_ref, wa_ref, wv_ref, cwa_ref, cwv_ref, cba_ref, cbv_ref, o_ref, wab_ref, wvb_ref,
                   ca_ref, cv_ref, sa_ref, sv_ref, *, per):
    i = pl.program_id(1)

    @pl.when(i == 0)
    def _():
        wab_ref[...] = wa_ref[...].astype(BF16)
        wvb_ref[...] = wv_ref[...].astype(BF16)

    @pl.when(i % per == 0)
    def _():
        ca_ref[...] = jnp.zeros(ca_ref.shape, F32)
        cv_ref[...] = jnp.zeros(cv_ref.shape, F32)

    def conv(u, prev, cw_ref, cb_ref, s_ref):
        s_ref[0:8, :] = prev
        s_ref[8:8 + SUB_ROWS, :] = u
        u1 = s_ref[7:7 + SUB_ROWS, :]
        u2 = s_ref[6:6 + SUB_ROWS, :]
        return cb_ref[...] + cw_ref[0:1, :] * u2 + cw_ref[1:2, :] * u1 + cw_ref[2:3, :] * u

    prev_a, prev_v = ca_ref[...], cv_ref[...]
    for n, rows in enumerate(_row_chunks(h_ref.shape[0], SUB_ROWS)):
        hs = h_ref[rows, :]
        ua = jnp.dot(hs, wab_ref[...], preferred_element_type=F32)
        uv = jnp.dot(hs, wvb_ref[...], preferred_element_type=F32)
        a = conv(ua, prev_a, cwa_ref, cba_ref, sa_ref.at[n % 2])
        val = conv(uv, prev_v, cwv_ref, cbv_ref, sv_ref.at[n % 2])
        o_ref[rows, :] = (a * jax.nn.sigmoid(a) * val).astype(o_ref.dtype)
        prev_a, prev_v = ua[SUB_ROWS - 8:, :], uv[SUB_ROWS - 8:, :]
    ca_ref[...] = prev_a
    cv_ref[...] = prev_v


def _ffn_up(h2, w_up, conv_w, conv_b, T, tm=2048, tn=512):
    M, D = h2.shape
    F = w_up.shape[1] // 2
    nj = F // tn
    cb = conv_b.reshape(1, 2 * F)
    return pl.pallas_call(
        functools.partial(_ffn_up_kernel, per=T // tm),
        grid=(nj, M // tm),
        in_specs=[pl.BlockSpec((tm, D), lambda j, i: (i, 0)),
                  pl.BlockSpec((D, tn), lambda j, i: (0, j)),
                  pl.BlockSpec((D, tn), lambda j, i: (0, nj + j)),
                  pl.BlockSpec((3, tn), lambda j, i: (0, j)),
                  pl.BlockSpec((3, tn), lambda j, i: (0, nj + j)),
                  pl.BlockSpec((1, tn), lambda j, i: (0, j)),
                  pl.BlockSpec((1, tn), lambda j, i: (0, nj + j))],
        out_specs=pl.BlockSpec((tm, tn), lambda j, i: (i, j)),
        out_shape=jax.ShapeDtypeStruct((M, F), BF16),
        scratch_shapes=[pltpu.VMEM((D, tn), BF16), pltpu.VMEM((D, tn), BF16),
                        pltpu.VMEM((8, tn), F32), pltpu.VMEM((8, tn), F32),
                        pltpu.VMEM((2, SUB_ROWS + 8, tn), F32), pltpu.VMEM((2, SUB_ROWS + 8, tn), F32)],
        compiler_params=_cparams(2),
        name="ffn_up",
    )(h2, w_up, w_up, conv_w, conv_w, cb, cb)


def _ffn_down_kernel(a_ref, w_ref, x_ref, g2_ref, o_ref, wb_ref):
    @pl.when(pl.program_id(1) == 0)
    def _():
        wb_ref[...] = w_ref[...].astype(BF16)

    for rows in _row_chunks(a_ref.shape[0]):
        y = jnp.dot(a_ref[rows, :], wb_ref[...], preferred_element_type=F32)
        o_ref[rows, :] = x_ref[rows, :] + g2_ref[0] * y


def _ffn_down(act, w_down, x1, mod3, T, tm=512, tn=512):
    M, F = act.shape
    D = w_down.shape[1]
    per = T // tm
    return pl.pallas_call(
        _ffn_down_kernel,
        grid=(D // tn, M // tm),
        in_specs=[pl.BlockSpec((tm, F), lambda j, i: (i, 0)),
                  pl.BlockSpec((F, tn), lambda j, i: (0, j)),
                  pl.BlockSpec((tm, tn), lambda j, i: (i, j)),
                  pl.BlockSpec((1, 1, tn), lambda j, i: ((i // per) * 6 + 5, 0, j))],
        out_specs=pl.BlockSpec((tm, tn), lambda j, i: (i, j)),
        out_shape=jax.ShapeDtypeStruct((M, D), F32),
        scratch_shapes=[pltpu.VMEM((F, tn), BF16)],
        compiler_params=_cparams(2),
        name="ffn_down",
    )(act, w_down, x1, mod3)


def _layer(x, c, w_ada, b_ada, norm1_gain, norm2_gain, w_in, nsa_q_gain, nsa_k_gain, cmp_pe, cmp_w1, cmp_w2,
           diff_q_gain, diff_k_gain, diff_lambda_q, diff_lambda_k, diff_subln_gain, w_nsa_out, w_diff_out, w_o,
           w_ffn_up, ffn_conv_w, ffn_conv_b, w_ffn_down, rel_bias):
    B, T, D = x.shape
    dk, G = HEAD_DIM, NSA_KV_GROUPS
    M = B * T
    scale = dk ** -0.5

    n_q = NSA_HEADS * dk
    o_kv = n_q
    o_g = o_kv + 3 * 2 * G * dk
    o_dq = o_g + NSA_HEADS * 3
    o_dk = o_dq + DIFF_HEADS * 2 * dk
    o_dv = o_dk + DIFF_HEADS * 2 * dk
    o_mg = o_dv + DIFF_HEADS * 2 * dk
    n_kv = 2 * G * dk
    n_dqk = 2 * DIFF_HEADS * 2 * dk

    mod3 = _ada(c, w_ada, b_ada).reshape(B * 6, 1, D)

    ones = jnp.ones((n_kv // 2,), F32)
    g_q = jnp.tile(nsa_q_gain * (scale * LOG2E), NSA_HEADS).reshape(1, n_q)
    g_kv = jnp.concatenate([jnp.tile(nsa_k_gain[1], G), ones, jnp.tile(nsa_k_gain[2], G), ones]).reshape(1, 2 * n_kv)
    g_dqk = jnp.concatenate([jnp.tile(diff_q_gain * (scale * LOG2E), 2 * DIFF_HEADS),
                             jnp.tile(diff_k_gain, 2 * DIFF_HEADS)]).reshape(1, n_dqk)
    g_one = jnp.ones((1, 2 * D), F32)
    w_in_t = jnp.swapaxes(w_in, 1, 2)
    wide = INPROJ_TN
    yes, no = (True,) * (wide // LANE), (False,) * (wide // LANE)
    qn, h = _norm1_inproj(x.reshape(M, D), norm1_gain, mod3, w_in_t, g_q, T=T, src0=0, ncols=n_q,
                          heads=(True,) * NSA_HEADS, mode="raw", out_dtype=BF16, tm=INPROJ_TM // 2, name="inproj_q")
    proj = functools.partial(_inproj, h, w_in_t, tm=INPROJ_TM)
    cmpkv, gates = _inproj_cmp_gate(h, w_in_t, src_cmp=o_kv, n_cmp=n_kv, src_gate=o_g, tm=INPROJ_TM)
    kv_heads = ((True,) * G + (False,) * G) * 2
    kv = proj(g_kv, src0=o_kv + n_kv, ncols=2 * n_kv, heads=kv_heads, mode="raw", out_dtype=BF16, tn=2 * n_kv,
              name="inproj_kv")
    dqk = proj(g_dqk, src0=o_dq, ncols=n_dqk, heads=yes, mode="raw", out_dtype=BF16, tn=wide, name="inproj_dqk")
    dv = proj(g_one, src0=o_dv, ncols=o_mg - o_dv, heads=no, mode="raw", out_dtype=BF16, tn=wide, name="inproj_dv")
    mgate = proj(g_one, src0=o_mg, ncols=2 * D, heads=no, mode="sigmoid", out_dtype=BF16, tn=wide, name="inproj_mg")

    nbias = _bias_tiles(_window_idx(QB).transpose(0, 2, 1), rel_bias, head0=0, groups=G, hpg=NSA_HPG,
                        name="bias_nsa", rel=True, mult=LOG2E)
    cmp_idx, cmp_u0, cmp_per = _cmp_idx(T)
    cbias = _bias_tiles(cmp_idx.transpose(0, 2, 1), rel_bias, head0=0, groups=G, hpg=NSA_HPG, name="bias_cmp",
                        mult=LOG2E)
    dbias = _bias_tiles(_causal_idx(DQB), rel_bias, head0=NSA_HEADS, groups=DIFF_HEADS, hpg=1, name="bias_diff",
                        rel=True, mult=LOG2E)

    kc, vc = _compress(cmpkv.reshape(B, T, n_kv), cmp_pe, cmp_w1, cmp_w2, nsa_k_gain[0])
    o_nsa = _nsa_t(qn.reshape(B, T, n_q), kv.reshape(B, T, 2 * n_kv), kc, vc, gates.reshape(B, T, LANE), cbias,
                   cmp_u0, cmp_per, nbias)
    o_diff = _diff(dqk.reshape(B, T, n_dqk), dv.reshape(B, T, -1), diff_lambda_q, diff_lambda_k, diff_subln_gain,
                   dbias)

    merged = _merge(o_nsa.reshape(M, -1), o_diff.reshape(M, -1), w_nsa_out, w_diff_out, mgate)
    x1, h2 = _oproj(merged, w_o.astype(BF16), x.reshape(M, D), mod3, norm2_gain, T)
    act = _ffn_up(h2, w_ffn_up, ffn_conv_w, ffn_conv_b, T)
    out = _ffn_down(act, w_ffn_down, x1, mod3, T)
    return out.reshape(B, T, D)


def kernel(x, c, w_ada, b_ada, norm1_gain, norm2_gain, w_in, nsa_q_gain, nsa_k_gain, cmp_pe, cmp_w1, cmp_w2,
           diff_q_gain, diff_k_gain, diff_lambda_q, diff_lambda_k, diff_subln_gain, w_nsa_out, w_diff_out, w_o,
           w_ffn_up, ffn_conv_w, ffn_conv_b, w_ffn_down, rel_bias):
    return _layer(x, c, w_ada[0], b_ada[0], norm1_gain[0], norm2_gain[0], w_in, nsa_q_gain[0], nsa_k_gain[0],
                  cmp_pe[0], cmp_w1[0], cmp_w2[0], diff_q_gain[0], diff_k_gain[0], diff_lambda_q[0],
                  diff_lambda_k[0], diff_subln_gain[0], w_nsa_out[0], w_diff_out[0], w_o[0], w_ffn_up[0],
                  ffn_conv_w[0], ffn_conv_b[0], w_ffn_down[0], rel_bias)
```

```python
import functools
import math

import numpy as np
import jax
import jax.numpy as jnp
from jax import lax
from jax.experimental import pallas as pl
from jax.experimental.pallas import tpu as pltpu

F32 = jnp.float32
BF16 = jnp.bfloat16

HEAD_DIM = 128
NSA_HEADS = 8
NSA_KV_GROUPS = 2
NSA_HPG = NSA_HEADS // NSA_KV_GROUPS
CMP_LEN = 32
CMP_STRIDE = 16
SEL_BLOCK = 64
N_SEL = 16
WINDOW = 512
DIFF_HEADS = 4
NUM_BUCKETS = 32
MAX_DISTANCE = 128
EPS = 1e-6
NEG = -1e30
LAM_INIT = 0.8 - 0.6 * math.exp(-0.3 * 0)
LOG2E = math.log2(math.e)
MASK_BIG = -(2.0 ** 100)

LANE = 128
QB = 128
DQB = 256
SUB_ROWS = 256
MERGE_SUB_ROWS = 512
INPROJ_TM = 2048
INPROJ_TN = 1024
VMEM_LIMIT = 56 * 1024 * 1024


def _cparams(n_axes):
    return pltpu.CompilerParams(dimension_semantics=("arbitrary",) * n_axes,
                                vmem_limit_bytes=VMEM_LIMIT)


def _t5_bucket_np(dist):
    n = np.maximum(np.asarray(dist, np.int32), 0)
    max_exact = NUM_BUCKETS // 2
    nf = np.maximum(n, max_exact).astype(np.float32)
    large = max_exact + (np.log(nf / np.float32(max_exact)) / np.float32(math.log(MAX_DISTANCE / max_exact))
                         * np.float32(NUM_BUCKETS - max_exact)).astype(np.int32)
    large = np.minimum(large, NUM_BUCKETS - 1)
    return np.where(n < max_exact, n, large).astype(np.int32)


def _ada_kernel(ct_ref, w_ref, b_ref, o_ref):
    ct = ct_ref[...]
    s = ct * jax.nn.sigmoid(ct)
    w = w_ref[...]
    for b in range(ct.shape[1]):
        o_ref[b:b + 1, :] = jnp.sum(w * s[:, b:b + 1], axis=0, keepdims=True) + b_ref[...]


def _ada(c, w_ada, b_ada, tn=1024):
    B, D = c.shape
    N = w_ada.shape[1]
    return pl.pallas_call(
        _ada_kernel,
        grid=(N // tn,),
        in_specs=[pl.BlockSpec((D, B), lambda j: (0, 0)),
                  pl.BlockSpec((D, tn), lambda j: (0, j)),
                  pl.BlockSpec((1, tn), lambda j: (0, j))],
        out_specs=pl.BlockSpec((B, tn), lambda j: (0, j)),
        out_shape=jax.ShapeDtypeStruct((B, N), F32),
        compiler_params=_cparams(1),
        name="ada",
    )(c.T, w_ada, b_ada.reshape(1, N))


def _modnorm(x, gain, sc, sh):
    ms = jnp.mean(x * x, axis=-1, keepdims=True)
    return (x * lax.rsqrt(ms + EPS) * gain) * (1.0 + sc) + sh


def _row_chunks(tm, sub=SUB_ROWS):
    return [slice(r, r + sub) for r in range(0, tm, sub)]


def _inproj_epilogue(acc, g_ref, o_ref, rows, heads, mode):
    for k, is_head in enumerate(heads):
        lanes = slice(k * LANE, (k + 1) * LANE)
        y = acc[:, lanes]
        if is_head:
            ms = jnp.mean(y * y, axis=-1, keepdims=True)
            y = y * lax.rsqrt(ms + EPS) * g_ref[:, lanes]
        elif mode == "sigmoid":
            y = jax.nn.sigmoid(y)
        o_ref[rows, lanes] = y.astype(o_ref.dtype)


def _inproj_kernel(a_ref, wt_ref, g_ref, o_ref, wb_ref, *, heads, mode):
    @pl.when(pl.program_id(1) == 0)
    def _():
        wb_ref[...] = wt_ref[0].astype(BF16)

    for rows in _row_chunks(a_ref.shape[0]):
        acc = _qk(a_ref[rows, :], wb_ref[...])
        _inproj_epilogue(acc, g_ref, o_ref, rows, heads, mode)


def _norm1_inproj_kernel(x_ref, n1g_ref, sc_ref, sh_ref, wt_ref, g_ref, o_ref, h_ref, wb_ref, *, heads, mode):
    @pl.when(pl.program_id(0) == 0)
    def _():
        wb_ref[...] = wt_ref[0].astype(BF16)

    for rows in _row_chunks(x_ref.shape[0]):
        h = _modnorm(x_ref[rows, :], n1g_ref[...], sc_ref[0], sh_ref[0]).astype(BF16)
        h_ref[rows, :] = h
        _inproj_epilogue(_qk(h, wb_ref[...]), g_ref, o_ref, rows, heads, mode)


def _norm1_inproj(x2d, n1_gain, mod3, w_in_t, gains, *, T, src0, ncols, heads, mode, out_dtype, tm, name):
    M, K = x2d.shape
    per = T // tm
    assert src0 % 8 == 0 and len(heads) == ncols // LANE
    return pl.pallas_call(
        functools.partial(_norm1_inproj_kernel, heads=heads, mode=mode),
        grid=(M // tm,),
        in_specs=[pl.BlockSpec((tm, K), lambda i: (i, 0)),
                  pl.BlockSpec((1, K), lambda i: (0, 0)),
                  pl.BlockSpec((1, 1, K), lambda i: ((i // per) * 6 + 1, 0, 0)),
                  pl.BlockSpec((1, 1, K), lambda i: ((i // per) * 6 + 0, 0, 0)),
                  pl.BlockSpec((pl.Element(1), pl.Element(ncols), pl.Element(K)), lambda i: (0, src0, 0)),
                  pl.BlockSpec((1, ncols), lambda i: (0, 0))],
        out_specs=[pl.BlockSpec((tm, ncols), lambda i: (i, 0)), pl.BlockSpec((tm, K), lambda i: (i, 0))],
        out_shape=[jax.ShapeDtypeStruct((M, ncols), out_dtype), jax.ShapeDtypeStruct((M, K), BF16)],
        scratch_shapes=[pltpu.VMEM((ncols, K), BF16)],
        compiler_params=_cparams(1),
        name=name,
    )(x2d, n1_gain.reshape(1, K), mod3, mod3, w_in_t, gains)


def _inproj(h2d, w_in_t, gains, *, src0, ncols, heads, mode, out_dtype, tm, tn, name):
    M, K = h2d.shape
    assert src0 % 8 == 0 and ncols % tn == 0 and len(heads) == tn // LANE
    return pl.pallas_call(
        functools.partial(_inproj_kernel, heads=heads, mode=mode),
        grid=(ncols // tn, M // tm),
        in_specs=[pl.BlockSpec((tm, K), lambda j, i: (i, 0)),
                  pl.BlockSpec((pl.Element(1), pl.Element(tn), pl.Element(K)),
                               lambda j, i: (0, pl.multiple_of(src0 + j * tn, 8), 0)),
                  pl.BlockSpec((1, tn), lambda j, i: (0, j))],
        out_specs=pl.BlockSpec((tm, tn), lambda j, i: (i, j)),
        out_shape=jax.ShapeDtypeStruct((M, ncols), out_dtype),
        scratch_shapes=[pltpu.VMEM((tn, K), BF16)],
        compiler_params=_cparams(2),
        name=name,
    )(h2d, w_in_t, gains)


def _inproj_cmp_gate_kernel(a_ref, wc_ref, wg_ref, oc_ref, og_ref, wcb_ref, wgb_ref):
    @pl.when(pl.program_id(0) == 0)
    def _():
        wcb_ref[...] = wc_ref[0].astype(BF16)
        wgb_ref[...] = wg_ref[0].astype(BF16)

    for rows in _row_chunks(a_ref.shape[0]):
        a = a_ref[rows, :]
        oc_ref[rows, :] = _qk(a, wcb_ref[...])
        og_ref[rows, :] = jax.nn.sigmoid(_qk(a, wgb_ref[...]))


def _inproj_cmp_gate(h2d, w_in_t, *, src_cmp, n_cmp, src_gate, tm):
    M, K = h2d.shape
    assert src_cmp % 8 == 0 and src_gate % 8 == 0

    def w_rows(start, n):
        return pl.BlockSpec((pl.Element(1), pl.Element(n), pl.Element(K)), lambda i: (0, start, 0))

    return pl.pallas_call(
        _inproj_cmp_gate_kernel,
        grid=(M // tm,),
        in_specs=[pl.BlockSpec((tm, K), lambda i: (i, 0)), w_rows(src_cmp, n_cmp), w_rows(src_gate, LANE)],
        out_specs=[pl.BlockSpec((tm, n_cmp), lambda i: (i, 0)), pl.BlockSpec((tm, LANE), lambda i: (i, 0))],
        out_shape=[jax.ShapeDtypeStruct((M, n_cmp), F32), jax.ShapeDtypeStruct((M, LANE), F32)],
        scratch_shapes=[pltpu.VMEM((n_cmp, K), BF16), pltpu.VMEM((LANE, K), BF16)],
        compiler_params=_cparams(1),
        name="inproj_cmp_gate",
    )(h2d, w_in_t, w_in_t)


def _bias_kernel(idx_ref, tab_ref, o_ref, *, head0, hpg, rel, mult):
    n_tiles, _, C = idx_ref.shape
    for n in range(n_tiles):
        idx = idx_ref[n]
        for h in range(hpg):
            head = head0 + pl.program_id(0) * hpg + h
            acc = jnp.zeros(idx.shape, F32)
            for b in range(NUM_BUCKETS):
                acc = jnp.where(idx == b, tab_ref[b, head], acc)
            if rel:
                acc = acc - tab_ref[NUM_BUCKETS - 1, head]
            o_ref[0, n, :, h * C:(h + 1) * C] = jnp.where(idx < 0, NEG, acc * mult)


def _bias_tiles(idx, rel_bias, *, head0, groups, hpg, name, rel=False, mult=1.0):
    N, R, C = idx.shape
    return pl.pallas_call(
        functools.partial(_bias_kernel, head0=head0, hpg=hpg, rel=rel, mult=mult),
        grid=(groups,),
        in_specs=[pl.BlockSpec((N, R, C), lambda g: (0, 0, 0)),
                  pl.BlockSpec(memory_space=pltpu.SMEM)],
        out_specs=pl.BlockSpec((1, N, R, hpg * C), lambda g: (g, 0, 0, 0)),
        out_shape=jax.ShapeDtypeStruct((groups, N, R, hpg * C), F32),
        compiler_params=_cparams(1),
        name=name,
    )(jnp.asarray(idx), rel_bias)


def _causal_idx(R):
    r = np.arange(R)[:, None]
    c = np.arange(R)[None, :]
    return np.stack([np.where(r >= c, _t5_bucket_np(r - c), -1), _t5_bucket_np(R + r - c)]).astype(np.int32)


def _window_idx(R):
    r = np.arange(R)[:, None]
    c = np.arange(R)[None, :]
    edge = np.where(r < c, NUM_BUCKETS - 1, -1)
    return np.concatenate([_causal_idx(R), edge[None]]).astype(np.int32)


def _cmp_idx(T):
    per = QB // CMP_STRIDE
    u0 = (T // QB - 1) * per
    assert u0 + LANE <= 2 * LANE
    r = np.arange(QB)[:, None]
    end = (np.arange(2 * LANE)[None, :] - u0) * CMP_STRIDE + CMP_LEN - 1
    return np.where(end <= r, _t5_bucket_np(r - end), -1).astype(np.int32)[None], u0, per


def _compress_kernel(zk_ref, zv_ref, pe_ref, w1_ref, w2_ref, kg_ref, kc_ref, vc_ref):
    half = CMP_LEN // 2

    def one(z_ref, i):
        p1 = jnp.zeros((LANE, HEAD_DIM), F32)
        p2 = jnp.zeros((LANE, HEAD_DIM), F32)
        for l in range(half):
            z = z_ref[0, pl.ds(l, LANE, stride=CMP_STRIDE), :]
            w_lo = w1_ref[i, l * HEAD_DIM:(l + 1) * HEAD_DIM, :].astype(BF16)
            w_hi = w1_ref[i, (half + l) * HEAD_DIM:(half + l + 1) * HEAD_DIM, :].astype(BF16)
            p1 = p1 + jnp.dot((z + pe_ref[i, l:l + 1, :]).astype(BF16), w_lo, preferred_element_type=F32)
            p2 = p2 + jnp.dot((z + pe_ref[i, half + l:half + l + 1, :]).astype(BF16), w_hi,
                              preferred_element_type=F32)
        pre = p1 + pltpu.roll(p2, LANE - 1, axis=0)
        hid = jax.nn.gelu(pre)
        return jnp.dot(hid.astype(BF16), w2_ref[i].astype(BF16), preferred_element_type=F32)

    kc = one(zk_ref, 0)
    ms = jnp.mean(kc * kc, axis=-1, keepdims=True)
    kc_ref[0, 0] = (kc * lax.rsqrt(ms + EPS) * kg_ref[...]).astype(kc_ref.dtype)
    vc_ref[0, 0] = one(zv_ref, 1).astype(vc_ref.dtype)


def _compress(cmp_kv, pe, w1, w2, k_gain0):
    B, T, _ = cmp_kv.shape
    G, dk = NSA_KV_GROUPS, HEAD_DIM
    assert (T - CMP_LEN) // CMP_STRIDE + 1 == LANE - 1
    out = jax.ShapeDtypeStruct((B, G, LANE, dk), BF16)
    return pl.pallas_call(
        _compress_kernel,
        grid=(B, G),
        in_specs=[pl.BlockSpec((1, T, dk), lambda b, g: (b, 0, g)),
                  pl.BlockSpec((1, T, dk), lambda b, g: (b, 0, G + g)),
                  pl.BlockSpec((2, CMP_LEN, dk), lambda b, g: (0, 0, 0)),
                  pl.BlockSpec((2, CMP_LEN * dk, dk), lambda b, g: (0, 0, 0)),
                  pl.BlockSpec((2, dk, dk), lambda b, g: (0, 0, 0)),
                  pl.BlockSpec((1, dk), lambda b, g: (0, 0))],
        out_specs=[pl.BlockSpec((1, 1, LANE, dk), lambda b, g: (b, g, 0, 0))] * 2,
        out_shape=[out, out],
        compiler_params=_cparams(2),
        name="compress",
    )(cmp_kv, cmp_kv, pe, w1, w2, k_gain0.reshape(1, dk))


def _qk(q, k):
    return lax.dot_general(q, k, (((1,), (1,)), ((), ())), preferred_element_type=F32)


def _lane_fold(x, op):
    acc = x[..., :LANE]
    for t in range(1, x.shape[-1] // LANE):
        acc = op(acc, x[..., t * LANE:(t + 1) * LANE])
    return acc


def _softmax_parts(parts):
    m = jnp.max(functools.reduce(jnp.maximum, [_lane_fold(s, jnp.maximum) for s in parts]), axis=-1, keepdims=True)
    ps = [jnp.exp2(s - m) for s in parts]
    den = jnp.sum(functools.reduce(jnp.add, [_lane_fold(p, jnp.add) for p in ps]), axis=-1, keepdims=True)
    return ps, den


def _nsa_t_kernel(q_ref, ks_ref, vs_ref, kw_ref, vw_ref, kc_ref, vc_ref, gate_ref, cb_ref, nb_ref,
                  ovt_ref, expt_ref, o_ref, vst_ref, vwt_ref, vct_ref, ksa_ref, *, cmp_u0, cmp_per):
    H, W = NSA_HPG, NSA_HPG * QB
    T = ks_ref.shape[1]
    ns = ovt_ref.shape[0]
    nwb = WINDOW // QB
    nb = T // QB
    kc = kc_ref[0, 0]

    @pl.when(pl.program_id(2) == 0)
    def _():
        ksa_ref[:, :HEAD_DIM] = ks_ref[0]
        ksa_ref[:, HEAD_DIM:] = expt_ref[...]
        vst_ref[...] = vs_ref[0].astype(F32).T.astype(BF16)
        vwt_ref[...] = vw_ref[0].astype(F32).T.astype(BF16)
        vct_ref[...] = vc_ref[0, 0].astype(F32).T.astype(BF16)

    def softmax_t(parts):
        m = functools.reduce(jnp.maximum, [jnp.max(s, axis=0, keepdims=True) for s in parts])
        ps = [jnp.exp2(s - m) for s in parts]
        den = functools.reduce(jnp.add, [jnp.sum(p, axis=0, keepdims=True) for p in ps])
        return [p.astype(BF16) for p in ps], den

    def attend_task(vt_ref, spans_fn, out, key):
        spans = spans_fn()
        parts = []
        for _, _, k_slab, q_op, add in spans:
            s = _qk(k_slab, q_op)
            parts.append(s if add is None else s + add)
        yield
        ps, den = softmax_t(parts)
        yield
        o = None
        for (a, b, _, _, _), p in zip(spans, ps):
            pv = jnp.dot(vt_ref[:, a:b], p, preferred_element_type=F32)
            o = pv if o is None else o + pv
        out[key] = o / den
        yield

    def cmp_task(i, qs, out):
        lo = i * QB
        u = cmp_u0 - cmp_per * i
        s = _qk(kc, qs) + cb_ref[0, 0, u:u + LANE, :]
        yield
        e = jnp.exp2(s - jnp.max(s, axis=0, keepdims=True))
        if i == 0:
            key_end = lax.broadcasted_iota(jnp.int32, (LANE, W), 0) * CMP_STRIDE + (CMP_LEN - 1)
            query = lax.broadcasted_iota(jnp.int32, (LANE, W), 1) % QB
            e = jnp.where(key_end <= query, e, 0.0)
            den = jnp.sum(e, axis=0, keepdims=True)
            p = e / jnp.where(den > 0.0, den, 1.0)
        else:
            p = e / jnp.sum(e, axis=0, keepdims=True)
        if i >= 1:
            psum = functools.reduce(jnp.add, [p[:, h * QB:(h + 1) * QB] for h in range(H)])
            imp_t = jnp.dot(ovt_ref[...], psum, precision=lax.Precision.HIGHEST, preferred_element_type=F32)
        yield
        out["cmp", i] = jnp.dot(vct_ref[...], p.astype(BF16), preferred_element_type=F32)
        if i >= 1:
            blk = lax.broadcasted_iota(jnp.int32, (ns, QB), 0)
            cur = (lo + lax.broadcasted_iota(jnp.int32, (ns, QB), 1)) // SEL_BLOCK
            forced = (blk == 0) | (blk == cur) | (blk == cur - 1)
            score = jnp.where(forced, 1e4, jnp.where(blk <= cur, imp_t, -1e4))
            rank = jnp.zeros((ns, QB), F32)
            for b in range(ns):
                other = score[b:b + 1, :]
                rank = rank + jnp.where(blk > b, jnp.where(other >= score, 1.0, 0.0),
                                        jnp.where(other > score, 1.0, 0.0))
            unsel = jnp.where(rank < float(min(N_SEL, ns)), 0.0, 1.0)
            unsel_q = jnp.concatenate([unsel, jnp.zeros((LANE - ns, QB), F32)], axis=0).T.astype(BF16)
            out["qs_aug", i] = jnp.concatenate([qs, jnp.concatenate([unsel_q] * H, axis=0)], axis=1)
        yield

    def slc_spans(i, qs, out):
        lo, hi = i * QB, (i + 1) * QB
        spans = [(lo, hi, ks_ref[0, lo:hi, :], qs, nb_ref[0, 0])]
        if i >= 1:
            qs_aug = out["qs_aug", i]
            spans.insert(0, (lo - QB, lo, ksa_ref[lo - QB:lo, :], qs_aug, nb_ref[0, 1]))
            if i >= 2:
                spans.insert(0, (0, lo - QB, ksa_ref[0:lo - QB, :], qs_aug, None))
        return spans

    def win_spans(i, qs):
        lo, hi = i * QB, (i + 1) * QB

        def span(a, b, add):
            return (a, b, kw_ref[0, a:b, :], qs, add)

        spans = []
        if i >= nwb:
            spans.append(span((i - nwb) * QB, (i - nwb + 1) * QB, nb_ref[0, 2]))
        mid_a, mid_b = max(i - nwb + 1, 0) * QB, (i - 1) * QB
        if mid_b > mid_a:
            spans.append(span(mid_a, mid_b, None))
        if i >= 1:
            spans.append(span(lo - QB, lo, nb_ref[0, 1]))
        spans.append(span(lo, hi, nb_ref[0, 0]))
        return spans

    def combine(i, out):
        lo, hi = i * QB, (i + 1) * QB
        gate_t = gate_ref[0, lo:hi, :].T
        first = pl.program_id(1) == 0

        def grow(br):
            rows = []
            for h in range(H):
                c = 3 * h + br
                rows.append(jnp.where(first, gate_t[c:c + 1, :], gate_t[3 * H + c:3 * H + c + 1, :]))
            return jnp.concatenate(rows, axis=1)

        o_t = grow(0) * out["cmp", i] + grow(1) * out["slc", i] + grow(2) * out["win", i]
        o_ref[0, lo:hi, :] = jnp.concatenate([o_t[:, h * QB:(h + 1) * QB].T for h in range(H)],
                                             axis=1).astype(o_ref.dtype)

    def pair(k):
        out = {}
        tasks = []
        blocks = (nb - 1 - k, k)
        qss = {}
        for i in blocks:
            q = q_ref[0, i * QB:(i + 1) * QB, :]
            qss[i] = jnp.concatenate([q[:, h * HEAD_DIM:(h + 1) * HEAD_DIM] for h in range(H)], axis=0)
            tasks.append(cmp_task(i, qss[i], out))
            tasks.append(attend_task(vwt_ref, functools.partial(win_spans, i, qss[i]), out, ("win", i)))
        for i in blocks:
            tasks.append(attend_task(vst_ref, functools.partial(slc_spans, i, qss[i], out), out, ("slc", i)))
        n_stage = 3
        for step in range(len(tasks) + n_stage - 1):
            for t in range(step - n_stage + 1, step + 1):
                if 0 <= t < len(tasks):
                    next(tasks[t])
        for i in blocks:
            combine(i, out)

    for k in range(nb // 2):
        pl.when(pl.program_id(2) == k)(functools.partial(pair, k))


def _nsa_t(qn, kv, kc, vc, gates, cbias, cmp_u0, cmp_per, nbias):
    B, T, _ = qn.shape
    G, H, dk = NSA_KV_GROUPS, NSA_HPG, HEAD_DIM
    nb = T // QB
    ns = T // SEL_BLOCK
    cstart = np.arange(LANE) * CMP_STRIDE
    sstart = np.arange(ns) * SEL_BLOCK
    overlap = np.clip(np.minimum(cstart[:, None] + CMP_LEN, sstart[None, :] + SEL_BLOCK)
                      - np.maximum(cstart[:, None], sstart[None, :]), 0, None) / CMP_STRIDE
    overlap[LANE - 1:] = 0.0
    ovt = jnp.asarray(overlap.T, F32)
    expand_t = np.zeros((T, LANE), np.float32)
    expand_t[np.arange(T), np.arange(T) // SEL_BLOCK] = MASK_BIG
    return pl.pallas_call(
        functools.partial(_nsa_t_kernel, cmp_u0=cmp_u0, cmp_per=cmp_per),
        grid=(B, G, nb // 2),
        in_specs=[pl.BlockSpec((1, T, H * dk), lambda b, g, i: (b, 0, g)),
                  pl.BlockSpec((1, T, dk), lambda b, g, i: (b, 0, g)),
                  pl.BlockSpec((1, T, dk), lambda b, g, i: (b, 0, G + g)),
                  pl.BlockSpec((1, T, dk), lambda b, g, i: (b, 0, 2 * G + g)),
                  pl.BlockSpec((1, T, dk), lambda b, g, i: (b, 0, 3 * G + g)),
                  pl.BlockSpec((1, 1, LANE, dk), lambda b, g, i: (b, g, 0, 0)),
                  pl.BlockSpec((1, 1, LANE, dk), lambda b, g, i: (b, g, 0, 0)),
                  pl.BlockSpec((1, T, LANE), lambda b, g, i: (b, 0, 0)),
                  pl.BlockSpec((1, 1, 2 * LANE, H * QB), lambda b, g, i: (g, 0, 0, 0)),
                  pl.BlockSpec((1, 3, QB, H * QB), lambda b, g, i: (g, 0, 0, 0)),
                  pl.BlockSpec((ns, LANE), lambda b, g, i: (0, 0)),
                  pl.BlockSpec((T, LANE), lambda b, g, i: (0, 0))],
        out_specs=pl.BlockSpec((1, T, H * dk), lambda b, g, i: (b, 0, g)),
        out_shape=jax.ShapeDtypeStruct((B, T, NSA_HEADS * dk), BF16),
        scratch_shapes=[pltpu.VMEM((dk, T), BF16), pltpu.VMEM((dk, T), BF16), pltpu.VMEM((dk, LANE), BF16),
                        pltpu.VMEM((T, 2 * dk), BF16)],
        compiler_params=_cparams(3),
        name="nsa",
    )(qn, kv, kv, kv, kv, kc, vc, gates, cbias, nbias, ovt, jnp.asarray(expand_t, BF16))


def _diff_kernel(q_ref, k_ref, v_ref, lq_ref, lk_ref, sg_ref, db_ref, o_ref):
    dk = HEAD_DIM
    T = k_ref.shape[1]
    lqk = lq_ref[...] * lk_ref[...]
    lam = (jnp.exp(jnp.sum(lqk[0:1], axis=-1, keepdims=True))
           - jnp.exp(jnp.sum(lqk[1:2], axis=-1, keepdims=True)) + LAM_INIT)
    def task(i, mm, out):
        lo, hi = i * DQB, (i + 1) * DQB
        cols = slice(mm * dk, (mm + 1) * dk)
        q = q_ref[0, lo:hi, cols]
        bounds, parts = [], []
        if i >= 2:
            bounds.append((0, lo - DQB))
            parts.append(_qk(q, k_ref[0, 0:lo - DQB, cols]))
        if i >= 1:
            bounds.append((lo - DQB, lo))
            parts.append(_qk(q, k_ref[0, lo - DQB:lo, cols]) + db_ref[0, 1])
        bounds.append((lo, hi))
        parts.append(_qk(q, k_ref[0, lo:hi, cols]) + db_ref[0, 0])
        yield
        ps, den = _softmax_parts(parts)
        yield
        o = None
        for (a, b), p in zip(bounds, ps):
            pv = jnp.dot(p.astype(BF16), v_ref[0, a:b, :], preferred_element_type=F32)
            o = pv if o is None else o + pv
        out[i, mm] = o / den
        yield

    out = {}
    tasks = [task(i, mm, out) for i in reversed(range(T // DQB)) for mm in range(2)]
    n_stage = 3
    for step in range(len(tasks) + n_stage - 1):
        for t in range(step - n_stage + 1, step + 1):
            if 0 <= t < len(tasks):
                next(tasks[t])

    for i in range(T // DQB):
        lo, hi = i * DQB, (i + 1) * DQB
        o = out[i, 0] - lam * out[i, 1]
        ms = jnp.mean(o * o, axis=-1, keepdims=True)
        o_ref[0, lo:hi, :] = ((o * lax.rsqrt(ms + EPS) * sg_ref[...]) * (1.0 - LAM_INIT)).astype(o_ref.dtype)


def _diff(dqk, dv, lam_q, lam_k, subln_gain, dbias):
    B, T, _ = dqk.shape
    Hd, dk = DIFF_HEADS, HEAD_DIM
    w = 2 * dk
    return pl.pallas_call(
        _diff_kernel,
        grid=(B, Hd),
        in_specs=[pl.BlockSpec((1, T, w), lambda b, h: (b, 0, h)),
                  pl.BlockSpec((1, T, w), lambda b, h: (b, 0, Hd + h)),
                  pl.BlockSpec((1, T, w), lambda b, h: (b, 0, h)),
                  pl.BlockSpec((2, dk), lambda b, h: (0, 0)),
                  pl.BlockSpec((2, dk), lambda b, h: (0, 0)),
                  pl.BlockSpec((1, w), lambda b, h: (0, 0)),
                  pl.BlockSpec((1, 2, DQB, DQB), lambda b, h: (h, 0, 0, 0))],
        out_specs=pl.BlockSpec((1, T, w), lambda b, h: (b, 0, h)),
        out_shape=jax.ShapeDtypeStruct((B, T, Hd * w), BF16),
        compiler_params=_cparams(2),
        name="diff",
    )(dqk, dqk, dv, lam_q, lam_k, subln_gain.reshape(1, w), dbias)


def _merge_kernel(an_ref, ad_ref, wn_ref, wd_ref, gn_ref, gd_ref, o_ref, wnb_ref, wdb_ref):
    @pl.when(pl.program_id(1) == 0)
    def _():
        wnb_ref[...] = wn_ref[...].astype(BF16)
        wdb_ref[...] = wd_ref[...].astype(BF16)

    for rows in _row_chunks(an_ref.shape[0], MERGE_SUB_ROWS):
        yn = jnp.dot(an_ref[rows, :], wnb_ref[...], preferred_element_type=F32)
        yd = jnp.dot(ad_ref[rows, :], wdb_ref[...], preferred_element_type=F32)
        o_ref[rows, :] = (gn_ref[rows, :].astype(F32) * yn + gd_ref[rows, :].astype(F32) * yd).astype(o_ref.dtype)


def _merge(o_nsa, o_diff, w_n, w_d, mg, tm=1024, tn=1024):
    M, K = o_nsa.shape
    N = w_n.shape[1]
    nj = N // tn
    return pl.pallas_call(
        _merge_kernel,
        grid=(nj, M // tm),
        in_specs=[pl.BlockSpec((tm, K), lambda j, i: (i, 0)),
                  pl.BlockSpec((tm, K), lambda j, i: (i, 0)),
                  pl.BlockSpec((K, tn), lambda j, i: (0, j)),
                  pl.BlockSpec((K, tn), lambda j, i: (0, j)),
                  pl.BlockSpec((tm, tn), lambda j, i: (i, j)),
                  pl.BlockSpec((tm, tn), lambda j, i: (i, nj + j))],
        out_specs=pl.BlockSpec((tm, tn), lambda j, i: (i, j)),
        out_shape=jax.ShapeDtypeStruct((M, N), BF16),
        scratch_shapes=[pltpu.VMEM((K, tn), BF16), pltpu.VMEM((K, tn), BF16)],
        compiler_params=_cparams(2),
        name="merge",
    )(o_nsa, o_diff, w_n, w_d, mg, mg)


def _oproj_kernel(a_ref, w_ref, x_ref, g1_ref, gain_ref, sc_ref, sh_ref, x1_ref, h2_ref, wb_ref):
    @pl.when(pl.program_id(0) == 0)
    def _():
        wb_ref[...] = w_ref[...].astype(BF16)

    for rows in _row_chunks(a_ref.shape[0]):
        y = jnp.dot(a_ref[rows, :], wb_ref[...], preferred_element_type=F32)
        x1 = x_ref[rows, :] + g1_ref[0] * y
        x1_ref[rows, :] = x1
        h2_ref[rows, :] = _modnorm(x1, gain_ref[...], sc_ref[0], sh_ref[0]).astype(h2_ref.dtype)


def _oproj(merged, w_o, x2d, mod3, gain2, T, tm=512):
    M, D = x2d.shape
    per = T // tm
    return pl.pallas_call(
        _oproj_kernel,
        grid=(M // tm,),
        in_specs=[pl.BlockSpec((tm, D), lambda i: (i, 0)),
                  pl.BlockSpec((D, D), lambda i: (0, 0), pipeline_mode=pl.Buffered(1)),
                  pl.BlockSpec((tm, D), lambda i: (i, 0)),
                  pl.BlockSpec((1, 1, D), lambda i: ((i // per) * 6 + 2, 0, 0)),
                  pl.BlockSpec((1, D), lambda i: (0, 0)),
                  pl.BlockSpec((1, 1, D), lambda i: ((i // per) * 6 + 4, 0, 0)),
                  pl.BlockSpec((1, 1, D), lambda i: ((i // per) * 6 + 3, 0, 0))],
        out_specs=[pl.BlockSpec((tm, D), lambda i: (i, 0)),
                   pl.BlockSpec((tm, D), lambda i: (i, 0))],
        out_shape=[jax.ShapeDtypeStruct((M, D), F32), jax.ShapeDtypeStruct((M, D), BF16)],
        scratch_shapes=[pltpu.VMEM((D, D), BF16)],
        compiler_params=_cparams(1),
        name="oproj",
    )(merged, w_o, x2d, mod3, gain2.reshape(1, D), mod3, mod3)


def _ffn_up_kernel(h_ref, wa_ref, wv_ref, cwa_ref, cwv_ref, cba_ref, cbv_ref, o_ref, wab_ref, wvb_ref,
                   ca_ref, cv_ref, sa_ref, sv_ref, *, per):
    i = pl.program_id(1)

    @pl.when(i == 0)
    def _():
        wab_ref[...] = wa_ref[...].astype(BF16)
        wvb_ref[...] = wv_ref[...].astype(BF16)

    @pl.when(i % per == 0)
    def _():
        ca_ref[...] = jnp.zeros(ca_ref.shape, F32)
        cv_ref[...] = jnp.zeros(cv_ref.shape, F32)

    def conv(u, prev, cw_ref, cb_ref, s_ref):
        s_ref[0:8, :] = prev
        s_ref[8:8 + SUB_ROWS, :] = u
        u1 = s_ref[7:7 + SUB_ROWS, :]
        u2 = s_ref[6:6 + SUB_ROWS, :]
        return cb_ref[...] + cw_ref[0:1, :] * u2 + cw_ref[1:2, :] * u1 + cw_ref[2:3, :] * u

    prev_a, prev_v = ca_ref[...], cv_ref[...]
    for n, rows in enumerate(_row_chunks(h_ref.shape[0], SUB_ROWS)):
        hs = h_ref[rows, :]
        ua = jnp.dot(hs, wab_ref[...], preferred_element_type=F32)
        uv = jnp.dot(hs, wvb_ref[...], preferred_element_type=F32)
        a = conv(ua, prev_a, cwa_ref, cba_ref, sa_ref.at[n % 2])
        val = conv(uv, prev_v, cwv_ref, cbv_ref, sv_ref.at[n % 2])
        o_ref[rows, :] = (a * jax.nn.sigmoid(a) * val).astype(o_ref.dtype)
        prev_a, prev_v = ua[SUB_ROWS - 8:, :], uv[SUB_ROWS - 8:, :]
    ca_ref[...] = prev_a
    cv_ref[...] = prev_v


def _ffn_up(h2, w_up, conv_w, conv_b, T, tm=2048, tn=512):
    M, D = h2.shape
    F = w_up.shape[1] // 2
    nj = F // tn
    cb = conv_b.reshape(1, 2 * F)
    return pl.pallas_call(
        functools.partial(_ffn_up_kernel, per=T // tm),
        grid=(nj, M // tm),
        in_specs=[pl.BlockSpec((tm, D), lambda j, i: (i, 0)),
                  pl.BlockSpec((D, tn), lambda j, i: (0, j)),
                  pl.BlockSpec((D, tn), lambda j, i: (0, nj + j)),
                  pl.BlockSpec((3, tn), lambda j, i: (0, j)),
                  pl.BlockSpec((3, tn), lambda j, i: (0, nj + j)),
                  pl.BlockSpec((1, tn), lambda j, i: (0, j)),
                  pl.BlockSpec((1, tn), lambda j, i: (0, nj + j))],
        out_specs=pl.BlockSpec((tm, tn), lambda j, i: (i, j)),
        out_shape=jax.ShapeDtypeStruct((M, F), BF16),
        scratch_shapes=[pltpu.VMEM((D, tn), BF16), pltpu.VMEM((D, tn), BF16),
                        pltpu.VMEM((8, tn), F32), pltpu.VMEM((8, tn), F32),
                        pltpu.VMEM((2, SUB_ROWS + 8, tn), F32), pltpu.VMEM((2, SUB_ROWS + 8, tn), F32)],
        compiler_params=_cparams(2),
        name="ffn_up",
    )(h2, w_up, w_up, conv_w, conv_w, cb, cb)


def _ffn_down_kernel(a_ref, w_ref, x_ref, g2_ref, o_ref, wb_ref):
    @pl.when(pl.program_id(1) == 0)
    def _():
        wb_ref[...] = w_ref[...].astype(BF16)

    for rows in _row_chunks(a_ref.shape[0]):
        y = jnp.dot(a_ref[rows, :], wb_ref[...], preferred_element_type=F32)
        o_ref[rows, :] = x_ref[rows, :] + g2_ref[0] * y


def _ffn_down(act, w_down, x1, mod3, T, tm=512, tn=512):
    M, F = act.shape
    D = w_down.shape[1]
    per = T // tm
    return pl.pallas_call(
        _ffn_down_kernel,
        grid=(D // tn, M // tm),
        in_specs=[pl.BlockSpec((tm, F), lambda j, i: (i, 0)),
                  pl.BlockSpec((F, tn), lambda j, i: (0, j)),
                  pl.BlockSpec((tm, tn), lambda j, i: (i, j)),
                  pl.BlockSpec((1, 1, tn), lambda j, i: ((i // per) * 6 + 5, 0, j))],
        out_specs=pl.BlockSpec((tm, tn), lambda j, i: (i, j)),
        out_shape=jax.ShapeDtypeStruct((M, D), F32),
        scratch_shapes=[pltpu.VMEM((F, tn), BF16)],
        compiler_params=_cparams(2),
        name="ffn_down",
    )(act, w_down, x1, mod3)


def _layer(x, c, w_ada, b_ada, norm1_gain, norm2_gain, w_in, nsa_q_gain, nsa_k_gain, cmp_pe, cmp_w1, cmp_w2,
           diff_q_gain, diff_k_gain, diff_lambda_q, diff_lambda_k, diff_subln_gain, w_nsa_out, w_diff_out, w_o,
           w_ffn_up, ffn_conv_w, ffn_conv_b, w_ffn_down, rel_bias):
    B, T, D = x.shape
    dk, G = HEAD_DIM, NSA_KV_GROUPS
    M = B * T
    scale = dk ** -0.5

    n_q = NSA_HEADS * dk
    o_kv = n_q
    o_g = o_kv + 3 * 2 * G * dk
    o_dq = o_g + NSA_HEADS * 3
    o_dk = o_dq + DIFF_HEADS * 2 * dk
    o_dv = o_dk + DIFF_HEADS * 2 * dk
    o_mg = o_dv + DIFF_HEADS * 2 * dk
    n_kv = 2 * G * dk
    n_dqk = 2 * DIFF_HEADS * 2 * dk

    mod3 = _ada(c, w_ada, b_ada).reshape(B * 6, 1, D)

    ones = jnp.ones((n_kv // 2,), F32)
    g_q = jnp.tile(nsa_q_gain * (scale * LOG2E), NSA_HEADS).reshape(1, n_q)
    g_kv = jnp.concatenate([jnp.tile(nsa_k_gain[1], G), ones, jnp.tile(nsa_k_gain[2], G), ones]).reshape(1, 2 * n_kv)
    g_dqk = jnp.concatenate([jnp.tile(diff_q_gain * (scale * LOG2E), 2 * DIFF_HEADS),
                             jnp.tile(diff_k_gain, 2 * DIFF_HEADS)]).reshape(1, n_dqk)
    g_one = jnp.ones((1, 2 * D), F32)
    w_in_t = jnp.swapaxes(w_in, 1, 2)
    wide = INPROJ_TN
    yes, no = (True,) * (wide // LANE), (False,) * (wide // LANE)
    qn, h = _norm1_inproj(x.reshape(M, D), norm1_gain, mod3, w_in_t, g_q, T=T, src0=0, ncols=n_q,
                          heads=(True,) * NSA_HEADS, mode="raw", out_dtype=BF16, tm=INPROJ_TM // 2, name="inproj_q")
    proj = functools.partial(_inproj, h, w_in_t, tm=INPROJ_TM)
    cmpkv, gates = _inproj_cmp_gate(h, w_in_t, src_cmp=o_kv, n_cmp=n_kv, src_gate=o_g, tm=INPROJ_TM)
    kv_heads = ((True,) * G + (False,) * G) * 2
    kv = proj(g_kv, src0=o_kv + n_kv, ncols=2 * n_kv, heads=kv_heads, mode="raw", out_dtype=BF16, tn=2 * n_kv,
              name="inproj_kv")
    dqk = proj(g_dqk, src0=o_dq, ncols=n_dqk, heads=yes, mode="raw", out_dtype=BF16, tn=wide, name="inproj_dqk")
    dv = proj(g_one, src0=o_dv, ncols=o_mg - o_dv, heads=no, mode="raw", out_dtype=BF16, tn=wide, name="inproj_dv")
    mgate = proj(g_one, src0=o_mg, ncols=2 * D, heads=no, mode="sigmoid", out_dtype=BF16, tn=wide, name="inproj_mg")

    nbias = _bias_tiles(_window_idx(QB).transpose(0, 2, 1), rel_bias, head0=0, groups=G, hpg=NSA_HPG,
                        name="bias_nsa", rel=True, mult=LOG2E)
    cmp_idx, cmp_u0, cmp_per = _cmp_idx(T)
    cbias = _bias_tiles(cmp_idx.transpose(0, 2, 1), rel_bias, head0=0, groups=G, hpg=NSA_HPG, name="bias_cmp",
                        mult=LOG2E)
    dbias = _bias_tiles(_causal_idx(DQB), rel_bias, head0=NSA_HEADS, groups=DIFF_HEADS, hpg=1, name="bias_diff",
                        rel=True, mult=LOG2E)

    kc, vc = _compress(cmpkv.reshape(B, T, n_kv), cmp_pe, cmp_w1, cmp_w2, nsa_k_gain[0])
    o_nsa = _nsa_t(qn.reshape(B, T, n_q), kv.reshape(B, T, 2 * n_kv), kc, vc, gates.reshape(B, T, LANE), cbias,
                   cmp_u0, cmp_per, nbias)
    o_diff = _diff(dqk.reshape(B, T, n_dqk), dv.reshape(B, T, -1), diff_lambda_q, diff_lambda_k, diff_subln_gain,
                   dbias)

    merged = _merge(o_nsa.reshape(M, -1), o_diff.reshape(M, -1), w_nsa_out, w_diff_out, mgate)
    x1, h2 = _oproj(merged, w_o, x.reshape(M, D), mod3, norm2_gain, T)
    act = _ffn_up(h2, w_ffn_up, ffn_conv_w, ffn_conv_b, T)
    out = _ffn_down(act, w_ffn_down, x1, mod3, T)
    return out.reshape(B, T, D)


def kernel(x, c, w_ada, b_ada, norm1_gain, norm2_gain, w_in, nsa_q_gain, nsa_k_gain, cmp_pe, cmp_w1, cmp_w2,
           diff_q_gain, diff_k_gain, diff_lambda_q, diff_lambda_k, diff_subln_gain, w_nsa_out, w_diff_out, w_o,
           w_ffn_up, ffn_conv_w, ffn_conv_b, w_ffn_down, rel_bias):
    return _layer(x, c, w_ada[0], b_ada[0], norm1_gain[0], norm2_gain[0], w_in, nsa_q_gain[0], nsa_k_gain[0],
                  cmp_pe[0], cmp_w1[0], cmp_w2[0], diff_q_gain[0], diff_k_gain[0], diff_lambda_q[0],
                  diff_lambda_k[0], diff_subln_gain[0], w_nsa_out[0], w_diff_out[0], w_o[0], w_ffn_up[0],
                  ffn_conv_w[0], ffn_conv_b[0], w_ffn_down[0], rel_bias)
```

```python
import functools
import math

import numpy as np
import jax
import jax.numpy as jnp
from jax import lax
from jax.experimental import pallas as pl
from jax.experimental.pallas import tpu as pltpu

F32 = jnp.float32
BF16 = jnp.bfloat16

HEAD_DIM = 128
NSA_HEADS = 8
NSA_KV_GROUPS = 2
NSA_HPG = NSA_HEADS // NSA_KV_GROUPS
CMP_LEN = 32
CMP_STRIDE = 16
SEL_BLOCK = 64
N_SEL = 16
WINDOW = 512
DIFF_HEADS = 4
NUM_BUCKETS = 32
MAX_DISTANCE = 128
EPS = 1e-6
NEG = -1e30
LAM_INIT = 0.8 - 0.6 * math.exp(-0.3 * 0)
LOG2E = math.log2(math.e)
MASK_BIG = -(2.0 ** 100)

LANE = 128
QB = 128
DQB = 256
SUB_ROWS = 256
MERGE_SUB_ROWS = 512
INPROJ_TM = 2048
INPROJ_TN = 1024
VMEM_LIMIT = 56 * 1024 * 1024


def _cparams(n_axes):
    return pltpu.CompilerParams(dimension_semantics=("arbitrary",) * n_axes,
                                vmem_limit_bytes=VMEM_LIMIT)


def _t5_bucket_np(dist):
    n = np.maximum(np.asarray(dist, np.int32), 0)
    max_exact = NUM_BUCKETS // 2
    nf = np.maximum(n, max_exact).astype(np.float32)
    large = max_exact + (np.log(nf / np.float32(max_exact)) / np.float32(math.log(MAX_DISTANCE / max_exact))
                         * np.float32(NUM_BUCKETS - max_exact)).astype(np.int32)
    large = np.minimum(large, NUM_BUCKETS - 1)
    return np.where(n < max_exact, n, large).astype(np.int32)


def _ada_kernel(ct_ref, w_ref, b_ref, o_ref):
    ct = ct_ref[...]
    s = ct * jax.nn.sigmoid(ct)
    w = w_ref[...]
    for b in range(ct.shape[1]):
        o_ref[b:b + 1, :] = jnp.sum(w * s[:, b:b + 1], axis=0, keepdims=True) + b_ref[...]


def _ada(c, w_ada, b_ada, tn=1024):
    B, D = c.shape
    N = w_ada.shape[1]
    return pl.pallas_call(
        _ada_kernel,
        grid=(N // tn,),
        in_specs=[pl.BlockSpec((D, B), lambda j: (0, 0)),
                  pl.BlockSpec((D, tn), lambda j: (0, j)),
                  pl.BlockSpec((1, tn), lambda j: (0, j))],
        out_specs=pl.BlockSpec((B, tn), lambda j: (0, j)),
        out_shape=jax.ShapeDtypeStruct((B, N), F32),
        compiler_params=_cparams(1),
        name="ada",
    )(c.T, w_ada, b_ada.reshape(1, N))


def _modnorm(x, gain, sc, sh):
    ms = jnp.mean(x * x, axis=-1, keepdims=True)
    return (x * lax.rsqrt(ms + EPS) * gain) * (1.0 + sc) + sh


def _row_chunks(tm, sub=SUB_ROWS):
    return [slice(r, r + sub) for r in range(0, tm, sub)]


def _inproj_epilogue(acc, g_ref, o_ref, rows, heads, mode):
    for k, is_head in enumerate(heads):
        lanes = slice(k * LANE, (k + 1) * LANE)
        y = acc[:, lanes]
        if is_head:
            ms = jnp.mean(y * y, axis=-1, keepdims=True)
            y = y * lax.rsqrt(ms + EPS) * g_ref[:, lanes]
        elif mode == "sigmoid":
            y = jax.nn.sigmoid(y)
        o_ref[rows, lanes] = y.astype(o_ref.dtype)


def _inproj_kernel(a_ref, wt_ref, g_ref, o_ref, wb_ref, *, heads, mode):
    @pl.when(pl.program_id(1) == 0)
    def _():
        wb_ref[...] = wt_ref[0].astype(BF16)

    for rows in _row_chunks(a_ref.shape[0]):
        acc = _qk(a_ref[rows, :], wb_ref[...])
        _inproj_epilogue(acc, g_ref, o_ref, rows, heads, mode)


def _norm1_inproj_kernel(x_ref, n1g_ref, sc_ref, sh_ref, wt_ref, g_ref, o_ref, h_ref, wb_ref, *, heads, mode):
    @pl.when(pl.program_id(0) == 0)
    def _():
        wb_ref[...] = wt_ref[0].astype(BF16)

    for rows in _row_chunks(x_ref.shape[0]):
        h = _modnorm(x_ref[rows, :], n1g_ref[...], sc_ref[0], sh_ref[0]).astype(BF16)
        h_ref[rows, :] = h
        _inproj_epilogue(_qk(h, wb_ref[...]), g_ref, o_ref, rows, heads, mode)


def _norm1_inproj(x2d, n1_gain, mod3, w_in_t, gains, *, T, src0, ncols, heads, mode, out_dtype, tm, name):
    M, K = x2d.shape
    per = T // tm
    assert src0 % 8 == 0 and len(heads) == ncols // LANE
    return pl.pallas_call(
        functools.partial(_norm1_inproj_kernel, heads=heads, mode=mode),
        grid=(M // tm,),
        in_specs=[pl.BlockSpec((tm, K), lambda i: (i, 0)),
                  pl.BlockSpec((1, K), lambda i: (0, 0)),
                  pl.BlockSpec((1, 1, K), lambda i: ((i // per) * 6 + 1, 0, 0)),
                  pl.BlockSpec((1, 1, K), lambda i: ((i // per) * 6 + 0, 0, 0)),
                  pl.BlockSpec((pl.Element(1), pl.Element(ncols), pl.Element(K)), lambda i: (0, src0, 0)),
                  pl.BlockSpec((1, ncols), lambda i: (0, 0))],
        out_specs=[pl.BlockSpec((tm, ncols), lambda i: (i, 0)), pl.BlockSpec((tm, K), lambda i: (i, 0))],
        out_shape=[jax.ShapeDtypeStruct((M, ncols), out_dtype), jax.ShapeDtypeStruct((M, K), BF16)],
        scratch_shapes=[pltpu.VMEM((ncols, K), BF16)],
        compiler_params=_cparams(1),
        name=name,
    )(x2d, n1_gain.reshape(1, K), mod3, mod3, w_in_t, gains)


def _inproj(h2d, w_in_t, gains, *, src0, ncols, heads, mode, out_dtype, tm, tn, name):
    M, K = h2d.shape
    assert src0 % 8 == 0 and ncols % tn == 0 and len(heads) == tn // LANE
    return pl.pallas_call(
        functools.partial(_inproj_kernel, heads=heads, mode=mode),
        grid=(ncols // tn, M // tm),
        in_specs=[pl.BlockSpec((tm, K), lambda j, i: (i, 0)),
                  pl.BlockSpec((pl.Element(1), pl.Element(tn), pl.Element(K)),
                               lambda j, i: (0, pl.multiple_of(src0 + j * tn, 8), 0)),
                  pl.BlockSpec((1, tn), lambda j, i: (0, j))],
        out_specs=pl.BlockSpec((tm, tn), lambda j, i: (i, j)),
        out_shape=jax.ShapeDtypeStruct((M, ncols), out_dtype),
        scratch_shapes=[pltpu.VMEM((tn, K), BF16)],
        compiler_params=_cparams(2),
        name=name,
    )(h2d, w_in_t, gains)


def _inproj_cmp_gate_kernel(a_ref, wc_ref, wg_ref, oc_ref, og_ref, wcb_ref, wgb_ref):
    @pl.when(pl.program_id(0) == 0)
    def _():
        wcb_ref[...] = wc_ref[0].astype(BF16)
        wgb_ref[...] = wg_ref[0].astype(BF16)

    for rows in _row_chunks(a_ref.shape[0]):
        a = a_ref[rows, :]
        oc_ref[rows, :] = _qk(a, wcb_ref[...])
        og_ref[rows, :] = jax.nn.sigmoid(_qk(a, wgb_ref[...]))


def _inproj_cmp_gate(h2d, w_in_t, *, src_cmp, n_cmp, src_gate, tm):
    M, K = h2d.shape
    assert src_cmp % 8 == 0 and src_gate % 8 == 0

    def w_rows(start, n):
        return pl.BlockSpec((pl.Element(1), pl.Element(n), pl.Element(K)), lambda i: (0, start, 0))

    return pl.pallas_call(
        _inproj_cmp_gate_kernel,
        grid=(M // tm,),
        in_specs=[pl.BlockSpec((tm, K), lambda i: (i, 0)), w_rows(src_cmp, n_cmp), w_rows(src_gate, LANE)],
        out_specs=[pl.BlockSpec((tm, n_cmp), lambda i: (i, 0)), pl.BlockSpec((tm, LANE), lambda i: (i, 0))],
        out_shape=[jax.ShapeDtypeStruct((M, n_cmp), F32), jax.ShapeDtypeStruct((M, LANE), F32)],
        scratch_shapes=[pltpu.VMEM((n_cmp, K), BF16), pltpu.VMEM((LANE, K), BF16)],
        compiler_params=_cparams(1),
        name="inproj_cmp_gate",
    )(h2d, w_in_t, w_in_t)


def _bias_kernel(idx_ref, tab_ref, o_ref, *, head0, hpg, rel, mult):
    n_tiles, _, C = idx_ref.shape
    for n in range(n_tiles):
        idx = idx_ref[n]
        for h in range(hpg):
            head = head0 + pl.program_id(0) * hpg + h
            acc = jnp.zeros(idx.shape, F32)
            for b in range(NUM_BUCKETS):
                acc = jnp.where(idx == b, tab_ref[b, head], acc)
            if rel:
                acc = acc - tab_ref[NUM_BUCKETS - 1, head]
            o_ref[0, n, :, h * C:(h + 1) * C] = jnp.where(idx < 0, NEG, acc * mult)


def _bias_tiles(idx, rel_bias, *, head0, groups, hpg, name, rel=False, mult=1.0):
    N, R, C = idx.shape
    return pl.pallas_call(
        functools.partial(_bias_kernel, head0=head0, hpg=hpg, rel=rel, mult=mult),
        grid=(groups,),
        in_specs=[pl.BlockSpec((N, R, C), lambda g: (0, 0, 0)),
                  pl.BlockSpec(memory_space=pltpu.SMEM)],
        out_specs=pl.BlockSpec((1, N, R, hpg * C), lambda g: (g, 0, 0, 0)),
        out_shape=jax.ShapeDtypeStruct((groups, N, R, hpg * C), F32),
        compiler_params=_cparams(1),
        name=name,
    )(jnp.asarray(idx), rel_bias)


def _causal_idx(R):
    r = np.arange(R)[:, None]
    c = np.arange(R)[None, :]
    return np.stack([np.where(r >= c, _t5_bucket_np(r - c), -1), _t5_bucket_np(R + r - c)]).astype(np.int32)


def _window_idx(R):
    r = np.arange(R)[:, None]
    c = np.arange(R)[None, :]
    edge = np.where(r < c, NUM_BUCKETS - 1, -1)
    return np.concatenate([_causal_idx(R), edge[None]]).astype(np.int32)


def _cmp_idx(T):
    per = QB // CMP_STRIDE
    u0 = (T // QB - 1) * per
    assert u0 + LANE <= 2 * LANE
    r = np.arange(QB)[:, None]
    end = (np.arange(2 * LANE)[None, :] - u0) * CMP_STRIDE + CMP_LEN - 1
    return np.where(end <= r, _t5_bucket_np(r - end), -1).astype(np.int32)[None], u0, per


def _compress_kernel(zk_ref, zv_ref, pe_ref, w1_ref, w2_ref, kg_ref, kc_ref, vc_ref):
    half = CMP_LEN // 2

    def one(z_ref, i):
        p1 = jnp.zeros((LANE, HEAD_DIM), F32)
        p2 = jnp.zeros((LANE, HEAD_DIM), F32)
        for l in range(half):
            z = z_ref[0, pl.ds(l, LANE, stride=CMP_STRIDE), :]
            w_lo = w1_ref[i, l * HEAD_DIM:(l + 1) * HEAD_DIM, :].astype(BF16)
            w_hi = w1_ref[i, (half + l) * HEAD_DIM:(half + l + 1) * HEAD_DIM, :].astype(BF16)
            p1 = p1 + jnp.dot((z + pe_ref[i, l:l + 1, :]).astype(BF16), w_lo, preferred_element_type=F32)
            p2 = p2 + jnp.dot((z + pe_ref[i, half + l:half + l + 1, :]).astype(BF16), w_hi,
                              preferred_element_type=F32)
        pre = p1 + pltpu.roll(p2, LANE - 1, axis=0)
        hid = jax.nn.gelu(pre)
        return jnp.dot(hid.astype(BF16), w2_ref[i].astype(BF16), preferred_element_type=F32)

    kc = one(zk_ref, 0)
    ms = jnp.mean(kc * kc, axis=-1, keepdims=True)
    kc_ref[0, 0] = (kc * lax.rsqrt(ms + EPS) * kg_ref[...]).astype(kc_ref.dtype)
    vc_ref[0, 0] = one(zv_ref, 1).astype(vc_ref.dtype)


def _compress(cmp_kv, pe, w1, w2, k_gain0):
    B, T, _ = cmp_kv.shape
    G, dk = NSA_KV_GROUPS, HEAD_DIM
    assert (T - CMP_LEN) // CMP_STRIDE + 1 == LANE - 1
    out = jax.ShapeDtypeStruct((B, G, LANE, dk), BF16)
    return pl.pallas_call(
        _compress_kernel,
        grid=(B, G),
        in_specs=[pl.BlockSpec((1, T, dk), lambda b, g: (b, 0, g)),
                  pl.BlockSpec((1, T, dk), lambda b, g: (b, 0, G + g)),
                  pl.BlockSpec((2, CMP_LEN, dk), lambda b, g: (0, 0, 0)),
                  pl.BlockSpec((2, CMP_LEN * dk, dk), lambda b, g: (0, 0, 0)),
                  pl.BlockSpec((2, dk, dk), lambda b, g: (0, 0, 0)),
                  pl.BlockSpec((1, dk), lambda b, g: (0, 0))],
        out_specs=[pl.BlockSpec((1, 1, LANE, dk), lambda b, g: (b, g, 0, 0))] * 2,
        out_shape=[out, out],
        compiler_params=_cparams(2),
        name="compress",
    )(cmp_kv, cmp_kv, pe, w1, w2, k_gain0.reshape(1, dk))


def _qk(q, k):
    return lax.dot_general(q, k, (((1,), (1,)), ((), ())), preferred_element_type=F32)


def _lane_fold(x, op):
    acc = x[..., :LANE]
    for t in range(1, x.shape[-1] // LANE):
        acc = op(acc, x[..., t * LANE:(t + 1) * LANE])
    return acc


def _softmax_parts(parts):
    m = jnp.max(functools.reduce(jnp.maximum, [_lane_fold(s, jnp.maximum) for s in parts]), axis=-1, keepdims=True)
    ps = [jnp.exp2(s - m) for s in parts]
    den = jnp.sum(functools.reduce(jnp.add, [_lane_fold(p, jnp.add) for p in ps]), axis=-1, keepdims=True)
    return ps, den


def _nsa_t_kernel(q_ref, ks_ref, vs_ref, kw_ref, vw_ref, kc_ref, vc_ref, gate_ref, cb_ref, nb_ref,
                  ovt_ref, expt_ref, o_ref, vst_ref, vwt_ref, vct_ref, ksa_ref, *, cmp_u0, cmp_per):
    H, W = NSA_HPG, NSA_HPG * QB
    T = ks_ref.shape[1]
    ns = ovt_ref.shape[0]
    nwb = WINDOW // QB
    nb = T // QB
    kc = kc_ref[0, 0]

    @pl.when(pl.program_id(2) == 0)
    def _():
        ksa_ref[:, :HEAD_DIM] = ks_ref[0]
        ksa_ref[:, HEAD_DIM:] = expt_ref[...]
        vst_ref[...] = vs_ref[0].astype(F32).T.astype(BF16)
        vwt_ref[...] = vw_ref[0].astype(F32).T.astype(BF16)
        vct_ref[...] = vc_ref[0, 0].astype(F32).T.astype(BF16)

    def softmax_t(parts):
        m = functools.reduce(jnp.maximum, [jnp.max(s, axis=0, keepdims=True) for s in parts])
        ps = [jnp.exp2(s - m) for s in parts]
        den = functools.reduce(jnp.add, [jnp.sum(p, axis=0, keepdims=True) for p in ps])
        return [p.astype(BF16) for p in ps], den

    def attend_task(vt_ref, spans_fn, out, key):
        spans = spans_fn()
        parts = []
        for _, _, k_slab, q_op, add in spans:
            s = _qk(k_slab, q_op)
            parts.append(s if add is None else s + add)
        yield
        ps, den = softmax_t(parts)
        yield
        o = None
        for (a, b, _, _, _), p in zip(spans, ps):
            pv = jnp.dot(vt_ref[:, a:b], p, preferred_element_type=F32)
            o = pv if o is None else o + pv
        out[key] = o / den
        yield

    def cmp_task(i, qs, out):
        lo = i * QB
        u = cmp_u0 - cmp_per * i
        s = _qk(kc, qs) + cb_ref[0, 0, u:u + LANE, :]
        yield
        e = jnp.exp2(s - jnp.max(s, axis=0, keepdims=True))
        if i == 0:
            key_end = lax.broadcasted_iota(jnp.int32, (LANE, W), 0) * CMP_STRIDE + (CMP_LEN - 1)
            query = lax.broadcasted_iota(jnp.int32, (LANE, W), 1) % QB
            e = jnp.where(key_end <= query, e, 0.0)
            den = jnp.sum(e, axis=0, keepdims=True)
            p = e / jnp.where(den > 0.0, den, 1.0)
        else:
            p = e / jnp.sum(e, axis=0, keepdims=True)
        if i >= 1:
            psum = functools.reduce(jnp.add, [p[:, h * QB:(h + 1) * QB] for h in range(H)])
            imp_t = jnp.dot(ovt_ref[...], psum, precision=lax.Precision.HIGHEST, preferred_element_type=F32)
        yield
        out["cmp", i] = jnp.dot(vct_ref[...], p.astype(BF16), preferred_element_type=F32)
        if i >= 1:
            blk = lax.broadcasted_iota(jnp.int32, (ns, QB), 0)
            cur = (lo + lax.broadcasted_iota(jnp.int32, (ns, QB), 1)) // SEL_BLOCK
            forced = (blk == 0) | (blk == cur) | (blk == cur - 1)
            score = jnp.where(forced, 1e4, jnp.where(blk <= cur, imp_t, -1e4))
            rank = jnp.zeros((ns, QB), F32)
            for b in range(ns):
                other = score[b:b + 1, :]
                rank = rank + jnp.where(blk > b, jnp.where(other >= score, 1.0, 0.0),
                                        jnp.where(other > score, 1.0, 0.0))
            unsel = jnp.where(rank < float(min(N_SEL, ns)), 0.0, 1.0)
            unsel_q = jnp.concatenate([unsel, jnp.zeros((LANE - ns, QB), F32)], axis=0).T.astype(BF16)
            out["qs_aug", i] = jnp.concatenate([qs, jnp.concatenate([unsel_q] * H, axis=0)], axis=1)
        yield

    def slc_spans(i, qs, out):
        lo, hi = i * QB, (i + 1) * QB
        spans = [(lo, hi, ks_ref[0, lo:hi, :], qs, nb_ref[0, 0])]
        if i >= 1:
            qs_aug = out["qs_aug", i]
            spans.insert(0, (lo - QB, lo, ksa_ref[lo - QB:lo, :], qs_aug, nb_ref[0, 1]))
            if i >= 2:
                spans.insert(0, (0, lo - QB, ksa_ref[0:lo - QB, :], qs_aug, None))
        return spans

    def win_spans(i, qs):
        lo, hi = i * QB, (i + 1) * QB

        def span(a, b, add):
            return (a, b, kw_ref[0, a:b, :], qs, add)

        spans = []
        if i >= nwb:
            spans.append(span((i - nwb) * QB, (i - nwb + 1) * QB, nb_ref[0, 2]))
        mid_a, mid_b = max(i - nwb + 1, 0) * QB, (i - 1) * QB
        if mid_b > mid_a:
            spans.append(span(mid_a, mid_b, None))
        if i >= 1:
            spans.append(span(lo - QB, lo, nb_ref[0, 1]))
        spans.append(span(lo, hi, nb_ref[0, 0]))
        return spans

    def combine(i, out):
        lo, hi = i * QB, (i + 1) * QB
        gate_t = gate_ref[0, lo:hi, :].T
        first = pl.program_id(1) == 0

        def grow(br):
            rows = []
            for h in range(H):
                c = 3 * h + br
                rows.append(jnp.where(first, gate_t[c:c + 1, :], gate_t[3 * H + c:3 * H + c + 1, :]))
            return jnp.concatenate(rows, axis=1)

        o_t = grow(0) * out["cmp", i] + grow(1) * out["slc", i] + grow(2) * out["win", i]
        o_ref[0, lo:hi, :] = jnp.concatenate([o_t[:, h * QB:(h + 1) * QB].T for h in range(H)],
                                             axis=1).astype(o_ref.dtype)

    def pair(k):
        out = {}
        tasks = []
        blocks = (nb - 1 - k, k)
        qss = {}
        for i in blocks:
            q = q_ref[0, i * QB:(i + 1) * QB, :]
            qss[i] = jnp.concatenate([q[:, h * HEAD_DIM:(h + 1) * HEAD_DIM] for h in range(H)], axis=0)
            tasks.append(cmp_task(i, qss[i], out))
        for i in blocks:
            tasks.append(attend_task(vwt_ref, functools.partial(win_spans, i, qss[i]), out, ("win", i)))
        for i in blocks:
            tasks.append(attend_task(vst_ref, functools.partial(slc_spans, i, qss[i], out), out, ("slc", i)))
        n_stage = 3
        for step in range(len(tasks) + n_stage - 1):
            for t in range(step - n_stage + 1, step + 1):
                if 0 <= t < len(tasks):
                    next(tasks[t])
        for i in blocks:
            combine(i, out)

    for k in range(nb // 2):
        pl.when(pl.program_id(2) == k)(functools.partial(pair, k))


def _nsa_t(qn, kv, kc, vc, gates, cbias, cmp_u0, cmp_per, nbias):
    B, T, _ = qn.shape
    G, H, dk = NSA_KV_GROUPS, NSA_HPG, HEAD_DIM
    nb = T // QB
    ns = T // SEL_BLOCK
    cstart = np.arange(LANE) * CMP_STRIDE
    sstart = np.arange(ns) * SEL_BLOCK
    overlap = np.clip(np.minimum(cstart[:, None] + CMP_LEN, sstart[None, :] + SEL_BLOCK)
                      - np.maximum(cstart[:, None], sstart[None, :]), 0, None) / CMP_STRIDE
    overlap[LANE - 1:] = 0.0
    ovt = jnp.asarray(overlap.T, F32)
    expand_t = np.zeros((T, LANE), np.float32)
    expand_t[np.arange(T), np.arange(T) // SEL_BLOCK] = MASK_BIG
    return pl.pallas_call(
        functools.partial(_nsa_t_kernel, cmp_u0=cmp_u0, cmp_per=cmp_per),
        grid=(B, G, nb // 2),
        in_specs=[pl.BlockSpec((1, T, H * dk), lambda b, g, i: (b, 0, g)),
                  pl.BlockSpec((1, T, dk), lambda b, g, i: (b, 0, g)),
                  pl.BlockSpec((1, T, dk), lambda b, g, i: (b, 0, G + g)),
                  pl.BlockSpec((1, T, dk), lambda b, g, i: (b, 0, 2 * G + g)),
                  pl.BlockSpec((1, T, dk), lambda b, g, i: (b, 0, 3 * G + g)),
                  pl.BlockSpec((1, 1, LANE, dk), lambda b, g, i: (b, g, 0, 0)),
                  pl.BlockSpec((1, 1, LANE, dk), lambda b, g, i: (b, g, 0, 0)),
                  pl.BlockSpec((1, T, LANE), lambda b, g, i: (b, 0, 0)),
                  pl.BlockSpec((1, 1, 2 * LANE, H * QB), lambda b, g, i: (g, 0, 0, 0)),
                  pl.BlockSpec((1, 3, QB, H * QB), lambda b, g, i: (g, 0, 0, 0)),
                  pl.BlockSpec((ns, LANE), lambda b, g, i: (0, 0)),
                  pl.BlockSpec((T, LANE), lambda b, g, i: (0, 0))],
        out_specs=pl.BlockSpec((1, T, H * dk), lambda b, g, i: (b, 0, g)),
        out_shape=jax.ShapeDtypeStruct((B, T, NSA_HEADS * dk), BF16),
        scratch_shapes=[pltpu.VMEM((dk, T), BF16), pltpu.VMEM((dk, T), BF16), pltpu.VMEM((dk, LANE), BF16),
                        pltpu.VMEM((T, 2 * dk), BF16)],
        compiler_params=_cparams(3),
        name="nsa",
    )(qn, kv, kv, kv, kv, kc, vc, gates, cbias, nbias, ovt, jnp.asarray(expand_t, BF16))


def _diff_kernel(q_ref, k_ref, v_ref, lq_ref, lk_ref, sg_ref, db_ref, o_ref):
    dk = HEAD_DIM
    T = k_ref.shape[1]
    lqk = lq_ref[...] * lk_ref[...]
    lam = (jnp.exp(jnp.sum(lqk[0:1], axis=-1, keepdims=True))
           - jnp.exp(jnp.sum(lqk[1:2], axis=-1, keepdims=True)) + LAM_INIT)
    def task(i, mm, out):
        lo, hi = i * DQB, (i + 1) * DQB
        cols = slice(mm * dk, (mm + 1) * dk)
        q = q_ref[0, lo:hi, cols]
        bounds, parts = [], []
        if i >= 2:
            bounds.append((0, lo - DQB))
            parts.append(_qk(q, k_ref[0, 0:lo - DQB, cols]))
        if i >= 1:
            bounds.append((lo - DQB, lo))
            parts.append(_qk(q, k_ref[0, lo - DQB:lo, cols]) + db_ref[0, 1])
        bounds.append((lo, hi))
        parts.append(_qk(q, k_ref[0, lo:hi, cols]) + db_ref[0, 0])
        yield
        ps, den = _softmax_parts(parts)
        yield
        o = None
        for (a, b), p in zip(bounds, ps):
            pv = jnp.dot(p.astype(BF16), v_ref[0, a:b, :], preferred_element_type=F32)
            o = pv if o is None else o + pv
        out[i, mm] = o / den
        yield

    out = {}
    tasks = [task(i, mm, out) for i in reversed(range(T // DQB)) for mm in range(2)]
    n_stage = 3
    for step in range(len(tasks) + n_stage - 1):
        for t in range(step - n_stage + 1, step + 1):
            if 0 <= t < len(tasks):
                next(tasks[t])

    for i in range(T // DQB):
        lo, hi = i * DQB, (i + 1) * DQB
        o = out[i, 0] - lam * out[i, 1]
        ms = jnp.mean(o * o, axis=-1, keepdims=True)
        o_ref[0, lo:hi, :] = ((o * lax.rsqrt(ms + EPS) * sg_ref[...]) * (1.0 - LAM_INIT)).astype(o_ref.dtype)


def _diff(dqk, dv, lam_q, lam_k, subln_gain, dbias):
    B, T, _ = dqk.shape
    Hd, dk = DIFF_HEADS, HEAD_DIM
    w = 2 * dk
    return pl.pallas_call(
        _diff_kernel,
        grid=(B, Hd),
        in_specs=[pl.BlockSpec((1, T, w), lambda b, h: (b, 0, h)),
                  pl.BlockSpec((1, T, w), lambda b, h: (b, 0, Hd + h)),
                  pl.BlockSpec((1, T, w), lambda b, h: (b, 0, h)),
                  pl.BlockSpec((2, dk), lambda b, h: (0, 0)),
                  pl.BlockSpec((2, dk), lambda b, h: (0, 0)),
                  pl.BlockSpec((1, w), lambda b, h: (0, 0)),
                  pl.BlockSpec((1, 2, DQB, DQB), lambda b, h: (h, 0, 0, 0))],
        out_specs=pl.BlockSpec((1, T, w), lambda b, h: (b, 0, h)),
        out_shape=jax.ShapeDtypeStruct((B, T, Hd * w), BF16),
        compiler_params=_cparams(2),
        name="diff",
    )(dqk, dqk, dv, lam_q, lam_k, subln_gain.reshape(1, w), dbias)


def _merge_kernel(an_ref, ad_ref, wn_ref, wd_ref, gn_ref, gd_ref, o_ref, wnb_ref, wdb_ref):
    @pl.when(pl.program_id(1) == 0)
    def _():
        wnb_ref[...] = wn_ref[...].astype(BF16)
        wdb_ref[...] = wd_ref[...].astype(BF16)

    for rows in _row_chunks(an_ref.shape[0], MERGE_SUB_ROWS):
        yn = jnp.dot(an_ref[rows, :], wnb_ref[...], preferred_element_type=F32)
        yd = jnp.dot(ad_ref[rows, :], wdb_ref[...], preferred_element_type=F32)
        o_ref[rows, :] = (gn_ref[rows, :].astype(F32) * yn + gd_ref[rows, :].astype(F32) * yd).astype(o_ref.dtype)


def _merge(o_nsa, o_diff, w_n, w_d, mg, tm=1024, tn=1024):
    M, K = o_nsa.shape
    N = w_n.shape[1]
    nj = N // tn
    return pl.pallas_call(
        _merge_kernel,
        grid=(nj, M // tm),
        in_specs=[pl.BlockSpec((tm, K), lambda j, i: (i, 0)),
                  pl.BlockSpec((tm, K), lambda j, i: (i, 0)),
                  pl.BlockSpec((K, tn), lambda j, i: (0, j)),
                  pl.BlockSpec((K, tn), lambda j, i: (0, j)),
                  pl.BlockSpec((tm, tn), lambda j, i: (i, j)),
                  pl.BlockSpec((tm, tn), lambda j, i: (i, nj + j))],
        out_specs=pl.BlockSpec((tm, tn), lambda j, i: (i, j)),
        out_shape=jax.ShapeDtypeStruct((M, N), BF16),
        scratch_shapes=[pltpu.VMEM((K, tn), BF16), pltpu.VMEM((K, tn), BF16)],
        compiler_params=_cparams(2),
        name="merge",
    )(o_nsa, o_diff, w_n, w_d, mg, mg)


def _oproj_kernel(a_ref, w_ref, x_ref, g1_ref, gain_ref, sc_ref, sh_ref, x1_ref, h2_ref, wb_ref):
    @pl.when(pl.program_id(0) == 0)
    def _():
        wb_ref[...] = w_ref[...].astype(BF16)

    for rows in _row_chunks(a_ref.shape[0]):
        y = jnp.dot(a_ref[rows, :], wb_ref[...], preferred_element_type=F32)
        x1 = x_ref[rows, :] + g1_ref[0] * y
        x1_ref[rows, :] = x1
        h2_ref[rows, :] = _modnorm(x1, gain_ref[...], sc_ref[0], sh_ref[0]).astype(h2_ref.dtype)


def _oproj(merged, w_o, x2d, mod3, gain2, T, tm=512):
    M, D = x2d.shape
    per = T // tm
    return pl.pallas_call(
        _oproj_kernel,
        grid=(M // tm,),
        in_specs=[pl.BlockSpec((tm, D), lambda i: (i, 0)),
                  pl.BlockSpec((D, D), lambda i: (0, 0), pipeline_mode=pl.Buffered(1)),
                  pl.BlockSpec((tm, D), lambda i: (i, 0)),
                  pl.BlockSpec((1, 1, D), lambda i: ((i // per) * 6 + 2, 0, 0)),
                  pl.BlockSpec((1, D), lambda i: (0, 0)),
                  pl.BlockSpec((1, 1, D), lambda i: ((i // per) * 6 + 4, 0, 0)),
                  pl.BlockSpec((1, 1, D), lambda i: ((i // per) * 6 + 3, 0, 0))],
        out_specs=[pl.BlockSpec((tm, D), lambda i: (i, 0)),
                   pl.BlockSpec((tm, D), lambda i: (i, 0))],
        out_shape=[jax.ShapeDtypeStruct((M, D), F32), jax.ShapeDtypeStruct((M, D), BF16)],
        scratch_shapes=[pltpu.VMEM((D, D), BF16)],
        compiler_params=_cparams(1),
        name="oproj",
    )(merged, w_o, x2d, mod3, gain2.reshape(1, D), mod3, mod3)


def _ffn_up_kernel(h_ref, wa_ref, wv_ref, cwa_ref, cwv_ref, cba_ref, cbv_ref, o_ref, wab_ref, wvb_ref,
                   ca_ref, cv_ref, sa_ref, sv_ref, *, per):
    i = pl.program_id(1)

    @pl.when(i == 0)
    def _():
        wab_ref[...] = wa_ref[...].astype(BF16)
        wvb_ref[...] = wv_ref[...].astype(BF16)

    @pl.when(i % per == 0)
    def _():
        ca_ref[...] = jnp.zeros(ca_ref.shape, F32)
        cv_ref[...] = jnp.zeros(cv_ref.shape, F32)

    def conv(u, prev, cw_ref, cb_ref, s_ref):
        s_ref[0:8, :] = prev
        s_ref[8:8 + SUB_ROWS, :] = u
        u1 = s_ref[7:7 + SUB_ROWS, :]
        u2 = s_ref[6:6 + SUB_ROWS, :]
        return cb_ref[...] + cw_ref[0:1, :] * u2 + cw_ref[1:2, :] * u1 + cw_ref[2:3, :] * u

    prev_a, prev_v = ca_ref[...], cv_ref[...]
    for n, rows in enumerate(_row_chunks(h_ref.shape[0], SUB_ROWS)):
        hs = h_ref[rows, :]
        ua = jnp.dot(hs, wab_ref[...], preferred_element_type=F32)
        uv = jnp.dot(hs, wvb_ref[...], preferred_element_type=F32)
        a = conv(ua, prev_a, cwa_ref, cba_ref, sa_ref.at[n % 2])
        val = conv(uv, prev_v, cwv_ref, cbv_ref, sv_ref.at[n % 2])
        o_ref[rows, :] = (a * jax.nn.sigmoid(a) * val).astype(o_ref.dtype)
        prev_a, prev_v = ua[SUB_ROWS - 8:, :], uv[SUB_ROWS - 8:, :]
    ca_ref[...] = prev_a
    cv_ref[...] = prev_v


def _ffn_up(h2, w_up, conv_w, conv_b, T, tm=2048, tn=512):
    M, D = h2.shape
    F = w_up.shape[1] // 2
    nj = F // tn
    cb = conv_b.reshape(1, 2 * F)
    return pl.pallas_call(
        functools.partial(_ffn_up_kernel, per=T // tm),
        grid=(nj, M // tm),
        in_specs=[pl.BlockSpec((tm, D), lambda j, i: (i, 0)),
                  pl.BlockSpec((D, tn), lambda j, i: (0, j)),
                  pl.BlockSpec((D, tn), lambda j, i: (0, nj + j)),
                  pl.BlockSpec((3, tn), lambda j, i: (0, j)),
                  pl.BlockSpec((3, tn), lambda j, i: (0, nj + j)),
                  pl.BlockSpec((1, tn), lambda j, i: (0, j)),
                  pl.BlockSpec((1, tn), lambda j, i: (0, nj + j))],
        out_specs=pl.BlockSpec((tm, tn), lambda j, i: (i, j)),
        out_shape=jax.ShapeDtypeStruct((M, F), BF16),
        scratch_shapes=[pltpu.VMEM((D, tn), BF16), pltpu.VMEM((D, tn), BF16),
                        pltpu.VMEM((8, tn), F32), pltpu.VMEM((8, tn), F32),
                        pltpu.VMEM((2, SUB_ROWS + 8, tn), F32), pltpu.VMEM((2, SUB_ROWS + 8, tn), F32)],
        compiler_params=_cparams(2),
        name="ffn_up",
    )(h2, w_up, w_up, conv_w, conv_w, cb, cb)


def _ffn_down_kernel(a_ref, w_ref, x_ref, g2_ref, o_ref, wb_ref):
    @pl.when(pl.program_id(1) == 0)
    def _():
        wb_ref[...] = w_ref[...].astype(BF16)

    for rows in _row_chunks(a_ref.shape[0]):
        y = jnp.dot(a_ref[rows, :], wb_ref[...], preferred_element_type=F32)
        o_ref[rows, :] = x_ref[rows, :] + g2_ref[0] * y


def _ffn_down(act, w_down, x1, mod3, T, tm=512, tn=512):
    M, F = act.shape
    D = w_down.shape[1]
    per = T // tm
    return pl.pallas_call(
        _ffn_down_kernel,
        grid=(D // tn, M // tm),
        in_specs=[pl.BlockSpec((tm, F), lambda j, i: (i, 0)),
                  pl.BlockSpec((F, tn), lambda j, i: (0, j)),
                  pl.BlockSpec((tm, tn), lambda j, i: (i, j)),
                  pl.BlockSpec((1, 1, tn), lambda j, i: ((i // per) * 6 + 5, 0, j))],
        out_specs=pl.BlockSpec((tm, tn), lambda j, i: (i, j)),
        out_shape=jax.ShapeDtypeStruct((M, D), F32),
        scratch_shapes=[pltpu.VMEM((F, tn), BF16)],
        compiler_params=_cparams(2),
        name="ffn_down",
    )(act, w_down, x1, mod3)


def _layer(x, c, w_ada, b_ada, norm1_gain, norm2_gain, w_in, nsa_q_gain, nsa_k_gain, cmp_pe, cmp_w1, cmp_w2,
           diff_q_gain, diff_k_gain, diff_lambda_q, diff_lambda_k, diff_subln_gain, w_nsa_out, w_diff_out, w_o,
           w_ffn_up, ffn_conv_w, ffn_conv_b, w_ffn_down, rel_bias):
    B, T, D = x.shape
    dk, G = HEAD_DIM, NSA_KV_GROUPS
    M = B * T
    scale = dk ** -0.5

    n_q = NSA_HEADS * dk
    o_kv = n_q
    o_g = o_kv + 3 * 2 * G * dk
    o_dq = o_g + NSA_HEADS * 3
    o_dk = o_dq + DIFF_HEADS * 2 * dk
    o_dv = o_dk + DIFF_HEADS * 2 * dk
    o_mg = o_dv + DIFF_HEADS * 2 * dk
    n_kv = 2 * G * dk
    n_dqk = 2 * DIFF_HEADS * 2 * dk

    mod3 = _ada(c, w_ada, b_ada).reshape(B * 6, 1, D)

    ones = jnp.ones((n_kv // 2,), F32)
    g_q = jnp.tile(nsa_q_gain * (scale * LOG2E), NSA_HEADS).reshape(1, n_q)
    g_kv = jnp.concatenate([jnp.tile(nsa_k_gain[1], G), ones, jnp.tile(nsa_k_gain[2], G), ones]).reshape(1, 2 * n_kv)
    g_dqk = jnp.concatenate([jnp.tile(diff_q_gain * (scale * LOG2E), 2 * DIFF_HEADS),
                             jnp.tile(diff_k_gain, 2 * DIFF_HEADS)]).reshape(1, n_dqk)
    g_one = jnp.ones((1, 2 * D), F32)
    w_in_t = jnp.swapaxes(w_in, 1, 2)
    wide = INPROJ_TN
    yes, no = (True,) * (wide // LANE), (False,) * (wide // LANE)
    qn, h = _norm1_inproj(x.reshape(M, D), norm1_gain, mod3, w_in_t, g_q, T=T, src0=0, ncols=n_q,
                          heads=(True,) * NSA_HEADS, mode="raw", out_dtype=BF16, tm=INPROJ_TM // 2, name="inproj_q")
    proj = functools.partial(_inproj, h, w_in_t, tm=INPROJ_TM)
    cmpkv, gates = _inproj_cmp_gate(h, w_in_t, src_cmp=o_kv, n_cmp=n_kv, src_gate=o_g, tm=INPROJ_TM)
    kv_heads = ((True,) * G + (False,) * G) * 2
    kv = proj(g_kv, src0=o_kv + n_kv, ncols=2 * n_kv, heads=kv_heads, mode="raw", out_dtype=BF16, tn=2 * n_kv,
              name="inproj_kv")
    dqk = proj(g_dqk, src0=o_dq, ncols=n_dqk, heads=yes, mode="raw", out_dtype=BF16, tn=wide, name="inproj_dqk")
    dv = proj(g_one, src0=o_dv, ncols=o_mg - o_dv, heads=no, mode="raw", out_dtype=BF16, tn=wide, name="inproj_dv")
    mgate = proj(g_one, src0=o_mg, ncols=2 * D, heads=no, mode="sigmoid", out_dtype=BF16, tn=wide, name="inproj_mg")

    nbias = _bias_tiles(_window_idx(QB).transpose(0, 2, 1), rel_bias, head0=0, groups=G, hpg=NSA_HPG,
                        name="bias_nsa", rel=True, mult=LOG2E)
    cmp_idx, cmp_u0, cmp_per = _cmp_idx(T)
    cbias = _bias_tiles(cmp_idx.transpose(0, 2, 1), rel_bias, head0=0, groups=G, hpg=NSA_HPG, name="bias_cmp",
                        mult=LOG2E)
    dbias = _bias_tiles(_causal_idx(DQB), rel_bias, head0=NSA_HEADS, groups=DIFF_HEADS, hpg=1, name="bias_diff",
                        rel=True, mult=LOG2E)

    kc, vc = _compress(cmpkv.reshape(B, T, n_kv), cmp_pe, cmp_w1, cmp_w2, nsa_k_gain[0])
    o_nsa = _nsa_t(qn.reshape(B, T, n_q), kv.reshape(B, T, 2 * n_kv), kc, vc, gates.reshape(B, T, LANE), cbias,
                   cmp_u0, cmp_per, nbias)
    o_diff = _diff(dqk.reshape(B, T, n_dqk), dv.reshape(B, T, -1), diff_lambda_q, diff_lambda_k, diff_subln_gain,
                   dbias)

    merged = _merge(o_nsa.reshape(M, -1), o_diff.reshape(M, -1), w_nsa_out, w_diff_out, mgate)
    x1, h2 = _oproj(merged, w_o, x.reshape(M, D), mod3, norm2_gain, T)
    act = _ffn_up(h2, w_ffn_up, ffn_conv_w, ffn_conv_b, T)
    out = _ffn_down(act, w_ffn_down, x1, mod3, T)
    return out.reshape(B, T, D)


def kernel(x, c, w_ada, b_ada, norm1_gain, norm2_gain, w_in, nsa_q_gain, nsa_k_gain, cmp_pe, cmp_w1, cmp_w2,
           diff_q_gain, diff_k_gain, diff_lambda_q, diff_lambda_k, diff_subln_gain, w_nsa_out, w_diff_out, w_o,
           w_ffn_up, ffn_conv_w, ffn_conv_b, w_ffn_down, rel_bias):
    return _layer(x, c, w_ada[0], b_ada[0], norm1_gain[0], norm2_gain[0], w_in, nsa_q_gain[0], nsa_k_gain[0],
                  cmp_pe[0], cmp_w1[0], cmp_w2[0], diff_q_gain[0], diff_k_gain[0], diff_lambda_q[0],
                  diff_lambda_k[0], diff_subln_gain[0], w_nsa_out[0], w_diff_out[0], w_o[0], w_ffn_up[0],
                  ffn_conv_w[0], ffn_conv_b[0], w_ffn_down[0], rel_bias)
```

```python
import functools
import math

import numpy as np
import jax
import jax.numpy as jnp
from jax import lax
from jax.experimental import pallas as pl
from jax.experimental.pallas import tpu as pltpu

F32 = jnp.float32
BF16 = jnp.bfloat16

HEAD_DIM = 128
NSA_HEADS = 8
NSA_KV_GROUPS = 2
NSA_HPG = NSA_HEADS // NSA_KV_GROUPS
CMP_LEN = 32
CMP_STRIDE = 16
SEL_BLOCK = 64
N_SEL = 16
WINDOW = 512
DIFF_HEADS = 4
NUM_BUCKETS = 32
MAX_DISTANCE = 128
EPS = 1e-6
NEG = -1e30
LAM_INIT = 0.8 - 0.6 * math.exp(-0.3 * 0)
LOG2E = math.log2(math.e)
MASK_BIG = -(2.0 ** 100)

LANE = 128
QB = 128
DQB = 256
SUB_ROWS = 256
MERGE_SUB_ROWS = 512
INPROJ_TM = 2048
INPROJ_TN = 1024
VMEM_LIMIT = 56 * 1024 * 1024


def _cparams(n_axes):
    return pltpu.CompilerParams(dimension_semantics=("arbitrary",) * n_axes,
                                vmem_limit_bytes=VMEM_LIMIT)


def _t5_bucket_np(dist):
    n = np.maximum(np.asarray(dist, np.int32), 0)
    max_exact = NUM_BUCKETS // 2
    nf = np.maximum(n, max_exact).astype(np.float32)
    large = max_exact + (np.log(nf / np.float32(max_exact)) / np.float32(math.log(MAX_DISTANCE / max_exact))
                         * np.float32(NUM_BUCKETS - max_exact)).astype(np.int32)
    large = np.minimum(large, NUM_BUCKETS - 1)
    return np.where(n < max_exact, n, large).astype(np.int32)


def _ada_kernel(ct_ref, w_ref, b_ref, o_ref):
    ct = ct_ref[...]
    s = ct * jax.nn.sigmoid(ct)
    w = w_ref[...]
    for b in range(ct.shape[1]):
        o_ref[b:b + 1, :] = jnp.sum(w * s[:, b:b + 1], axis=0, keepdims=True) + b_ref[...]


def _ada(c, w_ada, b_ada, tn=1024):
    B, D = c.shape
    N = w_ada.shape[1]
    return pl.pallas_call(
        _ada_kernel,
        grid=(N // tn,),
        in_specs=[pl.BlockSpec((D, B), lambda j: (0, 0)),
                  pl.BlockSpec((D, tn), lambda j: (0, j)),
                  pl.BlockSpec((1, tn), lambda j: (0, j))],
        out_specs=pl.BlockSpec((B, tn), lambda j: (0, j)),
        out_shape=jax.ShapeDtypeStruct((B, N), F32),
        compiler_params=_cparams(1),
        name="ada",
    )(c.T, w_ada, b_ada.reshape(1, N))


def _modnorm(x, gain, sc, sh):
    ms = jnp.mean(x * x, axis=-1, keepdims=True)
    return (x * lax.rsqrt(ms + EPS) * gain) * (1.0 + sc) + sh


def _row_chunks(tm, sub=SUB_ROWS):
    return [slice(r, r + sub) for r in range(0, tm, sub)]


def _inproj_epilogue(acc, g_ref, o_ref, rows, heads, mode):
    for k, is_head in enumerate(heads):
        lanes = slice(k * LANE, (k + 1) * LANE)
        y = acc[:, lanes]
        if is_head:
            ms = jnp.mean(y * y, axis=-1, keepdims=True)
            y = y * lax.rsqrt(ms + EPS) * g_ref[:, lanes]
        elif mode == "sigmoid":
            y = jax.nn.sigmoid(y)
        o_ref[rows, lanes] = y.astype(o_ref.dtype)


def _inproj_kernel(a_ref, wt_ref, g_ref, o_ref, wb_ref, *, heads, mode):
    @pl.when(pl.program_id(1) == 0)
    def _():
        wb_ref[...] = wt_ref[0].astype(BF16)

    for rows in _row_chunks(a_ref.shape[0]):
        acc = _qk(a_ref[rows, :], wb_ref[...])
        _inproj_epilogue(acc, g_ref, o_ref, rows, heads, mode)


def _norm1_inproj_kernel(x_ref, n1g_ref, sc_ref, sh_ref, wt_ref, g_ref, o_ref, h_ref, wb_ref, *, heads, mode):
    @pl.when(pl.program_id(0) == 0)
    def _():
        wb_ref[...] = wt_ref[0].astype(BF16)

    for rows in _row_chunks(x_ref.shape[0]):
        h = _modnorm(x_ref[rows, :], n1g_ref[...], sc_ref[0], sh_ref[0]).astype(BF16)
        h_ref[rows, :] = h
        _inproj_epilogue(_qk(h, wb_ref[...]), g_ref, o_ref, rows, heads, mode)


def _norm1_inproj(x2d, n1_gain, mod3, w_in_t, gains, *, T, src0, ncols, heads, mode, out_dtype, tm, name):
    M, K = x2d.shape
    per = T // tm
    assert src0 % 8 == 0 and len(heads) == ncols // LANE
    return pl.pallas_call(
        functools.partial(_norm1_inproj_kernel, heads=heads, mode=mode),
        grid=(M // tm,),
        in_specs=[pl.BlockSpec((tm, K), lambda i: (i, 0)),
                  pl.BlockSpec((1, K), lambda i: (0, 0)),
                  pl.BlockSpec((1, 1, K), lambda i: ((i // per) * 6 + 1, 0, 0)),
                  pl.BlockSpec((1, 1, K), lambda i: ((i // per) * 6 + 0, 0, 0)),
                  pl.BlockSpec((pl.Element(1), pl.Element(ncols), pl.Element(K)), lambda i: (0, src0, 0)),
                  pl.BlockSpec((1, ncols), lambda i: (0, 0))],
        out_specs=[pl.BlockSpec((tm, ncols), lambda i: (i, 0)), pl.BlockSpec((tm, K), lambda i: (i, 0))],
        out_shape=[jax.ShapeDtypeStruct((M, ncols), out_dtype), jax.ShapeDtypeStruct((M, K), BF16)],
        scratch_shapes=[pltpu.VMEM((ncols, K), BF16)],
        compiler_params=_cparams(1),
        name=name,
    )(x2d, n1_gain.reshape(1, K), mod3, mod3, w_in_t, gains)


def _inproj(h2d, w_in_t, gains, *, src0, ncols, heads, mode, out_dtype, tm, tn, name):
    M, K = h2d.shape
    assert src0 % 8 == 0 and ncols % tn == 0 and len(heads) == tn // LANE
    return pl.pallas_call(
        functools.partial(_inproj_kernel, heads=heads, mode=mode),
        grid=(ncols // tn, M // tm),
        in_specs=[pl.BlockSpec((tm, K), lambda j, i: (i, 0)),
                  pl.BlockSpec((pl.Element(1), pl.Element(tn), pl.Element(K)),
                               lambda j, i: (0, pl.multiple_of(src0 + j * tn, 8), 0)),
                  pl.BlockSpec((1, tn), lambda j, i: (0, j))],
        out_specs=pl.BlockSpec((tm, tn), lambda j, i: (i, j)),
        out_shape=jax.ShapeDtypeStruct((M, ncols), out_dtype),
        scratch_shapes=[pltpu.VMEM((tn, K), BF16)],
        compiler_params=_cparams(2),
        name=name,
    )(h2d, w_in_t, gains)


def _inproj_cmp_gate_kernel(a_ref, wc_ref, wg_ref, oc_ref, og_ref, wcb_ref, wgb_ref):
    @pl.when(pl.program_id(0) == 0)
    def _():
        wcb_ref[...] = wc_ref[0].astype(BF16)
        wgb_ref[...] = wg_ref[0].astype(BF16)

    for rows in _row_chunks(a_ref.shape[0]):
        a = a_ref[rows, :]
        oc_ref[rows, :] = _qk(a, wcb_ref[...])
        og_ref[rows, :] = jax.nn.sigmoid(_qk(a, wgb_ref[...]))


def _inproj_cmp_gate(h2d, w_in_t, *, src_cmp, n_cmp, src_gate, tm):
    M, K = h2d.shape
    assert src_cmp % 8 == 0 and src_gate % 8 == 0

    def w_rows(start, n):
        return pl.BlockSpec((pl.Element(1), pl.Element(n), pl.Element(K)), lambda i: (0, start, 0))

    return pl.pallas_call(
        _inproj_cmp_gate_kernel,
        grid=(M // tm,),
        in_specs=[pl.BlockSpec((tm, K), lambda i: (i, 0)), w_rows(src_cmp, n_cmp), w_rows(src_gate, LANE)],
        out_specs=[pl.BlockSpec((tm, n_cmp), lambda i: (i, 0)), pl.BlockSpec((tm, LANE), lambda i: (i, 0))],
        out_shape=[jax.ShapeDtypeStruct((M, n_cmp), F32), jax.ShapeDtypeStruct((M, LANE), F32)],
        scratch_shapes=[pltpu.VMEM((n_cmp, K), BF16), pltpu.VMEM((LANE, K), BF16)],
        compiler_params=_cparams(1),
        name="inproj_cmp_gate",
    )(h2d, w_in_t, w_in_t)


def _bias_kernel(idx_ref, tab_ref, o_ref, *, head0, hpg, rel, mult):
    n_tiles, _, C = idx_ref.shape
    for n in range(n_tiles):
        idx = idx_ref[n]
        for h in range(hpg):
            head = head0 + pl.program_id(0) * hpg + h
            acc = jnp.zeros(idx.shape, F32)
            for b in range(NUM_BUCKETS):
                acc = jnp.where(idx == b, tab_ref[b, head], acc)
            if rel:
                acc = acc - tab_ref[NUM_BUCKETS - 1, head]
            o_ref[0, n, :, h * C:(h + 1) * C] = jnp.where(idx < 0, NEG, acc * mult)


def _bias_tiles(idx, rel_bias, *, head0, groups, hpg, name, rel=False, mult=1.0):
    N, R, C = idx.shape
    return pl.pallas_call(
        functools.partial(_bias_kernel, head0=head0, hpg=hpg, rel=rel, mult=mult),
        grid=(groups,),
        in_specs=[pl.BlockSpec((N, R, C), lambda g: (0, 0, 0)),
                  pl.BlockSpec(memory_space=pltpu.SMEM)],
        out_specs=pl.BlockSpec((1, N, R, hpg * C), lambda g: (g, 0, 0, 0)),
        out_shape=jax.ShapeDtypeStruct((groups, N, R, hpg * C), F32),
        compiler_params=_cparams(1),
        name=name,
    )(jnp.asarray(idx), rel_bias)


def _causal_idx(R):
    r = np.arange(R)[:, None]
    c = np.arange(R)[None, :]
    return np.stack([np.where(r >= c, _t5_bucket_np(r - c), -1), _t5_bucket_np(R + r - c)]).astype(np.int32)


def _window_idx(R):
    r = np.arange(R)[:, None]
    c = np.arange(R)[None, :]
    edge = np.where(r < c, NUM_BUCKETS - 1, -1)
    return np.concatenate([_causal_idx(R), edge[None]]).astype(np.int32)


def _cmp_idx(T):
    per = QB // CMP_STRIDE
    u0 = (T // QB - 1) * per
    assert u0 + LANE <= 2 * LANE
    r = np.arange(QB)[:, None]
    end = (np.arange(2 * LANE)[None, :] - u0) * CMP_STRIDE + CMP_LEN - 1
    return np.where(end <= r, _t5_bucket_np(r - end), -1).astype(np.int32)[None], u0, per


def _compress_kernel(zk_ref, zv_ref, pe_ref, w1_ref, w2_ref, kg_ref, kc_ref, vc_ref):
    half = CMP_LEN // 2

    def one(z_ref, i):
        p1 = jnp.zeros((LANE, HEAD_DIM), F32)
        p2 = jnp.zeros((LANE, HEAD_DIM), F32)
        for l in range(half):
            z = z_ref[0, pl.ds(l, LANE, stride=CMP_STRIDE), :]
            w_lo = w1_ref[i, l * HEAD_DIM:(l + 1) * HEAD_DIM, :].astype(BF16)
            w_hi = w1_ref[i, (half + l) * HEAD_DIM:(half + l + 1) * HEAD_DIM, :].astype(BF16)
            p1 = p1 + jnp.dot((z + pe_ref[i, l:l + 1, :]).astype(BF16), w_lo, preferred_element_type=F32)
            p2 = p2 + jnp.dot((z + pe_ref[i, half + l:half + l + 1, :]).astype(BF16), w_hi,
                              preferred_element_type=F32)
        pre = p1 + pltpu.roll(p2, LANE - 1, axis=0)
        hid = jax.nn.gelu(pre)
        return jnp.dot(hid.astype(BF16), w2_ref[i].astype(BF16), preferred_element_type=F32)

    kc = one(zk_ref, 0)
    ms = jnp.mean(kc * kc, axis=-1, keepdims=True)
    kc_ref[0, 0] = (kc * lax.rsqrt(ms + EPS) * kg_ref[...]).astype(kc_ref.dtype)
    vc_ref[0, 0] = one(zv_ref, 1).astype(vc_ref.dtype)


def _compress(cmp_kv, pe, w1, w2, k_gain0):
    B, T, _ = cmp_kv.shape
    G, dk = NSA_KV_GROUPS, HEAD_DIM
    assert (T - CMP_LEN) // CMP_STRIDE + 1 == LANE - 1
    out = jax.ShapeDtypeStruct((B, G, LANE, dk), BF16)
    return pl.pallas_call(
        _compress_kernel,
        grid=(B, G),
        in_specs=[pl.BlockSpec((1, T, dk), lambda b, g: (b, 0, g)),
                  pl.BlockSpec((1, T, dk), lambda b, g: (b, 0, G + g)),
                  pl.BlockSpec((2, CMP_LEN, dk), lambda b, g: (0, 0, 0)),
                  pl.BlockSpec((2, CMP_LEN * dk, dk), lambda b, g: (0, 0, 0)),
                  pl.BlockSpec((2, dk, dk), lambda b, g: (0, 0, 0)),
                  pl.BlockSpec((1, dk), lambda b, g: (0, 0))],
        out_specs=[pl.BlockSpec((1, 1, LANE, dk), lambda b, g: (b, g, 0, 0))] * 2,
        out_shape=[out, out],
        compiler_params=_cparams(2),
        name="compress",
    )(cmp_kv, cmp_kv, pe, w1, w2, k_gain0.reshape(1, dk))


def _qk(q, k):
    return lax.dot_general(q, k, (((1,), (1,)), ((), ())), preferred_element_type=F32)


def _lane_fold(x, op):
    acc = x[..., :LANE]
    for t in range(1, x.shape[-1] // LANE):
        acc = op(acc, x[..., t * LANE:(t + 1) * LANE])
    return acc


def _softmax_parts(parts):
    m = jnp.max(functools.reduce(jnp.maximum, [_lane_fold(s, jnp.maximum) for s in parts]), axis=-1, keepdims=True)
    ps = [jnp.exp2(s - m) for s in parts]
    den = jnp.sum(functools.reduce(jnp.add, [_lane_fold(p, jnp.add) for p in ps]), axis=-1, keepdims=True)
    return ps, den


def _nsa_t_kernel(q_ref, ks_ref, vs_ref, kw_ref, vw_ref, kc_ref, vc_ref, gate_ref, cb_ref, nb_ref,
                  ovt_ref, expt_ref, o_ref, vst_ref, vwt_ref, vct_ref, ksa_ref, *, cmp_u0, cmp_per):
    H, W = NSA_HPG, NSA_HPG * QB
    T = ks_ref.shape[1]
    ns = ovt_ref.shape[0]
    nwb = WINDOW // QB
    nb = T // QB
    kc = kc_ref[0, 0]

    @pl.when(pl.program_id(2) == 0)
    def _():
        ksa_ref[:, :HEAD_DIM] = ks_ref[0]
        ksa_ref[:, HEAD_DIM:] = expt_ref[...]
        vst_ref[...] = vs_ref[0].astype(F32).T.astype(BF16)
        vwt_ref[...] = vw_ref[0].astype(F32).T.astype(BF16)
        vct_ref[...] = vc_ref[0, 0].astype(F32).T.astype(BF16)

    def softmax_t(parts):
        m = functools.reduce(jnp.maximum, [jnp.max(s, axis=0, keepdims=True) for s in parts])
        ps = [jnp.exp2(s - m) for s in parts]
        den = functools.reduce(jnp.add, [jnp.sum(p, axis=0, keepdims=True) for p in ps])
        return [p.astype(BF16) for p in ps], den

    def attend_task(vt_ref, spans_fn, out, key):
        spans = spans_fn()
        parts = []
        for _, _, k_slab, q_op, add in spans:
            s = _qk(k_slab, q_op)
            parts.append(s if add is None else s + add)
        yield
        ps, den = softmax_t(parts)
        yield
        o = None
        for (a, b, _, _, _), p in zip(spans, ps):
            pv = jnp.dot(vt_ref[:, a:b], p, preferred_element_type=F32)
            o = pv if o is None else o + pv
        out[key] = o / den
        yield

    def cmp_task(i, qs, out):
        lo = i * QB
        u = cmp_u0 - cmp_per * i
        s = _qk(kc, qs) + cb_ref[0, 0, u:u + LANE, :]
        yield
        e = jnp.exp2(s - jnp.max(s, axis=0, keepdims=True))
        if i == 0:
            key_end = lax.broadcasted_iota(jnp.int32, (LANE, W), 0) * CMP_STRIDE + (CMP_LEN - 1)
            query = lax.broadcasted_iota(jnp.int32, (LANE, W), 1) % QB
            e = jnp.where(key_end <= query, e, 0.0)
            den = jnp.sum(e, axis=0, keepdims=True)
            p = e / jnp.where(den > 0.0, den, 1.0)
        else:
            p = e / jnp.sum(e, axis=0, keepdims=True)
        if i >= 1:
            psum = functools.reduce(jnp.add, [p[:, h * QB:(h + 1) * QB] for h in range(H)])
            imp_t = jnp.dot(ovt_ref[...], psum, precision=lax.Precision.HIGHEST, preferred_element_type=F32)
        yield
        out["cmp", i] = jnp.dot(vct_ref[...], p.astype(BF16), preferred_element_type=F32)
        if i >= 1:
            blk = lax.broadcasted_iota(jnp.int32, (ns, QB), 0)
            cur = (lo + lax.broadcasted_iota(jnp.int32, (ns, QB), 1)) // SEL_BLOCK
            forced = (blk == 0) | (blk == cur) | (blk == cur - 1)
            score = jnp.where(forced, 1e4, jnp.where(blk <= cur, imp_t, -1e4))
            rank = jnp.zeros((ns, QB), F32)
            for b in range(ns):
                other = score[b:b + 1, :]
                rank = rank + jnp.where(blk > b, jnp.where(other >= score, 1.0, 0.0),
                                        jnp.where(other > score, 1.0, 0.0))
            unsel = jnp.where(rank < float(min(N_SEL, ns)), 0.0, 1.0)
            unsel_q = jnp.concatenate([unsel, jnp.zeros((LANE - ns, QB), F32)], axis=0).T.astype(BF16)
            out["qs_aug", i] = jnp.concatenate([qs, jnp.concatenate([unsel_q] * H, axis=0)], axis=1)
        yield

    def slc_spans(i, qs, out):
        lo, hi = i * QB, (i + 1) * QB
        spans = [(lo, hi, ks_ref[0, lo:hi, :], qs, nb_ref[0, 0])]
        if i >= 1:
            qs_aug = out["qs_aug", i]
            spans.insert(0, (lo - QB, lo, ksa_ref[lo - QB:lo, :], qs_aug, nb_ref[0, 1]))
            if i >= 2:
                spans.insert(0, (0, lo - QB, ksa_ref[0:lo - QB, :], qs_aug, None))
        return spans

    def win_spans(i, qs):
        lo, hi = i * QB, (i + 1) * QB

        def span(a, b, add):
            return (a, b, kw_ref[0, a:b, :], qs, add)

        spans = []
        if i >= nwb:
            spans.append(span((i - nwb) * QB, (i - nwb + 1) * QB, nb_ref[0, 2]))
        mid_a, mid_b = max(i - nwb + 1, 0) * QB, (i - 1) * QB
        if mid_b > mid_a:
            spans.append(span(mid_a, mid_b, None))
        if i >= 1:
            spans.append(span(lo - QB, lo, nb_ref[0, 1]))
        spans.append(span(lo, hi, nb_ref[0, 0]))
        return spans

    def combine(i, out):
        lo, hi = i * QB, (i + 1) * QB
        gate_t = gate_ref[0, lo:hi, :].T
        first = pl.program_id(1) == 0

        def grow(br):
            rows = []
            for h in range(H):
                c = 3 * h + br
                rows.append(jnp.where(first, gate_t[c:c + 1, :], gate_t[3 * H + c:3 * H + c + 1, :]))
            return jnp.concatenate(rows, axis=1)

        o_t = grow(0) * out["cmp", i] + grow(1) * out["slc", i] + grow(2) * out["win", i]
        o_ref[0, lo:hi, :] = jnp.concatenate([o_t[:, h * QB:(h + 1) * QB].T for h in range(H)],
                                             axis=1).astype(o_ref.dtype)

    def pair(k):
        out = {}
        tasks = []
        blocks = (nb - 1 - k, k)
        qss = {}
        for i in blocks:
            q = q_ref[0, i * QB:(i + 1) * QB, :]
            qss[i] = jnp.concatenate([q[:, h * HEAD_DIM:(h + 1) * HEAD_DIM] for h in range(H)], axis=0)
            tasks.append(cmp_task(i, qss[i], out))
        for i in blocks:
            tasks.append(attend_task(vwt_ref, functools.partial(win_spans, i, qss[i]), out, ("win", i)))
        for i in blocks:
            tasks.append(attend_task(vst_ref, functools.partial(slc_spans, i, qss[i], out), out, ("slc", i)))
        n_stage = 3
        for step in range(len(tasks) + n_stage - 1):
            for t in range(step - n_stage + 1, step + 1):
                if 0 <= t < len(tasks):
                    next(tasks[t])
        for i in blocks:
            combine(i, out)

    for k in range(nb // 2):
        pl.when(pl.program_id(2) == k)(functools.partial(pair, k))


def _nsa_t(qn, kv, kc, vc, gates, cbias, cmp_u0, cmp_per, nbias):
    B, T, _ = qn.shape
    G, H, dk = NSA_KV_GROUPS, NSA_HPG, HEAD_DIM
    nb = T // QB
    ns = T // SEL_BLOCK
    cstart = np.arange(LANE) * CMP_STRIDE
    sstart = np.arange(ns) * SEL_BLOCK
    overlap = np.clip(np.minimum(cstart[:, None] + CMP_LEN, sstart[None, :] + SEL_BLOCK)
                      - np.maximum(cstart[:, None], sstart[None, :]), 0, None) / CMP_STRIDE
    overlap[LANE - 1:] = 0.0
    ovt = jnp.asarray(overlap.T, F32)
    expand_t = np.zeros((T, LANE), np.float32)
    expand_t[np.arange(T), np.arange(T) // SEL_BLOCK] = MASK_BIG
    return pl.pallas_call(
        functools.partial(_nsa_t_kernel, cmp_u0=cmp_u0, cmp_per=cmp_per),
        grid=(B, G, nb // 2),
        in_specs=[pl.BlockSpec((1, T, H * dk), lambda b, g, i: (b, 0, g)),
                  pl.BlockSpec((1, T, dk), lambda b, g, i: (b, 0, g)),
                  pl.BlockSpec((1, T, dk), lambda b, g, i: (b, 0, G + g)),
                  pl.BlockSpec((1, T, dk), lambda b, g, i: (b, 0, 2 * G + g)),
                  pl.BlockSpec((1, T, dk), lambda b, g, i: (b, 0, 3 * G + g)),
                  pl.BlockSpec((1, 1, LANE, dk), lambda b, g, i: (b, g, 0, 0)),
                  pl.BlockSpec((1, 1, LANE, dk), lambda b, g, i: (b, g, 0, 0)),
                  pl.BlockSpec((1, T, LANE), lambda b, g, i: (b, 0, 0)),
                  pl.BlockSpec((1, 1, 2 * LANE, H * QB), lambda b, g, i: (g, 0, 0, 0)),
                  pl.BlockSpec((1, 3, QB, H * QB), lambda b, g, i: (g, 0, 0, 0)),
                  pl.BlockSpec((ns, LANE), lambda b, g, i: (0, 0)),
                  pl.BlockSpec((T, LANE), lambda b, g, i: (0, 0))],
        out_specs=pl.BlockSpec((1, T, H * dk), lambda b, g, i: (b, 0, g)),
        out_shape=jax.ShapeDtypeStruct((B, T, NSA_HEADS * dk), BF16),
        scratch_shapes=[pltpu.VMEM((dk, T), BF16), pltpu.VMEM((dk, T), BF16), pltpu.VMEM((dk, LANE), BF16),
                        pltpu.VMEM((T, 2 * dk), BF16)],
        compiler_params=_cparams(3),
        name="nsa",
    )(qn, kv, kv, kv, kv, kc, vc, gates, cbias, nbias, ovt, jnp.asarray(expand_t, BF16))


def _diff_kernel(q_ref, k_ref, v_ref, lq_ref, lk_ref, sg_ref, db_ref, o_ref):
    dk = HEAD_DIM
    T = k_ref.shape[1]
    lqk = lq_ref[...] * lk_ref[...]
    lam = (jnp.exp(jnp.sum(lqk[0:1], axis=-1, keepdims=True))
           - jnp.exp(jnp.sum(lqk[1:2], axis=-1, keepdims=True)) + LAM_INIT)
    def task(i, mm, out):
        lo, hi = i * DQB, (i + 1) * DQB
        cols = slice(mm * dk, (mm + 1) * dk)
        q = q_ref[0, lo:hi, cols]
        bounds, parts = [], []
        if i >= 2:
            bounds.append((0, lo - DQB))
            parts.append(_qk(q, k_ref[0, 0:lo - DQB, cols]))
        if i >= 1:
            bounds.append((lo - DQB, lo))
            parts.append(_qk(q, k_ref[0, lo - DQB:lo, cols]) + db_ref[0, 1])
        bounds.append((lo, hi))
        parts.append(_qk(q, k_ref[0, lo:hi, cols]) + db_ref[0, 0])
        yield
        ps, den = _softmax_parts(parts)
        yield
        o = None
        for (a, b), p in zip(bounds, ps):
            pv = jnp.dot(p.astype(BF16), v_ref[0, a:b, :], preferred_element_type=F32)
            o = pv if o is None else o + pv
        out[i, mm] = o / den
        yield

    out = {}
    tasks = [task(i, mm, out) for i in reversed(range(T // DQB)) for mm in range(2)]
    n_stage = 3
    for step in range(len(tasks) + n_stage - 1):
        for t in range(step - n_stage + 1, step + 1):
            if 0 <= t < len(tasks):
                next(tasks[t])

    for i in range(T // DQB):
        lo, hi = i * DQB, (i + 1) * DQB
        o = out[i, 0] - lam * out[i, 1]
        ms = jnp.mean(o * o, axis=-1, keepdims=True)
        o_ref[0, lo:hi, :] = ((o * lax.rsqrt(ms + EPS) * sg_ref[...]) * (1.0 - LAM_INIT)).astype(o_ref.dtype)


def _diff(dqk, dv, lam_q, lam_k, subln_gain, dbias):
    B, T, _ = dqk.shape
    Hd, dk = DIFF_HEADS, HEAD_DIM
    w = 2 * dk
    return pl.pallas_call(
        _diff_kernel,
        grid=(B, Hd),
        in_specs=[pl.BlockSpec((1, T, w), lambda b, h: (b, 0, h)),
                  pl.BlockSpec((1, T, w), lambda b, h: (b, 0, Hd + h)),
                  pl.BlockSpec((1, T, w), lambda b, h: (b, 0, h)),
                  pl.BlockSpec((2, dk), lambda b, h: (0, 0)),
                  pl.BlockSpec((2, dk), lambda b, h: (0, 0)),
                  pl.BlockSpec((1, w), lambda b, h: (0, 0)),
                  pl.BlockSpec((1, 2, DQB, DQB), lambda b, h: (h, 0, 0, 0))],
        out_specs=pl.BlockSpec((1, T, w), lambda b, h: (b, 0, h)),
        out_shape=jax.ShapeDtypeStruct((B, T, Hd * w), BF16),
        compiler_params=_cparams(2),
        name="diff",
    )(dqk, dqk, dv, lam_q, lam_k, subln_gain.reshape(1, w), dbias)


def _merge_kernel(an_ref, ad_ref, wn_ref, wd_ref, gn_ref, gd_ref, o_ref, wnb_ref, wdb_ref):
    @pl.when(pl.program_id(1) == 0)
    def _():
        wnb_ref[...] = wn_ref[...].astype(BF16)
        wdb_ref[...] = wd_ref[...].astype(BF16)

    for rows in _row_chunks(an_ref.shape[0], MERGE_SUB_ROWS):
        yn = jnp.dot(an_ref[rows, :], wnb_ref[...], preferred_element_type=F32)
        yd = jnp.dot(ad_ref[rows, :], wdb_ref[...], preferred_element_type=F32)
        o_ref[rows, :] = (gn_ref[rows, :].astype(F32) * yn + gd_ref[rows, :].astype(F32) * yd).astype(o_ref.dtype)


def _merge(o_nsa, o_diff, w_n, w_d, mg, tm=1024, tn=1024):
    M, K = o_nsa.shape
    N = w_n.shape[1]
    nj = N // tn
    return pl.pallas_call(
        _merge_kernel,
        grid=(nj, M // tm),
        in_specs=[pl.BlockSpec((tm, K), lambda j, i: (i, 0)),
                  pl.BlockSpec((tm, K), lambda j, i: (i, 0)),
                  pl.BlockSpec((K, tn), lambda j, i: (0, j)),
                  pl.BlockSpec((K, tn), lambda j, i: (0, j)),
                  pl.BlockSpec((tm, tn), lambda j, i: (i, j)),
                  pl.BlockSpec((tm, tn), lambda j, i: (i, nj + j))],
        out_specs=pl.BlockSpec((tm, tn), lambda j, i: (i, j)),
        out_shape=jax.ShapeDtypeStruct((M, N), BF16),
        scratch_shapes=[pltpu.VMEM((K, tn), BF16), pltpu.VMEM((K, tn), BF16)],
        compiler_params=_cparams(2),
        name="merge",
    )(o_nsa, o_diff, w_n, w_d, mg, mg)


def _oproj_kernel(a_ref, w_ref, x_ref, g1_ref, gain_ref, sc_ref, sh_ref, x1_ref, h2_ref, wb_ref):
    @pl.when(pl.program_id(0) == 0)
    def _():
        wb_ref[...] = w_ref[...].astype(BF16)

    for rows in _row_chunks(a_ref.shape[0]):
        y = jnp.dot(a_ref[rows, :], wb_ref[...], preferred_element_type=F32)
        x1 = x_ref[rows, :] + g1_ref[0] * y
        x1_ref[rows, :] = x1
        h2_ref[rows, :] = _modnorm(x1, gain_ref[...], sc_ref[0], sh_ref[0]).astype(h2_ref.dtype)


def _oproj(merged, w_o, x2d, mod3, gain2, T, tm=512):
    M, D = x2d.shape
    per = T // tm
    return pl.pallas_call(
        _oproj_kernel,
        grid=(M // tm,),
        in_specs=[pl.BlockSpec((tm, D), lambda i: (i, 0)),
                  pl.BlockSpec((D, D), lambda i: (0, 0), pipeline_mode=pl.Buffered(1)),
                  pl.BlockSpec((tm, D), lambda i: (i, 0)),
                  pl.BlockSpec((1, 1, D), lambda i: ((i // per) * 6 + 2, 0, 0)),
                  pl.BlockSpec((1, D), lambda i: (0, 0)),
                  pl.BlockSpec((1, 1, D), lambda i: ((i // per) * 6 + 4, 0, 0)),
                  pl.BlockSpec((1, 1, D), lambda i: ((i // per) * 6 + 3, 0, 0))],
        out_specs=[pl.BlockSpec((tm, D), lambda i: (i, 0)),
                   pl.BlockSpec((tm, D), lambda i: (i, 0))],
        out_shape=[jax.ShapeDtypeStruct((M, D), F32), jax.ShapeDtypeStruct((M, D), BF16)],
        scratch_shapes=[pltpu.VMEM((D, D), BF16)],
        compiler_params=_cparams(1),
        name="oproj",
    )(merged, w_o, x2d, mod3, gain2.reshape(1, D), mod3, mod3)


def _ffn_up_kernel(h_ref, wa_ref, wv_ref, cwa_ref, cwv_ref, cba_ref, cbv_ref, o_ref, wab_ref, wvb_ref,
                   ca_ref, cv_ref, sa_ref, sv_ref, *, per):
    i = pl.program_id(1)

    @pl.when(i == 0)
    def _():
        wab_ref[...] = wa_ref[...].astype(BF16)
        wvb_ref[...] = wv_ref[...].astype(BF16)

    @pl.when(i % per == 0)
    def _():
        ca_ref[...] = jnp.zeros(ca_ref.shape, F32)
        cv_ref[...] = jnp.zeros(cv_ref.shape, F32)

    def conv(u, prev, cw_ref, cb_ref, s_ref):
        s_ref[0:8, :] = prev
        s_ref[8:8 + SUB_ROWS, :] = u
        u1 = s_ref[7:7 + SUB_ROWS, :]
        u2 = s_ref[6:6 + SUB_ROWS, :]
        return cb_ref[...] + cw_ref[0:1, :] * u2 + cw_ref[1:2, :] * u1 + cw_ref[2:3, :] * u

    prev_a, prev_v = ca_ref[...], cv_ref[...]
    for n, rows in enumerate(_row_chunks(h_ref.shape[0], SUB_ROWS)):
        hs = h_ref[rows, :]
        ua = jnp.dot(hs, wab_ref[...], preferred_element_type=F32)
        uv = jnp.dot(hs, wvb_ref[...], preferred_element_type=F32)
        a = conv(ua, prev_a, cwa_ref, cba_ref, sa_ref.at[n % 2])
        val = conv(uv, prev_v, cwv_ref, cbv_ref, sv_ref.at[n % 2])
        o_ref[rows, :] = (a * jax.nn.sigmoid(a) * val).astype(o_ref.dtype)
        prev_a, prev_v = ua[SUB_ROWS - 8:, :], uv[SUB_ROWS - 8:, :]
    ca_ref[...] = prev_a
    cv_ref[...] = prev_v


def _ffn_up(h2, w_up, conv_w, conv_b, T, tm=2048, tn=512):
    M, D = h2.shape
    F = w_up.shape[1] // 2
    nj = F // tn
    cb = conv_b.reshape(1, 2 * F)
    return pl.pallas_call(
        functools.partial(_ffn_up_kernel, per=T // tm),
        grid=(nj, M // tm),
        in_specs=[pl.BlockSpec((tm, D), lambda j, i: (i, 0)),
                  pl.BlockSpec((D, tn), lambda j, i: (0, j)),
                  pl.BlockSpec((D, tn), lambda j, i: (0, nj + j)),
                  pl.BlockSpec((3, tn), lambda j, i: (0, j)),
                  pl.BlockSpec((3, tn), lambda j, i: (0, nj + j)),
                  pl.BlockSpec((1, tn), lambda j, i: (0, j)),
                  pl.BlockSpec((1, tn), lambda j, i: (0, nj + j))],
        out_specs=pl.BlockSpec((tm, tn), lambda j, i: (i, j)),
        out_shape=jax.ShapeDtypeStruct((M, F), BF16),
        scratch_shapes=[pltpu.VMEM((D, tn), BF16), pltpu.VMEM((D, tn), BF16),
                        pltpu.VMEM((8, tn), F32), pltpu.VMEM((8, tn), F32),
                        pltpu.VMEM((2, SUB_ROWS + 8, tn), F32), pltpu.VMEM((2, SUB_ROWS + 8, tn), F32)],
        compiler_params=_cparams(2),
        name="ffn_up",
    )(h2, w_up, w_up, conv_w, conv_w, cb, cb)


def _ffn_down_kernel(a_ref, w_ref, x_ref, g2_ref, o_ref, wb_ref):
    @pl.when(pl.program_id(1) == 0)
    def _():
        wb_ref[...] = w_ref[...].astype(BF16)

    for rows in _row_chunks(a_ref.shape[0], MERGE_SUB_ROWS):
        y = jnp.dot(a_ref[rows, :], wb_ref[...], preferred_element_type=F32)
        o_ref[rows, :] = x_ref[rows, :] + g2_ref[0] * y


def _ffn_down(act, w_down, x1, mod3, T, tm=512, tn=512):
    M, F = act.shape
    D = w_down.shape[1]
    per = T // tm
    return pl.pallas_call(
        _ffn_down_kernel,
        grid=(D // tn, M // tm),
        in_specs=[pl.BlockSpec((tm, F), lambda j, i: (i, 0)),
                  pl.BlockSpec((F, tn), lambda j, i: (0, j)),
                  pl.BlockSpec((tm, tn), lambda j, i: (i, j)),
                  pl.BlockSpec((1, 1, tn), lambda j, i: ((i // per) * 6 + 5, 0, j))],
        out_specs=pl.BlockSpec((tm, tn), lambda j, i: (i, j)),
        out_shape=jax.ShapeDtypeStruct((M, D), F32),
        scratch_shapes=[pltpu.VMEM((F, tn), BF16)],
        compiler_params=_cparams(2),
        name="ffn_down",
    )(act, w_down, x1, mod3)


def _layer(x, c, w_ada, b_ada, norm1_gain, norm2_gain, w_in, nsa_q_gain, nsa_k_gain, cmp_pe, cmp_w1, cmp_w2,
           diff_q_gain, diff_k_gain, diff_lambda_q, diff_lambda_k, diff_subln_gain, w_nsa_out, w_diff_out, w_o,
           w_ffn_up, ffn_conv_w, ffn_conv_b, w_ffn_down, rel_bias):
    B, T, D = x.shape
    dk, G = HEAD_DIM, NSA_KV_GROUPS
    M = B * T
    scale = dk ** -0.5

    n_q = NSA_HEADS * dk
    o_kv = n_q
    o_g = o_kv + 3 * 2 * G * dk
    o_dq = o_g + NSA_HEADS * 3
    o_dk = o_dq + DIFF_HEADS * 2 * dk
    o_dv = o_dk + DIFF_HEADS * 2 * dk
    o_mg = o_dv + DIFF_HEADS * 2 * dk
    n_kv = 2 * G * dk
    n_dqk = 2 * DIFF_HEADS * 2 * dk

    mod3 = _ada(c, w_ada, b_ada).reshape(B * 6, 1, D)

    ones = jnp.ones((n_kv // 2,), F32)
    g_q = jnp.tile(nsa_q_gain * (scale * LOG2E), NSA_HEADS).reshape(1, n_q)
    g_kv = jnp.concatenate([jnp.tile(nsa_k_gain[1], G), ones, jnp.tile(nsa_k_gain[2], G), ones]).reshape(1, 2 * n_kv)
    g_dqk = jnp.concatenate([jnp.tile(diff_q_gain * (scale * LOG2E), 2 * DIFF_HEADS),
                             jnp.tile(diff_k_gain, 2 * DIFF_HEADS)]).reshape(1, n_dqk)
    g_one = jnp.ones((1, 2 * D), F32)
    w_in_t = jnp.swapaxes(w_in, 1, 2)
    wide = INPROJ_TN
    yes, no = (True,) * (wide // LANE), (False,) * (wide // LANE)
    qn, h = _norm1_inproj(x.reshape(M, D), norm1_gain, mod3, w_in_t, g_q, T=T, src0=0, ncols=n_q,
                          heads=(True,) * NSA_HEADS, mode="raw", out_dtype=BF16, tm=INPROJ_TM // 2, name="inproj_q")
    proj = functools.partial(_inproj, h, w_in_t, tm=INPROJ_TM)
    cmpkv, gates = _inproj_cmp_gate(h, w_in_t, src_cmp=o_kv, n_cmp=n_kv, src_gate=o_g, tm=INPROJ_TM)
    kv_heads = ((True,) * G + (False,) * G) * 2
    kv = proj(g_kv, src0=o_kv + n_kv, ncols=2 * n_kv, heads=kv_heads, mode="raw", out_dtype=BF16, tn=2 * n_kv,
              name="inproj_kv")
    dqk = proj(g_dqk, src0=o_dq, ncols=n_dqk, heads=yes, mode="raw", out_dtype=BF16, tn=wide, name="inproj_dqk")
    dv = proj(g_one, src0=o_dv, ncols=o_mg - o_dv, heads=no, mode="raw", out_dtype=BF16, tn=wide, name="inproj_dv")
    mgate = proj(g_one, src0=o_mg, ncols=2 * D, heads=no, mode="sigmoid", out_dtype=BF16, tn=wide, name="inproj_mg")

    nbias = _bias_tiles(_window_idx(QB).transpose(0, 2, 1), rel_bias, head0=0, groups=G, hpg=NSA_HPG,
                        name="bias_nsa", rel=True, mult=LOG2E)
    cmp_idx, cmp_u0, cmp_per = _cmp_idx(T)
    cbias = _bias_tiles(cmp_idx.transpose(0, 2, 1), rel_bias, head0=0, groups=G, hpg=NSA_HPG, name="bias_cmp",
                        mult=LOG2E)
    dbias = _bias_tiles(_causal_idx(DQB), rel_bias, head0=NSA_HEADS, groups=DIFF_HEADS, hpg=1, name="bias_diff",
                        rel=True, mult=LOG2E)

    kc, vc = _compress(cmpkv.reshape(B, T, n_kv), cmp_pe, cmp_w1, cmp_w2, nsa_k_gain[0])
    o_nsa = _nsa_t(qn.reshape(B, T, n_q), kv.reshape(B, T, 2 * n_kv), kc, vc, gates.reshape(B, T, LANE), cbias,
                   cmp_u0, cmp_per, nbias)
    o_diff = _diff(dqk.reshape(B, T, n_dqk), dv.reshape(B, T, -1), diff_lambda_q, diff_lambda_k, diff_subln_gain,
                   dbias)

    merged = _merge(o_nsa.reshape(M, -1), o_diff.reshape(M, -1), w_nsa_out, w_diff_out, mgate)
    x1, h2 = _oproj(merged, w_o, x.reshape(M, D), mod3, norm2_gain, T)
    act = _ffn_up(h2, w_ffn_up, ffn_conv_w, ffn_conv_b, T)
    out = _ffn_down(act, w_ffn_down, x1, mod3, T)
    return out.reshape(B, T, D)


def kernel(x, c, w_ada, b_ada, norm1_gain, norm2_gain, w_in, nsa_q_gain, nsa_k_gain, cmp_pe, cmp_w1, cmp_w2,
           diff_q_gain, diff_k_gain, diff_lambda_q, diff_lambda_k, diff_subln_gain, w_nsa_out, w_diff_out, w_o,
           w_ffn_up, ffn_conv_w, ffn_conv_b, w_ffn_down, rel_bias):
    return _layer(x, c, w_ada[0], b_ada[0], norm1_gain[0], norm2_gain[0], w_in, nsa_q_gain[0], nsa_k_gain[0],
                  cmp_pe[0], cmp_w1[0], cmp_w2[0], diff_q_gain[0], diff_k_gain[0], diff_lambda_q[0],
                  diff_lambda_k[0], diff_subln_gain[0], w_nsa_out[0], w_diff_out[0], w_o[0], w_ffn_up[0],
                  ffn_conv_w[0], ffn_conv_b[0], w_ffn_down[0], rel_bias)
```
